```python
import jax, jax.numpy as jnp
from jax import lax
import numpy as np

D_MODEL = 2048
BATCH = 2
SEQ = 8192
DEPTH = 1
DEC_BATCH = 16
DEC_SEQ = 32
PAST_LEN = 1024

CHUNK = 64
MIX_WIDTH = D_MODEL
GMLP_CHUNK = 128
H_A = 8
DH_A = (MIX_WIDTH // 2) // H_A
D_A = H_A * DH_A
H_B = 8
DK_B = 128
DV_B = (MIX_WIDTH - D_A) // H_B
D_B = H_B * DV_B
DKT_B = H_B * DK_B
HGRN_CHUNK = CHUNK
PROJ_WIDTH = 2 * D_A + 2 * DKT_B + 2 * D_B
SPLITS = (D_A, 2 * D_A, 2 * D_A + DKT_B, 2 * D_A + 2 * DKT_B, 2 * D_A + 2 * DKT_B + D_B)
N_GROUPS = 4
EXPERTS_PER_GROUP = 8
N_EXPERTS = N_GROUPS * EXPERTS_PER_GROUP
TOP_K_IN_GROUP = 2
D_EXPERT = D_MODEL // 4
LN_EPS = 1e-5
DEEPNORM_ALPHA = (2.0 * DEPTH) ** 0.25
DEEPNORM_BETA = (8.0 * DEPTH) ** -0.25

kernel_name = 'hymba_gmlp_hgrn2_hmoe_stream'


def ln_plain(x):
    xf = x.astype(jnp.float32)
    mu = jnp.mean(xf, axis=-1, keepdims=True)
    var = jnp.mean(jnp.square(xf - mu), axis=-1, keepdims=True)
    return ((xf - mu) * lax.rsqrt(var + LN_EPS)).astype(x.dtype)


def ln_affine(x, g, b):
    xf = x.astype(jnp.float32)
    mu = jnp.mean(xf, axis=-1, keepdims=True)
    var = jnp.mean(jnp.square(xf - mu), axis=-1, keepdims=True)
    y = (xf - mu) * lax.rsqrt(var + LN_EPS) * g.astype(jnp.float32) + b.astype(jnp.float32)
    return y.astype(x.dtype)


def rms_norm(x, g):
    xf = x.astype(jnp.float32)
    y = xf * lax.rsqrt(jnp.mean(jnp.square(xf), axis=-1, keepdims=True) + LN_EPS)
    return y * g.astype(jnp.float32)


def gmlp_spatial(v, w_s, b_s):
    B, T = v.shape[0], v.shape[1]
    L = min(GMLP_CHUNK, T)
    sub = jnp.arange(L) // CHUNK
    mask = sub[:, None] >= sub[None, :]
    w = jnp.where(mask[None], w_s[:, :L, :L], 0.0).astype(v.dtype)
    vc = v.reshape(B, T // L, L, H_A, DH_A)
    bias = jnp.transpose(b_s[:, :L])[None, None, :, :, None].astype(v.dtype)
    out = jnp.einsum('hts,bnshc->bnthc', w, vc) + bias
    return out.reshape(B, T, H_A, DH_A)


def hgrn2_chunkwise(q, k, v, logf, s0):
    B, T, H, DK = q.shape
    DV = v.shape[-1]
    C = min(HGRN_CHUNK, T)
    N = T // C

    def to_chunks(a):
        return a.reshape(B, N, C, H, a.shape[-1]).transpose(1, 0, 3, 2, 4)

    causal = jnp.tril(jnp.ones((C, C), dtype=bool))[:, :, None]

    def step(s, inp):
        qc, kc, vc, lc = inp
        a = jnp.cumsum(lc, axis=2)
        diff = a[:, :, :, None, :] - a[:, :, None, :, :]
        decay = jnp.where(causal, jnp.exp(jnp.where(causal, diff, 0.0)), 0.0)
        scores = jnp.einsum('bhtd,bhtjd,bhjd->bhtj', qc, decay, kc)
        o = jnp.einsum('bhtj,bhjv->bhtv', scores, vc) + jnp.einsum('bhtd,bhdv->bhtv', qc * jnp.exp(a), s)
        a_last = a[:, :, -1:, :]
        s_new = jnp.exp(a_last[:, :, 0, :])[..., None] * s + jnp.einsum('bhjd,bhjv->bhdv', kc * jnp.exp(a_last - a), vc)
        return s_new, o

    s_fin, o = lax.scan(step, s0, (to_chunks(q), to_chunks(k), to_chunks(v), to_chunks(logf)))
    o = o.transpose(1, 0, 3, 2, 4).reshape(B, T, H, DV)
    return o, s_fin


def mixer_sublayer(h, s0, lb, w_in, ln_v_g, ln_v_b, w_s, b_s, gnorm_g, w_out):
    B, T, _ = h.shape
    proj = h @ w_in
    u, v, q, f, i, g = jnp.split(proj, SPLITS, axis=-1)
    u = jax.nn.gelu(u)
    v = jax.nn.gelu(v).reshape(B, T, H_A, DH_A)
    v_n = ln_affine(v, ln_v_g, ln_v_b)
    a_out = u * gmlp_spatial(v_n, w_s, b_s).reshape(B, T, D_A)
    f_gate = lb + (1.0 - lb) * jax.nn.sigmoid(f.astype(jnp.float32))
    logf = jnp.log(f_gate).reshape(B, T, H_B, DK_B)
    k = (1.0 - f_gate).reshape(B, T, H_B, DK_B)
    qf = jax.nn.silu(q.astype(jnp.float32)).reshape(B, T, H_B, DK_B)
    vf = i.astype(jnp.float32).reshape(B, T, H_B, DV_B)
    o, s_fin = hgrn2_chunkwise(qf, k, vf, logf, s0.astype(jnp.float32))
    o = rms_norm(o, gnorm_g).reshape(B, T, D_B) * jax.nn.silu(g.astype(jnp.float32))
    mix = jnp.concatenate([a_out, o.astype(h.dtype)], axis=-1) @ w_out
    return mix, s_fin, v_n


def hier_moe(h, w_rg, b_rg, w_re, b_re, w1, w3, w2):
    B, T, D = h.shape
    hf = h.reshape(B * T, D)
    g_logits = (hf @ w_rg + b_rg).astype(jnp.float32)
    g_prob = jax.nn.softmax(g_logits, axis=-1)
    g_sel = jnp.argmax(g_logits, axis=-1)
    p_group = jnp.take_along_axis(g_prob, g_sel[:, None], axis=-1)
    e_logits = (hf @ w_re + b_re).astype(jnp.float32).reshape(-1, N_GROUPS, EXPERTS_PER_GROUP)
    e_in_group = jnp.take_along_axis(e_logits, g_sel[:, None, None], axis=1)[:, 0]
    top_v, top_i = lax.top_k(e_in_group, TOP_K_IN_GROUP)
    top_w = jax.nn.softmax(top_v, axis=-1) * p_group
    expert_id = g_sel[:, None] * EXPERTS_PER_GROUP + top_i
    gates = jnp.einsum('nk,nke->ne', top_w, jax.nn.one_hot(expert_id, N_EXPERTS, dtype=jnp.float32))
    out = jnp.zeros((B * T, D), jnp.float32)
    for e in range(N_EXPERTS):
        ye = (jax.nn.silu(hf @ w1[e]) * (hf @ w3[e])) @ w2[e]
        out = out + gates[:, e:e + 1] * ye.astype(jnp.float32)
    return out.astype(h.dtype).reshape(B, T, D)


def trunk_layer(x, c, s0, lb, w_ada, b_ada, w_in, ln_v_g, ln_v_b, w_s, b_s, gnorm_g, w_out,
                ln1_g, ln1_b, w_rg, b_rg, w_re, b_re, w1, w3, w2, ln2_g, ln2_b):
    mod = jax.nn.silu(c) @ w_ada + b_ada
    sh1, sc1, g1, sh2, sc2, g2 = jnp.split(mod[:, None, :], 6, axis=-1)
    h = ln_plain(x) * (1.0 + sc1) + sh1
    mix, s_new, v_rows = mixer_sublayer(h, s0, lb, w_in, ln_v_g, ln_v_b, w_s, b_s, gnorm_g, w_out)
    x = ln_affine(DEEPNORM_ALPHA * x + g1 * mix, ln1_g, ln1_b)
    h = ln_plain(x) * (1.0 + sc2) + sh2
    x = ln_affine(DEEPNORM_ALPHA * x + g2 * hier_moe(h, w_rg, b_rg, w_re, b_re, w1, w3, w2), ln2_g, ln2_b)
    return x, s_new, v_rows


def setup_inputs(seed: int = 0) -> dict:
    key = jax.random.key(seed)
    ks = jax.random.split(key, 32)

    def nrm(k, shape, scale):
        return jax.random.normal(k, shape, jnp.float32) * scale

    D = D_MODEL
    return {
        'x_prompt': nrm(ks[0], (BATCH, SEQ, D), 1.0),
        'x_sample': nrm(ks[1], (DEC_BATCH, DEC_SEQ, D), 1.0),
        'state_hgrn': nrm(ks[2], (DEPTH, DEC_BATCH, H_B, DK_B, DV_B), 0.5),
        'c_prompt': nrm(ks[3], (BATCH, D), 1.0),
        'c_sample': nrm(ks[4], (DEC_BATCH, D), 1.0),
        'w_ada': nrm(ks[5], (DEPTH, D, 6 * D), 0.5 * D ** -0.5),
        'b_ada': nrm(ks[6], (DEPTH, 6 * D), 0.02),
        'w_in': nrm(ks[7], (DEPTH, D, PROJ_WIDTH), D ** -0.5),
        'ln_v_g': 1.0 + nrm(ks[8], (DEPTH, H_A, DH_A), 0.02),
        'ln_v_b': nrm(ks[9], (DEPTH, H_A, DH_A), 0.02),
        'w_s': nrm(ks[10], (DEPTH, H_A, GMLP_CHUNK, GMLP_CHUNK), GMLP_CHUNK ** -0.5),
        'b_s': 1.0 + nrm(ks[11], (DEPTH, H_A, GMLP_CHUNK), 0.1),
        'hgrn_lb': nrm(ks[12], (DEPTH + 1, DKT_B), 0.1),
        'gnorm_g': 1.0 + nrm(ks[13], (DEPTH, H_B, DV_B), 0.02),
        'w_out': nrm(ks[14], (DEPTH, MIX_WIDTH, D), MIX_WIDTH ** -0.5 * DEEPNORM_BETA),
        'ln1_g': 1.0 + nrm(ks[15], (DEPTH, D), 0.02),
        'ln1_b': nrm(ks[16], (DEPTH, D), 0.02),
        'w_router_g': nrm(ks[17], (DEPTH, D, N_GROUPS), D ** -0.5),
        'b_router_g': nrm(ks[18], (DEPTH, N_GROUPS), 0.01),
        'w_router_e': nrm(ks[19], (DEPTH, D, N_EXPERTS), D ** -0.5),
        'b_router_e': nrm(ks[20], (DEPTH, N_EXPERTS), 0.01),
        'w1': nrm(ks[21], (DEPTH, N_EXPERTS, D, D_EXPERT), D ** -0.5),
        'w3': nrm(ks[22], (DEPTH, N_EXPERTS, D, D_EXPERT), D ** -0.5),
        'w2': nrm(ks[23], (DEPTH, N_EXPERTS, D_EXPERT, D), D_EXPERT ** -0.5 * DEEPNORM_BETA),
        'ln2_g': 1.0 + nrm(ks[24], (DEPTH, D), 0.02),
        'ln2_b': nrm(ks[25], (DEPTH, D), 0.02),
    }


def reference(x_prompt, x_sample, state_hgrn, c_prompt, c_sample, w_ada, b_ada, w_in, ln_v_g, ln_v_b,
              w_s, b_s, hgrn_lb, gnorm_g, w_out, ln1_g, ln1_b, w_router_g, b_router_g, w_router_e,
              b_router_e, w1, w3, w2, ln2_g, ln2_b):
    lbs = jnp.cumsum(jax.nn.softmax(hgrn_lb.astype(jnp.float32), axis=0), axis=0)
    xp, xs = x_prompt, x_sample
    sp_list, ss_list, vs_list = [], [], []
    for l in range(DEPTH):
        p = (w_ada[l], b_ada[l], w_in[l], ln_v_g[l], ln_v_b[l], w_s[l], b_s[l], gnorm_g[l], w_out[l],
             ln1_g[l], ln1_b[l], w_router_g[l], b_router_g[l], w_router_e[l], b_router_e[l],
             w1[l], w3[l], w2[l], ln2_g[l], ln2_b[l])
        s0p = jnp.zeros((xp.shape[0], H_B, DK_B, DV_B), jnp.float32)
        xp, sp, _ = trunk_layer(xp, c_prompt, s0p, lbs[l], *p)
        xs, ss, vs = trunk_layer(xs, c_sample, state_hgrn[l], lbs[l], *p)
        sp_list.append(sp.astype(state_hgrn.dtype))
        ss_list.append(ss.astype(state_hgrn.dtype))
        vs_list.append(vs)
    state_hgrn_prompt = jnp.stack(sp_list, axis=0)
    state_hgrn_sample = jnp.stack(ss_list, axis=0)
    gmlp_v_sample = jnp.stack(vs_list, axis=0)
    return (xp, xs, state_hgrn_prompt, state_hgrn_sample, gmlp_v_sample)
```

```python
import functools

import jax
import jax.numpy as jnp
from jax import lax
from jax.experimental import pallas as pl
from jax.experimental.pallas import tpu as pltpu

F32 = jnp.float32
BF16 = jnp.bfloat16
I32 = jnp.int32

N_HEADS = 8
HEAD_DIM = 128
GMLP_CHUNK = 128
SUB_CHUNK = 64
N_GROUPS = 4
EXPERTS_PER_GROUP = 8
N_EXPERTS = N_GROUPS * EXPERTS_PER_GROUP
LN_EPS = 1e-5
PROJ_PER_HEAD = 6 * HEAD_DIM

LANES = 128
MIX_TILE = 512
HGRN_BLOCK = 64
DIAG_BLOCK = 16
TOK_TILE = 512
MOD_GROUP = 32
EXP_TILE = 256
ADA_TILE = 1024
VMEM_LIMIT = 56 * 1024 * 1024


def _cparams(sem):
    return pltpu.CompilerParams(dimension_semantics=sem, vmem_limit_bytes=VMEM_LIMIT)


def _ln_plain(x):
    mu = jnp.mean(x, axis=-1, keepdims=True)
    xc = x - mu
    var = jnp.mean(xc * xc, axis=-1, keepdims=True)
    return xc * lax.rsqrt(var + LN_EPS)


def _dot(a, b):
    return jnp.dot(a, b, preferred_element_type=F32)


def _dot_nt(a, b):
    return lax.dot_general(a, b, (((1,), (1,)), ((), ())), preferred_element_type=F32)


def _ada_kernel(c_ref, w_ref, b_ref, o_ref):
    s = jax.nn.silu(c_ref[...]).astype(BF16)
    o_ref[...] = _dot(s, w_ref[...].astype(BF16)) + b_ref[...]


def _ada_call(c_pad, w_ada, b_ada):
    rows, d = c_pad.shape
    n_out = w_ada.shape[1]
    return pl.pallas_call(
        _ada_kernel,
        grid=(n_out // ADA_TILE,),
        in_specs=[
            pl.BlockSpec((rows, d), lambda n: (0, 0)),
            pl.BlockSpec((d, ADA_TILE), lambda n: (0, n)),
            pl.BlockSpec((1, ADA_TILE), lambda n: (0, n)),
        ],
        out_specs=pl.BlockSpec((rows, ADA_TILE), lambda n: (0, n)),
        out_shape=jax.ShapeDtypeStruct((rows, n_out), F32),
        compiler_params=_cparams(("arbitrary",)),
        name="adaln",
    )(c_pad, w_ada, b_ada.reshape(1, n_out))


def _lower_bound(lb_ref, layer):
    raw = lb_ref[0]
    m = jnp.max(raw, axis=0, keepdims=True)
    e = jnp.exp(raw - m)
    p = e / jnp.sum(e, axis=0, keepdims=True)
    return jnp.sum(p[: layer + 1], axis=0, keepdims=True)


def _gmlp_rows(u_pre, v_pre, ln_g, ln_b):
    u = jax.nn.gelu(u_pre)
    v = jax.nn.gelu(v_pre)
    mu = jnp.mean(v, axis=-1, keepdims=True)
    vc = v - mu
    var = jnp.mean(vc * vc, axis=-1, keepdims=True)
    vn = vc * lax.rsqrt(var + LN_EPS) * ln_g + ln_b
    return u, vn


def _row_bcast(a, row, n):
    return jnp.broadcast_to(a[row:row + 1, :], (n, a.shape[1]))


def _block_id(idx, size):
    return lax.shift_right_logical(idx, I32(size.bit_length() - 1))


def _hgrn_masks(c):
    row = lax.broadcasted_iota(I32, (c, c), 0)
    col = lax.broadcasted_iota(I32, (c, c), 1)
    masks = []
    half = c // 2
    while half >= DIAG_BLOCK:
        span = 2 * half
        same = _block_id(row, span) == _block_id(col, span)
        masks.append(same & ((row & (span - 1)) >= half) & ((col & (span - 1)) < half))
        half //= 2
    diag = (_block_id(row, DIAG_BLOCK) == _block_id(col, DIAG_BLOCK)) & (col <= row)
    return masks, diag


def _hgrn_block(q, k, logf, v_bf, st, tri, masks, diag_mask):
    c = q.shape[0]
    hi = logf.astype(BF16)
    lo = (logf - hi.astype(F32)).astype(BF16)
    a = _dot(tri, hi) + _dot(tri, lo)

    scores = None
    half = c // 2
    level = 0
    while half >= DIAG_BLOCK:
        span = 2 * half
        ref = jnp.concatenate([_row_bcast(a, b * span + half, span) for b in range(c // span)], axis=0)
        qs = (q * jnp.exp(jnp.minimum(a - ref, 0.0))).astype(BF16)
        ks = (k * jnp.exp(jnp.minimum(ref - a, 0.0))).astype(BF16)
        part = jnp.where(masks[level], _dot_nt(qs, ks), 0.0)
        scores = part if scores is None else scores + part
        half //= 2
        level += 1
    ref = jnp.concatenate([_row_bcast(a, b * DIAG_BLOCK, DIAG_BLOCK) for b in range(c // DIAG_BLOCK)], axis=0)
    qd = (q * jnp.exp(a - ref)).astype(BF16)
    kd = (k * jnp.exp(jnp.minimum(ref - a, 80.0))).astype(BF16)
    part = jnp.where(diag_mask, _dot_nt(qd, kd), 0.0)
    scores = part if scores is None else scores + part

    qe = (q * jnp.exp(a)).astype(BF16)
    o = _dot(scores.astype(BF16), v_bf) + _dot_nt(qe, st.astype(BF16))
    a_last = a[c - 1:c, :]
    kl = (k * jnp.exp(a_last - a)).astype(BF16)
    v_t = jnp.transpose(v_bf.astype(F32)).astype(BF16)
    st_new = st * jnp.exp(a_last) + _dot(v_t, kl)
    return o, st_new


def _hgrn_rows(proj, lbv, gn, st, tri, masks, diag_mask):
    q = jax.nn.silu(proj[:, 0:HEAD_DIM])
    fg = lbv + (1.0 - lbv) * jax.nn.sigmoid(proj[:, HEAD_DIM:2 * HEAD_DIM])
    logf = jnp.log(fg)
    k = 1.0 - fg
    v_bf = proj[:, 2 * HEAD_DIM:3 * HEAD_DIM].astype(BF16)
    g = proj[:, 3 * HEAD_DIM:4 * HEAD_DIM]
    o, st_new = _hgrn_block(q, k, logf, v_bf, st, tri, masks, diag_mask)
    o = o * lax.rsqrt(jnp.mean(o * o, axis=-1, keepdims=True) + LN_EPS) * gn
    return o * jax.nn.silu(g), st_new


def _tri_ones(c):
    row = lax.broadcasted_iota(I32, (c, c), 0)
    col = lax.broadcasted_iota(I32, (c, c), 1)
    return jnp.where(col <= row, 1.0, 0.0).astype(BF16)


def _mixer_prompt_kernel(x_ref, sh_ref, sc_ref, win_ref, lvg_ref, lvb_ref, ws_ref, bs_ref, lb_ref, gn_ref,
                         cat_ref, state_ref, h_scr, proj_scr, st_scr, *, layer):
    i = pl.program_id(1)
    j = pl.program_id(2)
    tm = h_scr.shape[0]

    @pl.when(j == 0)
    def _():
        def slab(r, carry):
            rows = pl.ds(pl.multiple_of(r * HGRN_BLOCK, HGRN_BLOCK), HGRN_BLOCK)
            h = _ln_plain(x_ref[0, rows, :]) * (1.0 + sc_ref[0]) + sh_ref[0]
            h_scr[rows, :] = h.astype(BF16)
            return carry
        lax.fori_loop(0, tm // HGRN_BLOCK, slab, 0)

    @pl.when((i == 0) & (j == 0))
    def _():
        st_scr[...] = jnp.zeros_like(st_scr)

    proj_scr[...] = _dot(h_scr[...], win_ref[0])

    row = lax.broadcasted_iota(I32, (GMLP_CHUNK, GMLP_CHUNK), 0)
    col = lax.broadcasted_iota(I32, (GMLP_CHUNK, GMLP_CHUNK), 1)
    wm = jnp.where(_block_id(row, SUB_CHUNK) >= _block_id(col, SUB_CHUNK), ws_ref[0], 0.0).astype(BF16)
    b_col = bs_ref[0]
    for c in range(tm // GMLP_CHUNK):
        r0 = c * GMLP_CHUNK
        u, vn = _gmlp_rows(proj_scr[r0:r0 + GMLP_CHUNK, 0:HEAD_DIM],
                           proj_scr[r0:r0 + GMLP_CHUNK, HEAD_DIM:2 * HEAD_DIM], lvg_ref[0], lvb_ref[0])
        sp = _dot(wm, vn.astype(BF16)) + b_col
        cat_ref[r0:r0 + GMLP_CHUNK, 0:HEAD_DIM] = (u * sp).astype(BF16)

    lbv = _lower_bound(lb_ref, layer)
    gn = gn_ref[0]
    tri = _tri_ones(HGRN_BLOCK)
    masks, diag_mask = _hgrn_masks(HGRN_BLOCK)
    st = st_scr[j]
    for c in range(tm // HGRN_BLOCK):
        r0 = c * HGRN_BLOCK
        out, st = _hgrn_rows(proj_scr[r0:r0 + HGRN_BLOCK, 2 * HEAD_DIM:PROJ_PER_HEAD], lbv, gn, st,
                             tri, masks, diag_mask)
        cat_ref[r0:r0 + HGRN_BLOCK, HEAD_DIM:2 * HEAD_DIM] = out.astype(BF16)
    st_scr[j] = st
    state_ref[0, j] = jnp.transpose(st)


def _mixer_prompt_call(x, sh1, sc1, w_in_h, ln_v_g, ln_v_b, w_s, b_s, lb_h, gnorm_g, layer):
    bsz, t, d = x.shape
    n_t = t // MIX_TILE
    kern = functools.partial(_mixer_prompt_kernel, layer=layer)
    head_vec = lambda b, i, j: (j, 0, 0)
    return pl.pallas_call(
        kern,
        grid=(bsz, n_t, N_HEADS),
        in_specs=[
            pl.BlockSpec((1, MIX_TILE, d), lambda b, i, j: (b, i, 0)),
            pl.BlockSpec((1, 1, d), lambda b, i, j: (b, 0, 0)),
            pl.BlockSpec((1, 1, d), lambda b, i, j: (b, 0, 0)),
            pl.BlockSpec((1, d, PROJ_PER_HEAD), head_vec),
            pl.BlockSpec((1, 1, HEAD_DIM), head_vec),
            pl.BlockSpec((1, 1, HEAD_DIM), head_vec),
            pl.BlockSpec((1, GMLP_CHUNK, GMLP_CHUNK), head_vec),
            pl.BlockSpec((1, GMLP_CHUNK, 1), head_vec),
            pl.BlockSpec((1, lb_h.shape[1], HEAD_DIM), head_vec),
            pl.BlockSpec((1, 1, HEAD_DIM), head_vec),
        ],
        out_specs=[
            pl.BlockSpec((MIX_TILE, 2 * HEAD_DIM), lambda b, i, j: (b * n_t + i, j)),
            pl.BlockSpec((1, N_HEADS, HEAD_DIM, HEAD_DIM), lambda b, i, j: (b, 0, 0, 0)),
        ],
        out_shape=[
            jax.ShapeDtypeStruct((bsz * t, 2 * N_HEADS * HEAD_DIM), BF16),
            jax.ShapeDtypeStruct((bsz, N_HEADS, HEAD_DIM, HEAD_DIM), F32),
        ],
        scratch_shapes=[
            pltpu.VMEM((MIX_TILE, d), BF16),
            pltpu.VMEM((MIX_TILE, PROJ_PER_HEAD), F32),
            pltpu.VMEM((N_HEADS, HEAD_DIM, HEAD_DIM), F32),
        ],
        compiler_params=_cparams(("arbitrary", "arbitrary", "arbitrary")),
        name="mixer_prompt",
    )(x, sh1.reshape(bsz, 1, d), sc1.reshape(bsz, 1, d), w_in_h, ln_v_g, ln_v_b, w_s, b_s, lb_h, gnorm_g)


def _mixer_sample_kernel(x_ref, sh_ref, sc_ref, win_ref, lvg_ref, lvb_ref, ws_ref, bs_ref, lb_ref, gn_ref, s0_ref,
                         cat_ref, state_ref, vrows_ref, h_scr, proj_scr, *, layer, seq):
    j = pl.program_id(0)
    n_streams = x_ref.shape[0] // seq

    @pl.when(j == 0)
    def _():
        def slab(r, carry):
            rows = pl.ds(pl.multiple_of(r * seq, seq), seq)
            h = _ln_plain(x_ref[rows, :]) * (1.0 + sc_ref[pl.ds(r, 1), :]) + sh_ref[pl.ds(r, 1), :]
            h_scr[rows, :] = h.astype(BF16)
            return carry
        lax.fori_loop(0, n_streams, slab, 0)

    proj_scr[...] = _dot(h_scr[...], win_ref[0])

    wm = ws_ref[0][0:seq, 0:seq].astype(BF16)
    b_col = bs_ref[0][0:seq, :]
    lbv = _lower_bound(lb_ref, layer)
    gn = gn_ref[0]
    tri = _tri_ones(seq)
    masks, diag_mask = _hgrn_masks(seq)
    for n in range(n_streams):
        r0 = n * seq
        u, vn = _gmlp_rows(proj_scr[r0:r0 + seq, 0:HEAD_DIM], proj_scr[r0:r0 + seq, HEAD_DIM:2 * HEAD_DIM],
                           lvg_ref[0], lvb_ref[0])
        vrows_ref[r0:r0 + seq, :] = vn
        sp = _dot(wm, vn.astype(BF16)) + b_col
        cat_ref[r0:r0 + seq, 0:HEAD_DIM] = (u * sp).astype(BF16)

        st0 = jnp.transpose(s0_ref[n, 0])
        out, st = _hgrn_rows(proj_scr[r0:r0 + seq, 2 * HEAD_DIM:PROJ_PER_HEAD], lbv, gn, st0, tri, masks, diag_mask)
        cat_ref[r0:r0 + seq, HEAD_DIM:2 * HEAD_DIM] = out.astype(BF16)
        state_ref[n, 0] = jnp.transpose(st)


def _mixer_sample_call(x2d, sh1, sc1, w_in_h, ln_v_g, ln_v_b, w_s, b_s, lb_h, gnorm_g, s0, layer, seq):
    rows, d = x2d.shape
    n_streams = rows // seq
    kern = functools.partial(_mixer_sample_kernel, layer=layer, seq=seq)
    head_vec = lambda j: (j, 0, 0)
    return pl.pallas_call(
        kern,
        grid=(N_HEADS,),
        in_specs=[
            pl.BlockSpec((rows, d), lambda j: (0, 0)),
            pl.BlockSpec((n_streams, d), lambda j: (0, 0)),
            pl.BlockSpec((n_streams, d), lambda j: (0, 0)),
            pl.BlockSpec((1, d, PROJ_PER_HEAD), head_vec),
            pl.BlockSpec((1, 1, HEAD_DIM), head_vec),
            pl.BlockSpec((1, 1, HEAD_DIM), head_vec),
            pl.BlockSpec((1, GMLP_CHUNK, GMLP_CHUNK), head_vec),
            pl.BlockSpec((1, GMLP_CHUNK, 1), head_vec),
            pl.BlockSpec((1, lb_h.shape[1], HEAD_DIM), head_vec),
            pl.BlockSpec((1, 1, HEAD_DIM), head_vec),
            pl.BlockSpec((n_streams, 1, HEAD_DIM, HEAD_DIM), lambda j: (0, j, 0, 0)),
        ],
        out_specs=[
            pl.BlockSpec((rows, 2 * HEAD_DIM), lambda j: (0, j)),
            pl.BlockSpec((n_streams, 1, HEAD_DIM, HEAD_DIM), lambda j: (0, j, 0, 0)),
            pl.BlockSpec((rows, HEAD_DIM), lambda j: (0, j)),
        ],
        out_shape=[
            jax.ShapeDtypeStruct((rows, 2 * N_HEADS * HEAD_DIM), BF16),
            jax.ShapeDtypeStruct((n_streams, N_HEADS, HEAD_DIM, HEAD_DIM), F32),
            jax.ShapeDtypeStruct((rows, N_HEADS * HEAD_DIM), F32),
        ],
        scratch_shapes=[
            pltpu.VMEM((rows, d), BF16),
            pltpu.VMEM((rows, PROJ_PER_HEAD), F32),
        ],
        compiler_params=_cparams(("arbitrary",)),
        name="mixer_sample",
    )(x2d, sh1, sc1, w_in_h, ln_v_g, ln_v_b, w_s, b_s, lb_h, gnorm_g, s0)


def _post_kernel(catp_ref, cats_ref, xp_ref, xs_ref, g1_ref, sh2_ref, sc2_ref, wout_ref, l1g_ref, l1b_ref,
                 wr_ref, br_ref, tri_ref,
                 x1_ref, h2_ref, meta_ref, wts_ref, cnt_ref,
                 mix_scr, hhi_scr, hlo_scr, run_scr, *, n_prompt_tiles, alpha):
    i = pl.program_id(0)
    is_s = i >= n_prompt_tiles
    tm = mix_scr.shape[0]

    @pl.when(is_s)
    def _():
        mix_scr[...] = _dot(cats_ref[...], wout_ref[...])

    @pl.when(jnp.logical_not(is_s))
    def _():
        mix_scr[...] = _dot(catp_ref[...], wout_ref[...])

    @pl.when(i == 0)
    def _():
        run_scr[...] = jnp.zeros_like(run_scr)

    def slab(g, carry):
        rows = pl.ds(pl.multiple_of(g * MOD_GROUP, MOD_GROUP), MOD_GROUP)
        x = jnp.where(is_s, xs_ref[rows, :], xp_ref[rows, :])
        z = alpha * x + g1_ref[pl.ds(g, 1), :] * mix_scr[rows, :]
        x1 = _ln_plain(z) * l1g_ref[...] + l1b_ref[...]
        x1_ref[rows, :] = x1
        h2 = _ln_plain(x1) * (1.0 + sc2_ref[pl.ds(g, 1), :]) + sh2_ref[pl.ds(g, 1), :]
        h2_ref[rows, :] = h2
        hi = h2.astype(BF16)
        hhi_scr[rows, :] = hi
        hlo_scr[rows, :] = (h2 - hi.astype(F32)).astype(BF16)
        return carry
    lax.fori_loop(0, tm // MOD_GROUP, slab, 0)

    n_r = N_GROUPS + N_EXPERTS
    s = _dot(hhi_scr[...], wr_ref[...]) + _dot(hlo_scr[...], wr_ref[...])
    logits = s + pltpu.roll(s, LANES - n_r, axis=1) + br_ref[...]

    lane = lax.broadcasted_iota(I32, (tm, LANES), 1)
    lane_f = lane.astype(F32)
    neg = jnp.float32(-jnp.inf)
    big = jnp.float32(LANES)

    def first_lane_of(vals, top):
        return jnp.min(jnp.where(vals == top, lane_f, big), axis=1, keepdims=True)

    gmask = lane < N_GROUPS
    gl = jnp.where(gmask, logits, neg)
    gmax = jnp.max(gl, axis=1, keepdims=True)
    gsel = first_lane_of(gl, gmax)
    p_group = 1.0 / jnp.sum(jnp.exp(gl - gmax), axis=1, keepdims=True)

    e_lo = N_GROUPS + EXPERTS_PER_GROUP * gsel.astype(I32)
    emask = (lane >= e_lo) & (lane < e_lo + EXPERTS_PER_GROUP)
    el = jnp.where(emask, logits, neg)
    t1 = jnp.max(el, axis=1, keepdims=True)
    i1 = first_lane_of(el, t1)
    el2 = jnp.where(lane_f == i1, neg, el)
    t2 = jnp.max(el2, axis=1, keepdims=True)
    i2 = first_lane_of(el2, t2)
    e2 = jnp.exp(t2 - t1)
    den = 1.0 + e2
    w1 = (1.0 / den) * p_group
    w2 = (e2 / den) * p_group

    sel1 = lane_f == i1
    sel2 = lane_f == i2
    onehot = jnp.where(sel1 | sel2, 1.0, 0.0)
    before = _dot(tri_ref[...], onehot.astype(BF16)) + run_scr[...]
    r1 = jnp.sum(jnp.where(sel1, before, 0.0), axis=1, keepdims=True)
    r2 = jnp.sum(jnp.where(sel2, before, 0.0), axis=1, keepdims=True)
    run_scr[...] = run_scr[...] + jnp.sum(onehot, axis=0, keepdims=True)
    cnt_ref[...] = run_scr[...].astype(I32)

    base = jnp.float32(N_GROUPS)
    meta = jnp.where(lane == 0, i1 - base,
                     jnp.where(lane == 1, i2 - base,
                               jnp.where(lane == 2, r1, jnp.where(lane == 3, r2, 0.0))))
    meta_ref[...] = meta.astype(I32)
    wts_ref[...] = jnp.where(lane == 0, w1, jnp.where(lane == 1, w2, 0.0))


def _post_call(cat_p, cat_s, x_p, x_s, g1g, sh2g, sc2g, w_out_h, ln1_g, ln1_b, wr, br, tri, alpha):
    n_p, d = x_p.shape
    n_s = x_s.shape[0]
    n_pt = n_p // TOK_TILE
    n_tiles = n_pt + n_s // TOK_TILE
    n_tok = n_p + n_s
    groups = TOK_TILE // MOD_GROUP
    kern = functools.partial(_post_kernel, n_prompt_tiles=n_pt, alpha=alpha)
    p_idx = lambda i: (jnp.minimum(i, n_pt - 1), 0)
    s_idx = lambda i: (jnp.maximum(i - n_pt, 0), 0)
    tile = lambda i: (i, 0)
    const = lambda i: (0, 0)
    once = pl.Buffered(1)
    return pl.pallas_call(
        kern,
        grid=(n_tiles,),
        in_specs=[
            pl.BlockSpec((TOK_TILE, d), p_idx),
            pl.BlockSpec((TOK_TILE, d), s_idx, pipeline_mode=once),
            pl.BlockSpec((TOK_TILE, d), p_idx),
            pl.BlockSpec((TOK_TILE, d), s_idx, pipeline_mode=once),
            pl.BlockSpec((groups, d), tile),
            pl.BlockSpec((groups, d), tile),
            pl.BlockSpec((groups, d), tile),
            pl.BlockSpec((d, d), const, pipeline_mode=once),
            pl.BlockSpec((1, d), const),
            pl.BlockSpec((1, d), const),
            pl.BlockSpec((d, LANES), const),
            pl.BlockSpec((1, LANES), const),
            pl.BlockSpec((TOK_TILE, TOK_TILE), const),
        ],
        out_specs=[
            pl.BlockSpec((TOK_TILE, d), tile),
            pl.BlockSpec((TOK_TILE, d), tile),
            pl.BlockSpec((TOK_TILE, LANES), tile),
            pl.BlockSpec((TOK_TILE, LANES), tile),
            pl.BlockSpec((1, LANES), const),
        ],
        out_shape=[
            jax.ShapeDtypeStruct((n_tok, d), F32),
            jax.ShapeDtypeStruct((n_tok, d), F32),
            jax.ShapeDtypeStruct((n_tok, LANES), I32),
            jax.ShapeDtypeStruct((n_tok, LANES), F32),
            jax.ShapeDtypeStruct((1, LANES), I32),
        ],
        scratch_shapes=[
            pltpu.VMEM((TOK_TILE, d), F32),
            pltpu.VMEM((TOK_TILE, d), BF16),
            pltpu.VMEM((TOK_TILE, d), BF16),
            pltpu.VMEM((1, LANES), F32),
        ],
        compiler_params=_cparams(("arbitrary",)),
        name="post_router",
    )(cat_p, cat_s, x_p, x_s, g1g, sh2g, sc2g, w_out_h, ln1_g, ln1_b, wr, br, tri)


def _dispatch_kernel(slot_ref, pad_ref, h_ref, xs_ref, zero_scr, sem, *, n_unused):
    i = pl.program_id(0)
    tm = h_ref.shape[0]
    pad_per_step = pad_ref.shape[2]
    n_pad = jnp.clip(n_unused - i * pad_per_step, 0, pad_per_step)
    zero_scr[...] = jnp.zeros_like(zero_scr)

    def row_copy(r, k):
        return pltpu.make_async_copy(h_ref.at[pl.ds(r, 1)], xs_ref.at[pl.ds(slot_ref[0, 0, 2 * r + k], 1)], sem)

    def pad_copy(r):
        return pltpu.make_async_copy(zero_scr.at[pl.ds(0, 1)], xs_ref.at[pl.ds(pad_ref[0, 0, r], 1)], sem)

    def start_rows(r, carry):
        row_copy(r, 0).start()
        row_copy(r, 1).start()
        return carry
    lax.fori_loop(0, tm, start_rows, 0)

    def start_pad(r, carry):
        pad_copy(r).start()
        return carry
    lax.fori_loop(0, n_pad, start_pad, 0)

    def wait_rows(r, carry):
        row_copy(r, 0).wait()
        row_copy(r, 1).wait()
        return carry
    lax.fori_loop(0, tm, wait_rows, 0)

    def wait_pad(r, carry):
        pad_copy(r).wait()
        return carry
    lax.fori_loop(0, n_pad, wait_pad, 0)


def _dispatch_call(slots3, pads3, h2, n_sorted, n_unused):
    n_tok, d = h2.shape
    n_tiles = n_tok // TOK_TILE
    kern = functools.partial(_dispatch_kernel, n_unused=n_unused)
    return pl.pallas_call(
        kern,
        grid=(n_tiles,),
        in_specs=[
            pl.BlockSpec((1, 1, slots3.shape[2]), lambda i: (i, 0, 0), memory_space=pltpu.SMEM),
            pl.BlockSpec((1, 1, pads3.shape[2]), lambda i: (i, 0, 0), memory_space=pltpu.SMEM),
            pl.BlockSpec((TOK_TILE, d), lambda i: (i, 0)),
        ],
        out_specs=pl.BlockSpec(memory_space=pl.ANY),
        out_shape=jax.ShapeDtypeStruct((n_sorted, d), F32),
        scratch_shapes=[pltpu.VMEM((8, d), F32), pltpu.SemaphoreType.DMA(())],
        compiler_params=_cparams(("arbitrary",)),
        name="dispatch",
    )(slots3, pads3, h2)


def _expert_kernel(te_ref, nt_ref, xs_ref, w1_ref, w3_ref, w2_ref, ys_ref, w13_scr, w2_scr):
    t = pl.program_id(0)
    d_exp = w1_ref.shape[2]
    prev = te_ref[jnp.maximum(t - 1, 0)]
    changed = (t == 0) | (te_ref[t] != prev)
    valid = t < nt_ref[0]

    @pl.when(changed)
    def _():
        w13_scr[:, 0:d_exp] = w1_ref[0].astype(BF16)
        w13_scr[:, d_exp:2 * d_exp] = w3_ref[0].astype(BF16)
        w2_scr[...] = w2_ref[0].astype(BF16)

    @pl.when(valid)
    def _():
        h13 = _dot(xs_ref[...].astype(BF16), w13_scr[...])
        hm = jax.nn.silu(h13[:, 0:d_exp]) * h13[:, d_exp:2 * d_exp]
        ys_ref[...] = _dot(hm.astype(BF16), w2_scr[...])

    @pl.when(jnp.logical_not(valid))
    def _():
        ys_ref[...] = jnp.zeros_like(ys_ref)


def _expert_call(tile_expert, n_valid, xs, w1, w3, w2):
    n_sorted, d = xs.shape
    d_exp = w1.shape[2]
    n_tiles = n_sorted // EXP_TILE
    grid_spec = pltpu.PrefetchScalarGridSpec(
        num_scalar_prefetch=2,
        grid=(n_tiles,),
        in_specs=[
            pl.BlockSpec((EXP_TILE, d), lambda t, te, nt: (t, 0)),
            pl.BlockSpec((1, d, d_exp), lambda t, te, nt: (te[t], 0, 0)),
            pl.BlockSpec((1, d, d_exp), lambda t, te, nt: (te[t], 0, 0)),
            pl.BlockSpec((1, d_exp, d), lambda t, te, nt: (te[t], 0, 0)),
        ],
        out_specs=pl.BlockSpec((EXP_TILE, d), lambda t, te, nt: (t, 0)),
        scratch_shapes=[pltpu.VMEM((d, 2 * d_exp), BF16), pltpu.VMEM((d_exp, d), BF16)],
    )
    return pl.pallas_call(
        _expert_kernel,
        grid_spec=grid_spec,
        out_shape=jax.ShapeDtypeStruct((n_sorted, d), F32),
        compiler_params=_cparams(("arbitrary",)),
        name="experts",
    )(tile_expert, n_valid, xs, w1, w3, w2)


def _combine_kernel(slot_ref, x1_ref, wts_ref, g2_ref, l2g_ref, l2b_ref, ys_ref, outp_ref, outs_ref,
                    y0_scr, y1_scr, sem, *, n_prompt_tiles, alpha):
    i = pl.program_id(0)
    is_s = i >= n_prompt_tiles
    tm = x1_ref.shape[0]

    def row_copy(r, k):
        dst = y0_scr if k == 0 else y1_scr
        return pltpu.make_async_copy(ys_ref.at[pl.ds(slot_ref[0, 0, 2 * r + k], 1)], dst.at[pl.ds(r, 1)], sem)

    def start_rows(r, carry):
        row_copy(r, 0).start()
        row_copy(r, 1).start()
        return carry
    lax.fori_loop(0, tm, start_rows, 0)

    def wait_rows(r, carry):
        row_copy(r, 0).wait()
        row_copy(r, 1).wait()
        return carry
    lax.fori_loop(0, tm, wait_rows, 0)

    def slab(g, carry):
        rows = pl.ds(pl.multiple_of(g * MOD_GROUP, MOD_GROUP), MOD_GROUP)
        w = wts_ref[rows, :]
        moe = w[:, 0:1] * y0_scr[rows, :] + w[:, 1:2] * y1_scr[rows, :]
        z = alpha * x1_ref[rows, :] + g2_ref[pl.ds(g, 1), :] * moe
        y = _ln_plain(z) * l2g_ref[...] + l2b_ref[...]

        @pl.when(is_s)
        def _():
            outs_ref[rows, :] = y

        @pl.when(jnp.logical_not(is_s))
        def _():
            outp_ref[rows, :] = y
        return carry
    lax.fori_loop(0, tm // MOD_GROUP, slab, 0)


def _combine_call(slots3, x1, wts, g2g, ln2_g, ln2_b, ys, n_p, alpha):
    n_tok, d = x1.shape
    n_tiles = n_tok // TOK_TILE
    n_pt = n_p // TOK_TILE
    groups = TOK_TILE // MOD_GROUP
    kern = functools.partial(_combine_kernel, n_prompt_tiles=n_pt, alpha=alpha)
    return pl.pallas_call(
        kern,
        grid=(n_tiles,),
        in_specs=[
            pl.BlockSpec((1, 1, slots3.shape[2]), lambda i: (i, 0, 0), memory_space=pltpu.SMEM),
            pl.BlockSpec((TOK_TILE, d), lambda i: (i, 0)),
            pl.BlockSpec((TOK_TILE, LANES), lambda i: (i, 0)),
            pl.BlockSpec((groups, d), lambda i: (i, 0)),
            pl.BlockSpec((1, d), lambda i: (0, 0)),
            pl.BlockSpec((1, d), lambda i: (0, 0)),
            pl.BlockSpec(memory_space=pl.ANY),
        ],
        out_specs=[
            pl.BlockSpec((TOK_TILE, d), lambda i: (jnp.minimum(i, n_pt - 1), 0)),
            pl.BlockSpec((TOK_TILE, d), lambda i: (jnp.maximum(i - n_pt, 0), 0)),
        ],
        out_shape=[
            jax.ShapeDtypeStruct((n_p, d), F32),
            jax.ShapeDtypeStruct((n_tok - n_p, d), F32),
        ],
        scratch_shapes=[
            pltpu.VMEM((TOK_TILE, d), F32),
            pltpu.VMEM((TOK_TILE, d), F32),
            pltpu.SemaphoreType.DMA(()),
        ],
        compiler_params=_cparams(("arbitrary",)),
        name="combine",
    )(slots3, x1, wts, g2g, ln2_g, ln2_b, ys)


def _routing_tables(meta, cnt_row, n_sorted):
    n_tok = meta.shape[0]
    experts = meta[:, 0:2]
    ranks = meta[:, 2:4]
    cnt = cnt_row[0, N_GROUPS:N_GROUPS + N_EXPERTS]
    padded = ((cnt + EXP_TILE - 1) // EXP_TILE) * EXP_TILE
    ends = jnp.cumsum(padded)
    offs = ends - padded
    slots = offs[experts] + ranks
    total = ends[-1]
    n_tiles = n_sorted // EXP_TILE
    tile_expert = jnp.minimum(
        jnp.searchsorted(ends // EXP_TILE, jnp.arange(n_tiles, dtype=I32), side="right"), N_EXPERTS - 1).astype(I32)
    n_valid = (total // EXP_TILE).reshape(1).astype(I32)
    n_unused = n_sorted - 2 * n_tok
    starts = jnp.concatenate([offs + cnt, total.reshape(1)])
    lens = jnp.concatenate([padded - cnt, (n_sorted - total).reshape(1)])
    lens_end = jnp.cumsum(lens)
    k = jnp.arange(n_unused, dtype=I32)
    seg = jnp.searchsorted(lens_end, k, side="right")
    pads = (starts[seg] + (k - (lens_end[seg] - lens[seg]))).astype(I32)
    return slots.astype(I32), tile_expert, n_valid, pads


def _layer(layer, n_layers, xp, xs, s0_l, c_all, p):
    (w_ada, b_ada, w_in, ln_v_g, ln_v_b, w_s, b_s, hgrn_lb, gnorm_g, w_out, ln1_g, ln1_b,
     w_rg, b_rg, w_re, b_re, w1, w3, w2, ln2_g, ln2_b) = p
    bsz, t, d = xp.shape
    n_streams, seq, _ = xs.shape
    alpha = float((2.0 * n_layers) ** 0.25)
    n_p = bsz * t
    n_s = n_streams * seq
    n_tok = n_p + n_s

    n_c = c_all.shape[0]
    c_pad = jnp.pad(c_all, ((0, (-n_c) % 8), (0, 0)))
    mod = _ada_call(c_pad, w_ada, b_ada)[:n_c]
    sh1, sc1, g1, sh2, sc2, g2 = [mod[:, m * d:(m + 1) * d] for m in range(6)]

    w_in_h = jnp.transpose(w_in.reshape(d, 6, N_HEADS, HEAD_DIM), (2, 0, 1, 3)).reshape(
        N_HEADS, d, PROJ_PER_HEAD).astype(BF16)
    w_out_h = jnp.transpose(w_out.reshape(2, N_HEADS, HEAD_DIM, d), (1, 0, 2, 3)).reshape(
        2 * N_HEADS * HEAD_DIM, d).astype(BF16)
    lvg = ln_v_g.reshape(N_HEADS, 1, HEAD_DIM)
    lvb = ln_v_b.reshape(N_HEADS, 1, HEAD_DIM)
    b_s3 = b_s.reshape(N_HEADS, GMLP_CHUNK, 1)
    lb_h = jnp.transpose(hgrn_lb.reshape(hgrn_lb.shape[0], N_HEADS, HEAD_DIM), (1, 0, 2))
    gn = gnorm_g.reshape(N_HEADS, 1, HEAD_DIM)

    cat_p, state_p = _mixer_prompt_call(xp, sh1[:bsz], sc1[:bsz], w_in_h, lvg, lvb, w_s, b_s3, lb_h, gn, layer)
    cat_s, state_s, vrows = _mixer_sample_call(xs.reshape(n_s, d), sh1[bsz:], sc1[bsz:], w_in_h, lvg, lvb, w_s,
                                               b_s3, lb_h, gn, s0_l, layer, seq)

    group_stream = jnp.concatenate([
        jnp.repeat(jnp.arange(bsz, dtype=I32), t // MOD_GROUP),
        bsz + jnp.repeat(jnp.arange(n_streams, dtype=I32), seq // MOD_GROUP)])
    g1g, sh2g, sc2g, g2g = [m[group_stream] for m in (g1, sh2, sc2, g2)]

    wr = jnp.concatenate([w_rg, w_re], axis=1)
    wr_hi = wr.astype(BF16)
    wr_lo = (wr - wr_hi.astype(F32)).astype(BF16)
    n_r = wr.shape[1]
    wr_cat = jnp.concatenate([wr_hi, wr_lo, jnp.zeros((d, LANES - 2 * n_r), BF16)], axis=1)
    br = jnp.pad(jnp.concatenate([b_rg, b_re]), (0, LANES - n_r)).reshape(1, LANES)
    tri = jnp.tril(jnp.ones((TOK_TILE, TOK_TILE), F32), -1).astype(BF16)

    x1, h2, meta, wts, cnt_row = _post_call(cat_p, cat_s, xp.reshape(n_p, d), xs.reshape(n_s, d), g1g, sh2g, sc2g,
                                            w_out_h, ln1_g.reshape(1, d), ln1_b.reshape(1, d), wr_cat, br, tri, alpha)

    n_sorted = 2 * n_tok + N_EXPERTS * EXP_TILE
    slots, tile_expert, n_valid, pads = _routing_tables(meta, cnt_row, n_sorted)
    n_tiles = n_tok // TOK_TILE
    slots3 = slots.reshape(n_tiles, 1, 2 * TOK_TILE)
    n_unused = n_sorted - 2 * n_tok
    pad_per_step = -(-n_unused // n_tiles)
    pads3 = jnp.pad(pads, (0, n_tiles * pad_per_step - n_unused)).reshape(n_tiles, 1, pad_per_step)

    xs_sorted = _dispatch_call(slots3, pads3, h2, n_sorted, n_unused)
    ys_sorted = _expert_call(tile_expert, n_valid, xs_sorted, w1, w3, w2)
    yp, ys_out = _combine_call(slots3, x1, wts, g2g, ln2_g.reshape(1, d), ln2_b.reshape(1, d), ys_sorted, n_p, alpha)

    v_rows = vrows.reshape(n_streams, seq, N_HEADS, HEAD_DIM)
    return yp.reshape(bsz, t, d), ys_out.reshape(n_streams, seq, d), state_p, state_s, v_rows


def kernel(x_prompt, x_sample, state_hgrn, c_prompt, c_sample, w_ada, b_ada, w_in, ln_v_g, ln_v_b, w_s, b_s, hgrn_lb, gnorm_g, w_out, ln1_g, ln1_b, w_router_g, b_router_g, w_router_e, b_router_e, w1, w3, w2, ln2_g, ln2_b):
    n_layers = w_ada.shape[0]
    assert x_prompt.shape[1] % MIX_TILE == 0 and x_prompt.shape[2] == 2 * N_HEADS * HEAD_DIM
    assert x_sample.shape[1] % MOD_GROUP == 0 and x_sample.shape[1] <= SUB_CHUNK
    assert (x_sample.shape[0] * x_sample.shape[1]) % TOK_TILE == 0
    c_all = jnp.concatenate([c_prompt, c_sample], axis=0)
    xp, xs = x_prompt, x_sample
    sp_list, ss_list, vs_list = [], [], []
    for l in range(n_layers):
        p = (w_ada[l], b_ada[l], w_in[l], ln_v_g[l], ln_v_b[l], w_s[l], b_s[l], hgrn_lb, gnorm_g[l], w_out[l],
             ln1_g[l], ln1_b[l], w_router_g[l], b_router_g[l], w_router_e[l], b_router_e[l],
             w1[l], w3[l], w2[l], ln2_g[l], ln2_b[l])
        xp, xs, sp, ss, vs = _layer(l, n_layers, xp, xs, state_hgrn[l], c_all, p)
        sp_list.append(sp.astype(state_hgrn.dtype))
        ss_list.append(ss.astype(state_hgrn.dtype))
        vs_list.append(vs)
    return (xp, xs, jnp.stack(sp_list, axis=0), jnp.stack(ss_list, axis=0), jnp.stack(vs_list, axis=0))
```

```python
import functools

import jax
import jax.numpy as jnp
from jax import lax
from jax.experimental import pallas as pl
from jax.experimental.pallas import tpu as pltpu

F32 = jnp.float32
BF16 = jnp.bfloat16
I32 = jnp.int32

N_HEADS = 8
HEAD_DIM = 128
GMLP_CHUNK = 128
SUB_CHUNK = 64
N_GROUPS = 4
EXPERTS_PER_GROUP = 8
N_EXPERTS = N_GROUPS * EXPERTS_PER_GROUP
LN_EPS = 1e-5
N_KINDS = 6

LANES = 128
MIX_TILE = 512
HEADS_PER_STEP = 2
HGRN_BLOCK = 64
DIAG_BLOCK = 16
TOK_TILE = 512
MOD_GROUP = 32
EXP_TILE = 256
ROW_DMA_UNROLL = 8
ADA_TILE = 1024
VMEM_LIMIT = 56 * 1024 * 1024


def _cparams(sem):
    return pltpu.CompilerParams(dimension_semantics=sem, vmem_limit_bytes=VMEM_LIMIT)


def _ln_plain(x):
    mu = jnp.mean(x, axis=-1, keepdims=True)
    xc = x - mu
    var = jnp.mean(xc * xc, axis=-1, keepdims=True)
    return xc * lax.rsqrt(var + LN_EPS)


def _dot(a, b):
    return jnp.dot(a, b, preferred_element_type=F32)


def _dot_nt(a, b):
    return lax.dot_general(a, b, (((1,), (1,)), ((), ())), preferred_element_type=F32)


def _ada_kernel(c_ref, w_ref, b_ref, o_ref):
    s = jax.nn.silu(c_ref[...]).astype(BF16)
    o_ref[...] = _dot(s, w_ref[...].astype(BF16)) + b_ref[...]


def _ada_call(c_pad, w_ada, b_ada):
    rows, d = c_pad.shape
    n_out = w_ada.shape[1]
    return pl.pallas_call(
        _ada_kernel,
        grid=(n_out // ADA_TILE,),
        in_specs=[
            pl.BlockSpec((rows, d), lambda n: (0, 0)),
            pl.BlockSpec((d, ADA_TILE), lambda n: (0, n)),
            pl.BlockSpec((1, ADA_TILE), lambda n: (0, n)),
        ],
        out_specs=pl.BlockSpec((rows, ADA_TILE), lambda n: (0, n)),
        out_shape=jax.ShapeDtypeStruct((rows, n_out), F32),
        compiler_params=_cparams(("arbitrary",)),
        name="adaln",
    )(c_pad, w_ada, b_ada.reshape(1, n_out))


def _lower_bound(lb_ref, hd, layer):
    raw = lb_ref[hd]
    m = jnp.max(raw, axis=0, keepdims=True)
    e = jnp.exp(raw - m)
    p = e / jnp.sum(e, axis=0, keepdims=True)
    return jnp.sum(p[: layer + 1], axis=0, keepdims=True)


def _gmlp_rows(u_pre, v_pre, ln_g, ln_b):
    u = jax.nn.gelu(u_pre)
    v = jax.nn.gelu(v_pre)
    mu = jnp.mean(v, axis=-1, keepdims=True)
    vc = v - mu
    var = jnp.mean(vc * vc, axis=-1, keepdims=True)
    vn = vc * lax.rsqrt(var + LN_EPS) * ln_g + ln_b
    return u, vn


def _row_bcast(a, row, n):
    return jnp.broadcast_to(a[row:row + 1, :], (n, a.shape[1]))


def _block_id(idx, size):
    return lax.shift_right_logical(idx, I32(size.bit_length() - 1))


def _hgrn_masks(c):
    row = lax.broadcasted_iota(I32, (c, c), 0)
    col = lax.broadcasted_iota(I32, (c, c), 1)
    masks = []
    half = c // 2
    while half >= DIAG_BLOCK:
        span = 2 * half
        same = _block_id(row, span) == _block_id(col, span)
        masks.append(same & ((row & (span - 1)) >= half) & ((col & (span - 1)) < half))
        half //= 2
    diag = (_block_id(row, DIAG_BLOCK) == _block_id(col, DIAG_BLOCK)) & (col <= row)
    return masks, diag


def _tri_ones(c):
    row = lax.broadcasted_iota(I32, (c, c), 0)
    col = lax.broadcasted_iota(I32, (c, c), 1)
    return jnp.where(col <= row, 1.0, 0.0).astype(BF16)


def _hgrn_chains(chains, c):
    tri = _tri_ones(c)
    masks, diag_mask = _hgrn_masks(c)
    units = []
    for st0, lbv, gn, blocks in chains:
        for q_pre, f_pre, i_pre, g_pre in blocks:
            q = jax.nn.silu(q_pre)
            fg = lbv + (1.0 - lbv) * jax.nn.sigmoid(f_pre)
            logf = jnp.log(fg)
            hi = logf.astype(BF16)
            lo = (logf - hi.astype(F32)).astype(BF16)
            units.append(dict(q=q, k=1.0 - fg, hilo=jnp.concatenate([hi, lo], axis=1),
                              v=i_pre, g=g_pre, gn=gn))

    for u in units:
        p = _dot(tri, u["hilo"])
        u["a"] = p[:, 0:HEAD_DIM] + p[:, HEAD_DIM:2 * HEAD_DIM]

    for u in units:
        q, k, a = u["q"], u["k"], u["a"]
        parts = []
        half = c // 2
        level = 0
        while half >= DIAG_BLOCK:
            span = 2 * half
            ref = jnp.concatenate([_row_bcast(a, b * span + half, span) for b in range(c // span)], axis=0)
            qs = (q * jnp.exp(jnp.minimum(a - ref, 0.0))).astype(BF16)
            ks = (k * jnp.exp(jnp.minimum(ref - a, 0.0))).astype(BF16)
            parts.append((masks[level], _dot_nt(qs, ks)))
            half //= 2
            level += 1
        ref = jnp.concatenate([_row_bcast(a, b * DIAG_BLOCK, DIAG_BLOCK) for b in range(c // DIAG_BLOCK)], axis=0)
        qd = (q * jnp.exp(a - ref)).astype(BF16)
        kd = (k * jnp.exp(jnp.minimum(ref - a, 80.0))).astype(BF16)
        parts.append((diag_mask, _dot_nt(qd, kd)))
        u["parts"] = parts
        a_last = a[c - 1:c, :]
        u["decay"] = jnp.exp(a_last)
        kl = (k * jnp.exp(a_last - a)).astype(BF16)
        v_t = jnp.transpose(u["v"]).astype(BF16)
        u["upd"] = _dot(v_t, kl)
        u["qe"] = (q * jnp.exp(a)).astype(BF16)

    finals = []
    n = 0
    for st0, lbv, gn, blocks in chains:
        st = st0
        for _ in blocks:
            units[n]["st_in"] = st.astype(BF16)
            st = st * units[n]["decay"] + units[n]["upd"]
            n += 1
        finals.append(st)

    outs = []
    n = 0
    for st0, lbv, gn, blocks in chains:
        chain_out = []
        for _ in blocks:
            u = units[n]
            scores = None
            for mask, part in u["parts"]:
                part = jnp.where(mask, part, 0.0)
                scores = part if scores is None else scores + part
            o = _dot(scores.astype(BF16), u["v"].astype(BF16)) + _dot_nt(u["qe"], u["st_in"])
            o = o * lax.rsqrt(jnp.mean(o * o, axis=-1, keepdims=True) + LN_EPS) * u["gn"]
            chain_out.append(o * jax.nn.silu(u["g"]))
            n += 1
        outs.append(chain_out)
    return outs, finals


def _proj_cols(kind, hd):
    c0 = (kind * HEADS_PER_STEP + hd) * HEAD_DIM
    return slice(c0, c0 + HEAD_DIM)


def _in_projection(h_scr, w_refs, proj_scr):
    width = HEADS_PER_STEP * HEAD_DIM
    for kind, w_ref in enumerate(w_refs):
        proj_scr[:, kind * width:(kind + 1) * width] = _dot(h_scr[...], w_ref[...])


def _mixer_prompt_kernel(x_ref, sh_ref, sc_ref, wu_ref, wv_ref, wq_ref, wf_ref, wi_ref, wg_ref,
                         lvg_ref, lvb_ref, ws_ref, bs_ref, lb_ref, gn_ref,
                         cata_ref, cato_ref, state_ref, h_scr, proj_scr, st_scr, *, layer):
    i = pl.program_id(1)
    j = pl.program_id(2)
    tm = h_scr.shape[0]

    @pl.when(j == 0)
    def _():
        def slab(r, carry):
            rows = pl.ds(pl.multiple_of(r * HGRN_BLOCK, HGRN_BLOCK), HGRN_BLOCK)
            h = _ln_plain(x_ref[0, rows, :]) * (1.0 + sc_ref[0]) + sh_ref[0]
            h_scr[rows, :] = h.astype(BF16)
            return carry
        lax.fori_loop(0, tm // HGRN_BLOCK, slab, 0)

    @pl.when((i == 0) & (j == 0))
    def _():
        st_scr[...] = jnp.zeros_like(st_scr)

    _in_projection(h_scr, (wu_ref, wv_ref, wq_ref, wf_ref, wi_ref, wg_ref), proj_scr)

    row = lax.broadcasted_iota(I32, (GMLP_CHUNK, GMLP_CHUNK), 0)
    col = lax.broadcasted_iota(I32, (GMLP_CHUNK, GMLP_CHUNK), 1)
    causal = _block_id(row, SUB_CHUNK) >= _block_id(col, SUB_CHUNK)
    for hd in range(HEADS_PER_STEP):
        wm = jnp.where(causal, ws_ref[hd], 0.0).astype(BF16)
        b_col = bs_ref[hd]
        for c in range(tm // GMLP_CHUNK):
            rows = slice(c * GMLP_CHUNK, (c + 1) * GMLP_CHUNK)
            u, vn = _gmlp_rows(proj_scr[rows, _proj_cols(0, hd)], proj_scr[rows, _proj_cols(1, hd)],
                               lvg_ref[hd], lvb_ref[hd])
            sp = _dot(wm, vn.astype(BF16)) + b_col
            cata_ref[rows, hd * HEAD_DIM:(hd + 1) * HEAD_DIM] = (u * sp).astype(BF16)

    chains = []
    for hd in range(HEADS_PER_STEP):
        blocks = [tuple(proj_scr[c * HGRN_BLOCK:(c + 1) * HGRN_BLOCK, _proj_cols(kind, hd)] for kind in range(2, N_KINDS))
                  for c in range(tm // HGRN_BLOCK)]
        chains.append((st_scr[HEADS_PER_STEP * j + hd], _lower_bound(lb_ref, hd, layer), gn_ref[hd], blocks))
    outs, finals = _hgrn_chains(chains, HGRN_BLOCK)
    for hd in range(HEADS_PER_STEP):
        for c, out in enumerate(outs[hd]):
            cato_ref[c * HGRN_BLOCK:(c + 1) * HGRN_BLOCK, hd * HEAD_DIM:(hd + 1) * HEAD_DIM] = out.astype(BF16)
        st_scr[HEADS_PER_STEP * j + hd] = finals[hd]
        state_ref[0, HEADS_PER_STEP * j + hd] = jnp.transpose(finals[hd])


def _w_in_specs(d, n_grid_axes):
    width = HEADS_PER_STEP * HEAD_DIM
    blocks_per_kind = N_HEADS // HEADS_PER_STEP

    def spec(kind):
        if n_grid_axes == 3:
            return pl.BlockSpec((d, width), lambda b, i, j: (0, kind * blocks_per_kind + j))
        return pl.BlockSpec((d, width), lambda j: (0, kind * blocks_per_kind + j))
    return [spec(kind) for kind in range(N_KINDS)]


def _mixer_prompt_call(x, sh1, sc1, w_in_b, ln_v_g, ln_v_b, w_s, b_s, lb_h, gnorm_g, layer):
    bsz, t, d = x.shape
    n_t = t // MIX_TILE
    hp = HEADS_PER_STEP
    kern = functools.partial(_mixer_prompt_kernel, layer=layer)
    head_vec = lambda b, i, j: (j, 0, 0)
    return pl.pallas_call(
        kern,
        grid=(bsz, n_t, N_HEADS // hp),
        in_specs=[
            pl.BlockSpec((1, MIX_TILE, d), lambda b, i, j: (b, i, 0)),
            pl.BlockSpec((1, 1, d), lambda b, i, j: (b, 0, 0)),
            pl.BlockSpec((1, 1, d), lambda b, i, j: (b, 0, 0)),
            *_w_in_specs(d, 3),
            pl.BlockSpec((hp, 1, HEAD_DIM), head_vec),
            pl.BlockSpec((hp, 1, HEAD_DIM), head_vec),
            pl.BlockSpec((hp, GMLP_CHUNK, GMLP_CHUNK), head_vec),
            pl.BlockSpec((hp, GMLP_CHUNK, 1), head_vec),
            pl.BlockSpec((hp, lb_h.shape[1], HEAD_DIM), head_vec),
            pl.BlockSpec((hp, 1, HEAD_DIM), head_vec),
        ],
        out_specs=[
            pl.BlockSpec((MIX_TILE, hp * HEAD_DIM), lambda b, i, j: (b * n_t + i, j)),
            pl.BlockSpec((MIX_TILE, hp * HEAD_DIM), lambda b, i, j: (b * n_t + i, j)),
            pl.BlockSpec((1, N_HEADS, HEAD_DIM, HEAD_DIM), lambda b, i, j: (b, 0, 0, 0)),
        ],
        out_shape=[
            jax.ShapeDtypeStruct((bsz * t, N_HEADS * HEAD_DIM), BF16),
            jax.ShapeDtypeStruct((bsz * t, N_HEADS * HEAD_DIM), BF16),
            jax.ShapeDtypeStruct((bsz, N_HEADS, HEAD_DIM, HEAD_DIM), F32),
        ],
        scratch_shapes=[
            pltpu.VMEM((MIX_TILE, d), BF16),
            pltpu.VMEM((MIX_TILE, N_KINDS * hp * HEAD_DIM), F32),
            pltpu.VMEM((N_HEADS, HEAD_DIM, HEAD_DIM), F32),
        ],
        compiler_params=_cparams(("arbitrary", "arbitrary", "arbitrary")),
        name="mixer_prompt",
    )(x, sh1.reshape(bsz, 1, d), sc1.reshape(bsz, 1, d), *([w_in_b] * N_KINDS), ln_v_g, ln_v_b, w_s, b_s, lb_h, gnorm_g)


def _mixer_sample_kernel(x_ref, sh_ref, sc_ref, wu_ref, wv_ref, wq_ref, wf_ref, wi_ref, wg_ref,
                         lvg_ref, lvb_ref, ws_ref, bs_ref, lb_ref, gn_ref, s0_ref,
                         cata_ref, cato_ref, state_ref, vrows_ref, h_scr, proj_scr, *, layer, seq):
    j = pl.program_id(0)
    n_streams = x_ref.shape[0] // seq

    @pl.when(j == 0)
    def _():
        def slab(r, carry):
            rows = pl.ds(pl.multiple_of(r * seq, seq), seq)
            h = _ln_plain(x_ref[rows, :]) * (1.0 + sc_ref[pl.ds(r, 1), :]) + sh_ref[pl.ds(r, 1), :]
            h_scr[rows, :] = h.astype(BF16)
            return carry
        lax.fori_loop(0, n_streams, slab, 0)

    _in_projection(h_scr, (wu_ref, wv_ref, wq_ref, wf_ref, wi_ref, wg_ref), proj_scr)

    chains = []
    for hd in range(HEADS_PER_STEP):
        wm = ws_ref[hd][0:seq, 0:seq].astype(BF16)
        b_col = bs_ref[hd][0:seq, :]
        lbv = _lower_bound(lb_ref, hd, layer)
        for n in range(n_streams):
            rows = slice(n * seq, (n + 1) * seq)
            u, vn = _gmlp_rows(proj_scr[rows, _proj_cols(0, hd)], proj_scr[rows, _proj_cols(1, hd)],
                               lvg_ref[hd], lvb_ref[hd])
            vrows_ref[rows, hd * HEAD_DIM:(hd + 1) * HEAD_DIM] = vn
            sp = _dot(wm, vn.astype(BF16)) + b_col
            cata_ref[rows, hd * HEAD_DIM:(hd + 1) * HEAD_DIM] = (u * sp).astype(BF16)
            chains.append((jnp.transpose(s0_ref[n, hd]), lbv, gn_ref[hd],
                           [tuple(proj_scr[rows, _proj_cols(kind, hd)] for kind in range(2, N_KINDS))]))

    outs, finals = _hgrn_chains(chains, seq)
    for hd in range(HEADS_PER_STEP):
        for n in range(n_streams):
            r0 = n * seq
            m = hd * n_streams + n
            cato_ref[r0:r0 + seq, hd * HEAD_DIM:(hd + 1) * HEAD_DIM] = outs[m][0].astype(BF16)
            state_ref[n, hd] = jnp.transpose(finals[m])


def _mixer_sample_call(x2d, sh1, sc1, w_in_b, ln_v_g, ln_v_b, w_s, b_s, lb_h, gnorm_g, s0, layer, seq):
    rows, d = x2d.shape
    n_streams = rows // seq
    hp = HEADS_PER_STEP
    kern = functools.partial(_mixer_sample_kernel, layer=layer, seq=seq)
    head_vec = lambda j: (j, 0, 0)
    return pl.pallas_call(
        kern,
        grid=(N_HEADS // hp,),
        in_specs=[
            pl.BlockSpec((rows, d), lambda j: (0, 0)),
            pl.BlockSpec((n_streams, d), lambda j: (0, 0)),
            pl.BlockSpec((n_streams, d), lambda j: (0, 0)),
            *_w_in_specs(d, 1),
            pl.BlockSpec((hp, 1, HEAD_DIM), head_vec),
            pl.BlockSpec((hp, 1, HEAD_DIM), head_vec),
            pl.BlockSpec((hp, GMLP_CHUNK, GMLP_CHUNK), head_vec),
            pl.BlockSpec((hp, GMLP_CHUNK, 1), head_vec),
            pl.BlockSpec((hp, lb_h.shape[1], HEAD_DIM), head_vec),
            pl.BlockSpec((hp, 1, HEAD_DIM), head_vec),
            pl.BlockSpec((n_streams, hp, HEAD_DIM, HEAD_DIM), lambda j: (0, j, 0, 0)),
        ],
        out_specs=[
            pl.BlockSpec((rows, hp * HEAD_DIM), lambda j: (0, j)),
            pl.BlockSpec((rows, hp * HEAD_DIM), lambda j: (0, j)),
            pl.BlockSpec((n_streams, hp, HEAD_DIM, HEAD_DIM), lambda j: (0, j, 0, 0)),
            pl.BlockSpec((rows, hp * HEAD_DIM), lambda j: (0, j)),
        ],
        out_shape=[
            jax.ShapeDtypeStruct((rows, N_HEADS * HEAD_DIM), BF16),
            jax.ShapeDtypeStruct((rows, N_HEADS * HEAD_DIM), BF16),
            jax.ShapeDtypeStruct((n_streams, N_HEADS, HEAD_DIM, HEAD_DIM), F32),
            jax.ShapeDtypeStruct((rows, N_HEADS * HEAD_DIM), F32),
        ],
        scratch_shapes=[
            pltpu.VMEM((rows, d), BF16),
            pltpu.VMEM((rows, N_KINDS * hp * HEAD_DIM), F32),
        ],
        compiler_params=_cparams(("arbitrary",)),
        name="mixer_sample",
    )(x2d, sh1, sc1, *([w_in_b] * N_KINDS), ln_v_g, ln_v_b, w_s, b_s, lb_h, gnorm_g, s0)


def _post_kernel(cap_ref, cop_ref, cas_ref, cos_ref, xp_ref, xs_ref, g1_ref, sh2_ref, sc2_ref, wout_ref, l1g_ref, l1b_ref,
                 wr_ref, br_ref, tri_ref,
                 x1_ref, h2_ref, meta_ref, wts_ref, cnt_ref,
                 mix_scr, hhi_scr, hlo_scr, run_scr, *, n_prompt_tiles, alpha):
    i = pl.program_id(0)
    is_s = i >= n_prompt_tiles
    tm = mix_scr.shape[0]

    @pl.when(is_s)
    def _():
        mix_scr[...] = _dot(jnp.concatenate([cas_ref[...], cos_ref[...]], axis=1), wout_ref[...])

    @pl.when(jnp.logical_not(is_s))
    def _():
        mix_scr[...] = _dot(jnp.concatenate([cap_ref[...], cop_ref[...]], axis=1), wout_ref[...])

    @pl.when(i == 0)
    def _():
        run_scr[...] = jnp.zeros_like(run_scr)

    def slab(g, carry):
        rows = pl.ds(pl.multiple_of(g * MOD_GROUP, MOD_GROUP), MOD_GROUP)
        x = jnp.where(is_s, xs_ref[rows, :], xp_ref[rows, :])
        z = alpha * x + g1_ref[pl.ds(g, 1), :] * mix_scr[rows, :]
        x1 = _ln_plain(z) * l1g_ref[...] + l1b_ref[...]
        x1_ref[rows, :] = x1
        h2 = _ln_plain(x1) * (1.0 + sc2_ref[pl.ds(g, 1), :]) + sh2_ref[pl.ds(g, 1), :]
        h2_ref[rows, :] = h2
        hi = h2.astype(BF16)
        hhi_scr[rows, :] = hi
        hlo_scr[rows, :] = (h2 - hi.astype(F32)).astype(BF16)
        return carry
    lax.fori_loop(0, tm // MOD_GROUP, slab, 0)

    n_r = N_GROUPS + N_EXPERTS
    s = _dot(hhi_scr[...], wr_ref[...]) + _dot(hlo_scr[...], wr_ref[...])
    logits = s + pltpu.roll(s, LANES - n_r, axis=1) + br_ref[...]

    lane = lax.broadcasted_iota(I32, (tm, LANES), 1)
    lane_f = lane.astype(F32)
    neg = jnp.float32(-jnp.inf)
    big = jnp.float32(LANES)

    def first_lane_of(vals, top):
        return jnp.min(jnp.where(vals == top, lane_f, big), axis=1, keepdims=True)

    gmask = lane < N_GROUPS
    gl = jnp.where(gmask, logits, neg)
    gmax = jnp.max(gl, axis=1, keepdims=True)
    gsel = first_lane_of(gl, gmax)
    p_group = 1.0 / jnp.sum(jnp.exp(gl - gmax), axis=1, keepdims=True)

    e_lo = N_GROUPS + EXPERTS_PER_GROUP * gsel.astype(I32)
    emask = (lane >= e_lo) & (lane < e_lo + EXPERTS_PER_GROUP)
    el = jnp.where(emask, logits, neg)
    t1 = jnp.max(el, axis=1, keepdims=True)
    i1 = first_lane_of(el, t1)
    el2 = jnp.where(lane_f == i1, neg, el)
    t2 = jnp.max(el2, axis=1, keepdims=True)
    i2 = first_lane_of(el2, t2)
    e2 = jnp.exp(t2 - t1)
    den = 1.0 + e2
    w1 = (1.0 / den) * p_group
    w2 = (e2 / den) * p_group

    sel1 = lane_f == i1
    sel2 = lane_f == i2
    onehot = jnp.where(sel1 | sel2, 1.0, 0.0)
    before = _dot(tri_ref[...], onehot.astype(BF16)) + run_scr[...]
    r1 = jnp.sum(jnp.where(sel1, before, 0.0), axis=1, keepdims=True)
    r2 = jnp.sum(jnp.where(sel2, before, 0.0), axis=1, keepdims=True)
    run_scr[...] = run_scr[...] + jnp.sum(onehot, axis=0, keepdims=True)
    cnt_ref[...] = run_scr[...].astype(I32)

    base = jnp.float32(N_GROUPS)
    meta = jnp.where(lane == 0, i1 - base,
                     jnp.where(lane == 1, i2 - base,
                               jnp.where(lane == 2, r1, jnp.where(lane == 3, r2, 0.0))))
    meta_ref[...] = meta.astype(I32)
    wts_ref[...] = jnp.where(lane == 0, w1, jnp.where(lane == 1, w2, 0.0))


def _post_call(cats_p, cats_s, x_p, x_s, g1g, sh2g, sc2g, w_out_b, ln1_g, ln1_b, wr, br, tri, alpha):
    n_p, d = x_p.shape
    n_s = x_s.shape[0]
    d_a = cats_p[0].shape[1]
    n_pt = n_p // TOK_TILE
    n_tiles = n_pt + n_s // TOK_TILE
    n_tok = n_p + n_s
    groups = TOK_TILE // MOD_GROUP
    kern = functools.partial(_post_kernel, n_prompt_tiles=n_pt, alpha=alpha)
    p_idx = lambda i: (jnp.minimum(i, n_pt - 1), 0)
    s_idx = lambda i: (jnp.maximum(i - n_pt, 0), 0)
    tile = lambda i: (i, 0)
    const = lambda i: (0, 0)
    once = pl.Buffered(1)
    return pl.pallas_call(
        kern,
        grid=(n_tiles,),
        in_specs=[
            pl.BlockSpec((TOK_TILE, d_a), p_idx),
            pl.BlockSpec((TOK_TILE, d_a), p_idx),
            pl.BlockSpec((TOK_TILE, d_a), s_idx, pipeline_mode=once),
            pl.BlockSpec((TOK_TILE, d_a), s_idx, pipeline_mode=once),
            pl.BlockSpec((TOK_TILE, d), p_idx),
            pl.BlockSpec((TOK_TILE, d), s_idx, pipeline_mode=once),
            pl.BlockSpec((groups, d), tile),
            pl.BlockSpec((groups, d), tile),
            pl.BlockSpec((groups, d), tile),
            pl.BlockSpec((d, d), const, pipeline_mode=once),
            pl.BlockSpec((1, d), const),
            pl.BlockSpec((1, d), const),
            pl.BlockSpec((d, LANES), const, pipeline_mode=once),
            pl.BlockSpec((1, LANES), const),
            pl.BlockSpec((TOK_TILE, TOK_TILE), const, pipeline_mode=once),
        ],
        out_specs=[
            pl.BlockSpec((TOK_TILE, d), tile),
            pl.BlockSpec((TOK_TILE, d), tile),
            pl.BlockSpec((TOK_TILE, LANES), tile),
            pl.BlockSpec((TOK_TILE, LANES), tile),
            pl.BlockSpec((1, LANES), const),
        ],
        out_shape=[
            jax.ShapeDtypeStruct((n_tok, d), F32),
            jax.ShapeDtypeStruct((n_tok, d), F32),
            jax.ShapeDtypeStruct((n_tok, LANES), I32),
            jax.ShapeDtypeStruct((n_tok, LANES), F32),
            jax.ShapeDtypeStruct((1, LANES), I32),
        ],
        scratch_shapes=[
            pltpu.VMEM((TOK_TILE, d), F32),
            pltpu.VMEM((TOK_TILE, d), BF16),
            pltpu.VMEM((TOK_TILE, d), BF16),
            pltpu.VMEM((1, LANES), F32),
        ],
        compiler_params=_cparams(("arbitrary",)),
        name="post_router",
    )(*cats_p, *cats_s, x_p, x_s, g1g, sh2g, sc2g, w_out_b, ln1_g, ln1_b, wr, br, tri)


def _dispatch_kernel(slot_ref, pad_ref, h_ref, xs_ref, zero_scr, sem, *, n_unused):
    i = pl.program_id(0)
    tm = h_ref.shape[0]
    pad_per_step = pad_ref.shape[2]
    n_pad = jnp.clip(n_unused - i * pad_per_step, 0, pad_per_step)
    zero_scr[...] = jnp.zeros_like(zero_scr)

    def row_copy(r, k):
        return pltpu.make_async_copy(h_ref.at[pl.ds(r, 1)], xs_ref.at[pl.ds(slot_ref[0, 0, 2 * r + k], 1)], sem)

    def pad_copy(r):
        return pltpu.make_async_copy(zero_scr.at[pl.ds(0, 1)], xs_ref.at[pl.ds(pad_ref[0, 0, r], 1)], sem)

    def start_rows(r, carry):
        row_copy(r, 0).start(priority=0)
        row_copy(r, 1).start(priority=1)
        return carry
    lax.fori_loop(0, tm, start_rows, 0, unroll=ROW_DMA_UNROLL)

    def start_pad(r, carry):
        pad_copy(r).start()
        return carry
    lax.fori_loop(0, n_pad, start_pad, 0)

    def wait_rows(r, carry):
        row_copy(r, 0).wait()
        row_copy(r, 1).wait()
        return carry
    lax.fori_loop(0, tm, wait_rows, 0, unroll=ROW_DMA_UNROLL)

    def wait_pad(r, carry):
        pad_copy(r).wait()
        return carry
    lax.fori_loop(0, n_pad, wait_pad, 0)


def _dispatch_call(slots3, pads3, h2, n_sorted, n_unused):
    n_tok, d = h2.shape
    n_tiles = n_tok // TOK_TILE
    kern = functools.partial(_dispatch_kernel, n_unused=n_unused)
    return pl.pallas_call(
        kern,
        grid=(n_tiles,),
        in_specs=[
            pl.BlockSpec((1, 1, slots3.shape[2]), lambda i: (i, 0, 0), memory_space=pltpu.SMEM),
            pl.BlockSpec((1, 1, pads3.shape[2]), lambda i: (i, 0, 0), memory_space=pltpu.SMEM),
            pl.BlockSpec((TOK_TILE, d), lambda i: (i, 0)),
        ],
        out_specs=pl.BlockSpec(memory_space=pl.ANY),
        out_shape=jax.ShapeDtypeStruct((n_sorted, d), F32),
        scratch_shapes=[pltpu.VMEM((8, d), F32), pltpu.SemaphoreType.DMA(())],
        compiler_params=_cparams(("arbitrary",)),
        name="dispatch",
    )(slots3, pads3, h2)


def _expert_kernel(te_ref, nt_ref, xs_ref, w1_ref, w3_ref, w2_ref, ys_ref, w13_scr, w2_scr):
    t = pl.program_id(0)
    d_exp = w1_ref.shape[2]
    prev = te_ref[jnp.maximum(t - 1, 0)]
    changed = (t == 0) | (te_ref[t] != prev)
    valid = t < nt_ref[0]

    @pl.when(changed)
    def _():
        w13_scr[:, 0:d_exp] = w1_ref[0].astype(BF16)
        w13_scr[:, d_exp:2 * d_exp] = w3_ref[0].astype(BF16)
        w2_scr[...] = w2_ref[0].astype(BF16)

    @pl.when(valid)
    def _():
        h13 = _dot(xs_ref[...].astype(BF16), w13_scr[...])
        hm = jax.nn.silu(h13[:, 0:d_exp]) * h13[:, d_exp:2 * d_exp]
        ys_ref[...] = _dot(hm.astype(BF16), w2_scr[...])

    @pl.when(jnp.logical_not(valid))
    def _():
        ys_ref[...] = jnp.zeros_like(ys_ref)


def _expert_call(tile_expert, n_valid, xs, w1, w3, w2):
    n_sorted, d = xs.shape
    d_exp = w1.shape[2]
    n_tiles = n_sorted // EXP_TILE
    grid_spec = pltpu.PrefetchScalarGridSpec(
        num_scalar_prefetch=2,
        grid=(n_tiles,),
        in_specs=[
            pl.BlockSpec((EXP_TILE, d), lambda t, te, nt: (t, 0)),
            pl.BlockSpec((1, d, d_exp), lambda t, te, nt: (te[t], 0, 0)),
            pl.BlockSpec((1, d, d_exp), lambda t, te, nt: (te[t], 0, 0)),
            pl.BlockSpec((1, d_exp, d), lambda t, te, nt: (te[t], 0, 0)),
        ],
        out_specs=pl.BlockSpec((EXP_TILE, d), lambda t, te, nt: (t, 0)),
        scratch_shapes=[pltpu.VMEM((d, 2 * d_exp), BF16), pltpu.VMEM((d_exp, d), BF16)],
    )
    return pl.pallas_call(
        _expert_kernel,
        grid_spec=grid_spec,
        out_shape=jax.ShapeDtypeStruct((n_sorted, d), F32),
        compiler_params=_cparams(("arbitrary",)),
        name="experts",
    )(tile_expert, n_valid, xs, w1, w3, w2)


def _combine_kernel(slot_ref, x1_ref, wts_ref, g2_ref, l2g_ref, l2b_ref, ys_ref, outp_ref, outs_ref,
                    y0_scr, y1_scr, sem, *, n_prompt_tiles, alpha):
    i = pl.program_id(0)
    is_s = i >= n_prompt_tiles
    tm = x1_ref.shape[0]

    def row_copy(r, k):
        dst = y0_scr if k == 0 else y1_scr
        return pltpu.make_async_copy(ys_ref.at[pl.ds(slot_ref[0, 0, 2 * r + k], 1)], dst.at[pl.ds(r, 1)], sem)

    def start_rows(r, carry):
        row_copy(r, 0).start(priority=0)
        row_copy(r, 1).start(priority=1)
        return carry
    lax.fori_loop(0, tm, start_rows, 0, unroll=ROW_DMA_UNROLL)

    def wait_rows(r, carry):
        row_copy(r, 0).wait()
        row_copy(r, 1).wait()
        return carry
    lax.fori_loop(0, tm, wait_rows, 0, unroll=ROW_DMA_UNROLL)

    def slab(g, carry):
        rows = pl.ds(pl.multiple_of(g * MOD_GROUP, MOD_GROUP), MOD_GROUP)
        w = wts_ref[rows, :]
        moe = w[:, 0:1] * y0_scr[rows, :] + w[:, 1:2] * y1_scr[rows, :]
        z = alpha * x1_ref[rows, :] + g2_ref[pl.ds(g, 1), :] * moe
        y = _ln_plain(z) * l2g_ref[...] + l2b_ref[...]

        @pl.when(is_s)
        def _():
            outs_ref[rows, :] = y

        @pl.when(jnp.logical_not(is_s))
        def _():
            outp_ref[rows, :] = y
        return carry
    lax.fori_loop(0, tm // MOD_GROUP, slab, 0)


def _combine_call(slots3, x1, wts, g2g, ln2_g, ln2_b, ys, n_p, alpha):
    n_tok, d = x1.shape
    n_tiles = n_tok // TOK_TILE
    n_pt = n_p // TOK_TILE
    groups = TOK_TILE // MOD_GROUP
    kern = functools.partial(_combine_kernel, n_prompt_tiles=n_pt, alpha=alpha)
    return pl.pallas_call(
        kern,
        grid=(n_tiles,),
        in_specs=[
            pl.BlockSpec((1, 1, slots3.shape[2]), lambda i: (i, 0, 0), memory_space=pltpu.SMEM),
            pl.BlockSpec((TOK_TILE, d), lambda i: (i, 0)),
            pl.BlockSpec((TOK_TILE, LANES), lambda i: (i, 0)),
            pl.BlockSpec((groups, d), lambda i: (i, 0)),
            pl.BlockSpec((1, d), lambda i: (0, 0)),
            pl.BlockSpec((1, d), lambda i: (0, 0)),
            pl.BlockSpec(memory_space=pl.ANY),
        ],
        out_specs=[
            pl.BlockSpec((TOK_TILE, d), lambda i: (jnp.minimum(i, n_pt - 1), 0)),
            pl.BlockSpec((TOK_TILE, d), lambda i: (jnp.maximum(i - n_pt, 0), 0)),
        ],
        out_shape=[
            jax.ShapeDtypeStruct((n_p, d), F32),
            jax.ShapeDtypeStruct((n_tok - n_p, d), F32),
        ],
        scratch_shapes=[
            pltpu.VMEM((TOK_TILE, d), F32),
            pltpu.VMEM((TOK_TILE, d), F32),
            pltpu.SemaphoreType.DMA(()),
        ],
        compiler_params=_cparams(("arbitrary",)),
        name="combine",
    )(slots3, x1, wts, g2g, ln2_g, ln2_b, ys)


def _routing_tables(meta, cnt_row, n_sorted):
    n_tok = meta.shape[0]
    experts = meta[:, 0:2]
    ranks = meta[:, 2:4]
    cnt = cnt_row[0, N_GROUPS:N_GROUPS + N_EXPERTS]
    padded = ((cnt + EXP_TILE - 1) // EXP_TILE) * EXP_TILE
    ends = jnp.cumsum(padded)
    offs = ends - padded
    slots = offs[experts] + ranks
    total = ends[-1]
    n_tiles = n_sorted // EXP_TILE
    tile_ids = jnp.arange(n_tiles, dtype=I32)
    tile_expert = jnp.minimum(
        jnp.sum((tile_ids[:, None] >= (ends // EXP_TILE)[None, :]).astype(I32), axis=1), N_EXPERTS - 1)
    n_valid = (total // EXP_TILE).reshape(1).astype(I32)
    n_unused = n_sorted - 2 * n_tok
    starts = jnp.concatenate([offs + cnt, total.reshape(1)])
    lens = jnp.concatenate([padded - cnt, (n_sorted - total).reshape(1)])
    lens_end = jnp.cumsum(lens)
    k = jnp.arange(n_unused, dtype=I32)
    in_seg = k[:, None] >= lens_end[None, :]
    seg_start = jnp.max(jnp.where(in_seg, lens_end[None, :], 0), axis=1)
    seg = jnp.sum(in_seg.astype(I32), axis=1)
    one_hot = (seg[:, None] == jnp.arange(starts.shape[0], dtype=I32)[None, :]).astype(I32)
    pads = (jnp.sum(one_hot * starts[None, :], axis=1) + (k - seg_start)).astype(I32)
    return slots.astype(I32), tile_expert, n_valid, pads


def _layer(layer, n_layers, xp, xs, s0_l, c_all, p):
    (w_ada, b_ada, w_in, ln_v_g, ln_v_b, w_s, b_s, hgrn_lb, gnorm_g, w_out, ln1_g, ln1_b,
     w_rg, b_rg, w_re, b_re, w1, w3, w2, ln2_g, ln2_b) = p
    bsz, t, d = xp.shape
    n_streams, seq, _ = xs.shape
    alpha = float((2.0 * n_layers) ** 0.25)
    n_p = bsz * t
    n_s = n_streams * seq
    n_tok = n_p + n_s

    n_c = c_all.shape[0]
    c_pad = jnp.pad(c_all, ((0, (-n_c) % 8), (0, 0)))
    mod = _ada_call(c_pad, w_ada, b_ada)[:n_c]
    sh1, sc1, g1, sh2, sc2, g2 = [mod[:, m * d:(m + 1) * d] for m in range(6)]

    w_in_b = w_in.astype(BF16)
    w_out_b = w_out.astype(BF16)
    lvg = ln_v_g.reshape(N_HEADS, 1, HEAD_DIM)
    lvb = ln_v_b.reshape(N_HEADS, 1, HEAD_DIM)
    b_s3 = b_s.reshape(N_HEADS, GMLP_CHUNK, 1)
    lb_h = jnp.transpose(hgrn_lb.reshape(hgrn_lb.shape[0], N_HEADS, HEAD_DIM), (1, 0, 2))
    gn = gnorm_g.reshape(N_HEADS, 1, HEAD_DIM)

    ca_p, co_p, state_p = _mixer_prompt_call(xp, sh1[:bsz], sc1[:bsz], w_in_b, lvg, lvb, w_s, b_s3, lb_h, gn, layer)
    ca_s, co_s, state_s, vrows = _mixer_sample_call(xs.reshape(n_s, d), sh1[bsz:], sc1[bsz:], w_in_b, lvg, lvb, w_s,
                                               b_s3, lb_h, gn, s0_l, layer, seq)

    group_stream = jnp.concatenate([
        jnp.repeat(jnp.arange(bsz, dtype=I32), t // MOD_GROUP),
        bsz + jnp.repeat(jnp.arange(n_streams, dtype=I32), seq // MOD_GROUP)])
    g1g, sh2g, sc2g, g2g = [m[group_stream] for m in (g1, sh2, sc2, g2)]

    wr = jnp.concatenate([w_rg, w_re], axis=1)
    wr_hi = wr.astype(BF16)
    wr_lo = (wr - wr_hi.astype(F32)).astype(BF16)
    n_r = wr.shape[1]
    wr_cat = jnp.concatenate([wr_hi, wr_lo, jnp.zeros((d, LANES - 2 * n_r), BF16)], axis=1)
    br = jnp.pad(jnp.concatenate([b_rg, b_re]), (0, LANES - n_r)).reshape(1, LANES)
    tri = jnp.tril(jnp.ones((TOK_TILE, TOK_TILE), F32), -1).astype(BF16)

    x1, h2, meta, wts, cnt_row = _post_call((ca_p, co_p), (ca_s, co_s), xp.reshape(n_p, d), xs.reshape(n_s, d),
                                            g1g, sh2g, sc2g, w_out_b, ln1_g.reshape(1, d), ln1_b.reshape(1, d),
                                            wr_cat, br, tri, alpha)

    n_sorted = 2 * n_tok + N_EXPERTS * EXP_TILE
    slots, tile_expert, n_valid, pads = _routing_tables(meta, cnt_row, n_sorted)
    n_tiles = n_tok // TOK_TILE
    slots3 = slots.reshape(n_tiles, 1, 2 * TOK_TILE)
    n_unused = n_sorted - 2 * n_tok
    pad_per_step = -(-n_unused // n_tiles)
    pads3 = jnp.pad(pads, (0, n_tiles * pad_per_step - n_unused)).reshape(n_tiles, 1, pad_per_step)

    xs_sorted = _dispatch_call(slots3, pads3, h2, n_sorted, n_unused)
    ys_sorted = _expert_call(tile_expert, n_valid, xs_sorted, w1, w3, w2)
    yp, ys_out = _combine_call(slots3, x1, wts, g2g, ln2_g.reshape(1, d), ln2_b.reshape(1, d), ys_sorted, n_p, alpha)

    v_rows = vrows.reshape(n_streams, seq, N_HEADS, HEAD_DIM)
    return yp.reshape(bsz, t, d), ys_out.reshape(n_streams, seq, d), state_p, state_s, v_rows


def kernel(x_prompt, x_sample, state_hgrn, c_prompt, c_sample, w_ada, b_ada, w_in, ln_v_g, ln_v_b, w_s, b_s, hgrn_lb, gnorm_g, w_out, ln1_g, ln1_b, w_router_g, b_router_g, w_router_e, b_router_e, w1, w3, w2, ln2_g, ln2_b):
    n_layers = w_ada.shape[0]
    assert x_prompt.shape[1] % MIX_TILE == 0 and x_prompt.shape[2] == 2 * N_HEADS * HEAD_DIM
    assert x_sample.shape[1] % MOD_GROUP == 0 and x_sample.shape[1] <= SUB_CHUNK
    assert (x_sample.shape[0] * x_sample.shape[1]) % TOK_TILE == 0
    c_all = jnp.concatenate([c_prompt, c_sample], axis=0)
    xp, xs = x_prompt, x_sample
    sp_list, ss_list, vs_list = [], [], []
    for l in range(n_layers):
        p = (w_ada[l], b_ada[l], w_in[l], ln_v_g[l], ln_v_b[l], w_s[l], b_s[l], hgrn_lb, gnorm_g[l], w_out[l],
             ln1_g[l], ln1_b[l], w_router_g[l], b_router_g[l], w_router_e[l], b_router_e[l],
             w1[l], w3[l], w2[l], ln2_g[l], ln2_b[l])
        xp, xs, sp, ss, vs = _layer(l, n_layers, xp, xs, state_hgrn[l], c_all, p)
        sp_list.append(sp.astype(state_hgrn.dtype))
        ss_list.append(ss.astype(state_hgrn.dtype))
        vs_list.append(vs)
    return (xp, xs, jnp.stack(sp_list, axis=0), jnp.stack(ss_list, axis=0), jnp.stack(vs_list, axis=0))
```

```python
import functools

import jax
import jax.numpy as jnp
from jax import lax
from jax.experimental import pallas as pl
from jax.experimental.pallas import tpu as pltpu

F32 = jnp.float32
BF16 = jnp.bfloat16
I32 = jnp.int32

N_HEADS = 8
HEAD_DIM = 128
GMLP_CHUNK = 128
SUB_CHUNK = 64
N_GROUPS = 4
EXPERTS_PER_GROUP = 8
N_EXPERTS = N_GROUPS * EXPERTS_PER_GROUP
LN_EPS = 1e-5
N_KINDS = 6

LANES = 128
MIX_TILE = 512
HEADS_PER_STEP = 2
HGRN_BLOCK = 64
DIAG_BLOCK = 16
TOK_TILE = 512
MOD_GROUP = 32
SLAB_UNROLL = 4
EXP_TILE = 256
ROW_DMA_UNROLL = 8
GATHER_AHEAD = 2
ADA_TILE = 1024
VMEM_LIMIT = 56 * 1024 * 1024


def _cparams(sem):
    return pltpu.CompilerParams(dimension_semantics=sem, vmem_limit_bytes=VMEM_LIMIT)


def _ln_plain(x):
    mu = jnp.mean(x, axis=-1, keepdims=True)
    xc = x - mu
    var = jnp.mean(xc * xc, axis=-1, keepdims=True)
    return xc * lax.rsqrt(var + LN_EPS)


def _dot(a, b):
    return jnp.dot(a, b, preferred_element_type=F32)


def _dot_nt(a, b):
    return lax.dot_general(a, b, (((1,), (1,)), ((), ())), preferred_element_type=F32)


def _lines_per_row(d):
    return d // LANES


def _load_rows(ref, lead, row0, n_rows, d):
    lpr = _lines_per_row(d)
    parts = [ref[lead + (pl.ds(row0 * lpr + c, n_rows, stride=lpr), slice(None))] for c in range(lpr)]
    return jnp.concatenate(parts, axis=1)


def _store_rows(ref, lead, row0, val):
    n_rows, d = val.shape
    lpr = _lines_per_row(d)
    for c in range(lpr):
        ref[lead + (pl.ds(row0 * lpr + c, n_rows, stride=lpr), slice(None))] = val[:, c * LANES:(c + 1) * LANES]


def _ada_kernel(c_ref, w_ref, b_ref, o_ref):
    s = jax.nn.silu(c_ref[...]).astype(BF16)
    o_ref[...] = _dot(s, w_ref[...].astype(BF16)) + b_ref[...]


def _ada_call(c_pad, w_ada, b_ada):
    rows, d = c_pad.shape
    n_out = w_ada.shape[1]
    return pl.pallas_call(
        _ada_kernel,
        grid=(n_out // ADA_TILE,),
        in_specs=[
            pl.BlockSpec((rows, d), lambda n: (0, 0)),
            pl.BlockSpec((d, ADA_TILE), lambda n: (0, n)),
            pl.BlockSpec((1, ADA_TILE), lambda n: (0, n)),
        ],
        out_specs=pl.BlockSpec((rows, ADA_TILE), lambda n: (0, n)),
        out_shape=jax.ShapeDtypeStruct((rows, n_out), F32),
        compiler_params=_cparams(("arbitrary",)),
        name="adaln",
    )(c_pad, w_ada, b_ada.reshape(1, n_out))


def _lower_bound(lb_ref, hd, layer):
    raw = lb_ref[hd]
    m = jnp.max(raw, axis=0, keepdims=True)
    e = jnp.exp(raw - m)
    p = e / jnp.sum(e, axis=0, keepdims=True)
    return jnp.sum(p[: layer + 1], axis=0, keepdims=True)


def _gmlp_rows(u_pre, v_pre, ln_g, ln_b):
    u = jax.nn.gelu(u_pre)
    v = jax.nn.gelu(v_pre)
    mu = jnp.mean(v, axis=-1, keepdims=True)
    vc = v - mu
    var = jnp.mean(vc * vc, axis=-1, keepdims=True)
    vn = vc * lax.rsqrt(var + LN_EPS) * ln_g + ln_b
    return u, vn


def _row_bcast(a, row, n):
    return jnp.broadcast_to(a[row:row + 1, :], (n, a.shape[1]))


def _block_id(idx, size):
    return lax.shift_right_logical(idx, I32(size.bit_length() - 1))


def _hgrn_masks(c):
    row = lax.broadcasted_iota(I32, (c, c), 0)
    col = lax.broadcasted_iota(I32, (c, c), 1)
    masks = []
    half = c // 2
    while half >= DIAG_BLOCK:
        span = 2 * half
        same = _block_id(row, span) == _block_id(col, span)
        masks.append(same & ((row & (span - 1)) >= half) & ((col & (span - 1)) < half))
        half //= 2
    diag = (_block_id(row, DIAG_BLOCK) == _block_id(col, DIAG_BLOCK)) & (col <= row)
    return masks, diag


def _tri_ones(c):
    row = lax.broadcasted_iota(I32, (c, c), 0)
    col = lax.broadcasted_iota(I32, (c, c), 1)
    return jnp.where(col <= row, 1.0, 0.0).astype(BF16)


def _hgrn_chains(chains, c):
    tri = _tri_ones(c)
    masks, diag_mask = _hgrn_masks(c)
    units = []
    for st0, lbv, gn, blocks in chains:
        for q_pre, f_pre, i_pre, g_pre in blocks:
            q = jax.nn.silu(q_pre)
            fg = lbv + (1.0 - lbv) * jax.nn.sigmoid(f_pre)
            logf = jnp.log(fg)
            hi = logf.astype(BF16)
            lo = (logf - hi.astype(F32)).astype(BF16)
            units.append(dict(q=q, k=1.0 - fg, hilo=jnp.concatenate([hi, lo], axis=1),
                              v=i_pre, g=g_pre, gn=gn))

    for u in units:
        p = _dot(tri, u["hilo"])
        u["a"] = p[:, 0:HEAD_DIM] + p[:, HEAD_DIM:2 * HEAD_DIM]

    for u in units:
        q, k, a = u["q"], u["k"], u["a"]
        parts = []
        half = c // 2
        level = 0
        while half >= DIAG_BLOCK:
            span = 2 * half
            ref = jnp.concatenate([_row_bcast(a, b * span + half, span) for b in range(c // span)], axis=0)
            qs = (q * jnp.exp(jnp.minimum(a - ref, 0.0))).astype(BF16)
            ks = (k * jnp.exp(jnp.minimum(ref - a, 0.0))).astype(BF16)
            parts.append((masks[level], _dot_nt(qs, ks)))
            half //= 2
            level += 1
        ref = jnp.concatenate([_row_bcast(a, b * DIAG_BLOCK, DIAG_BLOCK) for b in range(c // DIAG_BLOCK)], axis=0)
        qd = (q * jnp.exp(a - ref)).astype(BF16)
        kd = (k * jnp.exp(jnp.minimum(ref - a, 80.0))).astype(BF16)
        parts.append((diag_mask, _dot_nt(qd, kd)))
        u["parts"] = parts
        a_last = a[c - 1:c, :]
        u["decay"] = jnp.exp(a_last)
        kl = (k * jnp.exp(a_last - a)).astype(BF16)
        v_t = jnp.transpose(u["v"]).astype(BF16)
        u["upd"] = _dot(v_t, kl)
        u["qe"] = (q * jnp.exp(a)).astype(BF16)

    finals = []
    n = 0
    for st0, lbv, gn, blocks in chains:
        st = st0
        for _ in blocks:
            units[n]["st_in"] = st.astype(BF16)
            st = st * units[n]["decay"] + units[n]["upd"]
            n += 1
        finals.append(st)

    outs = []
    n = 0
    for st0, lbv, gn, blocks in chains:
        chain_out = []
        for _ in blocks:
            u = units[n]
            scores = None
            for mask, part in u["parts"]:
                part = jnp.where(mask, part, 0.0)
                scores = part if scores is None else scores + part
            o = _dot(scores.astype(BF16), u["v"].astype(BF16)) + _dot_nt(u["qe"], u["st_in"])
            o = o * lax.rsqrt(jnp.mean(o * o, axis=-1, keepdims=True) + LN_EPS) * u["gn"]
            chain_out.append(o * jax.nn.silu(u["g"]))
            n += 1
        outs.append(chain_out)
    return outs, finals


def _proj_cols(kind, hd):
    c0 = (kind * HEADS_PER_STEP + hd) * HEAD_DIM
    return slice(c0, c0 + HEAD_DIM)


def _in_projection(h_scr, w_refs, proj_scr):
    width = HEADS_PER_STEP * HEAD_DIM
    for kind, w_ref in enumerate(w_refs):
        proj_scr[:, kind * width:(kind + 1) * width] = _dot(h_scr[...], w_ref[...])


def _mixer_prompt_kernel(x_ref, sh_ref, sc_ref, wu_ref, wv_ref, wq_ref, wf_ref, wi_ref, wg_ref,
                         lvg_ref, lvb_ref, ws_ref, bs_ref, lb_ref, gn_ref,
                         cata_ref, cato_ref, state_ref, h_scr, proj_scr, st_scr, *, layer):
    i = pl.program_id(1)
    j = pl.program_id(2)
    tm = h_scr.shape[0]

    @pl.when(j == 0)
    def _():
        def slab(r, carry):
            rows = pl.ds(pl.multiple_of(r * HGRN_BLOCK, HGRN_BLOCK), HGRN_BLOCK)
            h = _ln_plain(x_ref[0, rows, :]) * (1.0 + sc_ref[0]) + sh_ref[0]
            h_scr[rows, :] = h.astype(BF16)
            return carry
        lax.fori_loop(0, tm // HGRN_BLOCK, slab, 0)

    @pl.when((i == 0) & (j == 0))
    def _():
        st_scr[...] = jnp.zeros_like(st_scr)

    _in_projection(h_scr, (wu_ref, wv_ref, wq_ref, wf_ref, wi_ref, wg_ref), proj_scr)

    row = lax.broadcasted_iota(I32, (GMLP_CHUNK, GMLP_CHUNK), 0)
    col = lax.broadcasted_iota(I32, (GMLP_CHUNK, GMLP_CHUNK), 1)
    causal = _block_id(row, SUB_CHUNK) >= _block_id(col, SUB_CHUNK)
    for hd in range(HEADS_PER_STEP):
        wm = jnp.where(causal, ws_ref[hd], 0.0).astype(BF16)
        b_col = bs_ref[hd]
        for c in range(tm // GMLP_CHUNK):
            rows = slice(c * GMLP_CHUNK, (c + 1) * GMLP_CHUNK)
            u, vn = _gmlp_rows(proj_scr[rows, _proj_cols(0, hd)], proj_scr[rows, _proj_cols(1, hd)],
                               lvg_ref[hd], lvb_ref[hd])
            sp = _dot(wm, vn.astype(BF16)) + b_col
            cata_ref[rows, hd * HEAD_DIM:(hd + 1) * HEAD_DIM] = (u * sp).astype(BF16)

    chains = []
    for hd in range(HEADS_PER_STEP):
        blocks = [tuple(proj_scr[c * HGRN_BLOCK:(c + 1) * HGRN_BLOCK, _proj_cols(kind, hd)] for kind in range(2, N_KINDS))
                  for c in range(tm // HGRN_BLOCK)]
        chains.append((st_scr[HEADS_PER_STEP * j + hd], _lower_bound(lb_ref, hd, layer), gn_ref[hd], blocks))
    outs, finals = _hgrn_chains(chains, HGRN_BLOCK)
    for hd in range(HEADS_PER_STEP):
        for c, out in enumerate(outs[hd]):
            cato_ref[c * HGRN_BLOCK:(c + 1) * HGRN_BLOCK, hd * HEAD_DIM:(hd + 1) * HEAD_DIM] = out.astype(BF16)
        st_scr[HEADS_PER_STEP * j + hd] = finals[hd]
        state_ref[0, HEADS_PER_STEP * j + hd] = jnp.transpose(finals[hd])


def _w_in_specs(d, n_grid_axes):
    width = HEADS_PER_STEP * HEAD_DIM
    blocks_per_kind = N_HEADS // HEADS_PER_STEP

    def spec(kind):
        if n_grid_axes == 3:
            return pl.BlockSpec((d, width), lambda b, i, j: (0, kind * blocks_per_kind + j))
        return pl.BlockSpec((d, width), lambda j: (0, kind * blocks_per_kind + j))
    return [spec(kind) for kind in range(N_KINDS)]


def _mixer_prompt_call(x, sh1, sc1, w_in_b, ln_v_g, ln_v_b, w_s, b_s, lb_h, gnorm_g, layer):
    bsz, t, d = x.shape
    n_t = t // MIX_TILE
    hp = HEADS_PER_STEP
    kern = functools.partial(_mixer_prompt_kernel, layer=layer)
    head_vec = lambda b, i, j: (j, 0, 0)
    return pl.pallas_call(
        kern,
        grid=(bsz, n_t, N_HEADS // hp),
        in_specs=[
            pl.BlockSpec((1, MIX_TILE, d), lambda b, i, j: (b, i, 0)),
            pl.BlockSpec((1, 1, d), lambda b, i, j: (b, 0, 0)),
            pl.BlockSpec((1, 1, d), lambda b, i, j: (b, 0, 0)),
            *_w_in_specs(d, 3),
            pl.BlockSpec((hp, 1, HEAD_DIM), head_vec),
            pl.BlockSpec((hp, 1, HEAD_DIM), head_vec),
            pl.BlockSpec((hp, GMLP_CHUNK, GMLP_CHUNK), head_vec),
            pl.BlockSpec((hp, GMLP_CHUNK, 1), head_vec),
            pl.BlockSpec((hp, lb_h.shape[1], HEAD_DIM), head_vec),
            pl.BlockSpec((hp, 1, HEAD_DIM), head_vec),
        ],
        out_specs=[
            pl.BlockSpec((MIX_TILE, hp * HEAD_DIM), lambda b, i, j: (b * n_t + i, j)),
            pl.BlockSpec((MIX_TILE, hp * HEAD_DIM), lambda b, i, j: (b * n_t + i, j)),
            pl.BlockSpec((1, N_HEADS, HEAD_DIM, HEAD_DIM), lambda b, i, j: (b, 0, 0, 0)),
        ],
        out_shape=[
            jax.ShapeDtypeStruct((bsz * t, N_HEADS * HEAD_DIM), BF16),
            jax.ShapeDtypeStruct((bsz * t, N_HEADS * HEAD_DIM), BF16),
            jax.ShapeDtypeStruct((bsz, N_HEADS, HEAD_DIM, HEAD_DIM), F32),
        ],
        scratch_shapes=[
            pltpu.VMEM((MIX_TILE, d), BF16),
            pltpu.VMEM((MIX_TILE, N_KINDS * hp * HEAD_DIM), F32),
            pltpu.VMEM((N_HEADS, HEAD_DIM, HEAD_DIM), F32),
        ],
        compiler_params=_cparams(("arbitrary", "arbitrary", "arbitrary")),
        name="mixer_prompt",
    )(x, sh1.reshape(bsz, 1, d), sc1.reshape(bsz, 1, d), *([w_in_b] * N_KINDS), ln_v_g, ln_v_b, w_s, b_s, lb_h, gnorm_g)


def _mixer_sample_kernel(x_ref, sh_ref, sc_ref, wu_ref, wv_ref, wq_ref, wf_ref, wi_ref, wg_ref,
                         lvg_ref, lvb_ref, ws_ref, bs_ref, lb_ref, gn_ref, s0_ref,
                         cata_ref, cato_ref, state_ref, vrows_ref, h_scr, proj_scr, *, layer, seq):
    j = pl.program_id(0)
    n_streams = x_ref.shape[0] // seq

    @pl.when(j == 0)
    def _():
        def slab(r, carry):
            rows = pl.ds(pl.multiple_of(r * seq, seq), seq)
            h = _ln_plain(x_ref[rows, :]) * (1.0 + sc_ref[pl.ds(r, 1), :]) + sh_ref[pl.ds(r, 1), :]
            h_scr[rows, :] = h.astype(BF16)
            return carry
        lax.fori_loop(0, n_streams, slab, 0)

    _in_projection(h_scr, (wu_ref, wv_ref, wq_ref, wf_ref, wi_ref, wg_ref), proj_scr)

    chains = []
    for hd in range(HEADS_PER_STEP):
        wm = ws_ref[hd][0:seq, 0:seq].astype(BF16)
        b_col = bs_ref[hd][0:seq, :]
        lbv = _lower_bound(lb_ref, hd, layer)
        for n in range(n_streams):
            rows = slice(n * seq, (n + 1) * seq)
            u, vn = _gmlp_rows(proj_scr[rows, _proj_cols(0, hd)], proj_scr[rows, _proj_cols(1, hd)],
                               lvg_ref[hd], lvb_ref[hd])
            vrows_ref[rows, hd * HEAD_DIM:(hd + 1) * HEAD_DIM] = vn
            sp = _dot(wm, vn.astype(BF16)) + b_col
            cata_ref[rows, hd * HEAD_DIM:(hd + 1) * HEAD_DIM] = (u * sp).astype(BF16)
            chains.append((jnp.transpose(s0_ref[n, hd]), lbv, gn_ref[hd],
                           [tuple(proj_scr[rows, _proj_cols(kind, hd)] for kind in range(2, N_KINDS))]))

    outs, finals = _hgrn_chains(chains, seq)
    for hd in range(HEADS_PER_STEP):
        for n in range(n_streams):
            r0 = n * seq
            m = hd * n_streams + n
            cato_ref[r0:r0 + seq, hd * HEAD_DIM:(hd + 1) * HEAD_DIM] = outs[m][0].astype(BF16)
            state_ref[n, hd] = jnp.transpose(finals[m])


def _mixer_sample_call(x2d, sh1, sc1, w_in_b, ln_v_g, ln_v_b, w_s, b_s, lb_h, gnorm_g, s0, layer, seq):
    rows, d = x2d.shape
    n_streams = rows // seq
    hp = HEADS_PER_STEP
    kern = functools.partial(_mixer_sample_kernel, layer=layer, seq=seq)
    head_vec = lambda j: (j, 0, 0)
    return pl.pallas_call(
        kern,
        grid=(N_HEADS // hp,),
        in_specs=[
            pl.BlockSpec((rows, d), lambda j: (0, 0)),
            pl.BlockSpec((n_streams, d), lambda j: (0, 0)),
            pl.BlockSpec((n_streams, d), lambda j: (0, 0)),
            *_w_in_specs(d, 1),
            pl.BlockSpec((hp, 1, HEAD_DIM), head_vec),
            pl.BlockSpec((hp, 1, HEAD_DIM), head_vec),
            pl.BlockSpec((hp, GMLP_CHUNK, GMLP_CHUNK), head_vec),
            pl.BlockSpec((hp, GMLP_CHUNK, 1), head_vec),
            pl.BlockSpec((hp, lb_h.shape[1], HEAD_DIM), head_vec),
            pl.BlockSpec((hp, 1, HEAD_DIM), head_vec),
            pl.BlockSpec((n_streams, hp, HEAD_DIM, HEAD_DIM), lambda j: (0, j, 0, 0)),
        ],
        out_specs=[
            pl.BlockSpec((rows, hp * HEAD_DIM), lambda j: (0, j)),
            pl.BlockSpec((rows, hp * HEAD_DIM), lambda j: (0, j)),
            pl.BlockSpec((n_streams, hp, HEAD_DIM, HEAD_DIM), lambda j: (0, j, 0, 0)),
            pl.BlockSpec((rows, hp * HEAD_DIM), lambda j: (0, j)),
        ],
        out_shape=[
            jax.ShapeDtypeStruct((rows, N_HEADS * HEAD_DIM), BF16),
            jax.ShapeDtypeStruct((rows, N_HEADS * HEAD_DIM), BF16),
            jax.ShapeDtypeStruct((n_streams, N_HEADS, HEAD_DIM, HEAD_DIM), F32),
            jax.ShapeDtypeStruct((rows, N_HEADS * HEAD_DIM), F32),
        ],
        scratch_shapes=[
            pltpu.VMEM((rows, d), BF16),
            pltpu.VMEM((rows, N_KINDS * hp * HEAD_DIM), F32),
        ],
        compiler_params=_cparams(("arbitrary",)),
        name="mixer_sample",
    )(x2d, sh1, sc1, *([w_in_b] * N_KINDS), ln_v_g, ln_v_b, w_s, b_s, lb_h, gnorm_g, s0)


def _post_kernel(cap_ref, cop_ref, cas_ref, cos_ref, xp_ref, xs_ref, g1_ref, sh2_ref, sc2_ref, wout_ref, l1g_ref, l1b_ref,
                 wr_ref, br_ref, tri_ref,
                 x1_ref, h2_ref, meta_ref, wts_ref, cnt_ref,
                 mix_scr, hhi_scr, hlo_scr, run_scr, *, n_prompt_tiles, alpha):
    i = pl.program_id(0)
    is_s = i >= n_prompt_tiles
    tm = mix_scr.shape[0]

    @pl.when(is_s)
    def _():
        mix_scr[...] = _dot(jnp.concatenate([cas_ref[...], cos_ref[...]], axis=1), wout_ref[...])

    @pl.when(jnp.logical_not(is_s))
    def _():
        mix_scr[...] = _dot(jnp.concatenate([cap_ref[...], cop_ref[...]], axis=1), wout_ref[...])

    @pl.when(i == 0)
    def _():
        run_scr[...] = jnp.zeros_like(run_scr)

    def slab(g, carry):
        rows = pl.ds(pl.multiple_of(g * MOD_GROUP, MOD_GROUP), MOD_GROUP)
        x = jnp.where(is_s, xs_ref[rows, :], xp_ref[rows, :])
        z = alpha * x + g1_ref[pl.ds(g, 1), :] * mix_scr[rows, :]
        x1 = _ln_plain(z) * l1g_ref[...] + l1b_ref[...]
        x1_ref[rows, :] = x1
        h2 = _ln_plain(x1) * (1.0 + sc2_ref[pl.ds(g, 1), :]) + sh2_ref[pl.ds(g, 1), :]
        _store_rows(h2_ref, (), g * MOD_GROUP, h2)
        hi = h2.astype(BF16)
        hhi_scr[rows, :] = hi
        hlo_scr[rows, :] = (h2 - hi.astype(F32)).astype(BF16)
        return carry
    lax.fori_loop(0, tm // MOD_GROUP, slab, 0, unroll=SLAB_UNROLL)

    n_r = N_GROUPS + N_EXPERTS
    s = _dot(hhi_scr[...], wr_ref[...]) + _dot(hlo_scr[...], wr_ref[...])
    logits = s + pltpu.roll(s, LANES - n_r, axis=1) + br_ref[...]

    lane = lax.broadcasted_iota(I32, (tm, LANES), 1)
    lane_f = lane.astype(F32)
    neg = jnp.float32(-jnp.inf)
    big = jnp.float32(LANES)

    def first_lane_of(vals, top):
        return jnp.min(jnp.where(vals == top, lane_f, big), axis=1, keepdims=True)

    gmask = lane < N_GROUPS
    gl = jnp.where(gmask, logits, neg)
    gmax = jnp.max(gl, axis=1, keepdims=True)
    gsel = first_lane_of(gl, gmax)
    p_group = 1.0 / jnp.sum(jnp.exp(gl - gmax), axis=1, keepdims=True)

    e_lo = N_GROUPS + EXPERTS_PER_GROUP * gsel.astype(I32)
    emask = (lane >= e_lo) & (lane < e_lo + EXPERTS_PER_GROUP)
    el = jnp.where(emask, logits, neg)
    t1 = jnp.max(el, axis=1, keepdims=True)
    i1 = first_lane_of(el, t1)
    el2 = jnp.where(lane_f == i1, neg, el)
    t2 = jnp.max(el2, axis=1, keepdims=True)
    i2 = first_lane_of(el2, t2)
    e2 = jnp.exp(t2 - t1)
    den = 1.0 + e2
    w1 = (1.0 / den) * p_group
    w2 = (e2 / den) * p_group

    sel1 = lane_f == i1
    sel2 = lane_f == i2
    onehot = jnp.where(sel1 | sel2, 1.0, 0.0)
    before = _dot(tri_ref[...], onehot.astype(BF16)) + run_scr[...]
    r1 = jnp.sum(jnp.where(sel1, before, 0.0), axis=1, keepdims=True)
    r2 = jnp.sum(jnp.where(sel2, before, 0.0), axis=1, keepdims=True)
    run_scr[...] = run_scr[...] + jnp.sum(onehot, axis=0, keepdims=True)
    cnt_ref[...] = run_scr[...].astype(I32)

    base = jnp.float32(N_GROUPS)
    meta = jnp.where(lane == 0, i1 - base,
                     jnp.where(lane == 1, i2 - base,
                               jnp.where(lane == 2, r1, jnp.where(lane == 3, r2, 0.0))))
    meta_ref[...] = meta.astype(I32)
    wts_ref[...] = jnp.where(lane == 0, w1, jnp.where(lane == 1, w2, 0.0))


def _post_call(cats_p, cats_s, x_p, x_s, g1g, sh2g, sc2g, w_out_b, ln1_g, ln1_b, wr, br, tri, alpha):
    n_p, d = x_p.shape
    n_s = x_s.shape[0]
    d_a = cats_p[0].shape[1]
    n_pt = n_p // TOK_TILE
    n_tiles = n_pt + n_s // TOK_TILE
    n_tok = n_p + n_s
    groups = TOK_TILE // MOD_GROUP
    kern = functools.partial(_post_kernel, n_prompt_tiles=n_pt, alpha=alpha)
    p_idx = lambda i: (jnp.minimum(i, n_pt - 1), 0)
    s_idx = lambda i: (jnp.maximum(i - n_pt, 0), 0)
    tile = lambda i: (i, 0)
    const = lambda i: (0, 0)
    once = pl.Buffered(1)
    return pl.pallas_call(
        kern,
        grid=(n_tiles,),
        in_specs=[
            pl.BlockSpec((TOK_TILE, d_a), p_idx),
            pl.BlockSpec((TOK_TILE, d_a), p_idx),
            pl.BlockSpec((TOK_TILE, d_a), s_idx, pipeline_mode=once),
            pl.BlockSpec((TOK_TILE, d_a), s_idx, pipeline_mode=once),
            pl.BlockSpec((TOK_TILE, d), p_idx),
            pl.BlockSpec((TOK_TILE, d), s_idx, pipeline_mode=once),
            pl.BlockSpec((groups, d), tile),
            pl.BlockSpec((groups, d), tile),
            pl.BlockSpec((groups, d), tile),
            pl.BlockSpec((d, d), const, pipeline_mode=once),
            pl.BlockSpec((1, d), const),
            pl.BlockSpec((1, d), const),
            pl.BlockSpec((d, LANES), const, pipeline_mode=once),
            pl.BlockSpec((1, LANES), const),
            pl.BlockSpec((TOK_TILE, TOK_TILE), const, pipeline_mode=once),
        ],
        out_specs=[
            pl.BlockSpec((TOK_TILE, d), tile),
            pl.BlockSpec((TOK_TILE * _lines_per_row(d), LANES), tile),
            pl.BlockSpec((TOK_TILE, LANES), tile),
            pl.BlockSpec((TOK_TILE, LANES), tile),
            pl.BlockSpec((1, LANES), const),
        ],
        out_shape=[
            jax.ShapeDtypeStruct((n_tok, d), F32),
            jax.ShapeDtypeStruct((n_tok * _lines_per_row(d), LANES), F32),
            jax.ShapeDtypeStruct((n_tok, LANES), I32),
            jax.ShapeDtypeStruct((n_tok, LANES), F32),
            jax.ShapeDtypeStruct((1, LANES), I32),
        ],
        scratch_shapes=[
            pltpu.VMEM((TOK_TILE, d), F32),
            pltpu.VMEM((TOK_TILE, d), BF16),
            pltpu.VMEM((TOK_TILE, d), BF16),
            pltpu.VMEM((1, LANES), F32),
        ],
        compiler_params=_cparams(("arbitrary",)),
        name="post_router",
    )(*cats_p, *cats_s, x_p, x_s, g1g, sh2g, sc2g, w_out_b, ln1_g, ln1_b, wr, br, tri)


def _invert_kernel(slot_ref, pad_ref, dst_ref, *, n_tok, half, n_unused):
    i = pl.program_id(0)
    tm = slot_ref.shape[2] // 2
    pad_per_step = pad_ref.shape[2]
    n_pad = jnp.clip(n_unused - i * pad_per_step, 0, pad_per_step)
    spare = n_unused // 2

    def real(r, carry):
        tok = i * tm + r
        dst_ref[slot_ref[0, 0, 2 * r]] = tok
        dst_ref[slot_ref[0, 0, 2 * r + 1]] = half + tok
        return carry
    lax.fori_loop(0, tm, real, 0, unroll=ROW_DMA_UNROLL)

    def unowned(k, carry):
        p = i * pad_per_step + k
        dst_ref[pad_ref[0, 0, k]] = jnp.where(p < spare, n_tok + p, half + n_tok + p - spare)
        return carry
    lax.fori_loop(0, n_pad, unowned, 0)


def _invert_call(slots3, pads3, n_sorted, n_tok, half, n_unused):
    n_tiles = slots3.shape[0]
    kern = functools.partial(_invert_kernel, n_tok=n_tok, half=half, n_unused=n_unused)
    return pl.pallas_call(
        kern,
        grid=(n_tiles,),
        in_specs=[
            pl.BlockSpec((1, 1, slots3.shape[2]), lambda i: (i, 0, 0), memory_space=pltpu.SMEM),
            pl.BlockSpec((1, 1, pads3.shape[2]), lambda i: (i, 0, 0), memory_space=pltpu.SMEM),
        ],
        out_specs=pl.BlockSpec((n_sorted,), lambda i: (0,), memory_space=pltpu.SMEM),
        out_shape=jax.ShapeDtypeStruct((n_sorted,), I32),
        compiler_params=_cparams(("arbitrary",)),
        name="invert_routing",
    )(slots3, pads3)


def _expert_kernel(te_ref, nt_ref, dst_ref, scur_ref, snext_ref, sahead_ref, h_ref, w1_ref, w3_ref, w2_ref, yp_ref,
                   x_scr, y_scr, w13_scr, w2_scr, gsem, ssem):
    t = pl.program_id(0)
    n_t = pl.num_programs(0)
    d = w1_ref.shape[1]
    d_exp = w1_ref.shape[2]
    lpr = _lines_per_row(d)
    rows = x_scr.shape[1] // lpr
    x_slots = x_scr.shape[0]
    y_slots = y_scr.shape[0]
    xs = lax.rem(t, x_slots)
    ys = lax.rem(t, y_slots)
    prev = te_ref[jnp.maximum(t - 1, 0)]
    changed = (t == 0) | (te_ref[t] != prev)
    valid = t < nt_ref[0]

    def row_lines(r):
        return pl.ds(pl.multiple_of(r * lpr, lpr), lpr)

    def gather_row(s_ref, r, slot):
        return pltpu.make_async_copy(h_ref.at[row_lines(s_ref[0, 0, r])], x_scr.at[slot, row_lines(r)], gsem.at[slot])

    def scatter_row(r, slot):
        return pltpu.make_async_copy(y_scr.at[slot, row_lines(r)], yp_ref.at[row_lines(dst_ref[0, 0, r])],
                                     ssem.at[slot])

    def start_gathers(s_ref, slot):
        def body(q, carry):
            gather_row(s_ref, 2 * q, slot).start(priority=0)
            gather_row(s_ref, 2 * q + 1, slot).start(priority=1)
            return carry
        lax.fori_loop(0, rows // 2, body, 0, unroll=ROW_DMA_UNROLL // 2)

    def wait_tile(scr, sem, slot):
        pltpu.make_async_copy(scr.at[slot], scr.at[slot], sem.at[slot]).wait()

    @pl.when(changed)
    def _():
        w13_scr[:, 0:d_exp] = w1_ref[0].astype(BF16)
        w13_scr[:, d_exp:2 * d_exp] = w3_ref[0].astype(BF16)
        w2_scr[...] = w2_ref[0].astype(BF16)

    @pl.when(t == 0)
    def _():
        start_gathers(scur_ref, 0)
        start_gathers(snext_ref, 1)

    wait_tile(x_scr, gsem, xs)

    @pl.when(t >= y_slots)
    def _():
        wait_tile(y_scr, ssem, ys)

    @pl.when(valid)
    def _():
        h13 = _dot(_load_rows(x_scr, (xs,), 0, rows, d).astype(BF16), w13_scr[...])
        hm = jax.nn.silu(h13[:, 0:d_exp]) * h13[:, d_exp:2 * d_exp]
        _store_rows(y_scr, (ys,), 0, _dot(hm.astype(BF16), w2_scr[...]))

    @pl.when(jnp.logical_not(valid))
    def _():
        y_scr[ys] = jnp.zeros(y_scr.shape[1:], F32)

    def start_scatter(q, carry):
        scatter_row(2 * q, ys).start(priority=0)
        scatter_row(2 * q + 1, ys).start(priority=1)
        return carry
    lax.fori_loop(0, rows // 2, start_scatter, 0, unroll=ROW_DMA_UNROLL // 2)

    @pl.when(t + GATHER_AHEAD < n_t)
    def _():
        start_gathers(sahead_ref, lax.rem(t + GATHER_AHEAD, x_slots))

    @pl.when(t == n_t - 1)
    def _():
        wait_tile(y_scr, ssem, 1 - ys)
        wait_tile(y_scr, ssem, ys)


def _expert_call(tile_expert, n_valid, dst3, src3, h2, w1, w3, w2, n_pair_rows):
    d, d_exp = w1.shape[1], w1.shape[2]
    lpr = _lines_per_row(d)
    n_tiles = dst3.shape[0]
    assert n_tiles > GATHER_AHEAD == 2
    smem_tile = lambda ahead: pl.BlockSpec(
        (1, 1, EXP_TILE), lambda t, te, nt: (jnp.minimum(t + ahead, n_tiles - 1), 0, 0), memory_space=pltpu.SMEM)
    grid_spec = pltpu.PrefetchScalarGridSpec(
        num_scalar_prefetch=2,
        grid=(n_tiles,),
        in_specs=[
            smem_tile(0), smem_tile(0), smem_tile(1), smem_tile(GATHER_AHEAD),
            pl.BlockSpec(memory_space=pl.ANY),
            pl.BlockSpec((1, d, d_exp), lambda t, te, nt: (te[t], 0, 0)),
            pl.BlockSpec((1, d, d_exp), lambda t, te, nt: (te[t], 0, 0)),
            pl.BlockSpec((1, d_exp, d), lambda t, te, nt: (te[t], 0, 0)),
        ],
        out_specs=pl.BlockSpec(memory_space=pl.ANY),
        scratch_shapes=[
            pltpu.VMEM((GATHER_AHEAD + 1, EXP_TILE * lpr, LANES), F32),
            pltpu.VMEM((2, EXP_TILE * lpr, LANES), F32),
            pltpu.VMEM((d, 2 * d_exp), BF16),
            pltpu.VMEM((d_exp, d), BF16),
            pltpu.SemaphoreType.DMA((GATHER_AHEAD + 1,)),
            pltpu.SemaphoreType.DMA((2,)),
        ],
    )
    return pl.pallas_call(
        _expert_kernel,
        grid_spec=grid_spec,
        out_shape=jax.ShapeDtypeStruct((n_pair_rows * lpr, LANES), F32),
        compiler_params=_cparams(("arbitrary",)),
        name="experts",
    )(tile_expert, n_valid, dst3, src3, src3, src3, h2, w1, w3, w2)


def _combine_kernel(x1_ref, y0_ref, y1_ref, wts_ref, g2_ref, l2g_ref, l2b_ref, outp_ref, outs_ref,
                    *, n_prompt_tiles, alpha):
    i = pl.program_id(0)
    is_s = i >= n_prompt_tiles
    tm = x1_ref.shape[0]

    def slab(g, carry, out_ref):
        rows = pl.ds(pl.multiple_of(g * MOD_GROUP, MOD_GROUP), MOD_GROUP)
        w = wts_ref[rows, :]
        d = x1_ref.shape[1]
        moe = (w[:, 0:1] * _load_rows(y0_ref, (), g * MOD_GROUP, MOD_GROUP, d)
               + w[:, 1:2] * _load_rows(y1_ref, (), g * MOD_GROUP, MOD_GROUP, d))
        z = alpha * x1_ref[rows, :] + g2_ref[pl.ds(g, 1), :] * moe
        out_ref[rows, :] = _ln_plain(z) * l2g_ref[...] + l2b_ref[...]
        return carry

    @pl.when(is_s)
    def _():
        lax.fori_loop(0, tm // MOD_GROUP, functools.partial(slab, out_ref=outs_ref), 0, unroll=SLAB_UNROLL)

    @pl.when(jnp.logical_not(is_s))
    def _():
        lax.fori_loop(0, tm // MOD_GROUP, functools.partial(slab, out_ref=outp_ref), 0, unroll=SLAB_UNROLL)


def _combine_call(x1, ypair, wts, g2g, ln2_g, ln2_b, n_p, half, alpha):
    n_tok, d = x1.shape
    n_tiles = n_tok // TOK_TILE
    n_pt = n_p // TOK_TILE
    groups = TOK_TILE // MOD_GROUP
    second = half // TOK_TILE
    kern = functools.partial(_combine_kernel, n_prompt_tiles=n_pt, alpha=alpha)
    return pl.pallas_call(
        kern,
        grid=(n_tiles,),
        in_specs=[
            pl.BlockSpec((TOK_TILE, d), lambda i: (i, 0)),
            pl.BlockSpec((TOK_TILE * _lines_per_row(d), LANES), lambda i: (i, 0)),
            pl.BlockSpec((TOK_TILE * _lines_per_row(d), LANES), lambda i: (second + i, 0)),
            pl.BlockSpec((TOK_TILE, LANES), lambda i: (i, 0)),
            pl.BlockSpec((groups, d), lambda i: (i, 0)),
            pl.BlockSpec((1, d), lambda i: (0, 0)),
            pl.BlockSpec((1, d), lambda i: (0, 0)),
        ],
        out_specs=[
            pl.BlockSpec((TOK_TILE, d), lambda i: (jnp.minimum(i, n_pt - 1), 0)),
            pl.BlockSpec((TOK_TILE, d), lambda i: (jnp.maximum(i - n_pt, 0), 0)),
        ],
        out_shape=[
            jax.ShapeDtypeStruct((n_p, d), F32),
            jax.ShapeDtypeStruct((n_tok - n_p, d), F32),
        ],
        compiler_params=_cparams(("arbitrary",)),
        name="combine",
    )(x1, ypair, ypair, wts, g2g, ln2_g, ln2_b)


def _routing_tables(meta, cnt_row, n_sorted):
    n_tok = meta.shape[0]
    experts = meta[:, 0:2]
    ranks = meta[:, 2:4]
    cnt = cnt_row[0, N_GROUPS:N_GROUPS + N_EXPERTS]
    padded = ((cnt + EXP_TILE - 1) // EXP_TILE) * EXP_TILE
    ends = jnp.cumsum(padded)
    offs = ends - padded
    slots = offs[experts] + ranks
    total = ends[-1]
    n_tiles = n_sorted // EXP_TILE
    tile_ids = jnp.arange(n_tiles, dtype=I32)
    tile_expert = jnp.minimum(
        jnp.sum((tile_ids[:, None] >= (ends // EXP_TILE)[None, :]).astype(I32), axis=1), N_EXPERTS - 1)
    n_valid = (total // EXP_TILE).reshape(1).astype(I32)
    n_unused = n_sorted - 2 * n_tok
    starts = jnp.concatenate([offs + cnt, total.reshape(1)])
    lens = jnp.concatenate([padded - cnt, (n_sorted - total).reshape(1)])
    lens_end = jnp.cumsum(lens)
    k = jnp.arange(n_unused, dtype=I32)
    in_seg = k[:, None] >= lens_end[None, :]
    seg_start = jnp.max(jnp.where(in_seg, lens_end[None, :], 0), axis=1)
    seg = jnp.sum(in_seg.astype(I32), axis=1)
    one_hot = (seg[:, None] == jnp.arange(starts.shape[0], dtype=I32)[None, :]).astype(I32)
    pads = (jnp.sum(one_hot * starts[None, :], axis=1) + (k - seg_start)).astype(I32)
    return slots.astype(I32), tile_expert, n_valid, pads


def _layer(layer, n_layers, xp, xs, s0_l, c_all, p):
    (w_ada, b_ada, w_in, ln_v_g, ln_v_b, w_s, b_s, hgrn_lb, gnorm_g, w_out, ln1_g, ln1_b,
     w_rg, b_rg, w_re, b_re, w1, w3, w2, ln2_g, ln2_b) = p
    bsz, t, d = xp.shape
    n_streams, seq, _ = xs.shape
    alpha = float((2.0 * n_layers) ** 0.25)
    n_p = bsz * t
    n_s = n_streams * seq
    n_tok = n_p + n_s

    n_c = c_all.shape[0]
    c_pad = jnp.pad(c_all, ((0, (-n_c) % 8), (0, 0)))
    mod = _ada_call(c_pad, w_ada, b_ada)[:n_c]
    sh1, sc1, g1, sh2, sc2, g2 = [mod[:, m * d:(m + 1) * d] for m in range(6)]

    w_in_b = w_in.astype(BF16)
    w_out_b = w_out.astype(BF16)
    lvg = ln_v_g.reshape(N_HEADS, 1, HEAD_DIM)
    lvb = ln_v_b.reshape(N_HEADS, 1, HEAD_DIM)
    b_s3 = b_s.reshape(N_HEADS, GMLP_CHUNK, 1)
    lb_h = jnp.transpose(hgrn_lb.reshape(hgrn_lb.shape[0], N_HEADS, HEAD_DIM), (1, 0, 2))
    gn = gnorm_g.reshape(N_HEADS, 1, HEAD_DIM)

    ca_p, co_p, state_p = _mixer_prompt_call(xp, sh1[:bsz], sc1[:bsz], w_in_b, lvg, lvb, w_s, b_s3, lb_h, gn, layer)
    ca_s, co_s, state_s, vrows = _mixer_sample_call(xs.reshape(n_s, d), sh1[bsz:], sc1[bsz:], w_in_b, lvg, lvb, w_s,
                                               b_s3, lb_h, gn, s0_l, layer, seq)

    group_stream = jnp.concatenate([
        jnp.repeat(jnp.arange(bsz, dtype=I32), t // MOD_GROUP),
        bsz + jnp.repeat(jnp.arange(n_streams, dtype=I32), seq // MOD_GROUP)])
    g1g, sh2g, sc2g, g2g = [m[group_stream] for m in (g1, sh2, sc2, g2)]

    wr = jnp.concatenate([w_rg, w_re], axis=1)
    wr_hi = wr.astype(BF16)
    wr_lo = (wr - wr_hi.astype(F32)).astype(BF16)
    n_r = wr.shape[1]
    wr_cat = jnp.concatenate([wr_hi, wr_lo, jnp.zeros((d, LANES - 2 * n_r), BF16)], axis=1)
    br = jnp.pad(jnp.concatenate([b_rg, b_re]), (0, LANES - n_r)).reshape(1, LANES)
    tri = jnp.tril(jnp.ones((TOK_TILE, TOK_TILE), F32), -1).astype(BF16)

    x1, h2, meta, wts, cnt_row = _post_call((ca_p, co_p), (ca_s, co_s), xp.reshape(n_p, d), xs.reshape(n_s, d),
                                            g1g, sh2g, sc2g, w_out_b, ln1_g.reshape(1, d), ln1_b.reshape(1, d),
                                            wr_cat, br, tri, alpha)

    n_sorted = 2 * n_tok + N_EXPERTS * EXP_TILE
    slots, tile_expert, n_valid, pads = _routing_tables(meta, cnt_row, n_sorted)
    n_tiles = n_tok // TOK_TILE
    slots3 = slots.reshape(n_tiles, 1, 2 * TOK_TILE)
    n_unused = n_sorted - 2 * n_tok
    pad_per_step = -(-n_unused // n_tiles)
    pads3 = jnp.pad(pads, (0, n_tiles * pad_per_step - n_unused)).reshape(n_tiles, 1, pad_per_step)

    half = n_tok + n_unused // 2
    assert half % TOK_TILE == 0
    dst = _invert_call(slots3, pads3, n_sorted, n_tok, half, n_unused)
    src = jnp.minimum(dst - jnp.where(dst >= half, half, 0), n_tok - 1)
    dst3 = dst.reshape(n_sorted // EXP_TILE, 1, EXP_TILE)
    src3 = src.reshape(n_sorted // EXP_TILE, 1, EXP_TILE)
    ypair = _expert_call(tile_expert, n_valid, dst3, src3, h2, w1, w3, w2, 2 * half)
    yp, ys_out = _combine_call(x1, ypair, wts, g2g, ln2_g.reshape(1, d), ln2_b.reshape(1, d), n_p, half, alpha)

    v_rows = vrows.reshape(n_streams, seq, N_HEADS, HEAD_DIM)
    return yp.reshape(bsz, t, d), ys_out.reshape(n_streams, seq, d), state_p, state_s, v_rows


def kernel(x_prompt, x_sample, state_hgrn, c_prompt, c_sample, w_ada, b_ada, w_in, ln_v_g, ln_v_b, w_s, b_s, hgrn_lb, gnorm_g, w_out, ln1_g, ln1_b, w_router_g, b_router_g, w_router_e, b_router_e, w1, w3, w2, ln2_g, ln2_b):
    n_layers = w_ada.shape[0]
    assert x_prompt.shape[1] % MIX_TILE == 0 and x_prompt.shape[2] == 2 * N_HEADS * HEAD_DIM
    assert x_sample.shape[1] % MOD_GROUP == 0 and x_sample.shape[1] <= SUB_CHUNK
    assert (x_sample.shape[0] * x_sample.shape[1]) % TOK_TILE == 0
    c_all = jnp.concatenate([c_prompt, c_sample], axis=0)
    xp, xs = x_prompt, x_sample
    sp_list, ss_list, vs_list = [], [], []
    for l in range(n_layers):
        p = (w_ada[l], b_ada[l], w_in[l], ln_v_g[l], ln_v_b[l], w_s[l], b_s[l], hgrn_lb, gnorm_g[l], w_out[l],
             ln1_g[l], ln1_b[l], w_router_g[l], b_router_g[l], w_router_e[l], b_router_e[l],
             w1[l], w3[l], w2[l], ln2_g[l], ln2_b[l])
        xp, xs, sp, ss, vs = _layer(l, n_layers, xp, xs, state_hgrn[l], c_all, p)
        sp_list.append(sp.astype(state_hgrn.dtype))
        ss_list.append(ss.astype(state_hgrn.dtype))
        vs_list.append(vs)
    return (xp, xs, jnp.stack(sp_list, axis=0), jnp.stack(ss_list, axis=0), jnp.stack(vs_list, axis=0))
```

```python
import functools

import jax
import jax.numpy as jnp
from jax import lax
from jax.experimental import pallas as pl
from jax.experimental.pallas import tpu as pltpu

F32 = jnp.float32
BF16 = jnp.bfloat16
I32 = jnp.int32

N_HEADS = 8
HEAD_DIM = 128
GMLP_CHUNK = 128
SUB_CHUNK = 64
N_GROUPS = 4
EXPERTS_PER_GROUP = 8
N_EXPERTS = N_GROUPS * EXPERTS_PER_GROUP
LN_EPS = 1e-5
N_KINDS = 6

LANES = 128
MIX_TILE = 512
HEADS_PER_STEP = 2
HGRN_BLOCK = 64
DIAG_BLOCK = 16
TOK_TILE = 512
MOD_GROUP = 32
SLAB_UNROLL = 4
EXP_TILE = 256
ROW_DMA_UNROLL = 8
GATHER_AHEAD = 2
ADA_TILE = 1024
VMEM_LIMIT = 56 * 1024 * 1024


def _cparams(sem):
    return pltpu.CompilerParams(dimension_semantics=sem, vmem_limit_bytes=VMEM_LIMIT)


def _ln_plain(x):
    mu = jnp.mean(x, axis=-1, keepdims=True)
    xc = x - mu
    var = jnp.mean(xc * xc, axis=-1, keepdims=True)
    return xc * lax.rsqrt(var + LN_EPS)


def _dot(a, b):
    return jnp.dot(a, b, preferred_element_type=F32)


def _dot_nt(a, b):
    return lax.dot_general(a, b, (((1,), (1,)), ((), ())), preferred_element_type=F32)


def _lines_per_row(d):
    return d // LANES


def _load_rows(ref, lead, row0, n_rows, d):
    lpr = _lines_per_row(d)
    parts = [ref[lead + (pl.ds(row0 * lpr + c, n_rows, stride=lpr), slice(None))] for c in range(lpr)]
    return jnp.concatenate(parts, axis=1)


def _store_rows(ref, lead, row0, val):
    n_rows, d = val.shape
    lpr = _lines_per_row(d)
    for c in range(lpr):
        ref[lead + (pl.ds(row0 * lpr + c, n_rows, stride=lpr), slice(None))] = val[:, c * LANES:(c + 1) * LANES]


def _ada_kernel(c_ref, w_ref, b_ref, o_ref):
    s = jax.nn.silu(c_ref[...]).astype(BF16)
    o_ref[...] = _dot(s, w_ref[...].astype(BF16)) + b_ref[...]


def _ada_call(c_pad, w_ada, b_ada):
    rows, d = c_pad.shape
    n_out = w_ada.shape[1]
    return pl.pallas_call(
        _ada_kernel,
        grid=(n_out // ADA_TILE,),
        in_specs=[
            pl.BlockSpec((rows, d), lambda n: (0, 0)),
            pl.BlockSpec((d, ADA_TILE), lambda n: (0, n)),
            pl.BlockSpec((1, ADA_TILE), lambda n: (0, n)),
        ],
        out_specs=pl.BlockSpec((rows, ADA_TILE), lambda n: (0, n)),
        out_shape=jax.ShapeDtypeStruct((rows, n_out), F32),
        compiler_params=_cparams(("arbitrary",)),
        name="adaln",
    )(c_pad, w_ada, b_ada.reshape(1, n_out))


def _lower_bound(lb_ref, hd, layer):
    raw = lb_ref[hd]
    m = jnp.max(raw, axis=0, keepdims=True)
    e = jnp.exp(raw - m)
    p = e / jnp.sum(e, axis=0, keepdims=True)
    return jnp.sum(p[: layer + 1], axis=0, keepdims=True)


def _gmlp_rows(u_pre, v_pre, ln_g, ln_b):
    u = jax.nn.gelu(u_pre)
    v = jax.nn.gelu(v_pre)
    mu = jnp.mean(v, axis=-1, keepdims=True)
    vc = v - mu
    var = jnp.mean(vc * vc, axis=-1, keepdims=True)
    vn = vc * lax.rsqrt(var + LN_EPS) * ln_g + ln_b
    return u, vn


def _row_bcast(a, row, n):
    return jnp.broadcast_to(a[row:row + 1, :], (n, a.shape[1]))


def _block_id(idx, size):
    return lax.shift_right_logical(idx, I32(size.bit_length() - 1))


def _hgrn_masks(c):
    row = lax.broadcasted_iota(I32, (c, c), 0)
    col = lax.broadcasted_iota(I32, (c, c), 1)
    masks = []
    half = c // 2
    while half >= DIAG_BLOCK:
        span = 2 * half
        same = _block_id(row, span) == _block_id(col, span)
        masks.append(same & ((row & (span - 1)) >= half) & ((col & (span - 1)) < half))
        half //= 2
    diag = (_block_id(row, DIAG_BLOCK) == _block_id(col, DIAG_BLOCK)) & (col <= row)
    return masks, diag


def _tri_ones(c):
    row = lax.broadcasted_iota(I32, (c, c), 0)
    col = lax.broadcasted_iota(I32, (c, c), 1)
    return jnp.where(col <= row, 1.0, 0.0).astype(BF16)


def _hgrn_chains(chains, c):
    tri = _tri_ones(c)
    masks, diag_mask = _hgrn_masks(c)
    units = []
    for st0, lbv, gn, blocks in chains:
        for q_pre, f_pre, i_pre, g_pre in blocks:
            q = jax.nn.silu(q_pre)
            fg = lbv + (1.0 - lbv) * jax.nn.sigmoid(f_pre)
            logf = jnp.log(fg)
            hi = logf.astype(BF16)
            lo = (logf - hi.astype(F32)).astype(BF16)
            units.append(dict(q=q, k=1.0 - fg, hilo=jnp.concatenate([hi, lo], axis=1),
                              v=i_pre, g=g_pre, gn=gn))

    for u in units:
        p = _dot(tri, u["hilo"])
        u["a"] = p[:, 0:HEAD_DIM] + p[:, HEAD_DIM:2 * HEAD_DIM]

    for u in units:
        q, k, a = u["q"], u["k"], u["a"]
        parts = []
        half = c // 2
        level = 0
        while half >= DIAG_BLOCK:
            span = 2 * half
            ref = jnp.concatenate([_row_bcast(a, b * span + half, span) for b in range(c // span)], axis=0)
            qs = (q * jnp.exp(jnp.minimum(a - ref, 0.0))).astype(BF16)
            ks = (k * jnp.exp(jnp.minimum(ref - a, 0.0))).astype(BF16)
            parts.append((masks[level], _dot_nt(qs, ks)))
            half //= 2
            level += 1
        ref = jnp.concatenate([_row_bcast(a, b * DIAG_BLOCK, DIAG_BLOCK) for b in range(c // DIAG_BLOCK)], axis=0)
        qd = (q * jnp.exp(a - ref)).astype(BF16)
        kd = (k * jnp.exp(jnp.minimum(ref - a, 80.0))).astype(BF16)
        parts.append((diag_mask, _dot_nt(qd, kd)))
        u["parts"] = parts
        a_last = a[c - 1:c, :]
        u["decay"] = jnp.exp(a_last)
        kl = (k * jnp.exp(a_last - a)).astype(BF16)
        v_t = jnp.transpose(u["v"]).astype(BF16)
        u["upd"] = _dot(v_t, kl)
        u["qe"] = (q * jnp.exp(a)).astype(BF16)

    finals = []
    n = 0
    for st0, lbv, gn, blocks in chains:
        st = st0
        for _ in blocks:
            units[n]["st_in"] = st.astype(BF16)
            st = st * units[n]["decay"] + units[n]["upd"]
            n += 1
        finals.append(st)

    outs = []
    n = 0
    for st0, lbv, gn, blocks in chains:
        chain_out = []
        for _ in blocks:
            u = units[n]
            scores = None
            for mask, part in u["parts"]:
                part = jnp.where(mask, part, 0.0)
                scores = part if scores is None else scores + part
            o = _dot(scores.astype(BF16), u["v"].astype(BF16)) + _dot_nt(u["qe"], u["st_in"])
            o = o * lax.rsqrt(jnp.mean(o * o, axis=-1, keepdims=True) + LN_EPS) * u["gn"]
            chain_out.append(o * jax.nn.silu(u["g"]))
            n += 1
        outs.append(chain_out)
    return outs, finals


def _proj_cols(kind, hd):
    c0 = (kind * HEADS_PER_STEP + hd) * HEAD_DIM
    return slice(c0, c0 + HEAD_DIM)


def _in_projection(h_scr, w_refs, proj_scr):
    width = HEADS_PER_STEP * HEAD_DIM
    for kind, w_ref in enumerate(w_refs):
        proj_scr[:, kind * width:(kind + 1) * width] = _dot(h_scr[...], w_ref[...])


def _mixer_prompt_kernel(x_ref, sh_ref, sc_ref, wu_ref, wv_ref, wq_ref, wf_ref, wi_ref, wg_ref,
                         lvg_ref, lvb_ref, ws_ref, bs_ref, lb_ref, gn_ref,
                         cata_ref, cato_ref, state_ref, h_scr, proj_scr, st_scr, *, layer):
    i = pl.program_id(1)
    j = pl.program_id(2)
    tm = h_scr.shape[0]

    @pl.when(j == 0)
    def _():
        def slab(r, carry):
            rows = pl.ds(pl.multiple_of(r * HGRN_BLOCK, HGRN_BLOCK), HGRN_BLOCK)
            h = _ln_plain(x_ref[0, rows, :]) * (1.0 + sc_ref[0]) + sh_ref[0]
            h_scr[rows, :] = h.astype(BF16)
            return carry
        lax.fori_loop(0, tm // HGRN_BLOCK, slab, 0)

    @pl.when((i == 0) & (j == 0))
    def _():
        st_scr[...] = jnp.zeros_like(st_scr)

    _in_projection(h_scr, (wu_ref, wv_ref, wq_ref, wf_ref, wi_ref, wg_ref), proj_scr)

    row = lax.broadcasted_iota(I32, (GMLP_CHUNK, GMLP_CHUNK), 0)
    col = lax.broadcasted_iota(I32, (GMLP_CHUNK, GMLP_CHUNK), 1)
    causal = _block_id(row, SUB_CHUNK) >= _block_id(col, SUB_CHUNK)
    for hd in range(HEADS_PER_STEP):
        wm = jnp.where(causal, ws_ref[hd], 0.0).astype(BF16)
        b_col = bs_ref[hd]
        for c in range(tm // GMLP_CHUNK):
            rows = slice(c * GMLP_CHUNK, (c + 1) * GMLP_CHUNK)
            u, vn = _gmlp_rows(proj_scr[rows, _proj_cols(0, hd)], proj_scr[rows, _proj_cols(1, hd)],
                               lvg_ref[hd], lvb_ref[hd])
            sp = _dot(wm, vn.astype(BF16)) + b_col
            cata_ref[rows, hd * HEAD_DIM:(hd + 1) * HEAD_DIM] = (u * sp).astype(BF16)

    chains = []
    for hd in range(HEADS_PER_STEP):
        blocks = [tuple(proj_scr[c * HGRN_BLOCK:(c + 1) * HGRN_BLOCK, _proj_cols(kind, hd)] for kind in range(2, N_KINDS))
                  for c in range(tm // HGRN_BLOCK)]
        chains.append((st_scr[HEADS_PER_STEP * j + hd], _lower_bound(lb_ref, hd, layer), gn_ref[hd], blocks))
    outs, finals = _hgrn_chains(chains, HGRN_BLOCK)
    for hd in range(HEADS_PER_STEP):
        for c, out in enumerate(outs[hd]):
            cato_ref[c * HGRN_BLOCK:(c + 1) * HGRN_BLOCK, hd * HEAD_DIM:(hd + 1) * HEAD_DIM] = out.astype(BF16)
        st_scr[HEADS_PER_STEP * j + hd] = finals[hd]
        state_ref[0, HEADS_PER_STEP * j + hd] = jnp.transpose(finals[hd])


def _w_in_specs(d, n_grid_axes):
    width = HEADS_PER_STEP * HEAD_DIM
    blocks_per_kind = N_HEADS // HEADS_PER_STEP

    def spec(kind):
        if n_grid_axes == 3:
            return pl.BlockSpec((d, width), lambda b, i, j: (0, kind * blocks_per_kind + j))
        return pl.BlockSpec((d, width), lambda j: (0, kind * blocks_per_kind + j))
    return [spec(kind) for kind in range(N_KINDS)]


def _mixer_prompt_call(x, sh1, sc1, w_in_b, ln_v_g, ln_v_b, w_s, b_s, lb_h, gnorm_g, layer):
    bsz, t, d = x.shape
    n_t = t // MIX_TILE
    hp = HEADS_PER_STEP
    kern = functools.partial(_mixer_prompt_kernel, layer=layer)
    head_vec = lambda b, i, j: (j, 0, 0)
    return pl.pallas_call(
        kern,
        grid=(bsz, n_t, N_HEADS // hp),
        in_specs=[
            pl.BlockSpec((1, MIX_TILE, d), lambda b, i, j: (b, i, 0)),
            pl.BlockSpec((1, 1, d), lambda b, i, j: (b, 0, 0)),
            pl.BlockSpec((1, 1, d), lambda b, i, j: (b, 0, 0)),
            *_w_in_specs(d, 3),
            pl.BlockSpec((hp, 1, HEAD_DIM), head_vec),
            pl.BlockSpec((hp, 1, HEAD_DIM), head_vec),
            pl.BlockSpec((hp, GMLP_CHUNK, GMLP_CHUNK), head_vec),
            pl.BlockSpec((hp, GMLP_CHUNK, 1), head_vec),
            pl.BlockSpec((hp, lb_h.shape[1], HEAD_DIM), head_vec),
            pl.BlockSpec((hp, 1, HEAD_DIM), head_vec),
        ],
        out_specs=[
            pl.BlockSpec((MIX_TILE, hp * HEAD_DIM), lambda b, i, j: (b * n_t + i, j)),
            pl.BlockSpec((MIX_TILE, hp * HEAD_DIM), lambda b, i, j: (b * n_t + i, j)),
            pl.BlockSpec((1, N_HEADS, HEAD_DIM, HEAD_DIM), lambda b, i, j: (b, 0, 0, 0)),
        ],
        out_shape=[
            jax.ShapeDtypeStruct((bsz * t, N_HEADS * HEAD_DIM), BF16),
            jax.ShapeDtypeStruct((bsz * t, N_HEADS * HEAD_DIM), BF16),
            jax.ShapeDtypeStruct((bsz, N_HEADS, HEAD_DIM, HEAD_DIM), F32),
        ],
        scratch_shapes=[
            pltpu.VMEM((MIX_TILE, d), BF16),
            pltpu.VMEM((MIX_TILE, N_KINDS * hp * HEAD_DIM), F32),
            pltpu.VMEM((N_HEADS, HEAD_DIM, HEAD_DIM), F32),
        ],
        compiler_params=_cparams(("arbitrary", "arbitrary", "arbitrary")),
        name="mixer_prompt",
    )(x, sh1.reshape(bsz, 1, d), sc1.reshape(bsz, 1, d), *([w_in_b] * N_KINDS), ln_v_g, ln_v_b, w_s, b_s, lb_h, gnorm_g)


def _mixer_sample_kernel(x_ref, sh_ref, sc_ref, wu_ref, wv_ref, wq_ref, wf_ref, wi_ref, wg_ref,
                         lvg_ref, lvb_ref, ws_ref, bs_ref, lb_ref, gn_ref, s0_ref,
                         cata_ref, cato_ref, state_ref, vrows_ref, h_scr, proj_scr, *, layer, seq):
    j = pl.program_id(0)
    n_streams = x_ref.shape[0] // seq

    @pl.when(j == 0)
    def _():
        def slab(r, carry):
            rows = pl.ds(pl.multiple_of(r * seq, seq), seq)
            h = _ln_plain(x_ref[rows, :]) * (1.0 + sc_ref[pl.ds(r, 1), :]) + sh_ref[pl.ds(r, 1), :]
            h_scr[rows, :] = h.astype(BF16)
            return carry
        lax.fori_loop(0, n_streams, slab, 0)

    _in_projection(h_scr, (wu_ref, wv_ref, wq_ref, wf_ref, wi_ref, wg_ref), proj_scr)

    chains = []
    for hd in range(HEADS_PER_STEP):
        wm = ws_ref[hd][0:seq, 0:seq].astype(BF16)
        b_col = bs_ref[hd][0:seq, :]
        lbv = _lower_bound(lb_ref, hd, layer)
        for n in range(n_streams):
            rows = slice(n * seq, (n + 1) * seq)
            u, vn = _gmlp_rows(proj_scr[rows, _proj_cols(0, hd)], proj_scr[rows, _proj_cols(1, hd)],
                               lvg_ref[hd], lvb_ref[hd])
            vrows_ref[rows, hd * HEAD_DIM:(hd + 1) * HEAD_DIM] = vn
            sp = _dot(wm, vn.astype(BF16)) + b_col
            cata_ref[rows, hd * HEAD_DIM:(hd + 1) * HEAD_DIM] = (u * sp).astype(BF16)
            chains.append((jnp.transpose(s0_ref[n, hd]), lbv, gn_ref[hd],
                           [tuple(proj_scr[rows, _proj_cols(kind, hd)] for kind in range(2, N_KINDS))]))

    outs, finals = _hgrn_chains(chains, seq)
    for hd in range(HEADS_PER_STEP):
        for n in range(n_streams):
            r0 = n * seq
            m = hd * n_streams + n
            cato_ref[r0:r0 + seq, hd * HEAD_DIM:(hd + 1) * HEAD_DIM] = outs[m][0].astype(BF16)
            state_ref[n, hd] = jnp.transpose(finals[m])


def _mixer_sample_call(x2d, sh1, sc1, w_in_b, ln_v_g, ln_v_b, w_s, b_s, lb_h, gnorm_g, s0, layer, seq):
    rows, d = x2d.shape
    n_streams = rows // seq
    hp = HEADS_PER_STEP
    kern = functools.partial(_mixer_sample_kernel, layer=layer, seq=seq)
    head_vec = lambda j: (j, 0, 0)
    return pl.pallas_call(
        kern,
        grid=(N_HEADS // hp,),
        in_specs=[
            pl.BlockSpec((rows, d), lambda j: (0, 0)),
            pl.BlockSpec((n_streams, d), lambda j: (0, 0)),
            pl.BlockSpec((n_streams, d), lambda j: (0, 0)),
            *_w_in_specs(d, 1),
            pl.BlockSpec((hp, 1, HEAD_DIM), head_vec),
            pl.BlockSpec((hp, 1, HEAD_DIM), head_vec),
            pl.BlockSpec((hp, GMLP_CHUNK, GMLP_CHUNK), head_vec),
            pl.BlockSpec((hp, GMLP_CHUNK, 1), head_vec),
            pl.BlockSpec((hp, lb_h.shape[1], HEAD_DIM), head_vec),
            pl.BlockSpec((hp, 1, HEAD_DIM), head_vec),
            pl.BlockSpec((n_streams, hp, HEAD_DIM, HEAD_DIM), lambda j: (0, j, 0, 0)),
        ],
        out_specs=[
            pl.BlockSpec((rows, hp * HEAD_DIM), lambda j: (0, j)),
            pl.BlockSpec((rows, hp * HEAD_DIM), lambda j: (0, j)),
            pl.BlockSpec((n_streams, hp, HEAD_DIM, HEAD_DIM), lambda j: (0, j, 0, 0)),
            pl.BlockSpec((rows, hp * HEAD_DIM), lambda j: (0, j)),
        ],
        out_shape=[
            jax.ShapeDtypeStruct((rows, N_HEADS * HEAD_DIM), BF16),
            jax.ShapeDtypeStruct((rows, N_HEADS * HEAD_DIM), BF16),
            jax.ShapeDtypeStruct((n_streams, N_HEADS, HEAD_DIM, HEAD_DIM), F32),
            jax.ShapeDtypeStruct((rows, N_HEADS * HEAD_DIM), F32),
        ],
        scratch_shapes=[
            pltpu.VMEM((rows, d), BF16),
            pltpu.VMEM((rows, N_KINDS * hp * HEAD_DIM), F32),
        ],
        compiler_params=_cparams(("arbitrary",)),
        name="mixer_sample",
    )(x2d, sh1, sc1, *([w_in_b] * N_KINDS), ln_v_g, ln_v_b, w_s, b_s, lb_h, gnorm_g, s0)


def _post_kernel(cap_ref, cop_ref, cas_ref, cos_ref, xp_ref, xs_ref, g1_ref, sh2_ref, sc2_ref, wout_ref, l1g_ref, l1b_ref,
                 wr_ref, br_ref, tri_ref,
                 x1_ref, h2_ref, meta_ref, wts_ref, cnt_ref,
                 mix_scr, hhi_scr, hlo_scr, run_scr, *, n_prompt_tiles, alpha):
    i = pl.program_id(0)
    is_s = i >= n_prompt_tiles
    tm = mix_scr.shape[0]

    @pl.when(is_s)
    def _():
        mix_scr[...] = _dot(jnp.concatenate([cas_ref[...], cos_ref[...]], axis=1), wout_ref[...])

    @pl.when(jnp.logical_not(is_s))
    def _():
        mix_scr[...] = _dot(jnp.concatenate([cap_ref[...], cop_ref[...]], axis=1), wout_ref[...])

    @pl.when(i == 0)
    def _():
        run_scr[...] = jnp.zeros_like(run_scr)

    def slab(g, carry):
        rows = pl.ds(pl.multiple_of(g * MOD_GROUP, MOD_GROUP), MOD_GROUP)
        x = jnp.where(is_s, xs_ref[rows, :], xp_ref[rows, :])
        z = alpha * x + g1_ref[pl.ds(g, 1), :] * mix_scr[rows, :]
        x1 = _ln_plain(z) * l1g_ref[...] + l1b_ref[...]
        x1_ref[rows, :] = x1
        h2 = _ln_plain(x1) * (1.0 + sc2_ref[pl.ds(g, 1), :]) + sh2_ref[pl.ds(g, 1), :]
        _store_rows(h2_ref, (), g * MOD_GROUP, h2)
        hi = h2.astype(BF16)
        hhi_scr[rows, :] = hi
        hlo_scr[rows, :] = (h2 - hi.astype(F32)).astype(BF16)
        return carry
    lax.fori_loop(0, tm // MOD_GROUP, slab, 0, unroll=SLAB_UNROLL)

    n_r = N_GROUPS + N_EXPERTS
    s = _dot(hhi_scr[...], wr_ref[...]) + _dot(hlo_scr[...], wr_ref[...])
    logits = s + pltpu.roll(s, LANES - n_r, axis=1) + br_ref[...]

    lane = lax.broadcasted_iota(I32, (tm, LANES), 1)
    lane_f = lane.astype(F32)
    neg = jnp.float32(-jnp.inf)
    big = jnp.float32(LANES)

    def first_lane_of(vals, top):
        return jnp.min(jnp.where(vals == top, lane_f, big), axis=1, keepdims=True)

    gmask = lane < N_GROUPS
    gl = jnp.where(gmask, logits, neg)
    gmax = jnp.max(gl, axis=1, keepdims=True)
    gsel = first_lane_of(gl, gmax)
    p_group = 1.0 / jnp.sum(jnp.exp(gl - gmax), axis=1, keepdims=True)

    e_lo = N_GROUPS + EXPERTS_PER_GROUP * gsel.astype(I32)
    emask = (lane >= e_lo) & (lane < e_lo + EXPERTS_PER_GROUP)
    el = jnp.where(emask, logits, neg)
    t1 = jnp.max(el, axis=1, keepdims=True)
    i1 = first_lane_of(el, t1)
    el2 = jnp.where(lane_f == i1, neg, el)
    t2 = jnp.max(el2, axis=1, keepdims=True)
    i2 = first_lane_of(el2, t2)
    e2 = jnp.exp(t2 - t1)
    den = 1.0 + e2
    w1 = (1.0 / den) * p_group
    w2 = (e2 / den) * p_group

    sel1 = lane_f == i1
    sel2 = lane_f == i2
    onehot = jnp.where(sel1 | sel2, 1.0, 0.0)
    before = _dot(tri_ref[...], onehot.astype(BF16)) + run_scr[...]
    r1 = jnp.sum(jnp.where(sel1, before, 0.0), axis=1, keepdims=True)
    r2 = jnp.sum(jnp.where(sel2, before, 0.0), axis=1, keepdims=True)
    run_scr[...] = run_scr[...] + jnp.sum(onehot, axis=0, keepdims=True)
    cnt_ref[...] = run_scr[...].astype(I32)

    base = jnp.float32(N_GROUPS)
    meta = jnp.where(lane == 0, i1 - base,
                     jnp.where(lane == 1, i2 - base,
                               jnp.where(lane == 2, r1, jnp.where(lane == 3, r2, 0.0))))
    meta_ref[...] = meta.astype(I32)
    wts_ref[...] = jnp.where(lane == 0, w1, jnp.where(lane == 1, w2, 0.0))


def _post_call(cats_p, cats_s, x_p, x_s, g1g, sh2g, sc2g, w_out_b, ln1_g, ln1_b, wr, br, tri, alpha):
    n_p, d = x_p.shape
    n_s = x_s.shape[0]
    d_a = cats_p[0].shape[1]
    n_pt = n_p // TOK_TILE
    n_tiles = n_pt + n_s // TOK_TILE
    n_tok = n_p + n_s
    groups = TOK_TILE // MOD_GROUP
    kern = functools.partial(_post_kernel, n_prompt_tiles=n_pt, alpha=alpha)
    p_idx = lambda i: (jnp.minimum(i, n_pt - 1), 0)
    s_idx = lambda i: (jnp.maximum(i - n_pt, 0), 0)
    tile = lambda i: (i, 0)
    const = lambda i: (0, 0)
    once = pl.Buffered(1)
    return pl.pallas_call(
        kern,
        grid=(n_tiles,),
        in_specs=[
            pl.BlockSpec((TOK_TILE, d_a), p_idx),
            pl.BlockSpec((TOK_TILE, d_a), p_idx),
            pl.BlockSpec((TOK_TILE, d_a), s_idx, pipeline_mode=once),
            pl.BlockSpec((TOK_TILE, d_a), s_idx, pipeline_mode=once),
            pl.BlockSpec((TOK_TILE, d), p_idx),
            pl.BlockSpec((TOK_TILE, d), s_idx, pipeline_mode=once),
            pl.BlockSpec((groups, d), tile),
            pl.BlockSpec((groups, d), tile),
            pl.BlockSpec((groups, d), tile),
            pl.BlockSpec((d, d), const, pipeline_mode=once),
            pl.BlockSpec((1, d), const),
            pl.BlockSpec((1, d), const),
            pl.BlockSpec((d, LANES), const, pipeline_mode=once),
            pl.BlockSpec((1, LANES), const),
            pl.BlockSpec((TOK_TILE, TOK_TILE), const, pipeline_mode=once),
        ],
        out_specs=[
            pl.BlockSpec((TOK_TILE, d), tile),
            pl.BlockSpec((TOK_TILE * _lines_per_row(d), LANES), tile),
            pl.BlockSpec((TOK_TILE, LANES), tile),
            pl.BlockSpec((TOK_TILE, LANES), tile),
            pl.BlockSpec((1, LANES), const),
        ],
        out_shape=[
            jax.ShapeDtypeStruct((n_tok, d), F32),
            jax.ShapeDtypeStruct((n_tok * _lines_per_row(d), LANES), F32),
            jax.ShapeDtypeStruct((n_tok, LANES), I32),
            jax.ShapeDtypeStruct((n_tok, LANES), F32),
            jax.ShapeDtypeStruct((1, LANES), I32),
        ],
        scratch_shapes=[
            pltpu.VMEM((TOK_TILE, d), F32),
            pltpu.VMEM((TOK_TILE, d), BF16),
            pltpu.VMEM((TOK_TILE, d), BF16),
            pltpu.VMEM((1, LANES), F32),
        ],
        compiler_params=_cparams(("arbitrary",)),
        name="post_router",
    )(*cats_p, *cats_s, x_p, x_s, g1g, sh2g, sc2g, w_out_b, ln1_g, ln1_b, wr, br, tri)


def _invert_kernel(slot_ref, pad_ref, src_ref, *, n_tok):
    i = pl.program_id(0)
    tm = slot_ref.shape[2] // 2

    def real(r, carry):
        tok = i * tm + r
        src_ref[slot_ref[0, 0, 2 * r]] = tok
        src_ref[slot_ref[0, 0, 2 * r + 1]] = tok
        return carry
    lax.fori_loop(0, tm, real, 0, unroll=ROW_DMA_UNROLL)

    def unowned(k, carry):
        src_ref[pad_ref[0, 0, k]] = n_tok - 1
        return carry
    lax.fori_loop(0, pad_ref.shape[2], unowned, 0, unroll=ROW_DMA_UNROLL)


def _invert_call(slots3, pads3, n_sorted, n_tok):
    n_tiles = slots3.shape[0]
    kern = functools.partial(_invert_kernel, n_tok=n_tok)
    return pl.pallas_call(
        kern,
        grid=(n_tiles,),
        in_specs=[
            pl.BlockSpec((1, 1, slots3.shape[2]), lambda i: (i, 0, 0), memory_space=pltpu.SMEM),
            pl.BlockSpec((1, 1, pads3.shape[2]), lambda i: (i, 0, 0), memory_space=pltpu.SMEM),
        ],
        out_specs=pl.BlockSpec((n_sorted,), lambda i: (0,), memory_space=pltpu.SMEM),
        out_shape=jax.ShapeDtypeStruct((n_sorted,), I32),
        compiler_params=_cparams(("arbitrary",)),
        name="invert_routing",
    )(slots3, pads3)


def _expert_kernel(te_ref, nt_ref, scur_ref, snext_ref, sahead_ref, h_ref, w1_ref, w3_ref, w2_ref, ys_ref,
                   x_scr, w13_scr, w2_scr, gsem):
    t = pl.program_id(0)
    n_t = pl.num_programs(0)
    d = w1_ref.shape[1]
    d_exp = w1_ref.shape[2]
    lpr = _lines_per_row(d)
    rows = x_scr.shape[1] // lpr
    x_slots = x_scr.shape[0]
    xs = lax.rem(t, x_slots)
    prev = te_ref[jnp.maximum(t - 1, 0)]
    changed = (t == 0) | (te_ref[t] != prev)
    valid = t < nt_ref[0]

    def row_lines(r):
        return pl.ds(pl.multiple_of(r * lpr, lpr), lpr)

    def gather_row(s_ref, r, slot):
        return pltpu.make_async_copy(h_ref.at[row_lines(s_ref[0, 0, r])], x_scr.at[slot, row_lines(r)], gsem.at[slot])

    def start_gathers(s_ref, slot):
        def body(q, carry):
            gather_row(s_ref, 2 * q, slot).start(priority=0)
            gather_row(s_ref, 2 * q + 1, slot).start(priority=1)
            return carry
        lax.fori_loop(0, rows // 2, body, 0, unroll=ROW_DMA_UNROLL // 2)

    @pl.when(changed)
    def _():
        w13_scr[:, 0:d_exp] = w1_ref[0].astype(BF16)
        w13_scr[:, d_exp:2 * d_exp] = w3_ref[0].astype(BF16)
        w2_scr[...] = w2_ref[0].astype(BF16)

    @pl.when(t == 0)
    def _():
        start_gathers(scur_ref, 0)
        start_gathers(snext_ref, 1)

    @pl.when(t + GATHER_AHEAD < n_t)
    def _():
        start_gathers(sahead_ref, lax.rem(t + GATHER_AHEAD, x_slots))

    pltpu.make_async_copy(x_scr.at[xs], x_scr.at[xs], gsem.at[xs]).wait()

    @pl.when(valid)
    def _():
        h13 = _dot(_load_rows(x_scr, (xs,), 0, rows, d).astype(BF16), w13_scr[...])
        hm = jax.nn.silu(h13[:, 0:d_exp]) * h13[:, d_exp:2 * d_exp]
        _store_rows(ys_ref, (), 0, _dot(hm.astype(BF16), w2_scr[...]))

    @pl.when(jnp.logical_not(valid))
    def _():
        ys_ref[...] = jnp.zeros_like(ys_ref)


def _expert_call(tile_expert, n_valid, src3, h2, w1, w3, w2):
    d, d_exp = w1.shape[1], w1.shape[2]
    lpr = _lines_per_row(d)
    n_tiles = src3.shape[0]
    assert n_tiles > GATHER_AHEAD == 2
    smem_tile = lambda ahead: pl.BlockSpec(
        (1, 1, EXP_TILE), lambda t, te, nt: (jnp.minimum(t + ahead, n_tiles - 1), 0, 0), memory_space=pltpu.SMEM)
    grid_spec = pltpu.PrefetchScalarGridSpec(
        num_scalar_prefetch=2,
        grid=(n_tiles,),
        in_specs=[
            smem_tile(0), smem_tile(1), smem_tile(GATHER_AHEAD),
            pl.BlockSpec(memory_space=pl.ANY),
            pl.BlockSpec((1, d, d_exp), lambda t, te, nt: (te[t], 0, 0)),
            pl.BlockSpec((1, d, d_exp), lambda t, te, nt: (te[t], 0, 0)),
            pl.BlockSpec((1, d_exp, d), lambda t, te, nt: (te[t], 0, 0)),
        ],
        out_specs=pl.BlockSpec((EXP_TILE * lpr, LANES), lambda t, te, nt: (t, 0)),
        scratch_shapes=[
            pltpu.VMEM((GATHER_AHEAD + 1, EXP_TILE * lpr, LANES), F32),
            pltpu.VMEM((d, 2 * d_exp), BF16),
            pltpu.VMEM((d_exp, d), BF16),
            pltpu.SemaphoreType.DMA((GATHER_AHEAD + 1,)),
        ],
    )
    return pl.pallas_call(
        _expert_kernel,
        grid_spec=grid_spec,
        out_shape=jax.ShapeDtypeStruct((n_tiles * EXP_TILE * lpr, LANES), F32),
        compiler_params=_cparams(("arbitrary",)),
        name="experts",
    )(tile_expert, n_valid, src3, src3, src3, h2, w1, w3, w2)


def _combine_kernel(scur_ref, snext_ref, x1_ref, wts_ref, g2_ref, l2g_ref, l2b_ref, ys_ref, outp_ref, outs_ref,
                    y_scr, sem, *, n_prompt_tiles, alpha):
    i = pl.program_id(0)
    n_i = pl.num_programs(0)
    is_s = i >= n_prompt_tiles
    tm, d = x1_ref.shape
    lpr = _lines_per_row(d)
    cur = lax.rem(i, 2)

    def row_lines(r):
        return pl.ds(pl.multiple_of(r * lpr, lpr), lpr)

    def start_gathers(s_ref, buf):
        def body(r, carry):
            for k in range(2):
                pltpu.make_async_copy(ys_ref.at[row_lines(s_ref[0, 0, 2 * r + k])], y_scr.at[buf, k, row_lines(r)],
                                      sem.at[buf, k]).start(priority=k)
            return carry
        lax.fori_loop(0, tm, body, 0, unroll=ROW_DMA_UNROLL // 2)

    @pl.when(i == 0)
    def _():
        start_gathers(scur_ref, 0)

    @pl.when(i + 1 < n_i)
    def _():
        start_gathers(snext_ref, 1 - cur)

    for k in range(2):
        pltpu.make_async_copy(y_scr.at[cur, k], y_scr.at[cur, k], sem.at[cur, k]).wait()

    def slab(g, carry, out_ref):
        rows = pl.ds(pl.multiple_of(g * MOD_GROUP, MOD_GROUP), MOD_GROUP)
        w = wts_ref[rows, :]
        moe = (w[:, 0:1] * _load_rows(y_scr, (cur, 0), g * MOD_GROUP, MOD_GROUP, d)
               + w[:, 1:2] * _load_rows(y_scr, (cur, 1), g * MOD_GROUP, MOD_GROUP, d))
        z = alpha * x1_ref[rows, :] + g2_ref[pl.ds(g, 1), :] * moe
        out_ref[rows, :] = _ln_plain(z) * l2g_ref[...] + l2b_ref[...]
        return carry

    @pl.when(is_s)
    def _():
        lax.fori_loop(0, tm // MOD_GROUP, functools.partial(slab, out_ref=outs_ref), 0, unroll=SLAB_UNROLL)

    @pl.when(jnp.logical_not(is_s))
    def _():
        lax.fori_loop(0, tm // MOD_GROUP, functools.partial(slab, out_ref=outp_ref), 0, unroll=SLAB_UNROLL)


def _combine_call(slots3, x1, wts, g2g, ln2_g, ln2_b, ys, n_p, alpha):
    n_tok, d = x1.shape
    n_tiles = n_tok // TOK_TILE
    n_pt = n_p // TOK_TILE
    groups = TOK_TILE // MOD_GROUP
    kern = functools.partial(_combine_kernel, n_prompt_tiles=n_pt, alpha=alpha)
    slot_tile = lambda ahead: pl.BlockSpec(
        (1, 1, slots3.shape[2]), lambda i: (jnp.minimum(i + ahead, n_tiles - 1), 0, 0), memory_space=pltpu.SMEM)
    return pl.pallas_call(
        kern,
        grid=(n_tiles,),
        in_specs=[
            slot_tile(0), slot_tile(1),
            pl.BlockSpec((TOK_TILE, d), lambda i: (i, 0)),
            pl.BlockSpec((TOK_TILE, LANES), lambda i: (i, 0)),
            pl.BlockSpec((groups, d), lambda i: (i, 0)),
            pl.BlockSpec((1, d), lambda i: (0, 0)),
            pl.BlockSpec((1, d), lambda i: (0, 0)),
            pl.BlockSpec(memory_space=pl.ANY),
        ],
        out_specs=[
            pl.BlockSpec((TOK_TILE, d), lambda i: (jnp.minimum(i, n_pt - 1), 0)),
            pl.BlockSpec((TOK_TILE, d), lambda i: (jnp.maximum(i - n_pt, 0), 0)),
        ],
        out_shape=[
            jax.ShapeDtypeStruct((n_p, d), F32),
            jax.ShapeDtypeStruct((n_tok - n_p, d), F32),
        ],
        scratch_shapes=[
            pltpu.VMEM((2, 2, TOK_TILE * _lines_per_row(d), LANES), F32),
            pltpu.SemaphoreType.DMA((2, 2)),
        ],
        compiler_params=_cparams(("arbitrary",)),
        name="combine",
    )(slots3, slots3, x1, wts, g2g, ln2_g, ln2_b, ys)


def _routing_tables(meta, cnt_row, n_sorted):
    n_tok = meta.shape[0]
    experts = meta[:, 0:2]
    ranks = meta[:, 2:4]
    cnt = cnt_row[0, N_GROUPS:N_GROUPS + N_EXPERTS]
    padded = ((cnt + EXP_TILE - 1) // EXP_TILE) * EXP_TILE
    ends = jnp.cumsum(padded)
    offs = ends - padded
    slots = offs[experts] + ranks
    total = ends[-1]
    n_tiles = n_sorted // EXP_TILE
    tile_ids = jnp.arange(n_tiles, dtype=I32)
    tile_expert = jnp.minimum(
        jnp.sum((tile_ids[:, None] >= (ends // EXP_TILE)[None, :]).astype(I32), axis=1), N_EXPERTS - 1)
    n_valid = (total // EXP_TILE).reshape(1).astype(I32)
    n_unused = n_sorted - 2 * n_tok
    starts = jnp.concatenate([offs + cnt, total.reshape(1)])
    lens = jnp.concatenate([padded - cnt, (n_sorted - total).reshape(1)])
    lens_end = jnp.cumsum(lens)
    k = jnp.arange(n_unused, dtype=I32)
    in_seg = k[:, None] >= lens_end[None, :]
    seg_start = jnp.max(jnp.where(in_seg, lens_end[None, :], 0), axis=1)
    seg = jnp.sum(in_seg.astype(I32), axis=1)
    one_hot = (seg[:, None] == jnp.arange(starts.shape[0], dtype=I32)[None, :]).astype(I32)
    pads = (jnp.sum(one_hot * starts[None, :], axis=1) + (k - seg_start)).astype(I32)
    return slots.astype(I32), tile_expert, n_valid, pads


def _layer(layer, n_layers, xp, xs, s0_l, c_all, p):
    (w_ada, b_ada, w_in, ln_v_g, ln_v_b, w_s, b_s, hgrn_lb, gnorm_g, w_out, ln1_g, ln1_b,
     w_rg, b_rg, w_re, b_re, w1, w3, w2, ln2_g, ln2_b) = p
    bsz, t, d = xp.shape
    n_streams, seq, _ = xs.shape
    alpha = float((2.0 * n_layers) ** 0.25)
    n_p = bsz * t
    n_s = n_streams * seq
    n_tok = n_p + n_s

    n_c = c_all.shape[0]
    c_pad = jnp.pad(c_all, ((0, (-n_c) % 8), (0, 0)))
    mod = _ada_call(c_pad, w_ada, b_ada)[:n_c]
    sh1, sc1, g1, sh2, sc2, g2 = [mod[:, m * d:(m + 1) * d] for m in range(6)]

    w_in_b = w_in.astype(BF16)
    w_out_b = w_out.astype(BF16)
    lvg = ln_v_g.reshape(N_HEADS, 1, HEAD_DIM)
    lvb = ln_v_b.reshape(N_HEADS, 1, HEAD_DIM)
    b_s3 = b_s.reshape(N_HEADS, GMLP_CHUNK, 1)
    lb_h = jnp.transpose(hgrn_lb.reshape(hgrn_lb.shape[0], N_HEADS, HEAD_DIM), (1, 0, 2))
    gn = gnorm_g.reshape(N_HEADS, 1, HEAD_DIM)

    ca_p, co_p, state_p = _mixer_prompt_call(xp, sh1[:bsz], sc1[:bsz], w_in_b, lvg, lvb, w_s, b_s3, lb_h, gn, layer)
    ca_s, co_s, state_s, vrows = _mixer_sample_call(xs.reshape(n_s, d), sh1[bsz:], sc1[bsz:], w_in_b, lvg, lvb, w_s,
                                               b_s3, lb_h, gn, s0_l, layer, seq)

    group_stream = jnp.concatenate([
        jnp.repeat(jnp.arange(bsz, dtype=I32), t // MOD_GROUP),
        bsz + jnp.repeat(jnp.arange(n_streams, dtype=I32), seq // MOD_GROUP)])
    g1g, sh2g, sc2g, g2g = [m[group_stream] for m in (g1, sh2, sc2, g2)]

    wr = jnp.concatenate([w_rg, w_re], axis=1)
    wr_hi = wr.astype(BF16)
    wr_lo = (wr - wr_hi.astype(F32)).astype(BF16)
    n_r = wr.shape[1]
    wr_cat = jnp.concatenate([wr_hi, wr_lo, jnp.zeros((d, LANES - 2 * n_r), BF16)], axis=1)
    br = jnp.pad(jnp.concatenate([b_rg, b_re]), (0, LANES - n_r)).reshape(1, LANES)
    tri = jnp.tril(jnp.ones((TOK_TILE, TOK_TILE), F32), -1).astype(BF16)

    x1, h2, meta, wts, cnt_row = _post_call((ca_p, co_p), (ca_s, co_s), xp.reshape(n_p, d), xs.reshape(n_s, d),
                                            g1g, sh2g, sc2g, w_out_b, ln1_g.reshape(1, d), ln1_b.reshape(1, d),
                                            wr_cat, br, tri, alpha)

    n_sorted = 2 * n_tok + N_EXPERTS * EXP_TILE
    slots, tile_expert, n_valid, pads = _routing_tables(meta, cnt_row, n_sorted)
    n_tiles = n_tok // TOK_TILE
    slots3 = slots.reshape(n_tiles, 1, 2 * TOK_TILE)
    n_unused = n_sorted - 2 * n_tok
    pad_per_step = -(-n_unused // (n_tiles * ROW_DMA_UNROLL)) * ROW_DMA_UNROLL
    pads3 = jnp.pad(pads, (0, n_tiles * pad_per_step - n_unused), mode="edge").reshape(n_tiles, 1, pad_per_step)

    src = _invert_call(slots3, pads3, n_sorted, n_tok)
    src3 = src.reshape(n_sorted // EXP_TILE, 1, EXP_TILE)
    ys_sorted = _expert_call(tile_expert, n_valid, src3, h2, w1, w3, w2)
    yp, ys_out = _combine_call(slots3, x1, wts, g2g, ln2_g.reshape(1, d), ln2_b.reshape(1, d), ys_sorted, n_p, alpha)

    v_rows = vrows.reshape(n_streams, seq, N_HEADS, HEAD_DIM)
    return yp.reshape(bsz, t, d), ys_out.reshape(n_streams, seq, d), state_p, state_s, v_rows


def kernel(x_prompt, x_sample, state_hgrn, c_prompt, c_sample, w_ada, b_ada, w_in, ln_v_g, ln_v_b, w_s, b_s, hgrn_lb, gnorm_g, w_out, ln1_g, ln1_b, w_router_g, b_router_g, w_router_e, b_router_e, w1, w3, w2, ln2_g, ln2_b):
    n_layers = w_ada.shape[0]
    assert x_prompt.shape[1] % MIX_TILE == 0 and x_prompt.shape[2] == 2 * N_HEADS * HEAD_DIM
    assert x_sample.shape[1] % MOD_GROUP == 0 and x_sample.shape[1] <= SUB_CHUNK
    assert (x_sample.shape[0] * x_sample.shape[1]) % TOK_TILE == 0
    c_all = jnp.concatenate([c_prompt, c_sample], axis=0)
    xp, xs = x_prompt, x_sample
    sp_list, ss_list, vs_list = [], [], []
    for l in range(n_layers):
        p = (w_ada[l], b_ada[l], w_in[l], ln_v_g[l], ln_v_b[l], w_s[l], b_s[l], hgrn_lb, gnorm_g[l], w_out[l],
             ln1_g[l], ln1_b[l], w_router_g[l], b_router_g[l], w_router_e[l], b_router_e[l],
             w1[l], w3[l], w2[l], ln2_g[l], ln2_b[l])
        xp, xs, sp, ss, vs = _layer(l, n_layers, xp, xs, state_hgrn[l], c_all, p)
        sp_list.append(sp.astype(state_hgrn.dtype))
        ss_list.append(ss.astype(state_hgrn.dtype))
        vs_list.append(vs)
    return (xp, xs, jnp.stack(sp_list, axis=0), jnp.stack(ss_list, axis=0), jnp.stack(vs_list, axis=0))
```

```python
import functools

import jax
import jax.numpy as jnp
from jax import lax
from jax.experimental import pallas as pl
from jax.experimental.pallas import tpu as pltpu

F32 = jnp.float32
BF16 = jnp.bfloat16
I32 = jnp.int32

N_HEADS = 8
HEAD_DIM = 128
GMLP_CHUNK = 128
SUB_CHUNK = 64
N_GROUPS = 4
EXPERTS_PER_GROUP = 8
N_EXPERTS = N_GROUPS * EXPERTS_PER_GROUP
LN_EPS = 1e-5
N_KINDS = 6

LANES = 128
MIX_TILE = 512
HEADS_PER_STEP = 2
HGRN_BLOCK = 64
DIAG_BLOCK = 16
TOK_TILE = 512
POST_TILE = 256
MOD_GROUP = 32
SLAB_UNROLL = 4
EXP_TILE = 256
ROW_DMA_UNROLL = 8
ADA_TILE = 1024
VMEM_LIMIT = 56 * 1024 * 1024


def _cparams(sem):
    return pltpu.CompilerParams(dimension_semantics=sem, vmem_limit_bytes=VMEM_LIMIT)


def _ln_plain(x):
    mu = jnp.mean(x, axis=-1, keepdims=True)
    xc = x - mu
    var = jnp.mean(xc * xc, axis=-1, keepdims=True)
    return xc * lax.rsqrt(var + LN_EPS)


def _dot(a, b):
    return jnp.dot(a, b, preferred_element_type=F32)


def _dot_nt(a, b):
    return lax.dot_general(a, b, (((1,), (1,)), ((), ())), preferred_element_type=F32)


def _lines_per_row(d):
    return d // LANES


def _load_rows(ref, lead, row0, n_rows, d):
    lpr = _lines_per_row(d)
    parts = [ref[lead + (pl.ds(row0 * lpr + c, n_rows, stride=lpr), slice(None))] for c in range(lpr)]
    return jnp.concatenate(parts, axis=1)


def _store_rows(ref, lead, row0, val):
    n_rows, d = val.shape
    lpr = _lines_per_row(d)
    for c in range(lpr):
        ref[lead + (pl.ds(row0 * lpr + c, n_rows, stride=lpr), slice(None))] = val[:, c * LANES:(c + 1) * LANES]


def _ada_kernel(c_ref, w_ref, b_ref, o_ref):
    s = jax.nn.silu(c_ref[...]).astype(BF16)
    o_ref[...] = _dot(s, w_ref[...].astype(BF16)) + b_ref[...]


def _ada_call(c_pad, w_ada, b_ada):
    rows, d = c_pad.shape
    n_out = w_ada.shape[1]
    return pl.pallas_call(
        _ada_kernel,
        grid=(n_out // ADA_TILE,),
        in_specs=[
            pl.BlockSpec((rows, d), lambda n: (0, 0)),
            pl.BlockSpec((d, ADA_TILE), lambda n: (0, n)),
            pl.BlockSpec((1, ADA_TILE), lambda n: (0, n)),
        ],
        out_specs=pl.BlockSpec((rows, ADA_TILE), lambda n: (0, n)),
        out_shape=jax.ShapeDtypeStruct((rows, n_out), F32),
        compiler_params=_cparams(("arbitrary",)),
        name="adaln",
    )(c_pad, w_ada, b_ada.reshape(1, n_out))


def _lower_bound(lb_ref, hd, layer):
    raw = lb_ref[hd]
    m = jnp.max(raw, axis=0, keepdims=True)
    e = jnp.exp(raw - m)
    p = e / jnp.sum(e, axis=0, keepdims=True)
    return jnp.sum(p[: layer + 1], axis=0, keepdims=True)


def _gmlp_rows(u_pre, v_pre, ln_g, ln_b):
    u = jax.nn.gelu(u_pre)
    v = jax.nn.gelu(v_pre)
    mu = jnp.mean(v, axis=-1, keepdims=True)
    vc = v - mu
    var = jnp.mean(vc * vc, axis=-1, keepdims=True)
    vn = vc * lax.rsqrt(var + LN_EPS) * ln_g + ln_b
    return u, vn


def _row_bcast(a, row, n):
    return jnp.broadcast_to(a[row:row + 1, :], (n, a.shape[1]))


def _block_id(idx, size):
    return lax.shift_right_logical(idx, I32(size.bit_length() - 1))


def _hgrn_masks(c):
    row = lax.broadcasted_iota(I32, (c, c), 0)
    col = lax.broadcasted_iota(I32, (c, c), 1)
    masks = []
    half = c // 2
    while half >= DIAG_BLOCK:
        span = 2 * half
        same = _block_id(row, span) == _block_id(col, span)
        masks.append(same & ((row & (span - 1)) >= half) & ((col & (span - 1)) < half))
        half //= 2
    diag = (_block_id(row, DIAG_BLOCK) == _block_id(col, DIAG_BLOCK)) & (col <= row)
    return masks, diag


def _tri_ones(c):
    row = lax.broadcasted_iota(I32, (c, c), 0)
    col = lax.broadcasted_iota(I32, (c, c), 1)
    return jnp.where(col <= row, 1.0, 0.0).astype(BF16)


def _hgrn_chains(chains, c):
    tri = _tri_ones(c)
    masks, diag_mask = _hgrn_masks(c)
    units = []
    for st0, lbv, gn, blocks in chains:
        for q_pre, f_pre, i_pre, g_pre in blocks:
            q = jax.nn.silu(q_pre)
            fg = lbv + (1.0 - lbv) * jax.nn.sigmoid(f_pre)
            logf = jnp.log(fg)
            hi = logf.astype(BF16)
            lo = (logf - hi.astype(F32)).astype(BF16)
            units.append(dict(q=q, k=1.0 - fg, hilo=jnp.concatenate([hi, lo], axis=1),
                              v=i_pre, g=g_pre, gn=gn))

    for u in units:
        p = _dot(tri, u["hilo"])
        u["a"] = p[:, 0:HEAD_DIM] + p[:, HEAD_DIM:2 * HEAD_DIM]

    for u in units:
        q, k, a = u["q"], u["k"], u["a"]
        parts = []
        half = c // 2
        level = 0
        while half >= DIAG_BLOCK:
            span = 2 * half
            ref = jnp.concatenate([_row_bcast(a, b * span + half, span) for b in range(c // span)], axis=0)
            qs = (q * jnp.exp(jnp.minimum(a - ref, 0.0))).astype(BF16)
            ks = (k * jnp.exp(jnp.minimum(ref - a, 0.0))).astype(BF16)
            parts.append((masks[level], _dot_nt(qs, ks)))
            half //= 2
            level += 1
        ref = jnp.concatenate([_row_bcast(a, b * DIAG_BLOCK, DIAG_BLOCK) for b in range(c // DIAG_BLOCK)], axis=0)
        qd = (q * jnp.exp(a - ref)).astype(BF16)
        kd = (k * jnp.exp(jnp.minimum(ref - a, 80.0))).astype(BF16)
        parts.append((diag_mask, _dot_nt(qd, kd)))
        u["parts"] = parts
        a_last = a[c - 1:c, :]
        u["decay"] = jnp.exp(a_last)
        kl = (k * jnp.exp(a_last - a)).astype(BF16)
        v_t = jnp.transpose(u["v"]).astype(BF16)
        u["upd"] = _dot(v_t, kl)
        u["qe"] = (q * jnp.exp(a)).astype(BF16)

    finals = []
    n = 0
    for st0, lbv, gn, blocks in chains:
        st = st0
        for _ in blocks:
            units[n]["st_in"] = st.astype(BF16)
            st = st * units[n]["decay"] + units[n]["upd"]
            n += 1
        finals.append(st)

    outs = []
    n = 0
    for st0, lbv, gn, blocks in chains:
        chain_out = []
        for _ in blocks:
            u = units[n]
            scores = None
            for mask, part in u["parts"]:
                part = jnp.where(mask, part, 0.0)
                scores = part if scores is None else scores + part
            o = _dot(scores.astype(BF16), u["v"].astype(BF16)) + _dot_nt(u["qe"], u["st_in"])
            o = o * lax.rsqrt(jnp.mean(o * o, axis=-1, keepdims=True) + LN_EPS) * u["gn"]
            chain_out.append(o * jax.nn.silu(u["g"]))
            n += 1
        outs.append(chain_out)
    return outs, finals


def _proj_cols(kind, hd):
    c0 = (kind * HEADS_PER_STEP + hd) * HEAD_DIM
    return slice(c0, c0 + HEAD_DIM)


def _in_projection(h_scr, w_refs, proj_scr):
    width = HEADS_PER_STEP * HEAD_DIM
    for kind, w_ref in enumerate(w_refs):
        proj_scr[:, kind * width:(kind + 1) * width] = _dot(h_scr[...], w_ref[...])


def _mixer_prompt_kernel(x_ref, sh_ref, sc_ref, wu_ref, wv_ref, wq_ref, wf_ref, wi_ref, wg_ref,
                         lvg_ref, lvb_ref, ws_ref, bs_ref, lb_ref, gn_ref,
                         cata_ref, cato_ref, state_ref, h_scr, proj_scr, st_scr, *, layer):
    i = pl.program_id(1)
    j = pl.program_id(2)
    tm = h_scr.shape[0]

    @pl.when(j == 0)
    def _():
        def slab(r, carry):
            rows = pl.ds(pl.multiple_of(r * HGRN_BLOCK, HGRN_BLOCK), HGRN_BLOCK)
            h = _ln_plain(x_ref[0, rows, :]) * (1.0 + sc_ref[0]) + sh_ref[0]
            h_scr[rows, :] = h.astype(BF16)
            return carry
        lax.fori_loop(0, tm // HGRN_BLOCK, slab, 0)

    @pl.when((i == 0) & (j == 0))
    def _():
        st_scr[...] = jnp.zeros_like(st_scr)

    _in_projection(h_scr, (wu_ref, wv_ref, wq_ref, wf_ref, wi_ref, wg_ref), proj_scr)

    row = lax.broadcasted_iota(I32, (GMLP_CHUNK, GMLP_CHUNK), 0)
    col = lax.broadcasted_iota(I32, (GMLP_CHUNK, GMLP_CHUNK), 1)
    causal = _block_id(row, SUB_CHUNK) >= _block_id(col, SUB_CHUNK)
    for hd in range(HEADS_PER_STEP):
        wm = jnp.where(causal, ws_ref[hd], 0.0).astype(BF16)
        b_col = bs_ref[hd]
        for c in range(tm // GMLP_CHUNK):
            rows = slice(c * GMLP_CHUNK, (c + 1) * GMLP_CHUNK)
            u, vn = _gmlp_rows(proj_scr[rows, _proj_cols(0, hd)], proj_scr[rows, _proj_cols(1, hd)],
                               lvg_ref[hd], lvb_ref[hd])
            sp = _dot(wm, vn.astype(BF16)) + b_col
            cata_ref[rows, hd * HEAD_DIM:(hd + 1) * HEAD_DIM] = (u * sp).astype(BF16)

    chains = []
    for hd in range(HEADS_PER_STEP):
        blocks = [tuple(proj_scr[c * HGRN_BLOCK:(c + 1) * HGRN_BLOCK, _proj_cols(kind, hd)] for kind in range(2, N_KINDS))
                  for c in range(tm // HGRN_BLOCK)]
        chains.append((st_scr[HEADS_PER_STEP * j + hd], _lower_bound(lb_ref, hd, layer), gn_ref[hd], blocks))
    outs, finals = _hgrn_chains(chains, HGRN_BLOCK)
    for hd in range(HEADS_PER_STEP):
        for c, out in enumerate(outs[hd]):
            cato_ref[c * HGRN_BLOCK:(c + 1) * HGRN_BLOCK, hd * HEAD_DIM:(hd + 1) * HEAD_DIM] = out.astype(BF16)
        st_scr[HEADS_PER_STEP * j + hd] = finals[hd]
        state_ref[0, HEADS_PER_STEP * j + hd] = jnp.transpose(finals[hd])


def _w_in_specs(d, n_grid_axes):
    width = HEADS_PER_STEP * HEAD_DIM
    blocks_per_kind = N_HEADS // HEADS_PER_STEP

    def spec(kind):
        if n_grid_axes == 3:
            return pl.BlockSpec((d, width), lambda b, i, j: (0, kind * blocks_per_kind + j))
        return pl.BlockSpec((d, width), lambda j: (0, kind * blocks_per_kind + j))
    return [spec(kind) for kind in range(N_KINDS)]


def _mixer_prompt_call(x, sh1, sc1, w_in_b, ln_v_g, ln_v_b, w_s, b_s, lb_h, gnorm_g, layer):
    bsz, t, d = x.shape
    n_t = t // MIX_TILE
    hp = HEADS_PER_STEP
    kern = functools.partial(_mixer_prompt_kernel, layer=layer)
    head_vec = lambda b, i, j: (j, 0, 0)
    return pl.pallas_call(
        kern,
        grid=(bsz, n_t, N_HEADS // hp),
        in_specs=[
            pl.BlockSpec((1, MIX_TILE, d), lambda b, i, j: (b, i, 0)),
            pl.BlockSpec((1, 1, d), lambda b, i, j: (b, 0, 0)),
            pl.BlockSpec((1, 1, d), lambda b, i, j: (b, 0, 0)),
            *_w_in_specs(d, 3),
            pl.BlockSpec((hp, 1, HEAD_DIM), head_vec),
            pl.BlockSpec((hp, 1, HEAD_DIM), head_vec),
            pl.BlockSpec((hp, GMLP_CHUNK, GMLP_CHUNK), head_vec),
            pl.BlockSpec((hp, GMLP_CHUNK, 1), head_vec),
            pl.BlockSpec((hp, lb_h.shape[1], HEAD_DIM), head_vec),
            pl.BlockSpec((hp, 1, HEAD_DIM), head_vec),
        ],
        out_specs=[
            pl.BlockSpec((MIX_TILE, hp * HEAD_DIM), lambda b, i, j: (b * n_t + i, j)),
            pl.BlockSpec((MIX_TILE, hp * HEAD_DIM), lambda b, i, j: (b * n_t + i, j)),
            pl.BlockSpec((1, N_HEADS, HEAD_DIM, HEAD_DIM), lambda b, i, j: (b, 0, 0, 0)),
        ],
        out_shape=[
            jax.ShapeDtypeStruct((bsz * t, N_HEADS * HEAD_DIM), BF16),
            jax.ShapeDtypeStruct((bsz * t, N_HEADS * HEAD_DIM), BF16),
            jax.ShapeDtypeStruct((bsz, N_HEADS, HEAD_DIM, HEAD_DIM), F32),
        ],
        scratch_shapes=[
            pltpu.VMEM((MIX_TILE, d), BF16),
            pltpu.VMEM((MIX_TILE, N_KINDS * hp * HEAD_DIM), F32),
            pltpu.VMEM((N_HEADS, HEAD_DIM, HEAD_DIM), F32),
        ],
        compiler_params=_cparams(("arbitrary", "arbitrary", "arbitrary")),
        name="mixer_prompt",
    )(x, sh1.reshape(bsz, 1, d), sc1.reshape(bsz, 1, d), *([w_in_b] * N_KINDS), ln_v_g, ln_v_b, w_s, b_s, lb_h, gnorm_g)


def _mixer_sample_kernel(x_ref, sh_ref, sc_ref, wu_ref, wv_ref, wq_ref, wf_ref, wi_ref, wg_ref,
                         lvg_ref, lvb_ref, ws_ref, bs_ref, lb_ref, gn_ref, s0_ref,
                         cata_ref, cato_ref, state_ref, vrows_ref, h_scr, proj_scr, *, layer, seq):
    j = pl.program_id(0)
    n_streams = x_ref.shape[0] // seq

    @pl.when(j == 0)
    def _():
        def slab(r, carry):
            rows = pl.ds(pl.multiple_of(r * seq, seq), seq)
            h = _ln_plain(x_ref[rows, :]) * (1.0 + sc_ref[pl.ds(r, 1), :]) + sh_ref[pl.ds(r, 1), :]
            h_scr[rows, :] = h.astype(BF16)
            return carry
        lax.fori_loop(0, n_streams, slab, 0)

    _in_projection(h_scr, (wu_ref, wv_ref, wq_ref, wf_ref, wi_ref, wg_ref), proj_scr)

    chains = []
    for hd in range(HEADS_PER_STEP):
        wm = ws_ref[hd][0:seq, 0:seq].astype(BF16)
        b_col = bs_ref[hd][0:seq, :]
        lbv = _lower_bound(lb_ref, hd, layer)
        for n in range(n_streams):
            rows = slice(n * seq, (n + 1) * seq)
            u, vn = _gmlp_rows(proj_scr[rows, _proj_cols(0, hd)], proj_scr[rows, _proj_cols(1, hd)],
                               lvg_ref[hd], lvb_ref[hd])
            vrows_ref[rows, hd * HEAD_DIM:(hd + 1) * HEAD_DIM] = vn
            sp = _dot(wm, vn.astype(BF16)) + b_col
            cata_ref[rows, hd * HEAD_DIM:(hd + 1) * HEAD_DIM] = (u * sp).astype(BF16)
            chains.append((jnp.transpose(s0_ref[n, hd]), lbv, gn_ref[hd],
                           [tuple(proj_scr[rows, _proj_cols(kind, hd)] for kind in range(2, N_KINDS))]))

    outs, finals = _hgrn_chains(chains, seq)
    for hd in range(HEADS_PER_STEP):
        for n in range(n_streams):
            r0 = n * seq
            m = hd * n_streams + n
            cato_ref[r0:r0 + seq, hd * HEAD_DIM:(hd + 1) * HEAD_DIM] = outs[m][0].astype(BF16)
            state_ref[n, hd] = jnp.transpose(finals[m])


def _mixer_sample_call(x2d, sh1, sc1, w_in_b, ln_v_g, ln_v_b, w_s, b_s, lb_h, gnorm_g, s0, layer, seq):
    rows, d = x2d.shape
    n_streams = rows // seq
    hp = HEADS_PER_STEP
    kern = functools.partial(_mixer_sample_kernel, layer=layer, seq=seq)
    head_vec = lambda j: (j, 0, 0)
    return pl.pallas_call(
        kern,
        grid=(N_HEADS // hp,),
        in_specs=[
            pl.BlockSpec((rows, d), lambda j: (0, 0)),
            pl.BlockSpec((n_streams, d), lambda j: (0, 0)),
            pl.BlockSpec((n_streams, d), lambda j: (0, 0)),
            *_w_in_specs(d, 1),
            pl.BlockSpec((hp, 1, HEAD_DIM), head_vec),
            pl.BlockSpec((hp, 1, HEAD_DIM), head_vec),
            pl.BlockSpec((hp, GMLP_CHUNK, GMLP_CHUNK), head_vec),
            pl.BlockSpec((hp, GMLP_CHUNK, 1), head_vec),
            pl.BlockSpec((hp, lb_h.shape[1], HEAD_DIM), head_vec),
            pl.BlockSpec((hp, 1, HEAD_DIM), head_vec),
            pl.BlockSpec((n_streams, hp, HEAD_DIM, HEAD_DIM), lambda j: (0, j, 0, 0)),
        ],
        out_specs=[
            pl.BlockSpec((rows, hp * HEAD_DIM), lambda j: (0, j)),
            pl.BlockSpec((rows, hp * HEAD_DIM), lambda j: (0, j)),
            pl.BlockSpec((n_streams, hp, HEAD_DIM, HEAD_DIM), lambda j: (0, j, 0, 0)),
            pl.BlockSpec((rows, hp * HEAD_DIM), lambda j: (0, j)),
        ],
        out_shape=[
            jax.ShapeDtypeStruct((rows, N_HEADS * HEAD_DIM), BF16),
            jax.ShapeDtypeStruct((rows, N_HEADS * HEAD_DIM), BF16),
            jax.ShapeDtypeStruct((n_streams, N_HEADS, HEAD_DIM, HEAD_DIM), F32),
            jax.ShapeDtypeStruct((rows, N_HEADS * HEAD_DIM), F32),
        ],
        scratch_shapes=[
            pltpu.VMEM((rows, d), BF16),
            pltpu.VMEM((rows, N_KINDS * hp * HEAD_DIM), F32),
        ],
        compiler_params=_cparams(("arbitrary",)),
        name="mixer_sample",
    )(x2d, sh1, sc1, *([w_in_b] * N_KINDS), ln_v_g, ln_v_b, w_s, b_s, lb_h, gnorm_g, s0)


def _post_kernel(cap_ref, cop_ref, cas_ref, cos_ref, xp_ref, xs_ref, g1_ref, sh2_ref, sc2_ref, wout_ref, l1g_ref, l1b_ref,
                 wr_ref, br_ref, tri_ref,
                 x1_ref, h2_ref, meta_ref, wts_ref, cnt_ref,
                 cat_scr, mix_next, mix_cur, hhi_scr, hlo_scr, run_scr, *, n_prompt_tiles, n_tiles, alpha):
    s = pl.program_id(0)
    tm, d = mix_cur.shape
    d_a = cap_ref.shape[1]
    post_is_s = s > n_prompt_tiles
    n_slabs = tm // MOD_GROUP
    trips = n_slabs // 2
    chunk = d // trips

    @pl.when(s == 0)
    def _():
        run_scr[...] = jnp.zeros_like(run_scr)

    @pl.when(s >= n_prompt_tiles)
    def _():
        cat_scr[:, 0:d_a] = cas_ref[...]
        cat_scr[:, d_a:2 * d_a] = cos_ref[...]

    @pl.when(s < n_prompt_tiles)
    def _():
        cat_scr[:, 0:d_a] = cap_ref[...]
        cat_scr[:, d_a:2 * d_a] = cop_ref[...]

    def slab(g):
        rows = pl.ds(pl.multiple_of(g * MOD_GROUP, MOD_GROUP), MOD_GROUP)
        x = jnp.where(post_is_s, xs_ref[rows, :], xp_ref[rows, :])
        z = alpha * x + g1_ref[pl.ds(g, 1), :] * mix_cur[rows, :]
        x1 = _ln_plain(z) * l1g_ref[...] + l1b_ref[...]
        x1_ref[rows, :] = x1
        h2 = _ln_plain(x1) * (1.0 + sc2_ref[pl.ds(g, 1), :]) + sh2_ref[pl.ds(g, 1), :]
        h2_ref[rows, :] = h2
        hi = h2.astype(BF16)
        hhi_scr[rows, :] = hi
        hlo_scr[rows, :] = (h2 - hi.astype(F32)).astype(BF16)

    def project(q):
        cols = pl.ds(pl.multiple_of(q * chunk, chunk), chunk)
        mix_next[:, cols] = _dot(cat_scr[...], wout_ref[:, cols])

    def both(q, carry):
        slab(2 * q)
        slab(2 * q + 1)
        project(q)
        return carry

    def slabs_only(q, carry):
        slab(2 * q)
        slab(2 * q + 1)
        return carry

    @pl.when(s == 0)
    def _():
        mix_next[...] = _dot(cat_scr[...], wout_ref[...])

    @pl.when(s >= 1)
    def _():
        mix_cur[...] = mix_next[...]

    @pl.when((s >= 1) & (s < n_tiles))
    def _():
        lax.fori_loop(0, trips, both, 0, unroll=SLAB_UNROLL // 2)

    @pl.when(s == n_tiles)
    def _():
        lax.fori_loop(0, trips, slabs_only, 0, unroll=SLAB_UNROLL // 2)

    @pl.when(s >= 1)
    def _():
        _route(hhi_scr, hlo_scr, wr_ref, br_ref, tri_ref, run_scr, meta_ref, wts_ref, cnt_ref)


def _route(hhi_scr, hlo_scr, wr_ref, br_ref, tri_ref, run_scr, meta_ref, wts_ref, cnt_ref):
    tm = hhi_scr.shape[0]
    n_r = N_GROUPS + N_EXPERTS
    s = _dot(hhi_scr[...], wr_ref[...]) + _dot(hlo_scr[...], wr_ref[...])
    logits = s + pltpu.roll(s, LANES - n_r, axis=1) + br_ref[...]

    lane = lax.broadcasted_iota(I32, (tm, LANES), 1)
    lane_f = lane.astype(F32)
    neg = jnp.float32(-jnp.inf)
    big = jnp.float32(LANES)

    def first_lane_of(vals, top):
        return jnp.min(jnp.where(vals == top, lane_f, big), axis=1, keepdims=True)

    gmask = lane < N_GROUPS
    gl = jnp.where(gmask, logits, neg)
    gmax = jnp.max(gl, axis=1, keepdims=True)
    gsel = first_lane_of(gl, gmax)
    p_group = 1.0 / jnp.sum(jnp.exp(gl - gmax), axis=1, keepdims=True)

    e_lo = N_GROUPS + EXPERTS_PER_GROUP * gsel.astype(I32)
    emask = (lane >= e_lo) & (lane < e_lo + EXPERTS_PER_GROUP)
    el = jnp.where(emask, logits, neg)
    t1 = jnp.max(el, axis=1, keepdims=True)
    i1 = first_lane_of(el, t1)
    el2 = jnp.where(lane_f == i1, neg, el)
    t2 = jnp.max(el2, axis=1, keepdims=True)
    i2 = first_lane_of(el2, t2)
    e2 = jnp.exp(t2 - t1)
    den = 1.0 + e2
    w1 = (1.0 / den) * p_group
    w2 = (e2 / den) * p_group

    sel1 = lane_f == i1
    sel2 = lane_f == i2
    onehot = jnp.where(sel1 | sel2, 1.0, 0.0)
    before = _dot(tri_ref[...], onehot.astype(BF16)) + run_scr[...]
    r1 = jnp.sum(jnp.where(sel1, before, 0.0), axis=1, keepdims=True)
    r2 = jnp.sum(jnp.where(sel2, before, 0.0), axis=1, keepdims=True)
    run_scr[...] = run_scr[...] + jnp.sum(onehot, axis=0, keepdims=True)
    cnt_ref[...] = run_scr[...].astype(I32)

    base = jnp.float32(N_GROUPS)
    meta = jnp.where(lane == 0, i1 - base,
                     jnp.where(lane == 1, i2 - base,
                               jnp.where(lane == 2, r1, jnp.where(lane == 3, r2, 0.0))))
    meta_ref[...] = meta.astype(I32)
    wts_ref[...] = jnp.where(lane == 0, w1, jnp.where(lane == 1, w2, 0.0))


def _post_call(cats_p, cats_s, x_p, x_s, g1g, sh2g, sc2g, w_out_b, ln1_g, ln1_b, wr, br, tri, alpha):
    n_p, d = x_p.shape
    n_s = x_s.shape[0]
    d_a = cats_p[0].shape[1]
    tm = POST_TILE
    n_pt = n_p // tm
    n_tiles = n_pt + n_s // tm
    n_tok = n_p + n_s
    groups = tm // MOD_GROUP
    kern = functools.partial(_post_kernel, n_prompt_tiles=n_pt, n_tiles=n_tiles, alpha=alpha)
    p_cat = lambda s: (jnp.minimum(s, n_pt - 1), 0)
    s_cat = lambda s: (jnp.clip(s - n_pt, 0, n_tiles - n_pt - 1), 0)
    p_x = lambda s: (jnp.clip(s - 1, 0, n_pt - 1), 0)
    s_x = lambda s: (jnp.clip(s - 1 - n_pt, 0, n_tiles - n_pt - 1), 0)
    tile = lambda s: (jnp.maximum(s - 1, 0), 0)
    const = lambda s: (0, 0)
    once = pl.Buffered(1)
    return pl.pallas_call(
        kern,
        grid=(n_tiles + 1,),
        in_specs=[
            pl.BlockSpec((tm, d_a), p_cat),
            pl.BlockSpec((tm, d_a), p_cat),
            pl.BlockSpec((tm, d_a), s_cat),
            pl.BlockSpec((tm, d_a), s_cat),
            pl.BlockSpec((tm, d), p_x),
            pl.BlockSpec((tm, d), s_x),
            pl.BlockSpec((groups, d), tile),
            pl.BlockSpec((groups, d), tile),
            pl.BlockSpec((groups, d), tile),
            pl.BlockSpec((d, d), const, pipeline_mode=once),
            pl.BlockSpec((1, d), const),
            pl.BlockSpec((1, d), const),
            pl.BlockSpec((d, LANES), const, pipeline_mode=once),
            pl.BlockSpec((1, LANES), const),
            pl.BlockSpec((tm, tm), const, pipeline_mode=once),
        ],
        out_specs=[
            pl.BlockSpec((tm, d), tile),
            pl.BlockSpec((tm, d), tile),
            pl.BlockSpec((tm, LANES), tile),
            pl.BlockSpec((tm, LANES), tile),
            pl.BlockSpec((1, LANES), const),
        ],
        out_shape=[
            jax.ShapeDtypeStruct((n_tok, d), F32),
            jax.ShapeDtypeStruct((n_tok, d), F32),
            jax.ShapeDtypeStruct((n_tok, LANES), I32),
            jax.ShapeDtypeStruct((n_tok, LANES), F32),
            jax.ShapeDtypeStruct((1, LANES), I32),
        ],
        scratch_shapes=[
            pltpu.VMEM((tm, 2 * d_a), BF16),
            pltpu.VMEM((tm, d), F32),
            pltpu.VMEM((tm, d), F32),
            pltpu.VMEM((tm, d), BF16),
            pltpu.VMEM((tm, d), BF16),
            pltpu.VMEM((1, LANES), F32),
        ],
        compiler_params=_cparams(("arbitrary",)),
        name="post_router",
    )(*cats_p, *cats_s, x_p, x_s, g1g, sh2g, sc2g, w_out_b, ln1_g, ln1_b, wr, br, tri)


def _dispatch_kernel(slot_ref, pad_ref, h_ref, xs_ref, zero_scr, sem, *, n_unused):
    i = pl.program_id(0)
    tm = h_ref.shape[0]
    pad_per_step = pad_ref.shape[2]
    n_pad = jnp.clip(n_unused - i * pad_per_step, 0, pad_per_step)
    zero_scr[...] = jnp.zeros_like(zero_scr)

    def row_copy(r, k):
        return pltpu.make_async_copy(h_ref.at[pl.ds(r, 1)], xs_ref.at[pl.ds(slot_ref[0, 0, 2 * r + k], 1)], sem)

    def pad_copy(r):
        return pltpu.make_async_copy(zero_scr.at[pl.ds(0, 1)], xs_ref.at[pl.ds(pad_ref[0, 0, r], 1)], sem)

    def start_rows(r, carry):
        row_copy(r, 0).start(priority=0)
        row_copy(r, 1).start(priority=1)
        return carry
    lax.fori_loop(0, tm, start_rows, 0, unroll=ROW_DMA_UNROLL)

    def start_pad(r, carry):
        pad_copy(r).start()
        return carry
    lax.fori_loop(0, n_pad, start_pad, 0)

    def wait_rows(r, carry):
        row_copy(r, 0).wait()
        row_copy(r, 1).wait()
        return carry
    lax.fori_loop(0, tm, wait_rows, 0, unroll=ROW_DMA_UNROLL)

    def wait_pad(r, carry):
        pad_copy(r).wait()
        return carry
    lax.fori_loop(0, n_pad, wait_pad, 0)


def _dispatch_call(slots3, pads3, h2, n_sorted, n_unused):
    n_tok, d = h2.shape
    n_tiles = n_tok // TOK_TILE
    kern = functools.partial(_dispatch_kernel, n_unused=n_unused)
    return pl.pallas_call(
        kern,
        grid=(n_tiles,),
        in_specs=[
            pl.BlockSpec((1, 1, slots3.shape[2]), lambda i: (i, 0, 0), memory_space=pltpu.SMEM),
            pl.BlockSpec((1, 1, pads3.shape[2]), lambda i: (i, 0, 0), memory_space=pltpu.SMEM),
            pl.BlockSpec((TOK_TILE, d), lambda i: (i, 0)),
        ],
        out_specs=pl.BlockSpec(memory_space=pl.ANY),
        out_shape=jax.ShapeDtypeStruct((n_sorted, d), F32),
        scratch_shapes=[pltpu.VMEM((8, d), F32), pltpu.SemaphoreType.DMA(())],
        compiler_params=_cparams(("arbitrary",)),
        name="dispatch",
    )(slots3, pads3, h2)


def _expert_kernel(te_ref, nt_ref, xs_ref, w1_ref, w3_ref, w2_ref, ys_ref, w13_scr, w2_scr):
    t = pl.program_id(0)
    d_exp = w1_ref.shape[2]
    prev = te_ref[jnp.maximum(t - 1, 0)]
    changed = (t == 0) | (te_ref[t] != prev)
    valid = t < nt_ref[0]

    @pl.when(changed)
    def _():
        w13_scr[:, 0:d_exp] = w1_ref[0].astype(BF16)
        w13_scr[:, d_exp:2 * d_exp] = w3_ref[0].astype(BF16)
        w2_scr[...] = w2_ref[0].astype(BF16)

    @pl.when(valid)
    def _():
        h13 = _dot(xs_ref[...].astype(BF16), w13_scr[...])
        hm = jax.nn.silu(h13[:, 0:d_exp]) * h13[:, d_exp:2 * d_exp]
        _store_rows(ys_ref, (), 0, _dot(hm.astype(BF16), w2_scr[...]))

    @pl.when(jnp.logical_not(valid))
    def _():
        ys_ref[...] = jnp.zeros_like(ys_ref)


def _expert_call(tile_expert, n_valid, xs, w1, w3, w2):
    n_sorted, d = xs.shape
    d_exp = w1.shape[2]
    n_tiles = n_sorted // EXP_TILE
    grid_spec = pltpu.PrefetchScalarGridSpec(
        num_scalar_prefetch=2,
        grid=(n_tiles,),
        in_specs=[
            pl.BlockSpec((EXP_TILE, d), lambda t, te, nt: (t, 0)),
            pl.BlockSpec((1, d, d_exp), lambda t, te, nt: (te[t], 0, 0)),
            pl.BlockSpec((1, d, d_exp), lambda t, te, nt: (te[t], 0, 0)),
            pl.BlockSpec((1, d_exp, d), lambda t, te, nt: (te[t], 0, 0)),
        ],
        out_specs=pl.BlockSpec((EXP_TILE * _lines_per_row(d), LANES), lambda t, te, nt: (t, 0)),
        scratch_shapes=[pltpu.VMEM((d, 2 * d_exp), BF16), pltpu.VMEM((d_exp, d), BF16)],
    )
    return pl.pallas_call(
        _expert_kernel,
        grid_spec=grid_spec,
        out_shape=jax.ShapeDtypeStruct((n_sorted * _lines_per_row(d), LANES), F32),
        compiler_params=_cparams(("arbitrary",)),
        name="experts",
    )(tile_expert, n_valid, xs, w1, w3, w2)


def _combine_kernel(scur_ref, snext_ref, x1_ref, wts_ref, g2_ref, l2g_ref, l2b_ref, ys_ref, outp_ref, outs_ref,
                    y_scr, sem, *, n_prompt_tiles, alpha):
    i = pl.program_id(0)
    n_i = pl.num_programs(0)
    is_s = i >= n_prompt_tiles
    tm, d = x1_ref.shape
    lpr = _lines_per_row(d)
    cur = lax.rem(i, 2)

    def row_lines(r):
        return pl.ds(pl.multiple_of(r * lpr, lpr), lpr)

    def start_gathers(s_ref, buf):
        def body(r, carry):
            for k in range(2):
                pltpu.make_async_copy(ys_ref.at[row_lines(s_ref[0, 0, 2 * r + k])], y_scr.at[buf, k, row_lines(r)],
                                      sem.at[buf, k]).start(priority=k)
            return carry
        lax.fori_loop(0, tm, body, 0, unroll=ROW_DMA_UNROLL // 2)

    @pl.when(i == 0)
    def _():
        start_gathers(scur_ref, 0)

    @pl.when(i + 1 < n_i)
    def _():
        start_gathers(snext_ref, 1 - cur)

    for k in range(2):
        pltpu.make_async_copy(y_scr.at[cur, k], y_scr.at[cur, k], sem.at[cur, k]).wait()

    def slab(g, carry, out_ref):
        rows = pl.ds(pl.multiple_of(g * MOD_GROUP, MOD_GROUP), MOD_GROUP)
        w = wts_ref[rows, :]
        moe = (w[:, 0:1] * _load_rows(y_scr, (cur, 0), g * MOD_GROUP, MOD_GROUP, d)
               + w[:, 1:2] * _load_rows(y_scr, (cur, 1), g * MOD_GROUP, MOD_GROUP, d))
        z = alpha * x1_ref[rows, :] + g2_ref[pl.ds(g, 1), :] * moe
        out_ref[rows, :] = _ln_plain(z) * l2g_ref[...] + l2b_ref[...]
        return carry

    @pl.when(is_s)
    def _():
        lax.fori_loop(0, tm // MOD_GROUP, functools.partial(slab, out_ref=outs_ref), 0, unroll=SLAB_UNROLL)

    @pl.when(jnp.logical_not(is_s))
    def _():
        lax.fori_loop(0, tm // MOD_GROUP, functools.partial(slab, out_ref=outp_ref), 0, unroll=SLAB_UNROLL)


def _combine_call(slots3, x1, wts, g2g, ln2_g, ln2_b, ys, n_p, alpha):
    n_tok, d = x1.shape
    n_tiles = n_tok // TOK_TILE
    n_pt = n_p // TOK_TILE
    groups = TOK_TILE // MOD_GROUP
    kern = functools.partial(_combine_kernel, n_prompt_tiles=n_pt, alpha=alpha)
    slot_tile = lambda ahead: pl.BlockSpec(
        (1, 1, slots3.shape[2]), lambda i: (jnp.minimum(i + ahead, n_tiles - 1), 0, 0), memory_space=pltpu.SMEM)
    return pl.pallas_call(
        kern,
        grid=(n_tiles,),
        in_specs=[
            slot_tile(0), slot_tile(1),
            pl.BlockSpec((TOK_TILE, d), lambda i: (i, 0)),
            pl.BlockSpec((TOK_TILE, LANES), lambda i: (i, 0)),
            pl.BlockSpec((groups, d), lambda i: (i, 0)),
            pl.BlockSpec((1, d), lambda i: (0, 0)),
            pl.BlockSpec((1, d), lambda i: (0, 0)),
            pl.BlockSpec(memory_space=pl.ANY),
        ],
        out_specs=[
            pl.BlockSpec((TOK_TILE, d), lambda i: (jnp.minimum(i, n_pt - 1), 0)),
            pl.BlockSpec((TOK_TILE, d), lambda i: (jnp.maximum(i - n_pt, 0), 0)),
        ],
        out_shape=[
            jax.ShapeDtypeStruct((n_p, d), F32),
            jax.ShapeDtypeStruct((n_tok - n_p, d), F32),
        ],
        scratch_shapes=[
            pltpu.VMEM((2, 2, TOK_TILE * _lines_per_row(d), LANES), F32),
            pltpu.SemaphoreType.DMA((2, 2)),
        ],
        compiler_params=_cparams(("arbitrary",)),
        name="combine",
    )(slots3, slots3, x1, wts, g2g, ln2_g, ln2_b, ys)


def _routing_tables(meta, cnt_row, n_sorted):
    n_tok = meta.shape[0]
    experts = meta[:, 0:2]
    ranks = meta[:, 2:4]
    cnt = cnt_row[0, N_GROUPS:N_GROUPS + N_EXPERTS]
    padded = ((cnt + EXP_TILE - 1) // EXP_TILE) * EXP_TILE
    ends = jnp.cumsum(padded)
    offs = ends - padded
    slots = offs[experts] + ranks
    total = ends[-1]
    n_tiles = n_sorted // EXP_TILE
    tile_ids = jnp.arange(n_tiles, dtype=I32)
    tile_expert = jnp.minimum(
        jnp.sum((tile_ids[:, None] >= (ends // EXP_TILE)[None, :]).astype(I32), axis=1), N_EXPERTS - 1)
    n_valid = (total // EXP_TILE).reshape(1).astype(I32)
    n_unused = n_sorted - 2 * n_tok
    starts = jnp.concatenate([offs + cnt, total.reshape(1)])
    lens = jnp.concatenate([padded - cnt, (n_sorted - total).reshape(1)])
    lens_end = jnp.cumsum(lens)
    k = jnp.arange(n_unused, dtype=I32)
    in_seg = k[:, None] >= lens_end[None, :]
    seg_start = jnp.max(jnp.where(in_seg, lens_end[None, :], 0), axis=1)
    seg = jnp.sum(in_seg.astype(I32), axis=1)
    one_hot = (seg[:, None] == jnp.arange(starts.shape[0], dtype=I32)[None, :]).astype(I32)
    pads = (jnp.sum(one_hot * starts[None, :], axis=1) + (k - seg_start)).astype(I32)
    return slots.astype(I32), tile_expert, n_valid, pads


def _layer(layer, n_layers, xp, xs, s0_l, c_all, p):
    (w_ada, b_ada, w_in, ln_v_g, ln_v_b, w_s, b_s, hgrn_lb, gnorm_g, w_out, ln1_g, ln1_b,
     w_rg, b_rg, w_re, b_re, w1, w3, w2, ln2_g, ln2_b) = p
    bsz, t, d = xp.shape
    n_streams, seq, _ = xs.shape
    alpha = float((2.0 * n_layers) ** 0.25)
    n_p = bsz * t
    n_s = n_streams * seq
    n_tok = n_p + n_s

    n_c = c_all.shape[0]
    c_pad = jnp.pad(c_all, ((0, (-n_c) % 8), (0, 0)))
    mod = _ada_call(c_pad, w_ada, b_ada)[:n_c]
    sh1, sc1, g1, sh2, sc2, g2 = [mod[:, m * d:(m + 1) * d] for m in range(6)]

    w_in_b = w_in.astype(BF16)
    w_out_b = w_out.astype(BF16)
    lvg = ln_v_g.reshape(N_HEADS, 1, HEAD_DIM)
    lvb = ln_v_b.reshape(N_HEADS, 1, HEAD_DIM)
    b_s3 = b_s.reshape(N_HEADS, GMLP_CHUNK, 1)
    lb_h = jnp.transpose(hgrn_lb.reshape(hgrn_lb.shape[0], N_HEADS, HEAD_DIM), (1, 0, 2))
    gn = gnorm_g.reshape(N_HEADS, 1, HEAD_DIM)

    ca_p, co_p, state_p = _mixer_prompt_call(xp, sh1[:bsz], sc1[:bsz], w_in_b, lvg, lvb, w_s, b_s3, lb_h, gn, layer)
    ca_s, co_s, state_s, vrows = _mixer_sample_call(xs.reshape(n_s, d), sh1[bsz:], sc1[bsz:], w_in_b, lvg, lvb, w_s,
                                               b_s3, lb_h, gn, s0_l, layer, seq)

    group_stream = jnp.concatenate([
        jnp.repeat(jnp.arange(bsz, dtype=I32), t // MOD_GROUP),
        bsz + jnp.repeat(jnp.arange(n_streams, dtype=I32), seq // MOD_GROUP)])
    g1g, sh2g, sc2g, g2g = [m[group_stream] for m in (g1, sh2, sc2, g2)]

    wr = jnp.concatenate([w_rg, w_re], axis=1)
    wr_hi = wr.astype(BF16)
    wr_lo = (wr - wr_hi.astype(F32)).astype(BF16)
    n_r = wr.shape[1]
    wr_cat = jnp.concatenate([wr_hi, wr_lo, jnp.zeros((d, LANES - 2 * n_r), BF16)], axis=1)
    br = jnp.pad(jnp.concatenate([b_rg, b_re]), (0, LANES - n_r)).reshape(1, LANES)
    tri = jnp.tril(jnp.ones((POST_TILE, POST_TILE), F32), -1).astype(BF16)

    x1, h2, meta, wts, cnt_row = _post_call((ca_p, co_p), (ca_s, co_s), xp.reshape(n_p, d), xs.reshape(n_s, d),
                                            g1g, sh2g, sc2g, w_out_b, ln1_g.reshape(1, d), ln1_b.reshape(1, d),
                                            wr_cat, br, tri, alpha)

    n_sorted = 2 * n_tok + N_EXPERTS * EXP_TILE
    slots, tile_expert, n_valid, pads = _routing_tables(meta, cnt_row, n_sorted)
    n_tiles = n_tok // TOK_TILE
    slots3 = slots.reshape(n_tiles, 1, 2 * TOK_TILE)
    n_unused = n_sorted - 2 * n_tok
    pad_per_step = -(-n_unused // n_tiles)
    pads3 = jnp.pad(pads, (0, n_tiles * pad_per_step - n_unused)).reshape(n_tiles, 1, pad_per_step)

    xs_sorted = _dispatch_call(slots3, pads3, h2, n_sorted, n_unused)
    ys_sorted = _expert_call(tile_expert, n_valid, xs_sorted, w1, w3, w2)
    yp, ys_out = _combine_call(slots3, x1, wts, g2g, ln2_g.reshape(1, d), ln2_b.reshape(1, d), ys_sorted, n_p, alpha)

    v_rows = vrows.reshape(n_streams, seq, N_HEADS, HEAD_DIM)
    return yp.reshape(bsz, t, d), ys_out.reshape(n_streams, seq, d), state_p, state_s, v_rows


def kernel(x_prompt, x_sample, state_hgrn, c_prompt, c_sample, w_ada, b_ada, w_in, ln_v_g, ln_v_b, w_s, b_s, hgrn_lb, gnorm_g, w_out, ln1_g, ln1_b, w_router_g, b_router_g, w_router_e, b_router_e, w1, w3, w2, ln2_g, ln2_b):
    n_layers = w_ada.shape[0]
    assert x_prompt.shape[1] % MIX_TILE == 0 and x_prompt.shape[2] == 2 * N_HEADS * HEAD_DIM
    assert x_sample.shape[1] % MOD_GROUP == 0 and x_sample.shape[1] <= SUB_CHUNK
    assert (x_sample.shape[0] * x_sample.shape[1]) % TOK_TILE == 0 and TOK_TILE % POST_TILE == 0
    c_all = jnp.concatenate([c_prompt, c_sample], axis=0)
    xp, xs = x_prompt, x_sample
    sp_list, ss_list, vs_list = [], [], []
    for l in range(n_layers):
        p = (w_ada[l], b_ada[l], w_in[l], ln_v_g[l], ln_v_b[l], w_s[l], b_s[l], hgrn_lb, gnorm_g[l], w_out[l],
             ln1_g[l], ln1_b[l], w_router_g[l], b_router_g[l], w_router_e[l], b_router_e[l],
             w1[l], w3[l], w2[l], ln2_g[l], ln2_b[l])
        xp, xs, sp, ss, vs = _layer(l, n_layers, xp, xs, state_hgrn[l], c_all, p)
        sp_list.append(sp.astype(state_hgrn.dtype))
        ss_list.append(ss.astype(state_hgrn.dtype))
        vs_list.append(vs)
    return (xp, xs, jnp.stack(sp_list, axis=0), jnp.stack(ss_list, axis=0), jnp.stack(vs_list, axis=0))
```

```python
import functools

import jax
import jax.numpy as jnp
from jax import lax
from jax.experimental import pallas as pl
from jax.experimental.pallas import tpu as pltpu

F32 = jnp.float32
BF16 = jnp.bfloat16
I32 = jnp.int32

N_HEADS = 8
HEAD_DIM = 128
GMLP_CHUNK = 128
SUB_CHUNK = 64
N_GROUPS = 4
EXPERTS_PER_GROUP = 8
N_EXPERTS = N_GROUPS * EXPERTS_PER_GROUP
LN_EPS = 1e-5
N_KINDS = 6

LANES = 128
MIX_TILE = 512
HEADS_PER_STEP = 2
HGRN_BLOCK = 64
DIAG_BLOCK = 16
TOK_TILE = 512
POST_TILE = 256
MOD_GROUP = 32
SLAB_UNROLL = 4
EXP_TILE = 512
ROW_DMA_UNROLL = 8
ADA_TILE = 1024
VMEM_LIMIT = 56 * 1024 * 1024


def _cparams(sem):
    return pltpu.CompilerParams(dimension_semantics=sem, vmem_limit_bytes=VMEM_LIMIT)


def _ln_plain(x):
    mu = jnp.mean(x, axis=-1, keepdims=True)
    xc = x - mu
    var = jnp.mean(xc * xc, axis=-1, keepdims=True)
    return xc * lax.rsqrt(var + LN_EPS)


def _dot(a, b):
    return jnp.dot(a, b, preferred_element_type=F32)


def _dot_nt(a, b):
    return lax.dot_general(a, b, (((1,), (1,)), ((), ())), preferred_element_type=F32)


def _lines_per_row(d):
    return d // LANES


def _load_rows(ref, lead, row0, n_rows, d):
    lpr = _lines_per_row(d)
    parts = [ref[lead + (pl.ds(row0 * lpr + c, n_rows, stride=lpr), slice(None))] for c in range(lpr)]
    return jnp.concatenate(parts, axis=1)


def _store_rows(ref, lead, row0, val):
    n_rows, d = val.shape
    lpr = _lines_per_row(d)
    for c in range(lpr):
        ref[lead + (pl.ds(row0 * lpr + c, n_rows, stride=lpr), slice(None))] = val[:, c * LANES:(c + 1) * LANES]


def _ada_kernel(c_ref, w_ref, b_ref, o_ref):
    s = jax.nn.silu(c_ref[...]).astype(BF16)
    o_ref[...] = _dot(s, w_ref[...].astype(BF16)) + b_ref[...]


def _ada_call(c_pad, w_ada, b_ada):
    rows, d = c_pad.shape
    n_out = w_ada.shape[1]
    return pl.pallas_call(
        _ada_kernel,
        grid=(n_out // ADA_TILE,),
        in_specs=[
            pl.BlockSpec((rows, d), lambda n: (0, 0)),
            pl.BlockSpec((d, ADA_TILE), lambda n: (0, n)),
            pl.BlockSpec((1, ADA_TILE), lambda n: (0, n)),
        ],
        out_specs=pl.BlockSpec((rows, ADA_TILE), lambda n: (0, n)),
        out_shape=jax.ShapeDtypeStruct((rows, n_out), F32),
        compiler_params=_cparams(("arbitrary",)),
        name="adaln",
    )(c_pad, w_ada, b_ada.reshape(1, n_out))


def _lower_bound(lb_ref, hd, layer):
    raw = lb_ref[hd]
    m = jnp.max(raw, axis=0, keepdims=True)
    e = jnp.exp(raw - m)
    p = e / jnp.sum(e, axis=0, keepdims=True)
    return jnp.sum(p[: layer + 1], axis=0, keepdims=True)


def _gmlp_rows(u_pre, v_pre, ln_g, ln_b):
    u = jax.nn.gelu(u_pre)
    v = jax.nn.gelu(v_pre)
    mu = jnp.mean(v, axis=-1, keepdims=True)
    vc = v - mu
    var = jnp.mean(vc * vc, axis=-1, keepdims=True)
    vn = vc * lax.rsqrt(var + LN_EPS) * ln_g + ln_b
    return u, vn


def _row_bcast(a, row, n):
    return jnp.broadcast_to(a[row:row + 1, :], (n, a.shape[1]))


def _block_id(idx, size):
    return lax.shift_right_logical(idx, I32(size.bit_length() - 1))


def _hgrn_masks(c):
    row = lax.broadcasted_iota(I32, (c, c), 0)
    col = lax.broadcasted_iota(I32, (c, c), 1)
    masks = []
    half = c // 2
    while half >= DIAG_BLOCK:
        span = 2 * half
        same = _block_id(row, span) == _block_id(col, span)
        masks.append(same & ((row & (span - 1)) >= half) & ((col & (span - 1)) < half))
        half //= 2
    diag = (_block_id(row, DIAG_BLOCK) == _block_id(col, DIAG_BLOCK)) & (col <= row)
    return masks, diag


def _tri_ones(c):
    row = lax.broadcasted_iota(I32, (c, c), 0)
    col = lax.broadcasted_iota(I32, (c, c), 1)
    return jnp.where(col <= row, 1.0, 0.0).astype(BF16)


def _hgrn_chains(chains, c):
    tri = _tri_ones(c)
    masks, diag_mask = _hgrn_masks(c)
    units = []
    for st0, lbv, gn, blocks in chains:
        for q_pre, f_pre, i_pre, g_pre in blocks:
            q = jax.nn.silu(q_pre)
            fg = lbv + (1.0 - lbv) * jax.nn.sigmoid(f_pre)
            logf = jnp.log(fg)
            hi = logf.astype(BF16)
            lo = (logf - hi.astype(F32)).astype(BF16)
            units.append(dict(q=q, k=1.0 - fg, hilo=jnp.concatenate([hi, lo], axis=1),
                              v=i_pre, g=g_pre, gn=gn))

    for u in units:
        p = _dot(tri, u["hilo"])
        u["a"] = p[:, 0:HEAD_DIM] + p[:, HEAD_DIM:2 * HEAD_DIM]

    for u in units:
        q, k, a = u["q"], u["k"], u["a"]
        parts = []
        half = c // 2
        level = 0
        while half >= DIAG_BLOCK:
            span = 2 * half
            ref = jnp.concatenate([_row_bcast(a, b * span + half, span) for b in range(c // span)], axis=0)
            qs = (q * jnp.exp(jnp.minimum(a - ref, 0.0))).astype(BF16)
            ks = (k * jnp.exp(jnp.minimum(ref - a, 0.0))).astype(BF16)
            parts.append((masks[level], _dot_nt(qs, ks)))
            half //= 2
            level += 1
        ref = jnp.concatenate([_row_bcast(a, b * DIAG_BLOCK, DIAG_BLOCK) for b in range(c // DIAG_BLOCK)], axis=0)
        qd = (q * jnp.exp(a - ref)).astype(BF16)
        kd = (k * jnp.exp(jnp.minimum(ref - a, 80.0))).astype(BF16)
        parts.append((diag_mask, _dot_nt(qd, kd)))
        u["parts"] = parts
        a_last = a[c - 1:c, :]
        u["decay"] = jnp.exp(a_last)
        kl = (k * jnp.exp(a_last - a)).astype(BF16)
        v_t = jnp.transpose(u["v"]).astype(BF16)
        u["upd"] = _dot(v_t, kl)
        u["qe"] = (q * jnp.exp(a)).astype(BF16)

    finals = []
    n = 0
    for st0, lbv, gn, blocks in chains:
        st = st0
        for _ in blocks:
            units[n]["st_in"] = st.astype(BF16)
            st = st * units[n]["decay"] + units[n]["upd"]
            n += 1
        finals.append(st)

    outs = []
    n = 0
    for st0, lbv, gn, blocks in chains:
        chain_out = []
        for _ in blocks:
            u = units[n]
            scores = None
            for mask, part in u["parts"]:
                part = jnp.where(mask, part, 0.0)
                scores = part if scores is None else scores + part
            o = _dot(scores.astype(BF16), u["v"].astype(BF16)) + _dot_nt(u["qe"], u["st_in"])
            o = o * lax.rsqrt(jnp.mean(o * o, axis=-1, keepdims=True) + LN_EPS) * u["gn"]
            chain_out.append(o * jax.nn.silu(u["g"]))
            n += 1
        outs.append(chain_out)
    return outs, finals


def _proj_cols(kind, hd):
    c0 = (kind * HEADS_PER_STEP + hd) * HEAD_DIM
    return slice(c0, c0 + HEAD_DIM)


def _in_projection(h_scr, w_refs, proj_scr):
    width = HEADS_PER_STEP * HEAD_DIM
    for kind, w_ref in enumerate(w_refs):
        proj_scr[:, kind * width:(kind + 1) * width] = _dot(h_scr[...], w_ref[...])


def _mixer_prompt_kernel(x_ref, sh_ref, sc_ref, wu_ref, wv_ref, wq_ref, wf_ref, wi_ref, wg_ref,
                         lvg_ref, lvb_ref, ws_ref, bs_ref, lb_ref, gn_ref,
                         cata_ref, cato_ref, state_ref, h_scr, proj_scr, st_scr, *, layer):
    i = pl.program_id(1)
    j = pl.program_id(2)
    tm = h_scr.shape[0]

    @pl.when(j == 0)
    def _():
        def slab(r, carry):
            rows = pl.ds(pl.multiple_of(r * HGRN_BLOCK, HGRN_BLOCK), HGRN_BLOCK)
            h = _ln_plain(x_ref[0, rows, :]) * (1.0 + sc_ref[0]) + sh_ref[0]
            h_scr[rows, :] = h.astype(BF16)
            return carry
        lax.fori_loop(0, tm // HGRN_BLOCK, slab, 0)

    @pl.when((i == 0) & (j == 0))
    def _():
        st_scr[...] = jnp.zeros_like(st_scr)

    _in_projection(h_scr, (wu_ref, wv_ref, wq_ref, wf_ref, wi_ref, wg_ref), proj_scr)

    row = lax.broadcasted_iota(I32, (GMLP_CHUNK, GMLP_CHUNK), 0)
    col = lax.broadcasted_iota(I32, (GMLP_CHUNK, GMLP_CHUNK), 1)
    causal = _block_id(row, SUB_CHUNK) >= _block_id(col, SUB_CHUNK)
    for hd in range(HEADS_PER_STEP):
        wm = jnp.where(causal, ws_ref[hd], 0.0).astype(BF16)
        b_col = bs_ref[hd]
        for c in range(tm // GMLP_CHUNK):
            rows = slice(c * GMLP_CHUNK, (c + 1) * GMLP_CHUNK)
            u, vn = _gmlp_rows(proj_scr[rows, _proj_cols(0, hd)], proj_scr[rows, _proj_cols(1, hd)],
                               lvg_ref[hd], lvb_ref[hd])
            sp = _dot(wm, vn.astype(BF16)) + b_col
            cata_ref[rows, hd * HEAD_DIM:(hd + 1) * HEAD_DIM] = (u * sp).astype(BF16)

    chains = []
    for hd in range(HEADS_PER_STEP):
        blocks = [tuple(proj_scr[c * HGRN_BLOCK:(c + 1) * HGRN_BLOCK, _proj_cols(kind, hd)] for kind in range(2, N_KINDS))
                  for c in range(tm // HGRN_BLOCK)]
        chains.append((st_scr[HEADS_PER_STEP * j + hd], _lower_bound(lb_ref, hd, layer), gn_ref[hd], blocks))
    outs, finals = _hgrn_chains(chains, HGRN_BLOCK)
    for hd in range(HEADS_PER_STEP):
        for c, out in enumerate(outs[hd]):
            cato_ref[c * HGRN_BLOCK:(c + 1) * HGRN_BLOCK, hd * HEAD_DIM:(hd + 1) * HEAD_DIM] = out.astype(BF16)
        st_scr[HEADS_PER_STEP * j + hd] = finals[hd]
        state_ref[0, HEADS_PER_STEP * j + hd] = jnp.transpose(finals[hd])


def _w_in_specs(d, n_grid_axes):
    width = HEADS_PER_STEP * HEAD_DIM
    blocks_per_kind = N_HEADS // HEADS_PER_STEP

    def spec(kind):
        if n_grid_axes == 3:
            return pl.BlockSpec((d, width), lambda b, i, j: (0, kind * blocks_per_kind + j))
        return pl.BlockSpec((d, width), lambda j: (0, kind * blocks_per_kind + j))
    return [spec(kind) for kind in range(N_KINDS)]


def _mixer_prompt_call(x, sh1, sc1, w_in_b, ln_v_g, ln_v_b, w_s, b_s, lb_h, gnorm_g, layer):
    bsz, t, d = x.shape
    n_t = t // MIX_TILE
    hp = HEADS_PER_STEP
    kern = functools.partial(_mixer_prompt_kernel, layer=layer)
    head_vec = lambda b, i, j: (j, 0, 0)
    return pl.pallas_call(
        kern,
        grid=(bsz, n_t, N_HEADS // hp),
        in_specs=[
            pl.BlockSpec((1, MIX_TILE, d), lambda b, i, j: (b, i, 0)),
            pl.BlockSpec((1, 1, d), lambda b, i, j: (b, 0, 0)),
            pl.BlockSpec((1, 1, d), lambda b, i, j: (b, 0, 0)),
            *_w_in_specs(d, 3),
            pl.BlockSpec((hp, 1, HEAD_DIM), head_vec),
            pl.BlockSpec((hp, 1, HEAD_DIM), head_vec),
            pl.BlockSpec((hp, GMLP_CHUNK, GMLP_CHUNK), head_vec),
            pl.BlockSpec((hp, GMLP_CHUNK, 1), head_vec),
            pl.BlockSpec((hp, lb_h.shape[1], HEAD_DIM), head_vec),
            pl.BlockSpec((hp, 1, HEAD_DIM), head_vec),
        ],
        out_specs=[
            pl.BlockSpec((MIX_TILE, hp * HEAD_DIM), lambda b, i, j: (b * n_t + i, j)),
            pl.BlockSpec((MIX_TILE, hp * HEAD_DIM), lambda b, i, j: (b * n_t + i, j)),
            pl.BlockSpec((1, N_HEADS, HEAD_DIM, HEAD_DIM), lambda b, i, j: (b, 0, 0, 0)),
        ],
        out_shape=[
            jax.ShapeDtypeStruct((bsz * t, N_HEADS * HEAD_DIM), BF16),
            jax.ShapeDtypeStruct((bsz * t, N_HEADS * HEAD_DIM), BF16),
            jax.ShapeDtypeStruct((bsz, N_HEADS, HEAD_DIM, HEAD_DIM), F32),
        ],
        scratch_shapes=[
            pltpu.VMEM((MIX_TILE, d), BF16),
            pltpu.VMEM((MIX_TILE, N_KINDS * hp * HEAD_DIM), F32),
            pltpu.VMEM((N_HEADS, HEAD_DIM, HEAD_DIM), F32),
        ],
        compiler_params=_cparams(("arbitrary", "arbitrary", "arbitrary")),
        name="mixer_prompt",
    )(x, sh1.reshape(bsz, 1, d), sc1.reshape(bsz, 1, d), *([w_in_b] * N_KINDS), ln_v_g, ln_v_b, w_s, b_s, lb_h, gnorm_g)


def _mixer_sample_kernel(x_ref, sh_ref, sc_ref, wu_ref, wv_ref, wq_ref, wf_ref, wi_ref, wg_ref,
                         lvg_ref, lvb_ref, ws_ref, bs_ref, lb_ref, gn_ref, s0_ref,
                         cata_ref, cato_ref, state_ref, vrows_ref, h_scr, proj_scr, *, layer, seq):
    j = pl.program_id(0)
    n_streams = x_ref.shape[0] // seq

    @pl.when(j == 0)
    def _():
        def slab(r, carry):
            rows = pl.ds(pl.multiple_of(r * seq, seq), seq)
            h = _ln_plain(x_ref[rows, :]) * (1.0 + sc_ref[pl.ds(r, 1), :]) + sh_ref[pl.ds(r, 1), :]
            h_scr[rows, :] = h.astype(BF16)
            return carry
        lax.fori_loop(0, n_streams, slab, 0)

    _in_projection(h_scr, (wu_ref, wv_ref, wq_ref, wf_ref, wi_ref, wg_ref), proj_scr)

    chains = []
    for hd in range(HEADS_PER_STEP):
        wm = ws_ref[hd][0:seq, 0:seq].astype(BF16)
        b_col = bs_ref[hd][0:seq, :]
        lbv = _lower_bound(lb_ref, hd, layer)
        for n in range(n_streams):
            rows = slice(n * seq, (n + 1) * seq)
            u, vn = _gmlp_rows(proj_scr[rows, _proj_cols(0, hd)], proj_scr[rows, _proj_cols(1, hd)],
                               lvg_ref[hd], lvb_ref[hd])
            vrows_ref[rows, hd * HEAD_DIM:(hd + 1) * HEAD_DIM] = vn
            sp = _dot(wm, vn.astype(BF16)) + b_col
            cata_ref[rows, hd * HEAD_DIM:(hd + 1) * HEAD_DIM] = (u * sp).astype(BF16)
            chains.append((jnp.transpose(s0_ref[n, hd]), lbv, gn_ref[hd],
                           [tuple(proj_scr[rows, _proj_cols(kind, hd)] for kind in range(2, N_KINDS))]))

    outs, finals = _hgrn_chains(chains, seq)
    for hd in range(HEADS_PER_STEP):
        for n in range(n_streams):
            r0 = n * seq
            m = hd * n_streams + n
            cato_ref[r0:r0 + seq, hd * HEAD_DIM:(hd + 1) * HEAD_DIM] = outs[m][0].astype(BF16)
            state_ref[n, hd] = jnp.transpose(finals[m])


def _mixer_sample_call(x2d, sh1, sc1, w_in_b, ln_v_g, ln_v_b, w_s, b_s, lb_h, gnorm_g, s0, layer, seq):
    rows, d = x2d.shape
    n_streams = rows // seq
    hp = HEADS_PER_STEP
    kern = functools.partial(_mixer_sample_kernel, layer=layer, seq=seq)
    head_vec = lambda j: (j, 0, 0)
    return pl.pallas_call(
        kern,
        grid=(N_HEADS // hp,),
        in_specs=[
            pl.BlockSpec((rows, d), lambda j: (0, 0)),
            pl.BlockSpec((n_streams, d), lambda j: (0, 0)),
            pl.BlockSpec((n_streams, d), lambda j: (0, 0)),
            *_w_in_specs(d, 1),
            pl.BlockSpec((hp, 1, HEAD_DIM), head_vec),
            pl.BlockSpec((hp, 1, HEAD_DIM), head_vec),
            pl.BlockSpec((hp, GMLP_CHUNK, GMLP_CHUNK), head_vec),
            pl.BlockSpec((hp, GMLP_CHUNK, 1), head_vec),
            pl.BlockSpec((hp, lb_h.shape[1], HEAD_DIM), head_vec),
            pl.BlockSpec((hp, 1, HEAD_DIM), head_vec),
            pl.BlockSpec((n_streams, hp, HEAD_DIM, HEAD_DIM), lambda j: (0, j, 0, 0)),
        ],
        out_specs=[
            pl.BlockSpec((rows, hp * HEAD_DIM), lambda j: (0, j)),
            pl.BlockSpec((rows, hp * HEAD_DIM), lambda j: (0, j)),
            pl.BlockSpec((n_streams, hp, HEAD_DIM, HEAD_DIM), lambda j: (0, j, 0, 0)),
            pl.BlockSpec((rows, hp * HEAD_DIM), lambda j: (0, j)),
        ],
        out_shape=[
            jax.ShapeDtypeStruct((rows, N_HEADS * HEAD_DIM), BF16),
            jax.ShapeDtypeStruct((rows, N_HEADS * HEAD_DIM), BF16),
            jax.ShapeDtypeStruct((n_streams, N_HEADS, HEAD_DIM, HEAD_DIM), F32),
            jax.ShapeDtypeStruct((rows, N_HEADS * HEAD_DIM), F32),
        ],
        scratch_shapes=[
            pltpu.VMEM((rows, d), BF16),
            pltpu.VMEM((rows, N_KINDS * hp * HEAD_DIM), F32),
        ],
        compiler_params=_cparams(("arbitrary",)),
        name="mixer_sample",
    )(x2d, sh1, sc1, *([w_in_b] * N_KINDS), ln_v_g, ln_v_b, w_s, b_s, lb_h, gnorm_g, s0)


def _post_kernel(cap_ref, cop_ref, cas_ref, cos_ref, xp_ref, xs_ref, g1_ref, sh2_ref, sc2_ref, wout_ref, l1g_ref, l1b_ref,
                 wr_ref, br_ref, tri_ref,
                 x1_ref, h2_ref, meta_ref, wts_ref, cnt_ref,
                 cat_scr, mix_next, mix_cur, hhi_scr, hlo_scr, run_scr, *, n_prompt_tiles, n_tiles, alpha):
    s = pl.program_id(0)
    tm, d = mix_cur.shape
    d_a = cap_ref.shape[1]
    post_is_s = s > n_prompt_tiles
    n_slabs = tm // MOD_GROUP
    trips = n_slabs // 2
    chunk = d // trips

    @pl.when(s == 0)
    def _():
        run_scr[...] = jnp.zeros_like(run_scr)

    @pl.when(s >= n_prompt_tiles)
    def _():
        cat_scr[:, 0:d_a] = cas_ref[...]
        cat_scr[:, d_a:2 * d_a] = cos_ref[...]

    @pl.when(s < n_prompt_tiles)
    def _():
        cat_scr[:, 0:d_a] = cap_ref[...]
        cat_scr[:, d_a:2 * d_a] = cop_ref[...]

    def slab(g):
        rows = pl.ds(pl.multiple_of(g * MOD_GROUP, MOD_GROUP), MOD_GROUP)
        x = jnp.where(post_is_s, xs_ref[rows, :], xp_ref[rows, :])
        z = alpha * x + g1_ref[pl.ds(g, 1), :] * mix_cur[rows, :]
        x1 = _ln_plain(z) * l1g_ref[...] + l1b_ref[...]
        x1_ref[rows, :] = x1
        h2 = _ln_plain(x1) * (1.0 + sc2_ref[pl.ds(g, 1), :]) + sh2_ref[pl.ds(g, 1), :]
        h2_ref[rows, :] = h2
        hi = h2.astype(BF16)
        hhi_scr[rows, :] = hi
        hlo_scr[rows, :] = (h2 - hi.astype(F32)).astype(BF16)

    def project(q):
        cols = pl.ds(pl.multiple_of(q * chunk, chunk), chunk)
        mix_next[:, cols] = _dot(cat_scr[...], wout_ref[:, cols])

    def both(q, carry):
        slab(2 * q)
        slab(2 * q + 1)
        project(q)
        return carry

    def slabs_only(q, carry):
        slab(2 * q)
        slab(2 * q + 1)
        return carry

    @pl.when(s == 0)
    def _():
        mix_next[...] = _dot(cat_scr[...], wout_ref[...])

    @pl.when(s >= 1)
    def _():
        mix_cur[...] = mix_next[...]

    @pl.when((s >= 1) & (s < n_tiles))
    def _():
        lax.fori_loop(0, trips, both, 0, unroll=SLAB_UNROLL // 2)

    @pl.when(s == n_tiles)
    def _():
        lax.fori_loop(0, trips, slabs_only, 0, unroll=SLAB_UNROLL // 2)

    @pl.when(s >= 1)
    def _():
        _route(hhi_scr, hlo_scr, wr_ref, br_ref, tri_ref, run_scr, meta_ref, wts_ref, cnt_ref)


def _route(hhi_scr, hlo_scr, wr_ref, br_ref, tri_ref, run_scr, meta_ref, wts_ref, cnt_ref):
    tm = hhi_scr.shape[0]
    n_r = N_GROUPS + N_EXPERTS
    s = _dot(hhi_scr[...], wr_ref[...]) + _dot(hlo_scr[...], wr_ref[...])
    logits = s + pltpu.roll(s, LANES - n_r, axis=1) + br_ref[...]

    lane = lax.broadcasted_iota(I32, (tm, LANES), 1)
    lane_f = lane.astype(F32)
    neg = jnp.float32(-jnp.inf)
    big = jnp.float32(LANES)

    def first_lane_of(vals, top):
        return jnp.min(jnp.where(vals == top, lane_f, big), axis=1, keepdims=True)

    gmask = lane < N_GROUPS
    gl = jnp.where(gmask, logits, neg)
    gmax = jnp.max(gl, axis=1, keepdims=True)
    gsel = first_lane_of(gl, gmax)
    p_group = 1.0 / jnp.sum(jnp.exp(gl - gmax), axis=1, keepdims=True)

    e_lo = N_GROUPS + EXPERTS_PER_GROUP * gsel.astype(I32)
    emask = (lane >= e_lo) & (lane < e_lo + EXPERTS_PER_GROUP)
    el = jnp.where(emask, logits, neg)
    t1 = jnp.max(el, axis=1, keepdims=True)
    i1 = first_lane_of(el, t1)
    el2 = jnp.where(lane_f == i1, neg, el)
    t2 = jnp.max(el2, axis=1, keepdims=True)
    i2 = first_lane_of(el2, t2)
    e2 = jnp.exp(t2 - t1)
    den = 1.0 + e2
    w1 = (1.0 / den) * p_group
    w2 = (e2 / den) * p_group

    sel1 = lane_f == i1
    sel2 = lane_f == i2
    onehot = jnp.where(sel1 | sel2, 1.0, 0.0)
    before = _dot(tri_ref[...], onehot.astype(BF16)) + run_scr[...]
    r1 = jnp.sum(jnp.where(sel1, before, 0.0), axis=1, keepdims=True)
    r2 = jnp.sum(jnp.where(sel2, before, 0.0), axis=1, keepdims=True)
    run_scr[...] = run_scr[...] + jnp.sum(onehot, axis=0, keepdims=True)
    cnt_ref[...] = run_scr[...].astype(I32)

    base = jnp.float32(N_GROUPS)
    meta = jnp.where(lane == 0, i1 - base,
                     jnp.where(lane == 1, i2 - base,
                               jnp.where(lane == 2, r1, jnp.where(lane == 3, r2, 0.0))))
    meta_ref[...] = meta.astype(I32)
    wts_ref[...] = jnp.where(lane == 0, w1, jnp.where(lane == 1, w2, 0.0))


def _post_call(cats_p, cats_s, x_p, x_s, g1g, sh2g, sc2g, w_out_b, ln1_g, ln1_b, wr, br, tri, alpha):
    n_p, d = x_p.shape
    n_s = x_s.shape[0]
    d_a = cats_p[0].shape[1]
    tm = POST_TILE
    n_pt = n_p // tm
    n_tiles = n_pt + n_s // tm
    n_tok = n_p + n_s
    groups = tm // MOD_GROUP
    kern = functools.partial(_post_kernel, n_prompt_tiles=n_pt, n_tiles=n_tiles, alpha=alpha)
    p_cat = lambda s: (jnp.minimum(s, n_pt - 1), 0)
    s_cat = lambda s: (jnp.clip(s - n_pt, 0, n_tiles - n_pt - 1), 0)
    p_x = lambda s: (jnp.clip(s - 1, 0, n_pt - 1), 0)
    s_x = lambda s: (jnp.clip(s - 1 - n_pt, 0, n_tiles - n_pt - 1), 0)
    tile = lambda s: (jnp.maximum(s - 1, 0), 0)
    const = lambda s: (0, 0)
    once = pl.Buffered(1)
    return pl.pallas_call(
        kern,
        grid=(n_tiles + 1,),
        in_specs=[
            pl.BlockSpec((tm, d_a), p_cat),
            pl.BlockSpec((tm, d_a), p_cat),
            pl.BlockSpec((tm, d_a), s_cat),
            pl.BlockSpec((tm, d_a), s_cat),
            pl.BlockSpec((tm, d), p_x),
            pl.BlockSpec((tm, d), s_x),
            pl.BlockSpec((groups, d), tile),
            pl.BlockSpec((groups, d), tile),
            pl.BlockSpec((groups, d), tile),
            pl.BlockSpec((d, d), const, pipeline_mode=once),
            pl.BlockSpec((1, d), const),
            pl.BlockSpec((1, d), const),
            pl.BlockSpec((d, LANES), const, pipeline_mode=once),
            pl.BlockSpec((1, LANES), const),
            pl.BlockSpec((tm, tm), const, pipeline_mode=once),
        ],
        out_specs=[
            pl.BlockSpec((tm, d), tile),
            pl.BlockSpec((tm, d), tile),
            pl.BlockSpec((tm, LANES), tile),
            pl.BlockSpec((tm, LANES), tile),
            pl.BlockSpec((1, LANES), const),
        ],
        out_shape=[
            jax.ShapeDtypeStruct((n_tok, d), F32),
            jax.ShapeDtypeStruct((n_tok, d), F32),
            jax.ShapeDtypeStruct((n_tok, LANES), I32),
            jax.ShapeDtypeStruct((n_tok, LANES), F32),
            jax.ShapeDtypeStruct((1, LANES), I32),
        ],
        scratch_shapes=[
            pltpu.VMEM((tm, 2 * d_a), BF16),
            pltpu.VMEM((tm, d), F32),
            pltpu.VMEM((tm, d), F32),
            pltpu.VMEM((tm, d), BF16),
            pltpu.VMEM((tm, d), BF16),
            pltpu.VMEM((1, LANES), F32),
        ],
        compiler_params=_cparams(("arbitrary",)),
        name="post_router",
    )(*cats_p, *cats_s, x_p, x_s, g1g, sh2g, sc2g, w_out_b, ln1_g, ln1_b, wr, br, tri)


def _dispatch_kernel(slot_ref, gap_start_ref, gap_len_ref, h_ref, xs_ref, zero_scr, sem, tile_sem, *, gaps_per_step):
    i = pl.program_id(0)
    tm = h_ref.shape[0]
    n_gaps = gap_len_ref.shape[0]
    tile_rows = zero_scr.shape[0]

    @pl.when(i == 0)
    def _():
        zero_scr[...] = jnp.zeros_like(zero_scr)

    def row_copy(r, k):
        return pltpu.make_async_copy(h_ref.at[pl.ds(r, 1)], xs_ref.at[pl.ds(slot_ref[0, 0, 2 * r + k], 1)], sem)

    def start_rows(r, carry):
        row_copy(r, 0).start(priority=0)
        row_copy(r, 1).start(priority=1)
        return carry
    lax.fori_loop(0, tm, start_rows, 0, unroll=ROW_DMA_UNROLL)

    def gap_rows(g):
        return jnp.where(g < n_gaps - 1, gap_len_ref[jnp.minimum(g, n_gaps - 1)], 0)

    def gap_tiles(g):
        return jnp.where(g == n_gaps - 1, gap_len_ref[n_gaps - 1] // tile_rows, 0)

    def zero_row_copy(g, r):
        return pltpu.make_async_copy(zero_scr.at[pl.ds(0, 1)], xs_ref.at[pl.ds(gap_start_ref[g] + r, 1)], sem)

    def zero_tile_copy(g, t):
        start = pl.multiple_of(gap_start_ref[g] + t * tile_rows, tile_rows)
        return pltpu.make_async_copy(zero_scr, xs_ref.at[pl.ds(start, tile_rows)], tile_sem)

    for j in range(gaps_per_step):
        g = jnp.minimum(i * gaps_per_step + j, n_gaps - 1)
        live = i * gaps_per_step + j < n_gaps
        n_rows = jnp.where(live, gap_rows(g), 0)
        n_til = jnp.where(live, gap_tiles(g), 0)
        lax.fori_loop(0, n_rows, lambda r, c, g=g: (zero_row_copy(g, r).start(), c)[1], 0)
        lax.fori_loop(0, n_til, lambda t, c, g=g: (zero_tile_copy(g, t).start(), c)[1], 0)

    def wait_rows(r, carry):
        row_copy(r, 0).wait()
        row_copy(r, 1).wait()
        return carry
    lax.fori_loop(0, tm, wait_rows, 0, unroll=ROW_DMA_UNROLL)

    for j in range(gaps_per_step):
        g = jnp.minimum(i * gaps_per_step + j, n_gaps - 1)
        live = i * gaps_per_step + j < n_gaps
        n_rows = jnp.where(live, gap_rows(g), 0)
        n_til = jnp.where(live, gap_tiles(g), 0)
        lax.fori_loop(0, n_rows, lambda r, c, g=g: (zero_row_copy(g, r).wait(), c)[1], 0)
        lax.fori_loop(0, n_til, lambda t, c, g=g: (zero_tile_copy(g, t).wait(), c)[1], 0)


def _dispatch_call(slots3, gap_start, gap_len, h2, n_sorted):
    n_tok, d = h2.shape
    n_tiles = n_tok // TOK_TILE
    n_gaps = gap_len.shape[0]
    kern = functools.partial(_dispatch_kernel, gaps_per_step=-(-n_gaps // n_tiles))
    return pl.pallas_call(
        kern,
        grid=(n_tiles,),
        in_specs=[
            pl.BlockSpec((1, 1, slots3.shape[2]), lambda i: (i, 0, 0), memory_space=pltpu.SMEM),
            pl.BlockSpec((n_gaps,), lambda i: (0,), memory_space=pltpu.SMEM),
            pl.BlockSpec((n_gaps,), lambda i: (0,), memory_space=pltpu.SMEM),
            pl.BlockSpec((TOK_TILE, d), lambda i: (i, 0)),
        ],
        out_specs=pl.BlockSpec(memory_space=pl.ANY),
        out_shape=jax.ShapeDtypeStruct((n_sorted, d), F32),
        scratch_shapes=[pltpu.VMEM((EXP_TILE, d), F32), pltpu.SemaphoreType.DMA(()), pltpu.SemaphoreType.DMA(())],
        compiler_params=_cparams(("arbitrary",)),
        name="dispatch",
    )(slots3, gap_start, gap_len, h2)


def _expert_kernel(te_ref, nt_ref, xs_ref, w1_ref, w3_ref, w2_ref, ys_ref, w13_scr, w2_scr):
    t = pl.program_id(0)
    d_exp = w1_ref.shape[2]
    prev = te_ref[jnp.maximum(t - 1, 0)]
    changed = (t == 0) | (te_ref[t] != prev)
    valid = t < nt_ref[0]

    @pl.when(changed)
    def _():
        w13_scr[:, 0:d_exp] = w1_ref[0].astype(BF16)
        w13_scr[:, d_exp:2 * d_exp] = w3_ref[0].astype(BF16)
        w2_scr[...] = w2_ref[0].astype(BF16)

    @pl.when(valid)
    def _():
        h13 = _dot(xs_ref[...].astype(BF16), w13_scr[...])
        hm = jax.nn.silu(h13[:, 0:d_exp]) * h13[:, d_exp:2 * d_exp]
        _store_rows(ys_ref, (), 0, _dot(hm.astype(BF16), w2_scr[...]))

    @pl.when(jnp.logical_not(valid))
    def _():
        ys_ref[...] = jnp.zeros_like(ys_ref)


def _expert_call(tile_expert, n_valid, xs, w1, w3, w2):
    n_sorted, d = xs.shape
    d_exp = w1.shape[2]
    n_tiles = n_sorted // EXP_TILE
    grid_spec = pltpu.PrefetchScalarGridSpec(
        num_scalar_prefetch=2,
        grid=(n_tiles,),
        in_specs=[
            pl.BlockSpec((EXP_TILE, d), lambda t, te, nt: (t, 0)),
            pl.BlockSpec((1, d, d_exp), lambda t, te, nt: (te[t], 0, 0)),
            pl.BlockSpec((1, d, d_exp), lambda t, te, nt: (te[t], 0, 0)),
            pl.BlockSpec((1, d_exp, d), lambda t, te, nt: (te[t], 0, 0)),
        ],
        out_specs=pl.BlockSpec((EXP_TILE * _lines_per_row(d), LANES), lambda t, te, nt: (t, 0)),
        scratch_shapes=[pltpu.VMEM((d, 2 * d_exp), BF16), pltpu.VMEM((d_exp, d), BF16)],
    )
    return pl.pallas_call(
        _expert_kernel,
        grid_spec=grid_spec,
        out_shape=jax.ShapeDtypeStruct((n_sorted * _lines_per_row(d), LANES), F32),
        compiler_params=_cparams(("arbitrary",)),
        name="experts",
    )(tile_expert, n_valid, xs, w1, w3, w2)


def _combine_kernel(scur_ref, snext_ref, x1_ref, wts_ref, g2_ref, l2g_ref, l2b_ref, ys_ref, outp_ref, outs_ref,
                    y_scr, sem, *, n_prompt_tiles, alpha):
    i = pl.program_id(0)
    n_i = pl.num_programs(0)
    is_s = i >= n_prompt_tiles
    tm, d = x1_ref.shape
    lpr = _lines_per_row(d)
    cur = lax.rem(i, 2)

    def row_lines(r):
        return pl.ds(pl.multiple_of(r * lpr, lpr), lpr)

    def start_gathers(s_ref, buf):
        def body(r, carry):
            for k in range(2):
                pltpu.make_async_copy(ys_ref.at[row_lines(s_ref[0, 0, 2 * r + k])], y_scr.at[buf, k, row_lines(r)],
                                      sem.at[buf, k]).start(priority=k)
            return carry
        lax.fori_loop(0, tm, body, 0, unroll=ROW_DMA_UNROLL // 2)

    @pl.when(i == 0)
    def _():
        start_gathers(scur_ref, 0)

    @pl.when(i + 1 < n_i)
    def _():
        start_gathers(snext_ref, 1 - cur)

    for k in range(2):
        pltpu.make_async_copy(y_scr.at[cur, k], y_scr.at[cur, k], sem.at[cur, k]).wait()

    def slab(g, carry, out_ref):
        rows = pl.ds(pl.multiple_of(g * MOD_GROUP, MOD_GROUP), MOD_GROUP)
        w = wts_ref[rows, :]
        moe = (w[:, 0:1] * _load_rows(y_scr, (cur, 0), g * MOD_GROUP, MOD_GROUP, d)
               + w[:, 1:2] * _load_rows(y_scr, (cur, 1), g * MOD_GROUP, MOD_GROUP, d))
        z = alpha * x1_ref[rows, :] + g2_ref[pl.ds(g, 1), :] * moe
        out_ref[rows, :] = _ln_plain(z) * l2g_ref[...] + l2b_ref[...]
        return carry

    @pl.when(is_s)
    def _():
        lax.fori_loop(0, tm // MOD_GROUP, functools.partial(slab, out_ref=outs_ref), 0, unroll=SLAB_UNROLL)

    @pl.when(jnp.logical_not(is_s))
    def _():
        lax.fori_loop(0, tm // MOD_GROUP, functools.partial(slab, out_ref=outp_ref), 0, unroll=SLAB_UNROLL)


def _combine_call(slots3, x1, wts, g2g, ln2_g, ln2_b, ys, n_p, alpha):
    n_tok, d = x1.shape
    n_tiles = n_tok // TOK_TILE
    n_pt = n_p // TOK_TILE
    groups = TOK_TILE // MOD_GROUP
    kern = functools.partial(_combine_kernel, n_prompt_tiles=n_pt, alpha=alpha)
    slot_tile = lambda ahead: pl.BlockSpec(
        (1, 1, slots3.shape[2]), lambda i: (jnp.minimum(i + ahead, n_tiles - 1), 0, 0), memory_space=pltpu.SMEM)
    return pl.pallas_call(
        kern,
        grid=(n_tiles,),
        in_specs=[
            slot_tile(0), slot_tile(1),
            pl.BlockSpec((TOK_TILE, d), lambda i: (i, 0)),
            pl.BlockSpec((TOK_TILE, LANES), lambda i: (i, 0)),
            pl.BlockSpec((groups, d), lambda i: (i, 0)),
            pl.BlockSpec((1, d), lambda i: (0, 0)),
            pl.BlockSpec((1, d), lambda i: (0, 0)),
            pl.BlockSpec(memory_space=pl.ANY),
        ],
        out_specs=[
            pl.BlockSpec((TOK_TILE, d), lambda i: (jnp.minimum(i, n_pt - 1), 0)),
            pl.BlockSpec((TOK_TILE, d), lambda i: (jnp.maximum(i - n_pt, 0), 0)),
        ],
        out_shape=[
            jax.ShapeDtypeStruct((n_p, d), F32),
            jax.ShapeDtypeStruct((n_tok - n_p, d), F32),
        ],
        scratch_shapes=[
            pltpu.VMEM((2, 2, TOK_TILE * _lines_per_row(d), LANES), F32),
            pltpu.SemaphoreType.DMA((2, 2)),
        ],
        compiler_params=_cparams(("arbitrary",)),
        name="combine",
    )(slots3, slots3, x1, wts, g2g, ln2_g, ln2_b, ys)


def _routing_tables(meta, cnt_row, n_sorted):
    n_tok = meta.shape[0]
    experts = meta[:, 0:2]
    ranks = meta[:, 2:4]
    cnt = cnt_row[0, N_GROUPS:N_GROUPS + N_EXPERTS]
    padded = ((cnt + EXP_TILE - 1) // EXP_TILE) * EXP_TILE
    ends = jnp.cumsum(padded)
    offs = ends - padded
    slots = offs[experts] + ranks
    total = ends[-1]
    n_tiles = n_sorted // EXP_TILE
    tile_ids = jnp.arange(n_tiles, dtype=I32)
    tile_expert = jnp.minimum(
        jnp.sum((tile_ids[:, None] >= (ends // EXP_TILE)[None, :]).astype(I32), axis=1), N_EXPERTS - 1)
    n_valid = (total // EXP_TILE).reshape(1).astype(I32)
    gap_start = jnp.concatenate([offs + cnt, total.reshape(1)]).astype(I32)
    gap_len = jnp.concatenate([padded - cnt, (n_sorted - total).reshape(1)]).astype(I32)
    return slots.astype(I32), tile_expert, n_valid, gap_start, gap_len


def _layer(layer, n_layers, xp, xs, s0_l, c_all, p):
    (w_ada, b_ada, w_in, ln_v_g, ln_v_b, w_s, b_s, hgrn_lb, gnorm_g, w_out, ln1_g, ln1_b,
     w_rg, b_rg, w_re, b_re, w1, w3, w2, ln2_g, ln2_b) = p
    bsz, t, d = xp.shape
    n_streams, seq, _ = xs.shape
    alpha = float((2.0 * n_layers) ** 0.25)
    n_p = bsz * t
    n_s = n_streams * seq
    n_tok = n_p + n_s

    n_c = c_all.shape[0]
    c_pad = jnp.pad(c_all, ((0, (-n_c) % 8), (0, 0)))
    mod = _ada_call(c_pad, w_ada, b_ada)[:n_c]
    sh1, sc1, g1, sh2, sc2, g2 = [mod[:, m * d:(m + 1) * d] for m in range(6)]

    w_in_b = w_in.astype(BF16)
    w_out_b = w_out.astype(BF16)
    lvg = ln_v_g.reshape(N_HEADS, 1, HEAD_DIM)
    lvb = ln_v_b.reshape(N_HEADS, 1, HEAD_DIM)
    b_s3 = b_s.reshape(N_HEADS, GMLP_CHUNK, 1)
    lb_h = jnp.transpose(hgrn_lb.reshape(hgrn_lb.shape[0], N_HEADS, HEAD_DIM), (1, 0, 2))
    gn = gnorm_g.reshape(N_HEADS, 1, HEAD_DIM)

    ca_p, co_p, state_p = _mixer_prompt_call(xp, sh1[:bsz], sc1[:bsz], w_in_b, lvg, lvb, w_s, b_s3, lb_h, gn, layer)
    ca_s, co_s, state_s, vrows = _mixer_sample_call(xs.reshape(n_s, d), sh1[bsz:], sc1[bsz:], w_in_b, lvg, lvb, w_s,
                                               b_s3, lb_h, gn, s0_l, layer, seq)

    group_stream = jnp.concatenate([
        jnp.repeat(jnp.arange(bsz, dtype=I32), t // MOD_GROUP),
        bsz + jnp.repeat(jnp.arange(n_streams, dtype=I32), seq // MOD_GROUP)])
    g1g, sh2g, sc2g, g2g = [m[group_stream] for m in (g1, sh2, sc2, g2)]

    wr = jnp.concatenate([w_rg, w_re], axis=1)
    wr_hi = wr.astype(BF16)
    wr_lo = (wr - wr_hi.astype(F32)).astype(BF16)
    n_r = wr.shape[1]
    wr_cat = jnp.concatenate([wr_hi, wr_lo, jnp.zeros((d, LANES - 2 * n_r), BF16)], axis=1)
    br = jnp.pad(jnp.concatenate([b_rg, b_re]), (0, LANES - n_r)).reshape(1, LANES)
    tri = jnp.tril(jnp.ones((POST_TILE, POST_TILE), F32), -1).astype(BF16)

    x1, h2, meta, wts, cnt_row = _post_call((ca_p, co_p), (ca_s, co_s), xp.reshape(n_p, d), xs.reshape(n_s, d),
                                            g1g, sh2g, sc2g, w_out_b, ln1_g.reshape(1, d), ln1_b.reshape(1, d),
                                            wr_cat, br, tri, alpha)

    n_sorted = 2 * n_tok + N_EXPERTS * EXP_TILE
    slots, tile_expert, n_valid, gap_start, gap_len = _routing_tables(meta, cnt_row, n_sorted)
    n_tiles = n_tok // TOK_TILE
    slots3 = slots.reshape(n_tiles, 1, 2 * TOK_TILE)

    xs_sorted = _dispatch_call(slots3, gap_start, gap_len, h2, n_sorted)
    ys_sorted = _expert_call(tile_expert, n_valid, xs_sorted, w1, w3, w2)
    yp, ys_out = _combine_call(slots3, x1, wts, g2g, ln2_g.reshape(1, d), ln2_b.reshape(1, d), ys_sorted, n_p, alpha)

    v_rows = vrows.reshape(n_streams, seq, N_HEADS, HEAD_DIM)
    return yp.reshape(bsz, t, d), ys_out.reshape(n_streams, seq, d), state_p, state_s, v_rows


def kernel(x_prompt, x_sample, state_hgrn, c_prompt, c_sample, w_ada, b_ada, w_in, ln_v_g, ln_v_b, w_s, b_s, hgrn_lb, gnorm_g, w_out, ln1_g, ln1_b, w_router_g, b_router_g, w_router_e, b_router_e, w1, w3, w2, ln2_g, ln2_b):
    n_layers = w_ada.shape[0]
    assert x_prompt.shape[1] % MIX_TILE == 0 and x_prompt.shape[2] == 2 * N_HEADS * HEAD_DIM
    assert x_sample.shape[1] % MOD_GROUP == 0 and x_sample.shape[1] <= SUB_CHUNK
    assert (x_sample.shape[0] * x_sample.shape[1]) % TOK_TILE == 0 and TOK_TILE % POST_TILE == 0
    c_all = jnp.concatenate([c_prompt, c_sample], axis=0)
    xp, xs = x_prompt, x_sample
    sp_list, ss_list, vs_list = [], [], []
    for l in range(n_layers):
        p = (w_ada[l], b_ada[l], w_in[l], ln_v_g[l], ln_v_b[l], w_s[l], b_s[l], hgrn_lb, gnorm_g[l], w_out[l],
             ln1_g[l], ln1_b[l], w_router_g[l], b_router_g[l], w_router_e[l], b_router_e[l],
             w1[l], w3[l], w2[l], ln2_g[l], ln2_b[l])
        xp, xs, sp, ss, vs = _layer(l, n_layers, xp, xs, state_hgrn[l], c_all, p)
        sp_list.append(sp.astype(state_hgrn.dtype))
        ss_list.append(ss.astype(state_hgrn.dtype))
        vs_list.append(vs)
    return (xp, xs, jnp.stack(sp_list, axis=0), jnp.stack(ss_list, axis=0), jnp.stack(vs_list, axis=0))
```

```python
import functools

import jax
import jax.numpy as jnp
from jax import lax
from jax.experimental import pallas as pl
from jax.experimental.pallas import tpu as pltpu

F32 = jnp.float32
BF16 = jnp.bfloat16
I32 = jnp.int32

N_HEADS = 8
HEAD_DIM = 128
GMLP_CHUNK = 128
SUB_CHUNK = 64
N_GROUPS = 4
EXPERTS_PER_GROUP = 8
N_EXPERTS = N_GROUPS * EXPERTS_PER_GROUP
LN_EPS = 1e-5
N_KINDS = 6

LANES = 128
MIX_TILE = 512
HEADS_PER_STEP = 2
HGRN_BLOCK = 64
DIAG_BLOCK = 16
TOK_TILE = 512
POST_TILE = 256
META_ROWS = 8
MOD_GROUP = 32
SLAB_UNROLL = 4
EXP_TILE = 256
ROW_DMA_UNROLL = 8
ADA_TILE = 1024
VMEM_LIMIT = 56 * 1024 * 1024


def _cparams(sem):
    return pltpu.CompilerParams(dimension_semantics=sem, vmem_limit_bytes=VMEM_LIMIT)


def _ln_plain(x):
    mu = jnp.mean(x, axis=-1, keepdims=True)
    xc = x - mu
    var = jnp.mean(xc * xc, axis=-1, keepdims=True)
    return xc * lax.rsqrt(var + LN_EPS)


def _dot(a, b):
    return jnp.dot(a, b, preferred_element_type=F32)


def _dot_nt(a, b):
    return lax.dot_general(a, b, (((1,), (1,)), ((), ())), preferred_element_type=F32)


def _lines_per_row(d):
    return d // LANES


def _load_rows(ref, lead, row0, n_rows, d):
    lpr = _lines_per_row(d)
    parts = [ref[lead + (pl.ds(row0 * lpr + c, n_rows, stride=lpr), slice(None))] for c in range(lpr)]
    return jnp.concatenate(parts, axis=1)


def _store_rows(ref, lead, row0, val):
    n_rows, d = val.shape
    lpr = _lines_per_row(d)
    for c in range(lpr):
        ref[lead + (pl.ds(row0 * lpr + c, n_rows, stride=lpr), slice(None))] = val[:, c * LANES:(c + 1) * LANES]


def _ada_kernel(c_ref, w_ref, b_ref, o_ref):
    s = jax.nn.silu(c_ref[...]).astype(BF16)
    o_ref[...] = _dot(s, w_ref[...].astype(BF16)) + b_ref[...]


def _ada_call(c_pad, w_ada, b_ada):
    rows, d = c_pad.shape
    n_out = w_ada.shape[1]
    return pl.pallas_call(
        _ada_kernel,
        grid=(n_out // ADA_TILE,),
        in_specs=[
            pl.BlockSpec((rows, d), lambda n: (0, 0)),
            pl.BlockSpec((d, ADA_TILE), lambda n: (0, n)),
            pl.BlockSpec((1, ADA_TILE), lambda n: (0, n)),
        ],
        out_specs=pl.BlockSpec((rows, ADA_TILE), lambda n: (0, n)),
        out_shape=jax.ShapeDtypeStruct((rows, n_out), F32),
        compiler_params=_cparams(("arbitrary",)),
        name="adaln",
    )(c_pad, w_ada, b_ada.reshape(1, n_out))


def _lower_bound(lb_ref, hd, layer):
    raw = lb_ref[hd]
    m = jnp.max(raw, axis=0, keepdims=True)
    e = jnp.exp(raw - m)
    p = e / jnp.sum(e, axis=0, keepdims=True)
    return jnp.sum(p[: layer + 1], axis=0, keepdims=True)


def _gmlp_rows(u_pre, v_pre, ln_g, ln_b):
    u = jax.nn.gelu(u_pre)
    v = jax.nn.gelu(v_pre)
    mu = jnp.mean(v, axis=-1, keepdims=True)
    vc = v - mu
    var = jnp.mean(vc * vc, axis=-1, keepdims=True)
    vn = vc * lax.rsqrt(var + LN_EPS) * ln_g + ln_b
    return u, vn


def _row_bcast(a, row, n):
    return jnp.broadcast_to(a[row:row + 1, :], (n, a.shape[1]))


def _block_id(idx, size):
    return lax.shift_right_logical(idx, I32(size.bit_length() - 1))


def _hgrn_masks(c):
    row = lax.broadcasted_iota(I32, (c, c), 0)
    col = lax.broadcasted_iota(I32, (c, c), 1)
    masks = []
    half = c // 2
    while half >= DIAG_BLOCK:
        span = 2 * half
        same = _block_id(row, span) == _block_id(col, span)
        masks.append(same & ((row & (span - 1)) >= half) & ((col & (span - 1)) < half))
        half //= 2
    diag = (_block_id(row, DIAG_BLOCK) == _block_id(col, DIAG_BLOCK)) & (col <= row)
    return masks, diag


def _tri_ones(c):
    row = lax.broadcasted_iota(I32, (c, c), 0)
    col = lax.broadcasted_iota(I32, (c, c), 1)
    return jnp.where(col <= row, 1.0, 0.0).astype(BF16)


def _hgrn_prepare(chains, c):
    tri = _tri_ones(c)
    units = []
    for st0, lbv, gn, blocks in chains:
        for q_pre, f_pre, i_pre, g_pre in blocks:
            q = jax.nn.silu(q_pre)
            fg = lbv + (1.0 - lbv) * jax.nn.sigmoid(f_pre)
            logf = jnp.log(fg)
            hi = logf.astype(BF16)
            lo = (logf - hi.astype(F32)).astype(BF16)
            units.append(dict(q=q, k=1.0 - fg, hilo=jnp.concatenate([hi, lo], axis=1),
                              v=i_pre, g=g_pre, gn=gn))

    for u in units:
        p = _dot(tri, u["hilo"])
        u["a"] = p[:, 0:HEAD_DIM] + p[:, HEAD_DIM:2 * HEAD_DIM]
    return units


def _hgrn_scores(units, c):
    masks, diag_mask = _hgrn_masks(c)
    for u in units:
        q, k, a = u["q"], u["k"], u["a"]
        parts = []
        half = c // 2
        level = 0
        while half >= DIAG_BLOCK:
            span = 2 * half
            ref = jnp.concatenate([_row_bcast(a, b * span + half, span) for b in range(c // span)], axis=0)
            qs = (q * jnp.exp(jnp.minimum(a - ref, 0.0))).astype(BF16)
            ks = (k * jnp.exp(jnp.minimum(ref - a, 0.0))).astype(BF16)
            parts.append((masks[level], _dot_nt(qs, ks)))
            half //= 2
            level += 1
        ref = jnp.concatenate([_row_bcast(a, b * DIAG_BLOCK, DIAG_BLOCK) for b in range(c // DIAG_BLOCK)], axis=0)
        qd = (q * jnp.exp(a - ref)).astype(BF16)
        kd = (k * jnp.exp(jnp.minimum(ref - a, 80.0))).astype(BF16)
        parts.append((diag_mask, _dot_nt(qd, kd)))
        u["parts"] = parts
        a_last = a[c - 1:c, :]
        u["decay"] = jnp.exp(a_last)
        kl = (k * jnp.exp(a_last - a)).astype(BF16)
        v_t = jnp.transpose(u["v"]).astype(BF16)
        u["upd"] = _dot(v_t, kl)
        u["qe"] = (q * jnp.exp(a)).astype(BF16)
    return units


def _hgrn_finish(chains, units):
    finals = []
    n = 0
    for st0, lbv, gn, blocks in chains:
        st = st0
        for _ in blocks:
            units[n]["st_in"] = st.astype(BF16)
            st = st * units[n]["decay"] + units[n]["upd"]
            n += 1
        finals.append(st)

    outs = []
    n = 0
    for st0, lbv, gn, blocks in chains:
        chain_out = []
        for _ in blocks:
            u = units[n]
            scores = None
            for mask, part in u["parts"]:
                part = jnp.where(mask, part, 0.0)
                scores = part if scores is None else scores + part
            o = _dot(scores.astype(BF16), u["v"].astype(BF16)) + _dot_nt(u["qe"], u["st_in"])
            o = o * lax.rsqrt(jnp.mean(o * o, axis=-1, keepdims=True) + LN_EPS) * u["gn"]
            chain_out.append(o * jax.nn.silu(u["g"]))
            n += 1
        outs.append(chain_out)
    return outs, finals


def _hgrn_chains(chains, c):
    return _hgrn_finish(chains, _hgrn_scores(_hgrn_prepare(chains, c), c))


def _proj_cols(kind, hd):
    c0 = (kind * HEADS_PER_STEP + hd) * HEAD_DIM
    return slice(c0, c0 + HEAD_DIM)


def _in_projection(h_scr, w_refs, proj_scr):
    width = HEADS_PER_STEP * HEAD_DIM
    for kind, w_ref in enumerate(w_refs):
        proj_scr[:, kind * width:(kind + 1) * width] = _dot(h_scr[...], w_ref[...])


def _mixer_prompt_kernel(x_ref, sh_ref, sc_ref, wu_ref, wv_ref, wq_ref, wf_ref, wi_ref, wg_ref,
                         lvg_ref, lvb_ref, ws_ref, bs_ref, lb_ref, gn_ref,
                         cata_ref, cato_ref, state_ref, h_scr, proj_a, proj_b, st_scr, *, layer, n_items, n_hb, n_t):
    s = pl.program_id(0)
    tm = h_scr.shape[0]
    proj_hb = lax.rem(jnp.minimum(s, n_items - 1), n_hb)
    post_item = jnp.maximum(s - 1, 0)
    post_hb = lax.rem(post_item, n_hb)
    post_time = lax.rem(post_item // n_hb, n_t)
    proj_slot = lax.rem(s, 2)

    @pl.when((proj_hb == 0) & (s < n_items))
    def _():
        def slab(r, carry):
            rows = pl.ds(pl.multiple_of(r * HGRN_BLOCK, HGRN_BLOCK), HGRN_BLOCK)
            h = _ln_plain(x_ref[0, rows, :]) * (1.0 + sc_ref[0]) + sh_ref[0]
            h_scr[rows, :] = h.astype(BF16)
            return carry
        lax.fori_loop(0, tm // HGRN_BLOCK, slab, 0)

    @pl.when(s == 0)
    def _():
        proj_b[...] = jnp.zeros_like(proj_b)

    @pl.when((post_time == 0) & (post_hb == 0))
    def _():
        st_scr[...] = jnp.zeros_like(st_scr)

    def step(proj_ref, post_ref):
        width = HEADS_PER_STEP * HEAD_DIM
        w_refs = (wu_ref, wv_ref, wq_ref, wf_ref, wi_ref, wg_ref)

        def project(kinds):
            for kind in kinds:
                proj_ref[:, kind * width:(kind + 1) * width] = _dot(h_scr[...], w_refs[kind][...])

        def post(rows, kind, hd):
            return post_ref[rows, _proj_cols(kind, hd)]

        project((0, 1, 2))

        row = lax.broadcasted_iota(I32, (GMLP_CHUNK, GMLP_CHUNK), 0)
        col = lax.broadcasted_iota(I32, (GMLP_CHUNK, GMLP_CHUNK), 1)
        causal = _block_id(row, SUB_CHUNK) >= _block_id(col, SUB_CHUNK)
        for hd in range(HEADS_PER_STEP):
            wm = jnp.where(causal, ws_ref[hd], 0.0).astype(BF16)
            b_col = bs_ref[hd]
            for c in range(tm // GMLP_CHUNK):
                rows = slice(c * GMLP_CHUNK, (c + 1) * GMLP_CHUNK)
                u, vn = _gmlp_rows(post(rows, 0, hd), post(rows, 1, hd), lvg_ref[hd], lvb_ref[hd])
                sp = _dot(wm, vn.astype(BF16)) + b_col
                cata_ref[rows, hd * HEAD_DIM:(hd + 1) * HEAD_DIM] = (u * sp).astype(BF16)

        chains = []
        for hd in range(HEADS_PER_STEP):
            blocks = [tuple(post(slice(c * HGRN_BLOCK, (c + 1) * HGRN_BLOCK), kind, hd) for kind in range(2, N_KINDS))
                      for c in range(tm // HGRN_BLOCK)]
            chains.append((st_scr[HEADS_PER_STEP * post_hb + hd], _lower_bound(lb_ref, hd, layer), gn_ref[hd], blocks))
        units = _hgrn_prepare(chains, HGRN_BLOCK)
        project((3, 4))
        units = _hgrn_scores(units, HGRN_BLOCK)
        project((5,))
        outs, finals = _hgrn_finish(chains, units)
        for hd in range(HEADS_PER_STEP):
            for c, out in enumerate(outs[hd]):
                cato_ref[c * HGRN_BLOCK:(c + 1) * HGRN_BLOCK, hd * HEAD_DIM:(hd + 1) * HEAD_DIM] = out.astype(BF16)
            st_scr[HEADS_PER_STEP * post_hb + hd] = finals[hd]
            state_ref[0, HEADS_PER_STEP * post_hb + hd] = jnp.transpose(finals[hd])

    @pl.when(proj_slot == 0)
    def _():
        step(proj_a, proj_b)

    @pl.when(proj_slot == 1)
    def _():
        step(proj_b, proj_a)


def _w_in_specs(d):
    width = HEADS_PER_STEP * HEAD_DIM
    blocks_per_kind = N_HEADS // HEADS_PER_STEP
    return [pl.BlockSpec((d, width), lambda j, kind=kind: (0, kind * blocks_per_kind + j)) for kind in range(N_KINDS)]


def _mixer_prompt_call(x, sh1, sc1, w_in_b, ln_v_g, ln_v_b, w_s, b_s, lb_h, gnorm_g, layer):
    bsz, t, d = x.shape
    n_t = t // MIX_TILE
    hp = HEADS_PER_STEP
    n_hb = N_HEADS // hp
    n_items = bsz * n_t * n_hb
    width = hp * HEAD_DIM
    kern = functools.partial(_mixer_prompt_kernel, layer=layer, n_items=n_items, n_hb=n_hb, n_t=n_t)
    proj_item = lambda s: jnp.minimum(s, n_items - 1)
    post_item = lambda s: jnp.maximum(s - 1, 0)
    proj_tile = lambda s: proj_item(s) // n_hb
    post_tile = lambda s: post_item(s) // n_hb
    proj_batch = lambda s: (proj_tile(s) // n_t, 0, 0)
    post_vec = lambda s: (post_item(s) % n_hb, 0, 0)
    w_spec = lambda kind: pl.BlockSpec((d, width), lambda s: (0, kind * n_hb + proj_item(s) % n_hb))
    return pl.pallas_call(
        kern,
        grid=(n_items + 1,),
        in_specs=[
            pl.BlockSpec((1, MIX_TILE, d), lambda s: (proj_tile(s) // n_t, proj_tile(s) % n_t, 0)),
            pl.BlockSpec((1, 1, d), proj_batch),
            pl.BlockSpec((1, 1, d), proj_batch),
            *[w_spec(kind) for kind in range(N_KINDS)],
            pl.BlockSpec((hp, 1, HEAD_DIM), post_vec),
            pl.BlockSpec((hp, 1, HEAD_DIM), post_vec),
            pl.BlockSpec((hp, GMLP_CHUNK, GMLP_CHUNK), post_vec),
            pl.BlockSpec((hp, GMLP_CHUNK, 1), post_vec),
            pl.BlockSpec((hp, lb_h.shape[1], HEAD_DIM), post_vec),
            pl.BlockSpec((hp, 1, HEAD_DIM), post_vec),
        ],
        out_specs=[
            pl.BlockSpec((MIX_TILE, width), lambda s: (post_tile(s), post_item(s) % n_hb)),
            pl.BlockSpec((MIX_TILE, width), lambda s: (post_tile(s), post_item(s) % n_hb)),
            pl.BlockSpec((1, N_HEADS, HEAD_DIM, HEAD_DIM), lambda s: (post_tile(s) // n_t, 0, 0, 0)),
        ],
        out_shape=[
            jax.ShapeDtypeStruct((bsz * t, N_HEADS * HEAD_DIM), BF16),
            jax.ShapeDtypeStruct((bsz * t, N_HEADS * HEAD_DIM), BF16),
            jax.ShapeDtypeStruct((bsz, N_HEADS, HEAD_DIM, HEAD_DIM), F32),
        ],
        scratch_shapes=[
            pltpu.VMEM((MIX_TILE, d), BF16),
            pltpu.VMEM((MIX_TILE, N_KINDS * width), F32),
            pltpu.VMEM((MIX_TILE, N_KINDS * width), F32),
            pltpu.VMEM((N_HEADS, HEAD_DIM, HEAD_DIM), F32),
        ],
        compiler_params=_cparams(("arbitrary",)),
        name="mixer_prompt",
    )(x, sh1.reshape(bsz, 1, d), sc1.reshape(bsz, 1, d), *([w_in_b] * N_KINDS), ln_v_g, ln_v_b, w_s, b_s, lb_h, gnorm_g)


def _mixer_sample_kernel(x_ref, sh_ref, sc_ref, wu_ref, wv_ref, wq_ref, wf_ref, wi_ref, wg_ref,
                         lvg_ref, lvb_ref, ws_ref, bs_ref, lb_ref, gn_ref, s0_ref,
                         cata_ref, cato_ref, state_ref, vrows_ref, h_scr, proj_scr, *, layer, seq):
    j = pl.program_id(0)
    n_streams = x_ref.shape[0] // seq

    @pl.when(j == 0)
    def _():
        def slab(r, carry):
            rows = pl.ds(pl.multiple_of(r * seq, seq), seq)
            h = _ln_plain(x_ref[rows, :]) * (1.0 + sc_ref[pl.ds(r, 1), :]) + sh_ref[pl.ds(r, 1), :]
            h_scr[rows, :] = h.astype(BF16)
            return carry
        lax.fori_loop(0, n_streams, slab, 0)

    _in_projection(h_scr, (wu_ref, wv_ref, wq_ref, wf_ref, wi_ref, wg_ref), proj_scr)

    chains = []
    for hd in range(HEADS_PER_STEP):
        wm = ws_ref[hd][0:seq, 0:seq].astype(BF16)
        b_col = bs_ref[hd][0:seq, :]
        lbv = _lower_bound(lb_ref, hd, layer)
        for n in range(n_streams):
            rows = slice(n * seq, (n + 1) * seq)
            u, vn = _gmlp_rows(proj_scr[rows, _proj_cols(0, hd)], proj_scr[rows, _proj_cols(1, hd)],
                               lvg_ref[hd], lvb_ref[hd])
            vrows_ref[rows, hd * HEAD_DIM:(hd + 1) * HEAD_DIM] = vn
            sp = _dot(wm, vn.astype(BF16)) + b_col
            cata_ref[rows, hd * HEAD_DIM:(hd + 1) * HEAD_DIM] = (u * sp).astype(BF16)
            chains.append((jnp.transpose(s0_ref[n, hd]), lbv, gn_ref[hd],
                           [tuple(proj_scr[rows, _proj_cols(kind, hd)] for kind in range(2, N_KINDS))]))

    outs, finals = _hgrn_chains(chains, seq)
    for hd in range(HEADS_PER_STEP):
        for n in range(n_streams):
            r0 = n * seq
            m = hd * n_streams + n
            cato_ref[r0:r0 + seq, hd * HEAD_DIM:(hd + 1) * HEAD_DIM] = outs[m][0].astype(BF16)
            state_ref[n, hd] = jnp.transpose(finals[m])


def _mixer_sample_call(x2d, sh1, sc1, w_in_b, ln_v_g, ln_v_b, w_s, b_s, lb_h, gnorm_g, s0, layer, seq):
    rows, d = x2d.shape
    n_streams = rows // seq
    hp = HEADS_PER_STEP
    kern = functools.partial(_mixer_sample_kernel, layer=layer, seq=seq)
    head_vec = lambda j: (j, 0, 0)
    return pl.pallas_call(
        kern,
        grid=(N_HEADS // hp,),
        in_specs=[
            pl.BlockSpec((rows, d), lambda j: (0, 0)),
            pl.BlockSpec((n_streams, d), lambda j: (0, 0)),
            pl.BlockSpec((n_streams, d), lambda j: (0, 0)),
            *_w_in_specs(d),
            pl.BlockSpec((hp, 1, HEAD_DIM), head_vec),
            pl.BlockSpec((hp, 1, HEAD_DIM), head_vec),
            pl.BlockSpec((hp, GMLP_CHUNK, GMLP_CHUNK), head_vec),
            pl.BlockSpec((hp, GMLP_CHUNK, 1), head_vec),
            pl.BlockSpec((hp, lb_h.shape[1], HEAD_DIM), head_vec),
            pl.BlockSpec((hp, 1, HEAD_DIM), head_vec),
            pl.BlockSpec((n_streams, hp, HEAD_DIM, HEAD_DIM), lambda j: (0, j, 0, 0)),
        ],
        out_specs=[
            pl.BlockSpec((rows, hp * HEAD_DIM), lambda j: (0, j)),
            pl.BlockSpec((rows, hp * HEAD_DIM), lambda j: (0, j)),
            pl.BlockSpec((n_streams, hp, HEAD_DIM, HEAD_DIM), lambda j: (0, j, 0, 0)),
            pl.BlockSpec((rows, hp * HEAD_DIM), lambda j: (0, j)),
        ],
        out_shape=[
            jax.ShapeDtypeStruct((rows, N_HEADS * HEAD_DIM), BF16),
            jax.ShapeDtypeStruct((rows, N_HEADS * HEAD_DIM), BF16),
            jax.ShapeDtypeStruct((n_streams, N_HEADS, HEAD_DIM, HEAD_DIM), F32),
            jax.ShapeDtypeStruct((rows, N_HEADS * HEAD_DIM), F32),
        ],
        scratch_shapes=[
            pltpu.VMEM((rows, d), BF16),
            pltpu.VMEM((rows, N_KINDS * hp * HEAD_DIM), F32),
        ],
        compiler_params=_cparams(("arbitrary",)),
        name="mixer_sample",
    )(x2d, sh1, sc1, *([w_in_b] * N_KINDS), ln_v_g, ln_v_b, w_s, b_s, lb_h, gnorm_g, s0)


def _post_kernel(cap_ref, cop_ref, cas_ref, cos_ref, xp_ref, xs_ref, g1_ref, sh2_ref, sc2_ref, wout_ref, l1g_ref, l1b_ref,
                 wr_ref, br_ref, tri_ref,
                 x1_ref, h2_ref, meta_ref, wts_ref, cnt_ref,
                 cat_scr, mix_next, mix_cur, hhi_scr, hlo_scr, run_scr, *, n_prompt_tiles, n_tiles, alpha):
    s = pl.program_id(0)
    tm, d = mix_cur.shape
    d_a = cap_ref.shape[1]
    post_is_s = s > n_prompt_tiles
    n_slabs = tm // MOD_GROUP
    trips = n_slabs // 2
    chunk = d // trips

    @pl.when(s == 0)
    def _():
        run_scr[...] = jnp.zeros_like(run_scr)

    @pl.when(s >= n_prompt_tiles)
    def _():
        cat_scr[:, 0:d_a] = cas_ref[...]
        cat_scr[:, d_a:2 * d_a] = cos_ref[...]

    @pl.when(s < n_prompt_tiles)
    def _():
        cat_scr[:, 0:d_a] = cap_ref[...]
        cat_scr[:, d_a:2 * d_a] = cop_ref[...]

    def slab(g):
        rows = pl.ds(pl.multiple_of(g * MOD_GROUP, MOD_GROUP), MOD_GROUP)
        x = jnp.where(post_is_s, xs_ref[rows, :], xp_ref[rows, :])
        z = alpha * x + g1_ref[pl.ds(g, 1), :] * mix_cur[rows, :]
        x1 = _ln_plain(z) * l1g_ref[...] + l1b_ref[...]
        x1_ref[rows, :] = x1
        h2 = _ln_plain(x1) * (1.0 + sc2_ref[pl.ds(g, 1), :]) + sh2_ref[pl.ds(g, 1), :]
        h2_ref[rows, :] = h2
        hi = h2.astype(BF16)
        hhi_scr[rows, :] = hi
        hlo_scr[rows, :] = (h2 - hi.astype(F32)).astype(BF16)

    def project(q):
        cols = pl.ds(pl.multiple_of(q * chunk, chunk), chunk)
        mix_next[:, cols] = _dot(cat_scr[...], wout_ref[:, cols])

    def both(q, carry):
        slab(2 * q)
        slab(2 * q + 1)
        project(q)
        return carry

    def slabs_only(q, carry):
        slab(2 * q)
        slab(2 * q + 1)
        return carry

    @pl.when(s == 0)
    def _():
        mix_next[...] = _dot(cat_scr[...], wout_ref[...])

    @pl.when(s >= 1)
    def _():
        mix_cur[...] = mix_next[...]

    @pl.when((s >= 1) & (s < n_tiles))
    def _():
        lax.fori_loop(0, trips, both, 0, unroll=SLAB_UNROLL // 2)

    @pl.when(s == n_tiles)
    def _():
        lax.fori_loop(0, trips, slabs_only, 0, unroll=SLAB_UNROLL // 2)

    @pl.when(s >= 1)
    def _():
        _route(hhi_scr, hlo_scr, wr_ref, br_ref, tri_ref, run_scr, meta_ref, wts_ref, cnt_ref)


def _route(hhi_scr, hlo_scr, wr_ref, br_ref, tri_ref, run_scr, meta_ref, wts_ref, cnt_ref):
    tm = hhi_scr.shape[0]
    n_r = N_GROUPS + N_EXPERTS
    s = _dot(hhi_scr[...], wr_ref[...]) + _dot(hlo_scr[...], wr_ref[...])
    logits = s + pltpu.roll(s, LANES - n_r, axis=1) + br_ref[...]

    lane = lax.broadcasted_iota(I32, (tm, LANES), 1)
    lane_f = lane.astype(F32)
    neg = jnp.float32(-jnp.inf)
    big = jnp.float32(LANES)

    def first_lane_of(vals, top):
        return jnp.min(jnp.where(vals == top, lane_f, big), axis=1, keepdims=True)

    gmask = lane < N_GROUPS
    gl = jnp.where(gmask, logits, neg)
    gmax = jnp.max(gl, axis=1, keepdims=True)
    gsel = first_lane_of(gl, gmax)
    p_group = 1.0 / jnp.sum(jnp.exp(gl - gmax), axis=1, keepdims=True)

    e_lo = N_GROUPS + EXPERTS_PER_GROUP * gsel.astype(I32)
    emask = (lane >= e_lo) & (lane < e_lo + EXPERTS_PER_GROUP)
    el = jnp.where(emask, logits, neg)
    t1 = jnp.max(el, axis=1, keepdims=True)
    i1 = first_lane_of(el, t1)
    el2 = jnp.where(lane_f == i1, neg, el)
    t2 = jnp.max(el2, axis=1, keepdims=True)
    i2 = first_lane_of(el2, t2)
    e2 = jnp.exp(t2 - t1)
    den = 1.0 + e2
    w1 = (1.0 / den) * p_group
    w2 = (e2 / den) * p_group

    sel1 = lane_f == i1
    sel2 = lane_f == i2
    onehot = jnp.where(sel1 | sel2, 1.0, 0.0)
    before = _dot(tri_ref[...], onehot.astype(BF16)) + run_scr[...]
    r1 = jnp.sum(jnp.where(sel1, before, 0.0), axis=1, keepdims=True)
    r2 = jnp.sum(jnp.where(sel2, before, 0.0), axis=1, keepdims=True)
    run_scr[...] = run_scr[...] + jnp.sum(onehot, axis=0, keepdims=True)
    cnt_ref[...] = run_scr[...].astype(I32)

    base = jnp.float32(N_GROUPS)
    meta = jnp.where(lane == 0, i1 - base,
                     jnp.where(lane == 1, i2 - base,
                               jnp.where(lane == 2, r1, jnp.where(lane == 3, r2, 0.0))))
    meta_ref[0] = jnp.transpose(meta)[0:META_ROWS, :].astype(I32)
    wts_ref[...] = jnp.where(lane == 0, w1, jnp.where(lane == 1, w2, 0.0))


def _post_call(cats_p, cats_s, x_p, x_s, g1g, sh2g, sc2g, w_out_b, ln1_g, ln1_b, wr, br, tri, alpha):
    n_p, d = x_p.shape
    n_s = x_s.shape[0]
    d_a = cats_p[0].shape[1]
    tm = POST_TILE
    n_pt = n_p // tm
    n_tiles = n_pt + n_s // tm
    n_tok = n_p + n_s
    groups = tm // MOD_GROUP
    kern = functools.partial(_post_kernel, n_prompt_tiles=n_pt, n_tiles=n_tiles, alpha=alpha)
    p_cat = lambda s: (jnp.minimum(s, n_pt - 1), 0)
    s_cat = lambda s: (jnp.clip(s - n_pt, 0, n_tiles - n_pt - 1), 0)
    p_x = lambda s: (jnp.clip(s - 1, 0, n_pt - 1), 0)
    s_x = lambda s: (jnp.clip(s - 1 - n_pt, 0, n_tiles - n_pt - 1), 0)
    tile = lambda s: (jnp.maximum(s - 1, 0), 0)
    const = lambda s: (0, 0)
    once = pl.Buffered(1)
    return pl.pallas_call(
        kern,
        grid=(n_tiles + 1,),
        in_specs=[
            pl.BlockSpec((tm, d_a), p_cat),
            pl.BlockSpec((tm, d_a), p_cat),
            pl.BlockSpec((tm, d_a), s_cat),
            pl.BlockSpec((tm, d_a), s_cat),
            pl.BlockSpec((tm, d), p_x),
            pl.BlockSpec((tm, d), s_x),
            pl.BlockSpec((groups, d), tile),
            pl.BlockSpec((groups, d), tile),
            pl.BlockSpec((groups, d), tile),
            pl.BlockSpec((d, d), const, pipeline_mode=once),
            pl.BlockSpec((1, d), const),
            pl.BlockSpec((1, d), const),
            pl.BlockSpec((d, LANES), const, pipeline_mode=once),
            pl.BlockSpec((1, LANES), const),
            pl.BlockSpec((tm, tm), const, pipeline_mode=once),
        ],
        out_specs=[
            pl.BlockSpec((tm, d), tile),
            pl.BlockSpec((tm, d), tile),
            pl.BlockSpec((1, META_ROWS, tm), lambda s: (jnp.maximum(s - 1, 0), 0, 0)),
            pl.BlockSpec((tm, LANES), tile),
            pl.BlockSpec((1, LANES), const),
        ],
        out_shape=[
            jax.ShapeDtypeStruct((n_tok, d), F32),
            jax.ShapeDtypeStruct((n_tok, d), F32),
            jax.ShapeDtypeStruct((n_tiles, META_ROWS, tm), I32),
            jax.ShapeDtypeStruct((n_tok, LANES), F32),
            jax.ShapeDtypeStruct((1, LANES), I32),
        ],
        scratch_shapes=[
            pltpu.VMEM((tm, 2 * d_a), BF16),
            pltpu.VMEM((tm, d), F32),
            pltpu.VMEM((tm, d), F32),
            pltpu.VMEM((tm, d), BF16),
            pltpu.VMEM((tm, d), BF16),
            pltpu.VMEM((1, LANES), F32),
        ],
        compiler_params=_cparams(("arbitrary",)),
        name="post_router",
    )(*cats_p, *cats_s, x_p, x_s, g1g, sh2g, sc2g, w_out_b, ln1_g, ln1_b, wr, br, tri)


def _dispatch_kernel(slot_ref, gap_start_ref, gap_len_ref, h_ref, xs_ref, zero_scr, sem, tile_sem, *, gaps_per_step):
    i = pl.program_id(0)
    tm = h_ref.shape[0]
    n_gaps = gap_len_ref.shape[0]
    tile_rows = zero_scr.shape[0]

    @pl.when(i == 0)
    def _():
        zero_scr[...] = jnp.zeros_like(zero_scr)

    def row_copy(r, k):
        return pltpu.make_async_copy(h_ref.at[pl.ds(r, 1)], xs_ref.at[pl.ds(slot_ref[0, 0, 2 * r + k], 1)], sem)

    def start_rows(r, carry):
        row_copy(r, 0).start(priority=0)
        row_copy(r, 1).start(priority=1)
        return carry
    lax.fori_loop(0, tm, start_rows, 0, unroll=ROW_DMA_UNROLL)

    def gap_rows(g):
        return jnp.where(g < n_gaps - 1, gap_len_ref[jnp.minimum(g, n_gaps - 1)], 0)

    def gap_tiles(g):
        return jnp.where(g == n_gaps - 1, gap_len_ref[n_gaps - 1] // tile_rows, 0)

    def zero_row_copy(g, r):
        return pltpu.make_async_copy(zero_scr.at[pl.ds(0, 1)], xs_ref.at[pl.ds(gap_start_ref[g] + r, 1)], sem)

    def zero_tile_copy(g, t):
        start = pl.multiple_of(gap_start_ref[g] + t * tile_rows, tile_rows)
        return pltpu.make_async_copy(zero_scr, xs_ref.at[pl.ds(start, tile_rows)], tile_sem)

    for j in range(gaps_per_step):
        g = jnp.minimum(i * gaps_per_step + j, n_gaps - 1)
        live = i * gaps_per_step + j < n_gaps
        n_rows = jnp.where(live, gap_rows(g), 0)
        n_til = jnp.where(live, gap_tiles(g), 0)
        lax.fori_loop(0, n_rows, lambda r, c, g=g: (zero_row_copy(g, r).start(), c)[1], 0)
        lax.fori_loop(0, n_til, lambda t, c, g=g: (zero_tile_copy(g, t).start(), c)[1], 0)

    def wait_rows(r, carry):
        row_copy(r, 0).wait()
        row_copy(r, 1).wait()
        return carry
    lax.fori_loop(0, tm, wait_rows, 0, unroll=ROW_DMA_UNROLL)

    for j in range(gaps_per_step):
        g = jnp.minimum(i * gaps_per_step + j, n_gaps - 1)
        live = i * gaps_per_step + j < n_gaps
        n_rows = jnp.where(live, gap_rows(g), 0)
        n_til = jnp.where(live, gap_tiles(g), 0)
        lax.fori_loop(0, n_rows, lambda r, c, g=g: (zero_row_copy(g, r).wait(), c)[1], 0)
        lax.fori_loop(0, n_til, lambda t, c, g=g: (zero_tile_copy(g, t).wait(), c)[1], 0)


def _dispatch_call(slots3, gap_start, gap_len, h2, n_sorted):
    n_tok, d = h2.shape
    n_tiles = n_tok // TOK_TILE
    n_gaps = gap_len.shape[0]
    kern = functools.partial(_dispatch_kernel, gaps_per_step=-(-n_gaps // n_tiles))
    return pl.pallas_call(
        kern,
        grid=(n_tiles,),
        in_specs=[
            pl.BlockSpec((1, 1, slots3.shape[2]), lambda i: (i, 0, 0), memory_space=pltpu.SMEM),
            pl.BlockSpec((n_gaps,), lambda i: (0,), memory_space=pltpu.SMEM),
            pl.BlockSpec((n_gaps,), lambda i: (0,), memory_space=pltpu.SMEM),
            pl.BlockSpec((TOK_TILE, d), lambda i: (i, 0)),
        ],
        out_specs=pl.BlockSpec(memory_space=pl.ANY),
        out_shape=jax.ShapeDtypeStruct((n_sorted, d), F32),
        scratch_shapes=[pltpu.VMEM((EXP_TILE, d), F32), pltpu.SemaphoreType.DMA(()), pltpu.SemaphoreType.DMA(())],
        compiler_params=_cparams(("arbitrary",)),
        name="dispatch",
    )(slots3, gap_start, gap_len, h2)


def _expert_kernel(te_ref, nt_ref, xs_ref, w1_ref, w3_ref, w2_ref, ys_ref, w13_scr, w2_scr):
    t = pl.program_id(0)
    d_exp = w1_ref.shape[2]
    prev = te_ref[jnp.maximum(t - 1, 0)]
    changed = (t == 0) | (te_ref[t] != prev)
    valid = t < nt_ref[0]

    @pl.when(changed)
    def _():
        w13_scr[:, 0:d_exp] = w1_ref[0].astype(BF16)
        w13_scr[:, d_exp:2 * d_exp] = w3_ref[0].astype(BF16)
        w2_scr[...] = w2_ref[0].astype(BF16)

    @pl.when(valid)
    def _():
        h13 = _dot(xs_ref[...].astype(BF16), w13_scr[...])
        hm = jax.nn.silu(h13[:, 0:d_exp]) * h13[:, d_exp:2 * d_exp]
        _store_rows(ys_ref, (), 0, _dot(hm.astype(BF16), w2_scr[...]))

    @pl.when(jnp.logical_not(valid))
    def _():
        ys_ref[...] = jnp.zeros_like(ys_ref)


def _expert_call(tile_expert, n_valid, xs, w1, w3, w2):
    n_sorted, d = xs.shape
    d_exp = w1.shape[2]
    n_tiles = n_sorted // EXP_TILE
    grid_spec = pltpu.PrefetchScalarGridSpec(
        num_scalar_prefetch=2,
        grid=(n_tiles,),
        in_specs=[
            pl.BlockSpec((EXP_TILE, d), lambda t, te, nt: (t, 0)),
            pl.BlockSpec((1, d, d_exp), lambda t, te, nt: (te[t], 0, 0)),
            pl.BlockSpec((1, d, d_exp), lambda t, te, nt: (te[t], 0, 0)),
            pl.BlockSpec((1, d_exp, d), lambda t, te, nt: (te[t], 0, 0)),
        ],
        out_specs=pl.BlockSpec((EXP_TILE * _lines_per_row(d), LANES), lambda t, te, nt: (t, 0)),
        scratch_shapes=[pltpu.VMEM((d, 2 * d_exp), BF16), pltpu.VMEM((d_exp, d), BF16)],
    )
    return pl.pallas_call(
        _expert_kernel,
        grid_spec=grid_spec,
        out_shape=jax.ShapeDtypeStruct((n_sorted * _lines_per_row(d), LANES), F32),
        compiler_params=_cparams(("arbitrary",)),
        name="experts",
    )(tile_expert, n_valid, xs, w1, w3, w2)


def _combine_kernel(scur_ref, snext_ref, x1_ref, wts_ref, g2_ref, l2g_ref, l2b_ref, ys_ref, outp_ref, outs_ref,
                    y_scr, sem, *, n_prompt_tiles, alpha):
    i = pl.program_id(0)
    n_i = pl.num_programs(0)
    is_s = i >= n_prompt_tiles
    tm, d = x1_ref.shape
    lpr = _lines_per_row(d)
    cur = lax.rem(i, 2)

    def row_lines(r):
        return pl.ds(pl.multiple_of(r * lpr, lpr), lpr)

    def start_gathers(s_ref, buf):
        def body(r, carry):
            for k in range(2):
                pltpu.make_async_copy(ys_ref.at[row_lines(s_ref[0, 0, 2 * r + k])], y_scr.at[buf, k, row_lines(r)],
                                      sem.at[buf, k]).start(priority=k)
            return carry
        lax.fori_loop(0, tm, body, 0, unroll=ROW_DMA_UNROLL // 2)

    @pl.when(i == 0)
    def _():
        start_gathers(scur_ref, 0)

    @pl.when(i + 1 < n_i)
    def _():
        start_gathers(snext_ref, 1 - cur)

    for k in range(2):
        pltpu.make_async_copy(y_scr.at[cur, k], y_scr.at[cur, k], sem.at[cur, k]).wait()

    def slab(g, carry, out_ref):
        rows = pl.ds(pl.multiple_of(g * MOD_GROUP, MOD_GROUP), MOD_GROUP)
        w = wts_ref[rows, :]
        moe = (w[:, 0:1] * _load_rows(y_scr, (cur, 0), g * MOD_GROUP, MOD_GROUP, d)
               + w[:, 1:2] * _load_rows(y_scr, (cur, 1), g * MOD_GROUP, MOD_GROUP, d))
        z = alpha * x1_ref[rows, :] + g2_ref[pl.ds(g, 1), :] * moe
        out_ref[rows, :] = _ln_plain(z) * l2g_ref[...] + l2b_ref[...]
        return carry

    @pl.when(is_s)
    def _():
        lax.fori_loop(0, tm // MOD_GROUP, functools.partial(slab, out_ref=outs_ref), 0, unroll=SLAB_UNROLL)

    @pl.when(jnp.logical_not(is_s))
    def _():
        lax.fori_loop(0, tm // MOD_GROUP, functools.partial(slab, out_ref=outp_ref), 0, unroll=SLAB_UNROLL)


def _combine_call(slots3, x1, wts, g2g, ln2_g, ln2_b, ys, n_p, alpha):
    n_tok, d = x1.shape
    n_tiles = n_tok // TOK_TILE
    n_pt = n_p // TOK_TILE
    groups = TOK_TILE // MOD_GROUP
    kern = functools.partial(_combine_kernel, n_prompt_tiles=n_pt, alpha=alpha)
    slot_tile = lambda ahead: pl.BlockSpec(
        (1, 1, slots3.shape[2]), lambda i: (jnp.minimum(i + ahead, n_tiles - 1), 0, 0), memory_space=pltpu.SMEM)
    return pl.pallas_call(
        kern,
        grid=(n_tiles,),
        in_specs=[
            slot_tile(0), slot_tile(1),
            pl.BlockSpec((TOK_TILE, d), lambda i: (i, 0)),
            pl.BlockSpec((TOK_TILE, LANES), lambda i: (i, 0)),
            pl.BlockSpec((groups, d), lambda i: (i, 0)),
            pl.BlockSpec((1, d), lambda i: (0, 0)),
            pl.BlockSpec((1, d), lambda i: (0, 0)),
            pl.BlockSpec(memory_space=pl.ANY),
        ],
        out_specs=[
            pl.BlockSpec((TOK_TILE, d), lambda i: (jnp.minimum(i, n_pt - 1), 0)),
            pl.BlockSpec((TOK_TILE, d), lambda i: (jnp.maximum(i - n_pt, 0), 0)),
        ],
        out_shape=[
            jax.ShapeDtypeStruct((n_p, d), F32),
            jax.ShapeDtypeStruct((n_tok - n_p, d), F32),
        ],
        scratch_shapes=[
            pltpu.VMEM((2, 2, TOK_TILE * _lines_per_row(d), LANES), F32),
            pltpu.SemaphoreType.DMA((2, 2)),
        ],
        compiler_params=_cparams(("arbitrary",)),
        name="combine",
    )(slots3, slots3, x1, wts, g2g, ln2_g, ln2_b, ys)


def _routing_tables(meta, cnt_row, n_sorted):
    n_tok = meta.shape[0] * meta.shape[2]
    experts = jnp.transpose(meta[:, 0:2, :], (0, 2, 1)).reshape(n_tok, 2)
    ranks = jnp.transpose(meta[:, 2:4, :], (0, 2, 1)).reshape(n_tok, 2)
    cnt = cnt_row[0, N_GROUPS:N_GROUPS + N_EXPERTS]
    padded = ((cnt + EXP_TILE - 1) // EXP_TILE) * EXP_TILE
    ends = jnp.cumsum(padded)
    offs = ends - padded
    slots = offs[experts] + ranks
    total = ends[-1]
    n_tiles = n_sorted // EXP_TILE
    tile_ids = jnp.arange(n_tiles, dtype=I32)
    tile_expert = jnp.minimum(
        jnp.sum((tile_ids[:, None] >= (ends // EXP_TILE)[None, :]).astype(I32), axis=1), N_EXPERTS - 1)
    n_valid = (total // EXP_TILE).reshape(1).astype(I32)
    gap_start = jnp.concatenate([offs + cnt, total.reshape(1)]).astype(I32)
    gap_len = jnp.concatenate([padded - cnt, (n_sorted - total).reshape(1)]).astype(I32)
    return slots.astype(I32), tile_expert, n_valid, gap_start, gap_len


def _layer(layer, n_layers, xp, xs, s0_l, c_all, p):
    (w_ada, b_ada, w_in, ln_v_g, ln_v_b, w_s, b_s, hgrn_lb, gnorm_g, w_out, ln1_g, ln1_b,
     w_rg, b_rg, w_re, b_re, w1, w3, w2, ln2_g, ln2_b) = p
    bsz, t, d = xp.shape
    n_streams, seq, _ = xs.shape
    alpha = float((2.0 * n_layers) ** 0.25)
    n_p = bsz * t
    n_s = n_streams * seq
    n_tok = n_p + n_s

    n_c = c_all.shape[0]
    c_pad = jnp.pad(c_all, ((0, (-n_c) % 8), (0, 0)))
    mod = _ada_call(c_pad, w_ada, b_ada)[:n_c]
    sh1, sc1, g1, sh2, sc2, g2 = [mod[:, m * d:(m + 1) * d] for m in range(6)]

    w_in_b = w_in.astype(BF16)
    w_out_b = w_out.astype(BF16)
    lvg = ln_v_g.reshape(N_HEADS, 1, HEAD_DIM)
    lvb = ln_v_b.reshape(N_HEADS, 1, HEAD_DIM)
    b_s3 = b_s.reshape(N_HEADS, GMLP_CHUNK, 1)
    lb_h = jnp.transpose(hgrn_lb.reshape(hgrn_lb.shape[0], N_HEADS, HEAD_DIM), (1, 0, 2))
    gn = gnorm_g.reshape(N_HEADS, 1, HEAD_DIM)

    ca_p, co_p, state_p = _mixer_prompt_call(xp, sh1[:bsz], sc1[:bsz], w_in_b, lvg, lvb, w_s, b_s3, lb_h, gn, layer)
    ca_s, co_s, state_s, vrows = _mixer_sample_call(xs.reshape(n_s, d), sh1[bsz:], sc1[bsz:], w_in_b, lvg, lvb, w_s,
                                               b_s3, lb_h, gn, s0_l, layer, seq)

    group_stream = jnp.concatenate([
        jnp.repeat(jnp.arange(bsz, dtype=I32), t // MOD_GROUP),
        bsz + jnp.repeat(jnp.arange(n_streams, dtype=I32), seq // MOD_GROUP)])
    g1g, sh2g, sc2g, g2g = [m[group_stream] for m in (g1, sh2, sc2, g2)]

    wr = jnp.concatenate([w_rg, w_re], axis=1)
    wr_hi = wr.astype(BF16)
    wr_lo = (wr - wr_hi.astype(F32)).astype(BF16)
    n_r = wr.shape[1]
    wr_cat = jnp.concatenate([wr_hi, wr_lo, jnp.zeros((d, LANES - 2 * n_r), BF16)], axis=1)
    br = jnp.pad(jnp.concatenate([b_rg, b_re]), (0, LANES - n_r)).reshape(1, LANES)
    tri = jnp.tril(jnp.ones((POST_TILE, POST_TILE), F32), -1).astype(BF16)

    x1, h2, meta, wts, cnt_row = _post_call((ca_p, co_p), (ca_s, co_s), xp.reshape(n_p, d), xs.reshape(n_s, d),
                                            g1g, sh2g, sc2g, w_out_b, ln1_g.reshape(1, d), ln1_b.reshape(1, d),
                                            wr_cat, br, tri, alpha)

    n_sorted = 2 * n_tok + N_EXPERTS * EXP_TILE
    slots, tile_expert, n_valid, gap_start, gap_len = _routing_tables(meta, cnt_row, n_sorted)
    n_tiles = n_tok // TOK_TILE
    slots3 = slots.reshape(n_tiles, 1, 2 * TOK_TILE)

    xs_sorted = _dispatch_call(slots3, gap_start, gap_len, h2, n_sorted)
    ys_sorted = _expert_call(tile_expert, n_valid, xs_sorted, w1, w3, w2)
    yp, ys_out = _combine_call(slots3, x1, wts, g2g, ln2_g.reshape(1, d), ln2_b.reshape(1, d), ys_sorted, n_p, alpha)

    v_rows = vrows.reshape(n_streams, seq, N_HEADS, HEAD_DIM)
    return yp.reshape(bsz, t, d), ys_out.reshape(n_streams, seq, d), state_p, state_s, v_rows


def kernel(x_prompt, x_sample, state_hgrn, c_prompt, c_sample, w_ada, b_ada, w_in, ln_v_g, ln_v_b, w_s, b_s, hgrn_lb, gnorm_g, w_out, ln1_g, ln1_b, w_router_g, b_router_g, w_router_e, b_router_e, w1, w3, w2, ln2_g, ln2_b):
    n_layers = w_ada.shape[0]
    assert x_prompt.shape[1] % MIX_TILE == 0 and x_prompt.shape[2] == 2 * N_HEADS * HEAD_DIM
    assert x_sample.shape[1] % MOD_GROUP == 0 and x_sample.shape[1] <= SUB_CHUNK
    assert (x_sample.shape[0] * x_sample.shape[1]) % TOK_TILE == 0 and TOK_TILE % POST_TILE == 0
    c_all = jnp.concatenate([c_prompt, c_sample], axis=0)
    xp, xs = x_prompt, x_sample
    sp_list, ss_list, vs_list = [], [], []
    for l in range(n_layers):
        p = (w_ada[l], b_ada[l], w_in[l], ln_v_g[l], ln_v_b[l], w_s[l], b_s[l], hgrn_lb, gnorm_g[l], w_out[l],
             ln1_g[l], ln1_b[l], w_router_g[l], b_router_g[l], w_router_e[l], b_router_e[l],
             w1[l], w3[l], w2[l], ln2_g[l], ln2_b[l])
        xp, xs, sp, ss, vs = _layer(l, n_layers, xp, xs, state_hgrn[l], c_all, p)
        sp_list.append(sp.astype(state_hgrn.dtype))
        ss_list.append(ss.astype(state_hgrn.dtype))
        vs_list.append(vs)
    return (xp, xs, jnp.stack(sp_list, axis=0), jnp.stack(ss_list, axis=0), jnp.stack(vs_list, axis=0))
```

```python
import functools

import jax
import jax.numpy as jnp
from jax import lax
from jax.experimental import pallas as pl
from jax.experimental.pallas import tpu as pltpu

F32 = jnp.float32
BF16 = jnp.bfloat16
I32 = jnp.int32

N_HEADS = 8
HEAD_DIM = 128
GMLP_CHUNK = 128
SUB_CHUNK = 64
N_GROUPS = 4
EXPERTS_PER_GROUP = 8
N_EXPERTS = N_GROUPS * EXPERTS_PER_GROUP
LN_EPS = 1e-5
N_KINDS = 6

LANES = 128
MIX_TILE = 512
HEADS_PER_STEP = 2
HGRN_BLOCK = 64
DIAG_BLOCK = 16
TOK_TILE = 512
POST_TILE = 512
META_ROWS = 8
MOD_GROUP = 32
SLAB_UNROLL = 4
EXP_TILE = 256
ROW_DMA_UNROLL = 8
ADA_TILE = 1024
VMEM_LIMIT = 56 * 1024 * 1024


def _cparams(sem):
    return pltpu.CompilerParams(dimension_semantics=sem, vmem_limit_bytes=VMEM_LIMIT)


def _ln_plain(x):
    mu = jnp.mean(x, axis=-1, keepdims=True)
    xc = x - mu
    var = jnp.mean(xc * xc, axis=-1, keepdims=True)
    return xc * lax.rsqrt(var + LN_EPS)


def _dot(a, b):
    return jnp.dot(a, b, preferred_element_type=F32)


def _dot_nt(a, b):
    return lax.dot_general(a, b, (((1,), (1,)), ((), ())), preferred_element_type=F32)


def _lines_per_row(d):
    return d // LANES


def _load_rows(ref, lead, row0, n_rows, d):
    lpr = _lines_per_row(d)
    parts = [ref[lead + (pl.ds(row0 * lpr + c, n_rows, stride=lpr), slice(None))] for c in range(lpr)]
    return jnp.concatenate(parts, axis=1)


def _store_rows(ref, lead, row0, val):
    n_rows, d = val.shape
    lpr = _lines_per_row(d)
    for c in range(lpr):
        ref[lead + (pl.ds(row0 * lpr + c, n_rows, stride=lpr), slice(None))] = val[:, c * LANES:(c + 1) * LANES]


def _ada_kernel(c_ref, w_ref, b_ref, o_ref):
    s = jax.nn.silu(c_ref[...]).astype(BF16)
    o_ref[...] = _dot(s, w_ref[...].astype(BF16)) + b_ref[...]


def _ada_call(c_pad, w_ada, b_ada):
    rows, d = c_pad.shape
    n_out = w_ada.shape[1]
    return pl.pallas_call(
        _ada_kernel,
        grid=(n_out // ADA_TILE,),
        in_specs=[
            pl.BlockSpec((rows, d), lambda n: (0, 0)),
            pl.BlockSpec((d, ADA_TILE), lambda n: (0, n)),
            pl.BlockSpec((1, ADA_TILE), lambda n: (0, n)),
        ],
        out_specs=pl.BlockSpec((rows, ADA_TILE), lambda n: (0, n)),
        out_shape=jax.ShapeDtypeStruct((rows, n_out), F32),
        compiler_params=_cparams(("arbitrary",)),
        name="adaln",
    )(c_pad, w_ada, b_ada.reshape(1, n_out))


def _lower_bound(lb_ref, hd, layer):
    raw = lb_ref[hd]
    m = jnp.max(raw, axis=0, keepdims=True)
    e = jnp.exp(raw - m)
    p = e / jnp.sum(e, axis=0, keepdims=True)
    return jnp.sum(p[: layer + 1], axis=0, keepdims=True)


def _gmlp_rows(u_pre, v_pre, ln_g, ln_b):
    u = jax.nn.gelu(u_pre)
    v = jax.nn.gelu(v_pre)
    mu = jnp.mean(v, axis=-1, keepdims=True)
    vc = v - mu
    var = jnp.mean(vc * vc, axis=-1, keepdims=True)
    vn = vc * lax.rsqrt(var + LN_EPS) * ln_g + ln_b
    return u, vn


def _row_bcast(a, row, n):
    return jnp.broadcast_to(a[row:row + 1, :], (n, a.shape[1]))


def _block_id(idx, size):
    return lax.shift_right_logical(idx, I32(size.bit_length() - 1))


def _hgrn_masks(c):
    row = lax.broadcasted_iota(I32, (c, c), 0)
    col = lax.broadcasted_iota(I32, (c, c), 1)
    masks = []
    half = c // 2
    while half >= DIAG_BLOCK:
        span = 2 * half
        same = _block_id(row, span) == _block_id(col, span)
        masks.append(same & ((row & (span - 1)) >= half) & ((col & (span - 1)) < half))
        half //= 2
    diag = (_block_id(row, DIAG_BLOCK) == _block_id(col, DIAG_BLOCK)) & (col <= row)
    return masks, diag


def _tri_ones(c):
    row = lax.broadcasted_iota(I32, (c, c), 0)
    col = lax.broadcasted_iota(I32, (c, c), 1)
    return jnp.where(col <= row, 1.0, 0.0).astype(BF16)


def _hgrn_prepare(chains, c):
    tri = _tri_ones(c)
    units = []
    for st0, lbv, gn, blocks in chains:
        for q_pre, f_pre, i_pre, g_pre in blocks:
            q = jax.nn.silu(q_pre)
            fg = lbv + (1.0 - lbv) * jax.nn.sigmoid(f_pre)
            logf = jnp.log(fg)
            hi = logf.astype(BF16)
            lo = (logf - hi.astype(F32)).astype(BF16)
            units.append(dict(q=q, k=1.0 - fg, hilo=jnp.concatenate([hi, lo], axis=1),
                              v=i_pre, g=g_pre, gn=gn))

    for u in units:
        p = _dot(tri, u["hilo"])
        u["a"] = p[:, 0:HEAD_DIM] + p[:, HEAD_DIM:2 * HEAD_DIM]
    return units


def _hgrn_scores(units, c):
    masks, diag_mask = _hgrn_masks(c)
    for u in units:
        q, k, a = u["q"], u["k"], u["a"]
        parts = []
        half = c // 2
        level = 0
        while half >= DIAG_BLOCK:
            span = 2 * half
            ref = jnp.concatenate([_row_bcast(a, b * span + half, span) for b in range(c // span)], axis=0)
            qs = (q * jnp.exp(jnp.minimum(a - ref, 0.0))).astype(BF16)
            ks = (k * jnp.exp(jnp.minimum(ref - a, 0.0))).astype(BF16)
            parts.append((masks[level], _dot_nt(qs, ks)))
            half //= 2
            level += 1
        ref = jnp.concatenate([_row_bcast(a, b * DIAG_BLOCK, DIAG_BLOCK) for b in range(c // DIAG_BLOCK)], axis=0)
        qd = (q * jnp.exp(a - ref)).astype(BF16)
        kd = (k * jnp.exp(jnp.minimum(ref - a, 80.0))).astype(BF16)
        parts.append((diag_mask, _dot_nt(qd, kd)))
        u["parts"] = parts
        a_last = a[c - 1:c, :]
        u["decay"] = jnp.exp(a_last)
        kl = (k * jnp.exp(a_last - a)).astype(BF16)
        v_t = jnp.transpose(u["v"]).astype(BF16)
        u["upd"] = _dot(v_t, kl)
        u["qe"] = (q * jnp.exp(a)).astype(BF16)
    return units


def _hgrn_finish(chains, units):
    finals = []
    n = 0
    for st0, lbv, gn, blocks in chains:
        st = st0
        for _ in blocks:
            units[n]["st_in"] = st.astype(BF16)
            st = st * units[n]["decay"] + units[n]["upd"]
            n += 1
        finals.append(st)

    outs = []
    n = 0
    for st0, lbv, gn, blocks in chains:
        chain_out = []
        for _ in blocks:
            u = units[n]
            scores = None
            for mask, part in u["parts"]:
                part = jnp.where(mask, part, 0.0)
                scores = part if scores is None else scores + part
            o = _dot(scores.astype(BF16), u["v"].astype(BF16)) + _dot_nt(u["qe"], u["st_in"])
            o = o * lax.rsqrt(jnp.mean(o * o, axis=-1, keepdims=True) + LN_EPS) * u["gn"]
            chain_out.append(o * jax.nn.silu(u["g"]))
            n += 1
        outs.append(chain_out)
    return outs, finals


def _hgrn_chains(chains, c):
    return _hgrn_finish(chains, _hgrn_scores(_hgrn_prepare(chains, c), c))


def _proj_cols(kind, hd):
    c0 = (kind * HEADS_PER_STEP + hd) * HEAD_DIM
    return slice(c0, c0 + HEAD_DIM)


def _in_projection(h_scr, w_refs, proj_scr):
    width = HEADS_PER_STEP * HEAD_DIM
    for kind, w_ref in enumerate(w_refs):
        proj_scr[:, kind * width:(kind + 1) * width] = _dot(h_scr[...], w_ref[...])


def _mixer_prompt_kernel(x_ref, sh_ref, sc_ref, wu_ref, wv_ref, wq_ref, wf_ref, wi_ref, wg_ref,
                         lvg_ref, lvb_ref, ws_ref, bs_ref, lb_ref, gn_ref,
                         cata_ref, cato_ref, state_ref, h_scr, proj_a, proj_b, st_scr, *, layer, n_items, n_hb, n_t):
    s = pl.program_id(0)
    tm = h_scr.shape[0]
    proj_hb = lax.rem(jnp.minimum(s, n_items - 1), n_hb)
    post_item = jnp.maximum(s - 1, 0)
    post_hb = lax.rem(post_item, n_hb)
    post_time = lax.rem(post_item // n_hb, n_t)
    proj_slot = lax.rem(s, 2)

    @pl.when((proj_hb == 0) & (s < n_items))
    def _():
        def slab(r, carry):
            rows = pl.ds(pl.multiple_of(r * HGRN_BLOCK, HGRN_BLOCK), HGRN_BLOCK)
            h = _ln_plain(x_ref[0, rows, :]) * (1.0 + sc_ref[0]) + sh_ref[0]
            h_scr[rows, :] = h.astype(BF16)
            return carry
        lax.fori_loop(0, tm // HGRN_BLOCK, slab, 0)

    @pl.when(s == 0)
    def _():
        proj_b[...] = jnp.zeros_like(proj_b)

    @pl.when((post_time == 0) & (post_hb == 0))
    def _():
        st_scr[...] = jnp.zeros_like(st_scr)

    def step(proj_ref, post_ref):
        width = HEADS_PER_STEP * HEAD_DIM
        w_refs = (wu_ref, wv_ref, wq_ref, wf_ref, wi_ref, wg_ref)

        def project(kinds):
            for kind in kinds:
                proj_ref[:, kind * width:(kind + 1) * width] = _dot(h_scr[...], w_refs[kind][...])

        def post(rows, kind, hd):
            return post_ref[rows, _proj_cols(kind, hd)]

        project((0, 1, 2))

        row = lax.broadcasted_iota(I32, (GMLP_CHUNK, GMLP_CHUNK), 0)
        col = lax.broadcasted_iota(I32, (GMLP_CHUNK, GMLP_CHUNK), 1)
        causal = _block_id(row, SUB_CHUNK) >= _block_id(col, SUB_CHUNK)
        for hd in range(HEADS_PER_STEP):
            wm = jnp.where(causal, ws_ref[hd], 0.0).astype(BF16)
            b_col = bs_ref[hd]
            for c in range(tm // GMLP_CHUNK):
                rows = slice(c * GMLP_CHUNK, (c + 1) * GMLP_CHUNK)
                u, vn = _gmlp_rows(post(rows, 0, hd), post(rows, 1, hd), lvg_ref[hd], lvb_ref[hd])
                sp = _dot(wm, vn.astype(BF16)) + b_col
                cata_ref[rows, hd * HEAD_DIM:(hd + 1) * HEAD_DIM] = (u * sp).astype(BF16)

        chains = []
        for hd in range(HEADS_PER_STEP):
            blocks = [tuple(post(slice(c * HGRN_BLOCK, (c + 1) * HGRN_BLOCK), kind, hd) for kind in range(2, N_KINDS))
                      for c in range(tm // HGRN_BLOCK)]
            chains.append((st_scr[HEADS_PER_STEP * post_hb + hd], _lower_bound(lb_ref, hd, layer), gn_ref[hd], blocks))
        units = _hgrn_prepare(chains, HGRN_BLOCK)
        project((3, 4))
        units = _hgrn_scores(units, HGRN_BLOCK)
        project((5,))
        outs, finals = _hgrn_finish(chains, units)
        for hd in range(HEADS_PER_STEP):
            for c, out in enumerate(outs[hd]):
                cato_ref[c * HGRN_BLOCK:(c + 1) * HGRN_BLOCK, hd * HEAD_DIM:(hd + 1) * HEAD_DIM] = out.astype(BF16)
            st_scr[HEADS_PER_STEP * post_hb + hd] = finals[hd]
            state_ref[0, HEADS_PER_STEP * post_hb + hd] = jnp.transpose(finals[hd])

    @pl.when(proj_slot == 0)
    def _():
        step(proj_a, proj_b)

    @pl.when(proj_slot == 1)
    def _():
        step(proj_b, proj_a)


def _w_in_specs(d):
    width = HEADS_PER_STEP * HEAD_DIM
    blocks_per_kind = N_HEADS // HEADS_PER_STEP
    return [pl.BlockSpec((d, width), lambda j, kind=kind: (0, kind * blocks_per_kind + j)) for kind in range(N_KINDS)]


def _mixer_prompt_call(x, sh1, sc1, w_in_b, ln_v_g, ln_v_b, w_s, b_s, lb_h, gnorm_g, layer):
    bsz, t, d = x.shape
    n_t = t // MIX_TILE
    hp = HEADS_PER_STEP
    n_hb = N_HEADS // hp
    n_items = bsz * n_t * n_hb
    width = hp * HEAD_DIM
    kern = functools.partial(_mixer_prompt_kernel, layer=layer, n_items=n_items, n_hb=n_hb, n_t=n_t)
    proj_item = lambda s: jnp.minimum(s, n_items - 1)
    post_item = lambda s: jnp.maximum(s - 1, 0)
    proj_tile = lambda s: proj_item(s) // n_hb
    post_tile = lambda s: post_item(s) // n_hb
    proj_batch = lambda s: (proj_tile(s) // n_t, 0, 0)
    post_vec = lambda s: (post_item(s) % n_hb, 0, 0)
    w_spec = lambda kind: pl.BlockSpec((d, width), lambda s: (0, kind * n_hb + proj_item(s) % n_hb))
    return pl.pallas_call(
        kern,
        grid=(n_items + 1,),
        in_specs=[
            pl.BlockSpec((1, MIX_TILE, d), lambda s: (proj_tile(s) // n_t, proj_tile(s) % n_t, 0)),
            pl.BlockSpec((1, 1, d), proj_batch),
            pl.BlockSpec((1, 1, d), proj_batch),
            *[w_spec(kind) for kind in range(N_KINDS)],
            pl.BlockSpec((hp, 1, HEAD_DIM), post_vec),
            pl.BlockSpec((hp, 1, HEAD_DIM), post_vec),
            pl.BlockSpec((hp, GMLP_CHUNK, GMLP_CHUNK), post_vec),
            pl.BlockSpec((hp, GMLP_CHUNK, 1), post_vec),
            pl.BlockSpec((hp, lb_h.shape[1], HEAD_DIM), post_vec),
            pl.BlockSpec((hp, 1, HEAD_DIM), post_vec),
        ],
        out_specs=[
            pl.BlockSpec((MIX_TILE, width), lambda s: (post_tile(s), post_item(s) % n_hb)),
            pl.BlockSpec((MIX_TILE, width), lambda s: (post_tile(s), post_item(s) % n_hb)),
            pl.BlockSpec((1, N_HEADS, HEAD_DIM, HEAD_DIM), lambda s: (post_tile(s) // n_t, 0, 0, 0)),
        ],
        out_shape=[
            jax.ShapeDtypeStruct((bsz * t, N_HEADS * HEAD_DIM), BF16),
            jax.ShapeDtypeStruct((bsz * t, N_HEADS * HEAD_DIM), BF16),
            jax.ShapeDtypeStruct((bsz, N_HEADS, HEAD_DIM, HEAD_DIM), F32),
        ],
        scratch_shapes=[
            pltpu.VMEM((MIX_TILE, d), BF16),
            pltpu.VMEM((MIX_TILE, N_KINDS * width), F32),
            pltpu.VMEM((MIX_TILE, N_KINDS * width), F32),
            pltpu.VMEM((N_HEADS, HEAD_DIM, HEAD_DIM), F32),
        ],
        compiler_params=_cparams(("arbitrary",)),
        name="mixer_prompt",
    )(x, sh1.reshape(bsz, 1, d), sc1.reshape(bsz, 1, d), *([w_in_b] * N_KINDS), ln_v_g, ln_v_b, w_s, b_s, lb_h, gnorm_g)


def _mixer_sample_kernel(x_ref, sh_ref, sc_ref, wu_ref, wv_ref, wq_ref, wf_ref, wi_ref, wg_ref,
                         lvg_ref, lvb_ref, ws_ref, bs_ref, lb_ref, gn_ref, s0_ref,
                         cata_ref, cato_ref, state_ref, vrows_ref, h_scr, proj_scr, *, layer, seq):
    j = pl.program_id(0)
    n_streams = x_ref.shape[0] // seq

    @pl.when(j == 0)
    def _():
        def slab(r, carry):
            rows = pl.ds(pl.multiple_of(r * seq, seq), seq)
            h = _ln_plain(x_ref[rows, :]) * (1.0 + sc_ref[pl.ds(r, 1), :]) + sh_ref[pl.ds(r, 1), :]
            h_scr[rows, :] = h.astype(BF16)
            return carry
        lax.fori_loop(0, n_streams, slab, 0)

    _in_projection(h_scr, (wu_ref, wv_ref, wq_ref, wf_ref, wi_ref, wg_ref), proj_scr)

    chains = []
    for hd in range(HEADS_PER_STEP):
        wm = ws_ref[hd][0:seq, 0:seq].astype(BF16)
        b_col = bs_ref[hd][0:seq, :]
        lbv = _lower_bound(lb_ref, hd, layer)
        for n in range(n_streams):
            rows = slice(n * seq, (n + 1) * seq)
            u, vn = _gmlp_rows(proj_scr[rows, _proj_cols(0, hd)], proj_scr[rows, _proj_cols(1, hd)],
                               lvg_ref[hd], lvb_ref[hd])
            vrows_ref[rows, hd * HEAD_DIM:(hd + 1) * HEAD_DIM] = vn
            sp = _dot(wm, vn.astype(BF16)) + b_col
            cata_ref[rows, hd * HEAD_DIM:(hd + 1) * HEAD_DIM] = (u * sp).astype(BF16)
            chains.append((jnp.transpose(s0_ref[n, hd]), lbv, gn_ref[hd],
                           [tuple(proj_scr[rows, _proj_cols(kind, hd)] for kind in range(2, N_KINDS))]))

    outs, finals = _hgrn_chains(chains, seq)
    for hd in range(HEADS_PER_STEP):
        for n in range(n_streams):
            r0 = n * seq
            m = hd * n_streams + n
            cato_ref[r0:r0 + seq, hd * HEAD_DIM:(hd + 1) * HEAD_DIM] = outs[m][0].astype(BF16)
            state_ref[n, hd] = jnp.transpose(finals[m])


def _mixer_sample_call(x2d, sh1, sc1, w_in_b, ln_v_g, ln_v_b, w_s, b_s, lb_h, gnorm_g, s0, layer, seq):
    rows, d = x2d.shape
    n_streams = rows // seq
    hp = HEADS_PER_STEP
    kern = functools.partial(_mixer_sample_kernel, layer=layer, seq=seq)
    head_vec = lambda j: (j, 0, 0)
    return pl.pallas_call(
        kern,
        grid=(N_HEADS // hp,),
        in_specs=[
            pl.BlockSpec((rows, d), lambda j: (0, 0)),
            pl.BlockSpec((n_streams, d), lambda j: (0, 0)),
            pl.BlockSpec((n_streams, d), lambda j: (0, 0)),
            *_w_in_specs(d),
            pl.BlockSpec((hp, 1, HEAD_DIM), head_vec),
            pl.BlockSpec((hp, 1, HEAD_DIM), head_vec),
            pl.BlockSpec((hp, GMLP_CHUNK, GMLP_CHUNK), head_vec),
            pl.BlockSpec((hp, GMLP_CHUNK, 1), head_vec),
            pl.BlockSpec((hp, lb_h.shape[1], HEAD_DIM), head_vec),
            pl.BlockSpec((hp, 1, HEAD_DIM), head_vec),
            pl.BlockSpec((n_streams, hp, HEAD_DIM, HEAD_DIM), lambda j: (0, j, 0, 0)),
        ],
        out_specs=[
            pl.BlockSpec((rows, hp * HEAD_DIM), lambda j: (0, j)),
            pl.BlockSpec((rows, hp * HEAD_DIM), lambda j: (0, j)),
            pl.BlockSpec((n_streams, hp, HEAD_DIM, HEAD_DIM), lambda j: (0, j, 0, 0)),
            pl.BlockSpec((rows, hp * HEAD_DIM), lambda j: (0, j)),
        ],
        out_shape=[
            jax.ShapeDtypeStruct((rows, N_HEADS * HEAD_DIM), BF16),
            jax.ShapeDtypeStruct((rows, N_HEADS * HEAD_DIM), BF16),
            jax.ShapeDtypeStruct((n_streams, N_HEADS, HEAD_DIM, HEAD_DIM), F32),
            jax.ShapeDtypeStruct((rows, N_HEADS * HEAD_DIM), F32),
        ],
        scratch_shapes=[
            pltpu.VMEM((rows, d), BF16),
            pltpu.VMEM((rows, N_KINDS * hp * HEAD_DIM), F32),
        ],
        compiler_params=_cparams(("arbitrary",)),
        name="mixer_sample",
    )(x2d, sh1, sc1, *([w_in_b] * N_KINDS), ln_v_g, ln_v_b, w_s, b_s, lb_h, gnorm_g, s0)


def _post_kernel(cap_ref, cop_ref, cas_ref, cos_ref, xp_ref, xs_ref, g1_ref, sh2_ref, sc2_ref, wout_ref, l1g_ref, l1b_ref,
                 wr_ref, br_ref, tri_ref,
                 x1_ref, h2_ref, meta_ref, wts_ref, cnt_ref,
                 mix_scr, hhi_scr, hlo_scr, run_scr, *, n_prompt_tiles, alpha):
    i = pl.program_id(0)
    is_s = i >= n_prompt_tiles
    tm = mix_scr.shape[0]

    @pl.when(is_s)
    def _():
        mix_scr[...] = _dot(jnp.concatenate([cas_ref[...], cos_ref[...]], axis=1), wout_ref[...])

    @pl.when(jnp.logical_not(is_s))
    def _():
        mix_scr[...] = _dot(jnp.concatenate([cap_ref[...], cop_ref[...]], axis=1), wout_ref[...])

    @pl.when(i == 0)
    def _():
        run_scr[...] = jnp.zeros_like(run_scr)

    def slab(g, carry):
        rows = pl.ds(pl.multiple_of(g * MOD_GROUP, MOD_GROUP), MOD_GROUP)
        x = jnp.where(is_s, xs_ref[rows, :], xp_ref[rows, :])
        z = alpha * x + g1_ref[pl.ds(g, 1), :] * mix_scr[rows, :]
        x1 = _ln_plain(z) * l1g_ref[...] + l1b_ref[...]
        x1_ref[rows, :] = x1
        h2 = _ln_plain(x1) * (1.0 + sc2_ref[pl.ds(g, 1), :]) + sh2_ref[pl.ds(g, 1), :]
        h2_ref[rows, :] = h2
        hi = h2.astype(BF16)
        hhi_scr[rows, :] = hi
        hlo_scr[rows, :] = (h2 - hi.astype(F32)).astype(BF16)
        return carry
    lax.fori_loop(0, tm // MOD_GROUP, slab, 0, unroll=SLAB_UNROLL)

    _route(hhi_scr, hlo_scr, wr_ref, br_ref, tri_ref, run_scr, meta_ref, wts_ref, cnt_ref)


def _route(hhi_scr, hlo_scr, wr_ref, br_ref, before_ref, run_scr, meta_ref, wts_ref, cnt_ref):
    tm = hhi_scr.shape[0]
    n_r = N_GROUPS + N_EXPERTS
    n_rows = -(-n_r // 8) * 8
    s = _dot(hhi_scr[...], wr_ref[...]) + _dot(hlo_scr[...], wr_ref[...])
    logits = s + pltpu.roll(s, LANES - n_r, axis=1) + br_ref[...]
    lt = jnp.transpose(logits)[0:n_rows, :]

    row = lax.broadcasted_iota(I32, (n_rows, tm), 0)
    row_f = row.astype(F32)
    neg = jnp.float32(-jnp.inf)
    big = jnp.float32(LANES)

    def first_row_of(vals, top):
        return jnp.min(jnp.where(vals == top, row_f, big), axis=0, keepdims=True)

    gl = jnp.where(row < N_GROUPS, lt, neg)
    gmax = jnp.max(gl, axis=0, keepdims=True)
    gsel = first_row_of(gl, gmax)
    p_group = 1.0 / jnp.sum(jnp.exp(gl - gmax), axis=0, keepdims=True)

    e_lo = N_GROUPS + EXPERTS_PER_GROUP * gsel.astype(I32)
    emask = (row >= e_lo) & (row < e_lo + EXPERTS_PER_GROUP)
    el = jnp.where(emask, lt, neg)
    t1 = jnp.max(el, axis=0, keepdims=True)
    i1 = first_row_of(el, t1)
    el2 = jnp.where(row_f == i1, neg, el)
    t2 = jnp.max(el2, axis=0, keepdims=True)
    i2 = first_row_of(el2, t2)
    e2 = jnp.exp(t2 - t1)
    den = 1.0 + e2
    w1 = (1.0 / den) * p_group
    w2 = (e2 / den) * p_group

    sel1 = row_f == i1
    sel2 = row_f == i2
    onehot = jnp.where(sel1 | sel2, 1.0, 0.0)
    run_col = jnp.transpose(jnp.broadcast_to(run_scr[...], (LANES, LANES)))[0:n_rows, 0:1]
    before = _dot(onehot.astype(BF16), before_ref[...]) + run_col
    r1 = jnp.sum(jnp.where(sel1, before, 0.0), axis=0, keepdims=True)
    r2 = jnp.sum(jnp.where(sel2, before, 0.0), axis=0, keepdims=True)
    added = jnp.concatenate([jnp.sum(onehot, axis=1, keepdims=True), jnp.zeros((LANES - n_rows, 1), F32)], axis=0)
    run_scr[...] = run_scr[...] + jnp.transpose(jnp.broadcast_to(added, (LANES, LANES)))[0:1, :]
    cnt_ref[...] = run_scr[...].astype(I32)

    base = jnp.float32(N_GROUPS)
    mrow = lax.broadcasted_iota(I32, (META_ROWS, tm), 0)
    meta = jnp.where(mrow == 0, i1 - base, jnp.where(mrow == 1, i2 - base, jnp.where(mrow == 2, r1, jnp.where(mrow == 3, r2, 0.0))))
    meta_ref[0] = meta.astype(I32)
    wrow = lax.broadcasted_iota(I32, (LANES, tm), 0)
    wts_ref[...] = jnp.transpose(jnp.where(wrow == 0, w1, jnp.where(wrow == 1, w2, 0.0)))


def _post_call(cats_p, cats_s, x_p, x_s, g1g, sh2g, sc2g, w_out_b, ln1_g, ln1_b, wr, br, tri, alpha):
    n_p, d = x_p.shape
    n_s = x_s.shape[0]
    d_a = cats_p[0].shape[1]
    tm = POST_TILE
    n_pt = n_p // tm
    n_tiles = n_pt + n_s // tm
    n_tok = n_p + n_s
    groups = tm // MOD_GROUP
    kern = functools.partial(_post_kernel, n_prompt_tiles=n_pt, alpha=alpha)
    p_idx = lambda i: (jnp.minimum(i, n_pt - 1), 0)
    s_idx = lambda i: (jnp.maximum(i - n_pt, 0), 0)
    tile = lambda i: (i, 0)
    const = lambda i: (0, 0)
    once = pl.Buffered(1)
    s_mode = once if n_tiles - n_pt == 1 else None
    return pl.pallas_call(
        kern,
        grid=(n_tiles,),
        in_specs=[
            pl.BlockSpec((tm, d_a), p_idx),
            pl.BlockSpec((tm, d_a), p_idx),
            pl.BlockSpec((tm, d_a), s_idx, pipeline_mode=s_mode),
            pl.BlockSpec((tm, d_a), s_idx, pipeline_mode=s_mode),
            pl.BlockSpec((tm, d), p_idx),
            pl.BlockSpec((tm, d), s_idx, pipeline_mode=s_mode),
            pl.BlockSpec((groups, d), tile),
            pl.BlockSpec((groups, d), tile),
            pl.BlockSpec((groups, d), tile),
            pl.BlockSpec((d, d), const, pipeline_mode=once),
            pl.BlockSpec((1, d), const),
            pl.BlockSpec((1, d), const),
            pl.BlockSpec((d, LANES), const, pipeline_mode=once),
            pl.BlockSpec((1, LANES), const),
            pl.BlockSpec((tm, tm), const, pipeline_mode=once),
        ],
        out_specs=[
            pl.BlockSpec((tm, d), tile),
            pl.BlockSpec((tm, d), tile),
            pl.BlockSpec((1, META_ROWS, tm), lambda i: (i, 0, 0)),
            pl.BlockSpec((tm, LANES), tile),
            pl.BlockSpec((1, LANES), const),
        ],
        out_shape=[
            jax.ShapeDtypeStruct((n_tok, d), F32),
            jax.ShapeDtypeStruct((n_tok, d), F32),
            jax.ShapeDtypeStruct((n_tiles, META_ROWS, tm), I32),
            jax.ShapeDtypeStruct((n_tok, LANES), F32),
            jax.ShapeDtypeStruct((1, LANES), I32),
        ],
        scratch_shapes=[
            pltpu.VMEM((tm, d), F32),
            pltpu.VMEM((tm, d), BF16),
            pltpu.VMEM((tm, d), BF16),
            pltpu.VMEM((1, LANES), F32),
        ],
        compiler_params=_cparams(("arbitrary",)),
        name="post_router",
    )(*cats_p, *cats_s, x_p, x_s, g1g, sh2g, sc2g, w_out_b, ln1_g, ln1_b, wr, br, tri)


def _dispatch_kernel(slot_ref, gap_start_ref, gap_len_ref, h_ref, xs_ref, zero_scr, sem, tile_sem, *, gaps_per_step):
    i = pl.program_id(0)
    tm = h_ref.shape[0]
    n_gaps = gap_len_ref.shape[0]
    tile_rows = zero_scr.shape[0]

    @pl.when(i == 0)
    def _():
        zero_scr[...] = jnp.zeros_like(zero_scr)

    def row_copy(r, k):
        return pltpu.make_async_copy(h_ref.at[pl.ds(r, 1)], xs_ref.at[pl.ds(slot_ref[0, 0, 2 * r + k], 1)], sem)

    def start_rows(r, carry):
        row_copy(r, 0).start(priority=0)
        row_copy(r, 1).start(priority=1)
        return carry
    lax.fori_loop(0, tm, start_rows, 0, unroll=ROW_DMA_UNROLL)

    def gap_rows(g):
        return jnp.where(g < n_gaps - 1, gap_len_ref[jnp.minimum(g, n_gaps - 1)], 0)

    def gap_tiles(g):
        return jnp.where(g == n_gaps - 1, gap_len_ref[n_gaps - 1] // tile_rows, 0)

    def zero_row_copy(g, r):
        return pltpu.make_async_copy(zero_scr.at[pl.ds(0, 1)], xs_ref.at[pl.ds(gap_start_ref[g] + r, 1)], sem)

    def zero_tile_copy(g, t):
        start = pl.multiple_of(gap_start_ref[g] + t * tile_rows, tile_rows)
        return pltpu.make_async_copy(zero_scr, xs_ref.at[pl.ds(start, tile_rows)], tile_sem)

    for j in range(gaps_per_step):
        g = jnp.minimum(i * gaps_per_step + j, n_gaps - 1)
        live = i * gaps_per_step + j < n_gaps
        n_rows = jnp.where(live, gap_rows(g), 0)
        n_til = jnp.where(live, gap_tiles(g), 0)
        lax.fori_loop(0, n_rows, lambda r, c, g=g: (zero_row_copy(g, r).start(), c)[1], 0)
        lax.fori_loop(0, n_til, lambda t, c, g=g: (zero_tile_copy(g, t).start(), c)[1], 0)

    def wait_rows(r, carry):
        row_copy(r, 0).wait()
        row_copy(r, 1).wait()
        return carry
    lax.fori_loop(0, tm, wait_rows, 0, unroll=ROW_DMA_UNROLL)

    for j in range(gaps_per_step):
        g = jnp.minimum(i * gaps_per_step + j, n_gaps - 1)
        live = i * gaps_per_step + j < n_gaps
        n_rows = jnp.where(live, gap_rows(g), 0)
        n_til = jnp.where(live, gap_tiles(g), 0)
        lax.fori_loop(0, n_rows, lambda r, c, g=g: (zero_row_copy(g, r).wait(), c)[1], 0)
        lax.fori_loop(0, n_til, lambda t, c, g=g: (zero_tile_copy(g, t).wait(), c)[1], 0)


def _dispatch_call(slots3, gap_start, gap_len, h2, n_sorted):
    n_tok, d = h2.shape
    n_tiles = n_tok // TOK_TILE
    n_gaps = gap_len.shape[0]
    kern = functools.partial(_dispatch_kernel, gaps_per_step=-(-n_gaps // n_tiles))
    return pl.pallas_call(
        kern,
        grid=(n_tiles,),
        in_specs=[
            pl.BlockSpec((1, 1, slots3.shape[2]), lambda i: (i, 0, 0), memory_space=pltpu.SMEM),
            pl.BlockSpec((n_gaps,), lambda i: (0,), memory_space=pltpu.SMEM),
            pl.BlockSpec((n_gaps,), lambda i: (0,), memory_space=pltpu.SMEM),
            pl.BlockSpec((TOK_TILE, d), lambda i: (i, 0)),
        ],
        out_specs=pl.BlockSpec(memory_space=pl.ANY),
        out_shape=jax.ShapeDtypeStruct((n_sorted, d), F32),
        scratch_shapes=[pltpu.VMEM((EXP_TILE, d), F32), pltpu.SemaphoreType.DMA(()), pltpu.SemaphoreType.DMA(())],
        compiler_params=_cparams(("arbitrary",)),
        name="dispatch",
    )(slots3, gap_start, gap_len, h2)


def _expert_kernel(te_ref, nt_ref, xs_ref, w1_ref, w3_ref, w2_ref, ys_ref, w13_scr, w2_scr):
    t = pl.program_id(0)
    d_exp = w1_ref.shape[2]
    prev = te_ref[jnp.maximum(t - 1, 0)]
    changed = (t == 0) | (te_ref[t] != prev)
    valid = t < nt_ref[0]

    @pl.when(changed)
    def _():
        w13_scr[:, 0:d_exp] = w1_ref[0].astype(BF16)
        w13_scr[:, d_exp:2 * d_exp] = w3_ref[0].astype(BF16)
        w2_scr[...] = w2_ref[0].astype(BF16)

    @pl.when(valid)
    def _():
        h13 = _dot(xs_ref[...].astype(BF16), w13_scr[...])
        hm = jax.nn.silu(h13[:, 0:d_exp]) * h13[:, d_exp:2 * d_exp]
        _store_rows(ys_ref, (), 0, _dot(hm.astype(BF16), w2_scr[...]))

    @pl.when(jnp.logical_not(valid))
    def _():
        ys_ref[...] = jnp.zeros_like(ys_ref)


def _expert_call(tile_expert, n_valid, xs, w1, w3, w2):
    n_sorted, d = xs.shape
    d_exp = w1.shape[2]
    n_tiles = n_sorted // EXP_TILE
    grid_spec = pltpu.PrefetchScalarGridSpec(
        num_scalar_prefetch=2,
        grid=(n_tiles,),
        in_specs=[
            pl.BlockSpec((EXP_TILE, d), lambda t, te, nt: (t, 0)),
            pl.BlockSpec((1, d, d_exp), lambda t, te, nt: (te[t], 0, 0)),
            pl.BlockSpec((1, d, d_exp), lambda t, te, nt: (te[t], 0, 0)),
            pl.BlockSpec((1, d_exp, d), lambda t, te, nt: (te[t], 0, 0)),
        ],
        out_specs=pl.BlockSpec((EXP_TILE * _lines_per_row(d), LANES), lambda t, te, nt: (t, 0)),
        scratch_shapes=[pltpu.VMEM((d, 2 * d_exp), BF16), pltpu.VMEM((d_exp, d), BF16)],
    )
    return pl.pallas_call(
        _expert_kernel,
        grid_spec=grid_spec,
        out_shape=jax.ShapeDtypeStruct((n_sorted * _lines_per_row(d), LANES), F32),
        compiler_params=_cparams(("arbitrary",)),
        name="experts",
    )(tile_expert, n_valid, xs, w1, w3, w2)


def _combine_kernel(scur_ref, snext_ref, x1_ref, wts_ref, g2_ref, l2g_ref, l2b_ref, ys_ref, outp_ref, outs_ref,
                    y_scr, sem, *, n_prompt_tiles, alpha):
    i = pl.program_id(0)
    n_i = pl.num_programs(0)
    is_s = i >= n_prompt_tiles
    tm, d = x1_ref.shape
    lpr = _lines_per_row(d)
    cur = lax.rem(i, 2)

    def row_lines(r):
        return pl.ds(pl.multiple_of(r * lpr, lpr), lpr)

    def start_gathers(s_ref, buf):
        def body(r, carry):
            for k in range(2):
                pltpu.make_async_copy(ys_ref.at[row_lines(s_ref[0, 0, 2 * r + k])], y_scr.at[buf, k, row_lines(r)],
                                      sem.at[buf, k]).start(priority=k)
            return carry
        lax.fori_loop(0, tm, body, 0, unroll=ROW_DMA_UNROLL // 2)

    @pl.when(i == 0)
    def _():
        start_gathers(scur_ref, 0)

    @pl.when(i + 1 < n_i)
    def _():
        start_gathers(snext_ref, 1 - cur)

    for k in range(2):
        pltpu.make_async_copy(y_scr.at[cur, k], y_scr.at[cur, k], sem.at[cur, k]).wait()

    def slab(g, carry, out_ref):
        rows = pl.ds(pl.multiple_of(g * MOD_GROUP, MOD_GROUP), MOD_GROUP)
        w = wts_ref[rows, :]
        moe = (w[:, 0:1] * _load_rows(y_scr, (cur, 0), g * MOD_GROUP, MOD_GROUP, d)
               + w[:, 1:2] * _load_rows(y_scr, (cur, 1), g * MOD_GROUP, MOD_GROUP, d))
        z = alpha * x1_ref[rows, :] + g2_ref[pl.ds(g, 1), :] * moe
        out_ref[rows, :] = _ln_plain(z) * l2g_ref[...] + l2b_ref[...]
        return carry

    @pl.when(is_s)
    def _():
        lax.fori_loop(0, tm // MOD_GROUP, functools.partial(slab, out_ref=outs_ref), 0, unroll=SLAB_UNROLL)

    @pl.when(jnp.logical_not(is_s))
    def _():
        lax.fori_loop(0, tm // MOD_GROUP, functools.partial(slab, out_ref=outp_ref), 0, unroll=SLAB_UNROLL)


def _combine_call(slots3, x1, wts, g2g, ln2_g, ln2_b, ys, n_p, alpha):
    n_tok, d = x1.shape
    n_tiles = n_tok // TOK_TILE
    n_pt = n_p // TOK_TILE
    groups = TOK_TILE // MOD_GROUP
    kern = functools.partial(_combine_kernel, n_prompt_tiles=n_pt, alpha=alpha)
    slot_tile = lambda ahead: pl.BlockSpec(
        (1, 1, slots3.shape[2]), lambda i: (jnp.minimum(i + ahead, n_tiles - 1), 0, 0), memory_space=pltpu.SMEM)
    return pl.pallas_call(
        kern,
        grid=(n_tiles,),
        in_specs=[
            slot_tile(0), slot_tile(1),
            pl.BlockSpec((TOK_TILE, d), lambda i: (i, 0)),
            pl.BlockSpec((TOK_TILE, LANES), lambda i: (i, 0)),
            pl.BlockSpec((groups, d), lambda i: (i, 0)),
            pl.BlockSpec((1, d), lambda i: (0, 0)),
            pl.BlockSpec((1, d), lambda i: (0, 0)),
            pl.BlockSpec(memory_space=pl.ANY),
        ],
        out_specs=[
            pl.BlockSpec((TOK_TILE, d), lambda i: (jnp.minimum(i, n_pt - 1), 0)),
            pl.BlockSpec((TOK_TILE, d), lambda i: (jnp.maximum(i - n_pt, 0), 0)),
        ],
        out_shape=[
            jax.ShapeDtypeStruct((n_p, d), F32),
            jax.ShapeDtypeStruct((n_tok - n_p, d), F32),
        ],
        scratch_shapes=[
            pltpu.VMEM((2, 2, TOK_TILE * _lines_per_row(d), LANES), F32),
            pltpu.SemaphoreType.DMA((2, 2)),
        ],
        compiler_params=_cparams(("arbitrary",)),
        name="combine",
    )(slots3, slots3, x1, wts, g2g, ln2_g, ln2_b, ys)


def _routing_tables(meta, cnt_row, n_sorted):
    n_tok = meta.shape[0] * meta.shape[2]
    experts = jnp.transpose(meta[:, 0:2, :], (0, 2, 1)).reshape(n_tok, 2)
    ranks = jnp.transpose(meta[:, 2:4, :], (0, 2, 1)).reshape(n_tok, 2)
    cnt = cnt_row[0, N_GROUPS:N_GROUPS + N_EXPERTS]
    padded = ((cnt + EXP_TILE - 1) // EXP_TILE) * EXP_TILE
    ends = jnp.cumsum(padded)
    offs = ends - padded
    slots = offs[experts] + ranks
    total = ends[-1]
    n_tiles = n_sorted // EXP_TILE
    tile_ids = jnp.arange(n_tiles, dtype=I32)
    tile_expert = jnp.minimum(
        jnp.sum((tile_ids[:, None] >= (ends // EXP_TILE)[None, :]).astype(I32), axis=1), N_EXPERTS - 1)
    n_valid = (total // EXP_TILE).reshape(1).astype(I32)
    gap_start = jnp.concatenate([offs + cnt, total.reshape(1)]).astype(I32)
    gap_len = jnp.concatenate([padded - cnt, (n_sorted - total).reshape(1)]).astype(I32)
    return slots.astype(I32), tile_expert, n_valid, gap_start, gap_len


def _layer(layer, n_layers, xp, xs, s0_l, c_all, p):
    (w_ada, b_ada, w_in, ln_v_g, ln_v_b, w_s, b_s, hgrn_lb, gnorm_g, w_out, ln1_g, ln1_b,
     w_rg, b_rg, w_re, b_re, w1, w3, w2, ln2_g, ln2_b) = p
    bsz, t, d = xp.shape
    n_streams, seq, _ = xs.shape
    alpha = float((2.0 * n_layers) ** 0.25)
    n_p = bsz * t
    n_s = n_streams * seq
    n_tok = n_p + n_s

    n_c = c_all.shape[0]
    c_pad = jnp.pad(c_all, ((0, (-n_c) % 8), (0, 0)))
    mod = _ada_call(c_pad, w_ada, b_ada)[:n_c]
    sh1, sc1, g1, sh2, sc2, g2 = [mod[:, m * d:(m + 1) * d] for m in range(6)]

    w_in_b = w_in.astype(BF16)
    w_out_b = w_out.astype(BF16)
    lvg = ln_v_g.reshape(N_HEADS, 1, HEAD_DIM)
    lvb = ln_v_b.reshape(N_HEADS, 1, HEAD_DIM)
    b_s3 = b_s.reshape(N_HEADS, GMLP_CHUNK, 1)
    lb_h = jnp.transpose(hgrn_lb.reshape(hgrn_lb.shape[0], N_HEADS, HEAD_DIM), (1, 0, 2))
    gn = gnorm_g.reshape(N_HEADS, 1, HEAD_DIM)

    ca_p, co_p, state_p = _mixer_prompt_call(xp, sh1[:bsz], sc1[:bsz], w_in_b, lvg, lvb, w_s, b_s3, lb_h, gn, layer)
    ca_s, co_s, state_s, vrows = _mixer_sample_call(xs.reshape(n_s, d), sh1[bsz:], sc1[bsz:], w_in_b, lvg, lvb, w_s,
                                               b_s3, lb_h, gn, s0_l, layer, seq)

    group_stream = jnp.concatenate([
        jnp.repeat(jnp.arange(bsz, dtype=I32), t // MOD_GROUP),
        bsz + jnp.repeat(jnp.arange(n_streams, dtype=I32), seq // MOD_GROUP)])
    g1g, sh2g, sc2g, g2g = [m[group_stream] for m in (g1, sh2, sc2, g2)]

    wr = jnp.concatenate([w_rg, w_re], axis=1)
    wr_hi = wr.astype(BF16)
    wr_lo = (wr - wr_hi.astype(F32)).astype(BF16)
    n_r = wr.shape[1]
    wr_cat = jnp.concatenate([wr_hi, wr_lo, jnp.zeros((d, LANES - 2 * n_r), BF16)], axis=1)
    br = jnp.pad(jnp.concatenate([b_rg, b_re]), (0, LANES - n_r)).reshape(1, LANES)
    tri = jnp.triu(jnp.ones((POST_TILE, POST_TILE), F32), 1).astype(BF16)

    x1, h2, meta, wts, cnt_row = _post_call((ca_p, co_p), (ca_s, co_s), xp.reshape(n_p, d), xs.reshape(n_s, d),
                                            g1g, sh2g, sc2g, w_out_b, ln1_g.reshape(1, d), ln1_b.reshape(1, d),
                                            wr_cat, br, tri, alpha)

    n_sorted = 2 * n_tok + N_EXPERTS * EXP_TILE
    slots, tile_expert, n_valid, gap_start, gap_len = _routing_tables(meta, cnt_row, n_sorted)
    n_tiles = n_tok // TOK_TILE
    slots3 = slots.reshape(n_tiles, 1, 2 * TOK_TILE)

    xs_sorted = _dispatch_call(slots3, gap_start, gap_len, h2, n_sorted)
    ys_sorted = _expert_call(tile_expert, n_valid, xs_sorted, w1, w3, w2)
    yp, ys_out = _combine_call(slots3, x1, wts, g2g, ln2_g.reshape(1, d), ln2_b.reshape(1, d), ys_sorted, n_p, alpha)

    v_rows = vrows.reshape(n_streams, seq, N_HEADS, HEAD_DIM)
    return yp.reshape(bsz, t, d), ys_out.reshape(n_streams, seq, d), state_p, state_s, v_rows


def kernel(x_prompt, x_sample, state_hgrn, c_prompt, c_sample, w_ada, b_ada, w_in, ln_v_g, ln_v_b, w_s, b_s, hgrn_lb, gnorm_g, w_out, ln1_g, ln1_b, w_router_g, b_router_g, w_router_e, b_router_e, w1, w3, w2, ln2_g, ln2_b):
    n_layers = w_ada.shape[0]
    assert x_prompt.shape[1] % MIX_TILE == 0 and x_prompt.shape[2] == 2 * N_HEADS * HEAD_DIM
    assert x_sample.shape[1] % MOD_GROUP == 0 and x_sample.shape[1] <= SUB_CHUNK
    assert (x_sample.shape[0] * x_sample.shape[1]) % TOK_TILE == 0 and TOK_TILE % POST_TILE == 0
    c_all = jnp.concatenate([c_prompt, c_sample], axis=0)
    xp, xs = x_prompt, x_sample
    sp_list, ss_list, vs_list = [], [], []
    for l in range(n_layers):
        p = (w_ada[l], b_ada[l], w_in[l], ln_v_g[l], ln_v_b[l], w_s[l], b_s[l], hgrn_lb, gnorm_g[l], w_out[l],
             ln1_g[l], ln1_b[l], w_router_g[l], b_router_g[l], w_router_e[l], b_router_e[l],
             w1[l], w3[l], w2[l], ln2_g[l], ln2_b[l])
        xp, xs, sp, ss, vs = _layer(l, n_layers, xp, xs, state_hgrn[l], c_all, p)
        sp_list.append(sp.astype(state_hgrn.dtype))
        ss_list.append(ss.astype(state_hgrn.dtype))
        vs_list.append(vs)
    return (xp, xs, jnp.stack(sp_list, axis=0), jnp.stack(ss_list, axis=0), jnp.stack(vs_list, axis=0))
```

```python
import functools

import jax
import jax.numpy as jnp
from jax import lax
from jax.experimental import pallas as pl
from jax.experimental.pallas import tpu as pltpu

F32 = jnp.float32
BF16 = jnp.bfloat16
I32 = jnp.int32

N_HEADS = 8
HEAD_DIM = 128
GMLP_CHUNK = 128
SUB_CHUNK = 64
N_GROUPS = 4
EXPERTS_PER_GROUP = 8
N_EXPERTS = N_GROUPS * EXPERTS_PER_GROUP
LN_EPS = 1e-5
N_KINDS = 6

LANES = 128
MIX_TILE = 512
HEADS_PER_STEP = 2
HGRN_BLOCK = 64
DIAG_BLOCK = 16
TOK_TILE = 512
POST_TILE = 512
META_ROWS = 8
MOD_GROUP = 32
SLAB_UNROLL = 4
EXP_TILE = 256
X_AHEAD = 2
ROW_DMA_UNROLL = 8
ADA_TILE = 1024
VMEM_LIMIT = 56 * 1024 * 1024


def _cparams(sem):
    return pltpu.CompilerParams(dimension_semantics=sem, vmem_limit_bytes=VMEM_LIMIT)


def _ln_plain(x):
    mu = jnp.mean(x, axis=-1, keepdims=True)
    xc = x - mu
    var = jnp.mean(xc * xc, axis=-1, keepdims=True)
    return xc * lax.rsqrt(var + LN_EPS)


def _dot(a, b):
    return jnp.dot(a, b, preferred_element_type=F32)


def _dot_nt(a, b):
    return lax.dot_general(a, b, (((1,), (1,)), ((), ())), preferred_element_type=F32)


def _lines_per_row(d):
    return d // LANES


def _load_rows(ref, lead, row0, n_rows, d):
    lpr = _lines_per_row(d)
    parts = [ref[lead + (pl.ds(row0 * lpr + c, n_rows, stride=lpr), slice(None))] for c in range(lpr)]
    return jnp.concatenate(parts, axis=1)


def _store_rows(ref, lead, row0, val):
    n_rows, d = val.shape
    lpr = _lines_per_row(d)
    for c in range(lpr):
        ref[lead + (pl.ds(row0 * lpr + c, n_rows, stride=lpr), slice(None))] = val[:, c * LANES:(c + 1) * LANES]


def _ada_kernel(c_ref, w_ref, b_ref, o_ref):
    s = jax.nn.silu(c_ref[...]).astype(BF16)
    o_ref[...] = _dot(s, w_ref[...].astype(BF16)) + b_ref[...]


def _ada_call(c_pad, w_ada, b_ada):
    rows, d = c_pad.shape
    n_out = w_ada.shape[1]
    return pl.pallas_call(
        _ada_kernel,
        grid=(n_out // ADA_TILE,),
        in_specs=[
            pl.BlockSpec((rows, d), lambda n: (0, 0)),
            pl.BlockSpec((d, ADA_TILE), lambda n: (0, n)),
            pl.BlockSpec((1, ADA_TILE), lambda n: (0, n)),
        ],
        out_specs=pl.BlockSpec((rows, ADA_TILE), lambda n: (0, n)),
        out_shape=jax.ShapeDtypeStruct((rows, n_out), F32),
        compiler_params=_cparams(("arbitrary",)),
        name="adaln",
    )(c_pad, w_ada, b_ada.reshape(1, n_out))


def _lower_bound(lb_ref, hd, layer):
    raw = lb_ref[hd]
    m = jnp.max(raw, axis=0, keepdims=True)
    e = jnp.exp(raw - m)
    p = e / jnp.sum(e, axis=0, keepdims=True)
    return jnp.sum(p[: layer + 1], axis=0, keepdims=True)


def _gmlp_rows(u_pre, v_pre, ln_g, ln_b):
    u = jax.nn.gelu(u_pre)
    v = jax.nn.gelu(v_pre)
    mu = jnp.mean(v, axis=-1, keepdims=True)
    vc = v - mu
    var = jnp.mean(vc * vc, axis=-1, keepdims=True)
    vn = vc * lax.rsqrt(var + LN_EPS) * ln_g + ln_b
    return u, vn


def _row_bcast(a, row, n):
    return jnp.broadcast_to(a[row:row + 1, :], (n, a.shape[1]))


def _block_id(idx, size):
    return lax.shift_right_logical(idx, I32(size.bit_length() - 1))


def _hgrn_masks(c):
    row = lax.broadcasted_iota(I32, (c, c), 0)
    col = lax.broadcasted_iota(I32, (c, c), 1)
    masks = []
    half = c // 2
    while half >= DIAG_BLOCK:
        span = 2 * half
        same = _block_id(row, span) == _block_id(col, span)
        masks.append(same & ((row & (span - 1)) >= half) & ((col & (span - 1)) < half))
        half //= 2
    diag = (_block_id(row, DIAG_BLOCK) == _block_id(col, DIAG_BLOCK)) & (col <= row)
    return masks, diag


def _tri_ones(c):
    row = lax.broadcasted_iota(I32, (c, c), 0)
    col = lax.broadcasted_iota(I32, (c, c), 1)
    return jnp.where(col <= row, 1.0, 0.0).astype(BF16)


def _hgrn_prepare(chains, c):
    tri = _tri_ones(c)
    units = []
    for st0, lbv, gn, blocks in chains:
        for q_pre, f_pre, i_pre, g_pre in blocks:
            q = jax.nn.silu(q_pre)
            fg = lbv + (1.0 - lbv) * jax.nn.sigmoid(f_pre)
            logf = jnp.log(fg)
            hi = logf.astype(BF16)
            lo = (logf - hi.astype(F32)).astype(BF16)
            units.append(dict(q=q, k=1.0 - fg, hilo=jnp.concatenate([hi, lo], axis=1),
                              v=i_pre, g=g_pre, gn=gn))

    for u in units:
        p = _dot(tri, u["hilo"])
        u["a"] = p[:, 0:HEAD_DIM] + p[:, HEAD_DIM:2 * HEAD_DIM]
    return units


def _hgrn_scores(units, c):
    masks, diag_mask = _hgrn_masks(c)
    for u in units:
        q, k, a = u["q"], u["k"], u["a"]
        parts = []
        half = c // 2
        level = 0
        while half >= DIAG_BLOCK:
            span = 2 * half
            ref = jnp.concatenate([_row_bcast(a, b * span + half, span) for b in range(c // span)], axis=0)
            qs = (q * jnp.exp(jnp.minimum(a - ref, 0.0))).astype(BF16)
            ks = (k * jnp.exp(jnp.minimum(ref - a, 0.0))).astype(BF16)
            parts.append((masks[level], _dot_nt(qs, ks)))
            half //= 2
            level += 1
        ref = jnp.concatenate([_row_bcast(a, b * DIAG_BLOCK, DIAG_BLOCK) for b in range(c // DIAG_BLOCK)], axis=0)
        qd = (q * jnp.exp(a - ref)).astype(BF16)
        kd = (k * jnp.exp(jnp.minimum(ref - a, 80.0))).astype(BF16)
        parts.append((diag_mask, _dot_nt(qd, kd)))
        u["parts"] = parts
        a_last = a[c - 1:c, :]
        u["decay"] = jnp.exp(a_last)
        kl = (k * jnp.exp(a_last - a)).astype(BF16)
        v_t = jnp.transpose(u["v"]).astype(BF16)
        u["upd"] = _dot(v_t, kl)
        u["qe"] = (q * jnp.exp(a)).astype(BF16)
    return units


def _hgrn_finish(chains, units):
    finals = []
    n = 0
    for st0, lbv, gn, blocks in chains:
        st = st0
        for _ in blocks:
            units[n]["st_in"] = st.astype(BF16)
            st = st * units[n]["decay"] + units[n]["upd"]
            n += 1
        finals.append(st)

    outs = []
    n = 0
    for st0, lbv, gn, blocks in chains:
        chain_out = []
        for _ in blocks:
            u = units[n]
            scores = None
            for mask, part in u["parts"]:
                part = jnp.where(mask, part, 0.0)
                scores = part if scores is None else scores + part
            o = _dot(scores.astype(BF16), u["v"].astype(BF16)) + _dot_nt(u["qe"], u["st_in"])
            o = o * lax.rsqrt(jnp.mean(o * o, axis=-1, keepdims=True) + LN_EPS) * u["gn"]
            chain_out.append(o * jax.nn.silu(u["g"]))
            n += 1
        outs.append(chain_out)
    return outs, finals


def _hgrn_chains(chains, c):
    return _hgrn_finish(chains, _hgrn_scores(_hgrn_prepare(chains, c), c))


def _proj_cols(kind, hd):
    c0 = (kind * HEADS_PER_STEP + hd) * HEAD_DIM
    return slice(c0, c0 + HEAD_DIM)


def _in_projection(h_scr, w_refs, proj_scr):
    width = HEADS_PER_STEP * HEAD_DIM
    for kind, w_ref in enumerate(w_refs):
        proj_scr[:, kind * width:(kind + 1) * width] = _dot(h_scr[...], w_ref[...])


def _mixer_prompt_kernel(x_ref, sh_ref, sc_ref, wu_ref, wv_ref, wq_ref, wf_ref, wi_ref, wg_ref,
                         lvg_ref, lvb_ref, ws_ref, bs_ref, lb_ref, gn_ref,
                         cata_ref, cato_ref, state_ref, h_scr, proj_a, proj_b, st_scr, *, layer, n_items, n_hb, n_t):
    s = pl.program_id(0)
    tm = h_scr.shape[0]
    proj_hb = lax.rem(jnp.minimum(s, n_items - 1), n_hb)
    post_item = jnp.maximum(s - 1, 0)
    post_hb = lax.rem(post_item, n_hb)
    post_time = lax.rem(post_item // n_hb, n_t)
    proj_slot = lax.rem(s, 2)

    @pl.when((proj_hb == 0) & (s < n_items))
    def _():
        def slab(r, carry):
            rows = pl.ds(pl.multiple_of(r * HGRN_BLOCK, HGRN_BLOCK), HGRN_BLOCK)
            h = _ln_plain(x_ref[0, rows, :]) * (1.0 + sc_ref[0]) + sh_ref[0]
            h_scr[rows, :] = h.astype(BF16)
            return carry
        lax.fori_loop(0, tm // HGRN_BLOCK, slab, 0)

    @pl.when(s == 0)
    def _():
        proj_b[...] = jnp.zeros_like(proj_b)

    @pl.when((post_time == 0) & (post_hb == 0))
    def _():
        st_scr[...] = jnp.zeros_like(st_scr)

    def step(proj_ref, post_ref):
        width = HEADS_PER_STEP * HEAD_DIM
        w_refs = (wu_ref, wv_ref, wq_ref, wf_ref, wi_ref, wg_ref)

        def project(kinds):
            for kind in kinds:
                proj_ref[:, kind * width:(kind + 1) * width] = _dot(h_scr[...], w_refs[kind][...])

        def post(rows, kind, hd):
            return post_ref[rows, _proj_cols(kind, hd)]

        project((0, 1, 2))

        row = lax.broadcasted_iota(I32, (GMLP_CHUNK, GMLP_CHUNK), 0)
        col = lax.broadcasted_iota(I32, (GMLP_CHUNK, GMLP_CHUNK), 1)
        causal = _block_id(row, SUB_CHUNK) >= _block_id(col, SUB_CHUNK)
        for hd in range(HEADS_PER_STEP):
            wm = jnp.where(causal, ws_ref[hd], 0.0).astype(BF16)
            b_col = bs_ref[hd]
            for c in range(tm // GMLP_CHUNK):
                rows = slice(c * GMLP_CHUNK, (c + 1) * GMLP_CHUNK)
                u, vn = _gmlp_rows(post(rows, 0, hd), post(rows, 1, hd), lvg_ref[hd], lvb_ref[hd])
                sp = _dot(wm, vn.astype(BF16)) + b_col
                cata_ref[rows, hd * HEAD_DIM:(hd + 1) * HEAD_DIM] = (u * sp).astype(BF16)

        chains = []
        for hd in range(HEADS_PER_STEP):
            blocks = [tuple(post(slice(c * HGRN_BLOCK, (c + 1) * HGRN_BLOCK), kind, hd) for kind in range(2, N_KINDS))
                      for c in range(tm // HGRN_BLOCK)]
            chains.append((st_scr[HEADS_PER_STEP * post_hb + hd], _lower_bound(lb_ref, hd, layer), gn_ref[hd], blocks))
        units = _hgrn_prepare(chains, HGRN_BLOCK)
        project((3, 4))
        units = _hgrn_scores(units, HGRN_BLOCK)
        project((5,))
        outs, finals = _hgrn_finish(chains, units)
        for hd in range(HEADS_PER_STEP):
            for c, out in enumerate(outs[hd]):
                cato_ref[c * HGRN_BLOCK:(c + 1) * HGRN_BLOCK, hd * HEAD_DIM:(hd + 1) * HEAD_DIM] = out.astype(BF16)
            st_scr[HEADS_PER_STEP * post_hb + hd] = finals[hd]
            state_ref[0, HEADS_PER_STEP * post_hb + hd] = jnp.transpose(finals[hd])

    @pl.when(proj_slot == 0)
    def _():
        step(proj_a, proj_b)

    @pl.when(proj_slot == 1)
    def _():
        step(proj_b, proj_a)


def _w_in_specs(d):
    width = HEADS_PER_STEP * HEAD_DIM
    blocks_per_kind = N_HEADS // HEADS_PER_STEP
    return [pl.BlockSpec((d, width), lambda j, kind=kind: (0, kind * blocks_per_kind + j)) for kind in range(N_KINDS)]


def _mixer_prompt_call(x, sh1, sc1, w_in_b, ln_v_g, ln_v_b, w_s, b_s, lb_h, gnorm_g, layer):
    bsz, t, d = x.shape
    n_t = t // MIX_TILE
    hp = HEADS_PER_STEP
    n_hb = N_HEADS // hp
    n_items = bsz * n_t * n_hb
    width = hp * HEAD_DIM
    kern = functools.partial(_mixer_prompt_kernel, layer=layer, n_items=n_items, n_hb=n_hb, n_t=n_t)
    proj_item = lambda s: jnp.minimum(s, n_items - 1)
    post_item = lambda s: jnp.maximum(s - 1, 0)
    proj_tile = lambda s: proj_item(s) // n_hb
    post_tile = lambda s: post_item(s) // n_hb
    proj_batch = lambda s: (proj_tile(s) // n_t, 0, 0)
    post_vec = lambda s: (post_item(s) % n_hb, 0, 0)
    w_spec = lambda kind: pl.BlockSpec((d, width), lambda s: (0, kind * n_hb + proj_item(s) % n_hb))
    return pl.pallas_call(
        kern,
        grid=(n_items + 1,),
        in_specs=[
            pl.BlockSpec((1, MIX_TILE, d), lambda s: (proj_tile(s) // n_t, proj_tile(s) % n_t, 0)),
            pl.BlockSpec((1, 1, d), proj_batch),
            pl.BlockSpec((1, 1, d), proj_batch),
            *[w_spec(kind) for kind in range(N_KINDS)],
            pl.BlockSpec((hp, 1, HEAD_DIM), post_vec),
            pl.BlockSpec((hp, 1, HEAD_DIM), post_vec),
            pl.BlockSpec((hp, GMLP_CHUNK, GMLP_CHUNK), post_vec),
            pl.BlockSpec((hp, GMLP_CHUNK, 1), post_vec),
            pl.BlockSpec((hp, lb_h.shape[1], HEAD_DIM), post_vec),
            pl.BlockSpec((hp, 1, HEAD_DIM), post_vec),
        ],
        out_specs=[
            pl.BlockSpec((MIX_TILE, width), lambda s: (post_tile(s), post_item(s) % n_hb)),
            pl.BlockSpec((MIX_TILE, width), lambda s: (post_tile(s), post_item(s) % n_hb)),
            pl.BlockSpec((1, N_HEADS, HEAD_DIM, HEAD_DIM), lambda s: (post_tile(s) // n_t, 0, 0, 0)),
        ],
        out_shape=[
            jax.ShapeDtypeStruct((bsz * t, N_HEADS * HEAD_DIM), BF16),
            jax.ShapeDtypeStruct((bsz * t, N_HEADS * HEAD_DIM), BF16),
            jax.ShapeDtypeStruct((bsz, N_HEADS, HEAD_DIM, HEAD_DIM), F32),
        ],
        scratch_shapes=[
            pltpu.VMEM((MIX_TILE, d), BF16),
            pltpu.VMEM((MIX_TILE, N_KINDS * width), F32),
            pltpu.VMEM((MIX_TILE, N_KINDS * width), F32),
            pltpu.VMEM((N_HEADS, HEAD_DIM, HEAD_DIM), F32),
        ],
        compiler_params=_cparams(("arbitrary",)),
        name="mixer_prompt",
    )(x, sh1.reshape(bsz, 1, d), sc1.reshape(bsz, 1, d), *([w_in_b] * N_KINDS), ln_v_g, ln_v_b, w_s, b_s, lb_h, gnorm_g)


def _mixer_sample_kernel(x_ref, sh_ref, sc_ref, wu_ref, wv_ref, wq_ref, wf_ref, wi_ref, wg_ref,
                         lvg_ref, lvb_ref, ws_ref, bs_ref, lb_ref, gn_ref, s0_ref,
                         cata_ref, cato_ref, state_ref, vrows_ref, h_scr, proj_scr, *, layer, seq):
    j = pl.program_id(0)
    n_streams = x_ref.shape[0] // seq

    @pl.when(j == 0)
    def _():
        def slab(r, carry):
            rows = pl.ds(pl.multiple_of(r * seq, seq), seq)
            h = _ln_plain(x_ref[rows, :]) * (1.0 + sc_ref[pl.ds(r, 1), :]) + sh_ref[pl.ds(r, 1), :]
            h_scr[rows, :] = h.astype(BF16)
            return carry
        lax.fori_loop(0, n_streams, slab, 0)

    _in_projection(h_scr, (wu_ref, wv_ref, wq_ref, wf_ref, wi_ref, wg_ref), proj_scr)

    chains = []
    for hd in range(HEADS_PER_STEP):
        wm = ws_ref[hd][0:seq, 0:seq].astype(BF16)
        b_col = bs_ref[hd][0:seq, :]
        lbv = _lower_bound(lb_ref, hd, layer)
        for n in range(n_streams):
            rows = slice(n * seq, (n + 1) * seq)
            u, vn = _gmlp_rows(proj_scr[rows, _proj_cols(0, hd)], proj_scr[rows, _proj_cols(1, hd)],
                               lvg_ref[hd], lvb_ref[hd])
            vrows_ref[rows, hd * HEAD_DIM:(hd + 1) * HEAD_DIM] = vn
            sp = _dot(wm, vn.astype(BF16)) + b_col
            cata_ref[rows, hd * HEAD_DIM:(hd + 1) * HEAD_DIM] = (u * sp).astype(BF16)
            chains.append((jnp.transpose(s0_ref[n, hd]), lbv, gn_ref[hd],
                           [tuple(proj_scr[rows, _proj_cols(kind, hd)] for kind in range(2, N_KINDS))]))

    outs, finals = _hgrn_chains(chains, seq)
    for hd in range(HEADS_PER_STEP):
        for n in range(n_streams):
            r0 = n * seq
            m = hd * n_streams + n
            cato_ref[r0:r0 + seq, hd * HEAD_DIM:(hd + 1) * HEAD_DIM] = outs[m][0].astype(BF16)
            state_ref[n, hd] = jnp.transpose(finals[m])


def _mixer_sample_call(x2d, sh1, sc1, w_in_b, ln_v_g, ln_v_b, w_s, b_s, lb_h, gnorm_g, s0, layer, seq):
    rows, d = x2d.shape
    n_streams = rows // seq
    hp = HEADS_PER_STEP
    kern = functools.partial(_mixer_sample_kernel, layer=layer, seq=seq)
    head_vec = lambda j: (j, 0, 0)
    return pl.pallas_call(
        kern,
        grid=(N_HEADS // hp,),
        in_specs=[
            pl.BlockSpec((rows, d), lambda j: (0, 0)),
            pl.BlockSpec((n_streams, d), lambda j: (0, 0)),
            pl.BlockSpec((n_streams, d), lambda j: (0, 0)),
            *_w_in_specs(d),
            pl.BlockSpec((hp, 1, HEAD_DIM), head_vec),
            pl.BlockSpec((hp, 1, HEAD_DIM), head_vec),
            pl.BlockSpec((hp, GMLP_CHUNK, GMLP_CHUNK), head_vec),
            pl.BlockSpec((hp, GMLP_CHUNK, 1), head_vec),
            pl.BlockSpec((hp, lb_h.shape[1], HEAD_DIM), head_vec),
            pl.BlockSpec((hp, 1, HEAD_DIM), head_vec),
            pl.BlockSpec((n_streams, hp, HEAD_DIM, HEAD_DIM), lambda j: (0, j, 0, 0)),
        ],
        out_specs=[
            pl.BlockSpec((rows, hp * HEAD_DIM), lambda j: (0, j)),
            pl.BlockSpec((rows, hp * HEAD_DIM), lambda j: (0, j)),
            pl.BlockSpec((n_streams, hp, HEAD_DIM, HEAD_DIM), lambda j: (0, j, 0, 0)),
            pl.BlockSpec((rows, hp * HEAD_DIM), lambda j: (0, j)),
        ],
        out_shape=[
            jax.ShapeDtypeStruct((rows, N_HEADS * HEAD_DIM), BF16),
            jax.ShapeDtypeStruct((rows, N_HEADS * HEAD_DIM), BF16),
            jax.ShapeDtypeStruct((n_streams, N_HEADS, HEAD_DIM, HEAD_DIM), F32),
            jax.ShapeDtypeStruct((rows, N_HEADS * HEAD_DIM), F32),
        ],
        scratch_shapes=[
            pltpu.VMEM((rows, d), BF16),
            pltpu.VMEM((rows, N_KINDS * hp * HEAD_DIM), F32),
        ],
        compiler_params=_cparams(("arbitrary",)),
        name="mixer_sample",
    )(x2d, sh1, sc1, *([w_in_b] * N_KINDS), ln_v_g, ln_v_b, w_s, b_s, lb_h, gnorm_g, s0)


def _post_kernel(cap_ref, cop_ref, cas_ref, cos_ref, xp_ref, xs_ref, g1_ref, sh2_ref, sc2_ref, wout_ref, l1g_ref, l1b_ref,
                 wr_ref, br_ref, tri_ref,
                 x1_ref, h2_ref, meta_ref, wts_ref, cnt_ref,
                 mix_scr, hhi_scr, hlo_scr, run_scr, *, n_prompt_tiles, alpha):
    i = pl.program_id(0)
    is_s = i >= n_prompt_tiles
    tm = mix_scr.shape[0]

    @pl.when(is_s)
    def _():
        mix_scr[...] = _dot(jnp.concatenate([cas_ref[...], cos_ref[...]], axis=1), wout_ref[...])

    @pl.when(jnp.logical_not(is_s))
    def _():
        mix_scr[...] = _dot(jnp.concatenate([cap_ref[...], cop_ref[...]], axis=1), wout_ref[...])

    @pl.when(i == 0)
    def _():
        run_scr[...] = jnp.zeros_like(run_scr)

    def slab(g, carry):
        rows = pl.ds(pl.multiple_of(g * MOD_GROUP, MOD_GROUP), MOD_GROUP)
        x = jnp.where(is_s, xs_ref[rows, :], xp_ref[rows, :])
        z = alpha * x + g1_ref[pl.ds(g, 1), :] * mix_scr[rows, :]
        x1 = _ln_plain(z) * l1g_ref[...] + l1b_ref[...]
        x1_ref[rows, :] = x1
        h2 = _ln_plain(x1) * (1.0 + sc2_ref[pl.ds(g, 1), :]) + sh2_ref[pl.ds(g, 1), :]
        h2_ref[rows, :] = h2
        hi = h2.astype(BF16)
        hhi_scr[rows, :] = hi
        hlo_scr[rows, :] = (h2 - hi.astype(F32)).astype(BF16)
        return carry
    lax.fori_loop(0, tm // MOD_GROUP, slab, 0, unroll=SLAB_UNROLL)

    _route(hhi_scr, hlo_scr, wr_ref, br_ref, tri_ref, run_scr, meta_ref, wts_ref, cnt_ref)


def _route(hhi_scr, hlo_scr, wr_ref, br_ref, before_ref, run_scr, meta_ref, wts_ref, cnt_ref):
    tm = hhi_scr.shape[0]
    n_r = N_GROUPS + N_EXPERTS
    n_rows = -(-n_r // 8) * 8
    s = _dot(hhi_scr[...], wr_ref[...]) + _dot(hlo_scr[...], wr_ref[...])
    logits = s + pltpu.roll(s, LANES - n_r, axis=1) + br_ref[...]
    lt = jnp.transpose(logits)[0:n_rows, :]

    row = lax.broadcasted_iota(I32, (n_rows, tm), 0)
    row_f = row.astype(F32)
    neg = jnp.float32(-jnp.inf)
    big = jnp.float32(LANES)

    def first_row_of(vals, top):
        return jnp.min(jnp.where(vals == top, row_f, big), axis=0, keepdims=True)

    gl = jnp.where(row < N_GROUPS, lt, neg)
    gmax = jnp.max(gl, axis=0, keepdims=True)
    gsel = first_row_of(gl, gmax)
    p_group = 1.0 / jnp.sum(jnp.exp(gl - gmax), axis=0, keepdims=True)

    e_lo = N_GROUPS + EXPERTS_PER_GROUP * gsel.astype(I32)
    emask = (row >= e_lo) & (row < e_lo + EXPERTS_PER_GROUP)
    el = jnp.where(emask, lt, neg)
    t1 = jnp.max(el, axis=0, keepdims=True)
    i1 = first_row_of(el, t1)
    el2 = jnp.where(row_f == i1, neg, el)
    t2 = jnp.max(el2, axis=0, keepdims=True)
    i2 = first_row_of(el2, t2)
    e2 = jnp.exp(t2 - t1)
    den = 1.0 + e2
    w1 = (1.0 / den) * p_group
    w2 = (e2 / den) * p_group

    sel1 = row_f == i1
    sel2 = row_f == i2
    onehot = jnp.where(sel1 | sel2, 1.0, 0.0)
    run_col = jnp.transpose(jnp.broadcast_to(run_scr[...], (LANES, LANES)))[0:n_rows, 0:1]
    before = _dot(onehot.astype(BF16), before_ref[...]) + run_col
    r1 = jnp.sum(jnp.where(sel1, before, 0.0), axis=0, keepdims=True)
    r2 = jnp.sum(jnp.where(sel2, before, 0.0), axis=0, keepdims=True)
    added = jnp.concatenate([jnp.sum(onehot, axis=1, keepdims=True), jnp.zeros((LANES - n_rows, 1), F32)], axis=0)
    run_scr[...] = run_scr[...] + jnp.transpose(jnp.broadcast_to(added, (LANES, LANES)))[0:1, :]
    cnt_ref[...] = run_scr[...].astype(I32)

    base = jnp.float32(N_GROUPS)
    mrow = lax.broadcasted_iota(I32, (META_ROWS, tm), 0)
    meta = jnp.where(mrow == 0, i1 - base, jnp.where(mrow == 1, i2 - base, jnp.where(mrow == 2, r1, jnp.where(mrow == 3, r2, 0.0))))
    meta_ref[0] = meta.astype(I32)
    wrow = lax.broadcasted_iota(I32, (LANES, tm), 0)
    wts_ref[...] = jnp.transpose(jnp.where(wrow == 0, w1, jnp.where(wrow == 1, w2, 0.0)))


def _post_call(cats_p, cats_s, x_p, x_s, g1g, sh2g, sc2g, w_out_b, ln1_g, ln1_b, wr, br, tri, alpha):
    n_p, d = x_p.shape
    n_s = x_s.shape[0]
    d_a = cats_p[0].shape[1]
    tm = POST_TILE
    n_pt = n_p // tm
    n_tiles = n_pt + n_s // tm
    n_tok = n_p + n_s
    groups = tm // MOD_GROUP
    kern = functools.partial(_post_kernel, n_prompt_tiles=n_pt, alpha=alpha)
    p_idx = lambda i: (jnp.minimum(i, n_pt - 1), 0)
    s_idx = lambda i: (jnp.maximum(i - n_pt, 0), 0)
    tile = lambda i: (i, 0)
    const = lambda i: (0, 0)
    once = pl.Buffered(1)
    s_mode = once if n_tiles - n_pt == 1 else None
    return pl.pallas_call(
        kern,
        grid=(n_tiles,),
        in_specs=[
            pl.BlockSpec((tm, d_a), p_idx),
            pl.BlockSpec((tm, d_a), p_idx),
            pl.BlockSpec((tm, d_a), s_idx, pipeline_mode=s_mode),
            pl.BlockSpec((tm, d_a), s_idx, pipeline_mode=s_mode),
            pl.BlockSpec((tm, d), p_idx),
            pl.BlockSpec((tm, d), s_idx, pipeline_mode=s_mode),
            pl.BlockSpec((groups, d), tile),
            pl.BlockSpec((groups, d), tile),
            pl.BlockSpec((groups, d), tile),
            pl.BlockSpec((d, d), const, pipeline_mode=once),
            pl.BlockSpec((1, d), const),
            pl.BlockSpec((1, d), const),
            pl.BlockSpec((d, LANES), const, pipeline_mode=once),
            pl.BlockSpec((1, LANES), const),
            pl.BlockSpec((tm, tm), const, pipeline_mode=once),
        ],
        out_specs=[
            pl.BlockSpec((tm, d), tile),
            pl.BlockSpec((tm, d), tile),
            pl.BlockSpec((1, META_ROWS, tm), lambda i: (i, 0, 0)),
            pl.BlockSpec((tm, LANES), tile),
            pl.BlockSpec((1, LANES), const),
        ],
        out_shape=[
            jax.ShapeDtypeStruct((n_tok, d), F32),
            jax.ShapeDtypeStruct((n_tok, d), F32),
            jax.ShapeDtypeStruct((n_tiles, META_ROWS, tm), I32),
            jax.ShapeDtypeStruct((n_tok, LANES), F32),
            jax.ShapeDtypeStruct((1, LANES), I32),
        ],
        scratch_shapes=[
            pltpu.VMEM((tm, d), F32),
            pltpu.VMEM((tm, d), BF16),
            pltpu.VMEM((tm, d), BF16),
            pltpu.VMEM((1, LANES), F32),
        ],
        compiler_params=_cparams(("arbitrary",)),
        name="post_router",
    )(*cats_p, *cats_s, x_p, x_s, g1g, sh2g, sc2g, w_out_b, ln1_g, ln1_b, wr, br, tri)


def _dispatch_kernel(slot_ref, gap_start_ref, gap_len_ref, h_ref, xs_ref, zero_scr, sem, tile_sem, *, gaps_per_step):
    i = pl.program_id(0)
    tm = h_ref.shape[0]
    n_gaps = gap_len_ref.shape[0]
    tile_rows = zero_scr.shape[0]

    @pl.when(i == 0)
    def _():
        zero_scr[...] = jnp.zeros_like(zero_scr)

    def row_copy(r, k):
        return pltpu.make_async_copy(h_ref.at[pl.ds(r, 1)], xs_ref.at[pl.ds(slot_ref[0, 0, 2 * r + k], 1)], sem)

    def start_rows(r, carry):
        row_copy(r, 0).start(priority=0)
        row_copy(r, 1).start(priority=1)
        return carry
    lax.fori_loop(0, tm, start_rows, 0, unroll=ROW_DMA_UNROLL)

    def gap_rows(g):
        return jnp.where(g < n_gaps - 1, gap_len_ref[jnp.minimum(g, n_gaps - 1)], 0)

    def gap_tiles(g):
        return jnp.where(g == n_gaps - 1, gap_len_ref[n_gaps - 1] // tile_rows, 0)

    def zero_row_copy(g, r):
        return pltpu.make_async_copy(zero_scr.at[pl.ds(0, 1)], xs_ref.at[pl.ds(gap_start_ref[g] + r, 1)], sem)

    def zero_tile_copy(g, t):
        start = pl.multiple_of(gap_start_ref[g] + t * tile_rows, tile_rows)
        return pltpu.make_async_copy(zero_scr, xs_ref.at[pl.ds(start, tile_rows)], tile_sem)

    for j in range(gaps_per_step):
        g = jnp.minimum(i * gaps_per_step + j, n_gaps - 1)
        live = i * gaps_per_step + j < n_gaps
        n_rows = jnp.where(live, gap_rows(g), 0)
        n_til = jnp.where(live, gap_tiles(g), 0)
        lax.fori_loop(0, n_rows, lambda r, c, g=g: (zero_row_copy(g, r).start(), c)[1], 0)
        lax.fori_loop(0, n_til, lambda t, c, g=g: (zero_tile_copy(g, t).start(), c)[1], 0)

    def wait_rows(r, carry):
        row_copy(r, 0).wait()
        row_copy(r, 1).wait()
        return carry
    lax.fori_loop(0, tm, wait_rows, 0, unroll=ROW_DMA_UNROLL)

    for j in range(gaps_per_step):
        g = jnp.minimum(i * gaps_per_step + j, n_gaps - 1)
        live = i * gaps_per_step + j < n_gaps
        n_rows = jnp.where(live, gap_rows(g), 0)
        n_til = jnp.where(live, gap_tiles(g), 0)
        lax.fori_loop(0, n_rows, lambda r, c, g=g: (zero_row_copy(g, r).wait(), c)[1], 0)
        lax.fori_loop(0, n_til, lambda t, c, g=g: (zero_tile_copy(g, t).wait(), c)[1], 0)


def _dispatch_call(slots3, gap_start, gap_len, h2, n_sorted):
    n_tok, d = h2.shape
    n_tiles = n_tok // TOK_TILE
    n_gaps = gap_len.shape[0]
    kern = functools.partial(_dispatch_kernel, gaps_per_step=-(-n_gaps // n_tiles))
    return pl.pallas_call(
        kern,
        grid=(n_tiles,),
        in_specs=[
            pl.BlockSpec((1, 1, slots3.shape[2]), lambda i: (i, 0, 0), memory_space=pltpu.SMEM),
            pl.BlockSpec((n_gaps,), lambda i: (0,), memory_space=pltpu.SMEM),
            pl.BlockSpec((n_gaps,), lambda i: (0,), memory_space=pltpu.SMEM),
            pl.BlockSpec((TOK_TILE, d), lambda i: (i, 0)),
        ],
        out_specs=pl.BlockSpec(memory_space=pl.ANY),
        out_shape=jax.ShapeDtypeStruct((n_sorted, d), F32),
        scratch_shapes=[pltpu.VMEM((EXP_TILE, d), F32), pltpu.SemaphoreType.DMA(()), pltpu.SemaphoreType.DMA(())],
        compiler_params=_cparams(("arbitrary",)),
        name="dispatch",
    )(slots3, gap_start, gap_len, h2)


def _expert_kernel(te_ref, nt_ref, nxt_ref, run_ref, xs_ref, w1_ref, w3_ref, w2_ref, ys_ref,
                   x_scr, w1_scr, w3_scr, w2f_scr, w13_scr, w2_scr, xsem, wsem):
    t = pl.program_id(0)
    n_t = pl.num_programs(0)
    rows = x_scr.shape[1]
    d_exp = w1_scr.shape[2]
    x_slots = x_scr.shape[0]
    xs = lax.rem(t, x_slots)
    e = te_ref[t]
    prev = te_ref[jnp.maximum(t - 1, 0)]
    changed = (t == 0) | (e != prev)
    valid = t < nt_ref[0]
    wslot = lax.rem(run_ref[t], 2)

    def x_copy(tile, slot):
        start = pl.multiple_of(tile * rows, rows)
        return pltpu.make_async_copy(xs_ref.at[pl.ds(start, rows)], x_scr.at[slot], xsem.at[slot])

    def w_copies(expert, slot):
        return (pltpu.make_async_copy(w1_ref.at[expert], w1_scr.at[slot], wsem.at[slot]),
                pltpu.make_async_copy(w3_ref.at[expert], w3_scr.at[slot], wsem.at[slot]),
                pltpu.make_async_copy(w2_ref.at[expert], w2f_scr.at[slot], wsem.at[slot]))

    @pl.when(t == 0)
    def _():
        for ahead in range(X_AHEAD):
            x_copy(ahead, ahead).start()
        for c in w_copies(e, wslot):
            c.start()

    @pl.when(t + X_AHEAD < n_t)
    def _():
        x_copy(t + X_AHEAD, lax.rem(t + X_AHEAD, x_slots)).start()

    @pl.when(changed)
    def _():
        for c in w_copies(e, wslot):
            c.wait()
        nxt = nxt_ref[t]

        @pl.when(nxt != e)
        def _():
            for c in w_copies(nxt, 1 - wslot):
                c.start()
        w13_scr[:, 0:d_exp] = w1_scr[wslot].astype(BF16)
        w13_scr[:, d_exp:2 * d_exp] = w3_scr[wslot].astype(BF16)
        w2_scr[...] = w2f_scr[wslot].astype(BF16)

    x_copy(t, xs).wait()

    @pl.when(valid)
    def _():
        h13 = _dot(x_scr[xs].astype(BF16), w13_scr[...])
        hm = jax.nn.silu(h13[:, 0:d_exp]) * h13[:, d_exp:2 * d_exp]
        _store_rows(ys_ref, (), 0, _dot(hm.astype(BF16), w2_scr[...]))

    @pl.when(jnp.logical_not(valid))
    def _():
        ys_ref[...] = jnp.zeros_like(ys_ref)


def _expert_call(tile_expert, n_valid, next_expert, run_id, xs, w1, w3, w2):
    n_sorted, d = xs.shape
    n_exp, _, d_exp = w1.shape
    n_tiles = n_sorted // EXP_TILE
    assert n_tiles > X_AHEAD
    grid_spec = pltpu.PrefetchScalarGridSpec(
        num_scalar_prefetch=4,
        grid=(n_tiles,),
        in_specs=[pl.BlockSpec(memory_space=pl.ANY)] * 4,
        out_specs=pl.BlockSpec((EXP_TILE * _lines_per_row(d), LANES), lambda t, te, nt, nx, rn: (t, 0)),
        scratch_shapes=[
            pltpu.VMEM((X_AHEAD + 1, EXP_TILE, d), F32),
            pltpu.VMEM((2, d, d_exp), F32),
            pltpu.VMEM((2, d, d_exp), F32),
            pltpu.VMEM((2, d_exp, d), F32),
            pltpu.VMEM((d, 2 * d_exp), BF16),
            pltpu.VMEM((d_exp, d), BF16),
            pltpu.SemaphoreType.DMA((X_AHEAD + 1,)),
            pltpu.SemaphoreType.DMA((2,)),
        ],
    )
    return pl.pallas_call(
        _expert_kernel,
        grid_spec=grid_spec,
        out_shape=jax.ShapeDtypeStruct((n_sorted * _lines_per_row(d), LANES), F32),
        compiler_params=_cparams(("arbitrary",)),
        name="experts",
    )(tile_expert, n_valid, next_expert, run_id, xs, w1, w3, w2)


def _combine_kernel(scur_ref, snext_ref, x1_ref, wts_ref, g2_ref, l2g_ref, l2b_ref, ys_ref, outp_ref, outs_ref,
                    y_scr, sem, *, n_prompt_tiles, alpha):
    i = pl.program_id(0)
    n_i = pl.num_programs(0)
    is_s = i >= n_prompt_tiles
    tm, d = x1_ref.shape
    lpr = _lines_per_row(d)
    cur = lax.rem(i, 2)

    def row_lines(r):
        return pl.ds(pl.multiple_of(r * lpr, lpr), lpr)

    def start_gathers(s_ref, buf):
        def body(r, carry):
            for k in range(2):
                pltpu.make_async_copy(ys_ref.at[row_lines(s_ref[0, 0, 2 * r + k])], y_scr.at[buf, k, row_lines(r)],
                                      sem.at[buf, k]).start(priority=k)
            return carry
        lax.fori_loop(0, tm, body, 0, unroll=ROW_DMA_UNROLL // 2)

    @pl.when(i == 0)
    def _():
        start_gathers(scur_ref, 0)

    @pl.when(i + 1 < n_i)
    def _():
        start_gathers(snext_ref, 1 - cur)

    for k in range(2):
        pltpu.make_async_copy(y_scr.at[cur, k], y_scr.at[cur, k], sem.at[cur, k]).wait()

    def slab(g, carry, out_ref):
        rows = pl.ds(pl.multiple_of(g * MOD_GROUP, MOD_GROUP), MOD_GROUP)
        w = wts_ref[rows, :]
        moe = (w[:, 0:1] * _load_rows(y_scr, (cur, 0), g * MOD_GROUP, MOD_GROUP, d)
               + w[:, 1:2] * _load_rows(y_scr, (cur, 1), g * MOD_GROUP, MOD_GROUP, d))
        z = alpha * x1_ref[rows, :] + g2_ref[pl.ds(g, 1), :] * moe
        out_ref[rows, :] = _ln_plain(z) * l2g_ref[...] + l2b_ref[...]
        return carry

    @pl.when(is_s)
    def _():
        lax.fori_loop(0, tm // MOD_GROUP, functools.partial(slab, out_ref=outs_ref), 0, unroll=SLAB_UNROLL)

    @pl.when(jnp.logical_not(is_s))
    def _():
        lax.fori_loop(0, tm // MOD_GROUP, functools.partial(slab, out_ref=outp_ref), 0, unroll=SLAB_UNROLL)


def _combine_call(slots3, x1, wts, g2g, ln2_g, ln2_b, ys, n_p, alpha):
    n_tok, d = x1.shape
    n_tiles = n_tok // TOK_TILE
    n_pt = n_p // TOK_TILE
    groups = TOK_TILE // MOD_GROUP
    kern = functools.partial(_combine_kernel, n_prompt_tiles=n_pt, alpha=alpha)
    slot_tile = lambda ahead: pl.BlockSpec(
        (1, 1, slots3.shape[2]), lambda i: (jnp.minimum(i + ahead, n_tiles - 1), 0, 0), memory_space=pltpu.SMEM)
    return pl.pallas_call(
        kern,
        grid=(n_tiles,),
        in_specs=[
            slot_tile(0), slot_tile(1),
            pl.BlockSpec((TOK_TILE, d), lambda i: (i, 0)),
            pl.BlockSpec((TOK_TILE, LANES), lambda i: (i, 0)),
            pl.BlockSpec((groups, d), lambda i: (i, 0)),
            pl.BlockSpec((1, d), lambda i: (0, 0)),
            pl.BlockSpec((1, d), lambda i: (0, 0)),
            pl.BlockSpec(memory_space=pl.ANY),
        ],
        out_specs=[
            pl.BlockSpec((TOK_TILE, d), lambda i: (jnp.minimum(i, n_pt - 1), 0)),
            pl.BlockSpec((TOK_TILE, d), lambda i: (jnp.maximum(i - n_pt, 0), 0)),
        ],
        out_shape=[
            jax.ShapeDtypeStruct((n_p, d), F32),
            jax.ShapeDtypeStruct((n_tok - n_p, d), F32),
        ],
        scratch_shapes=[
            pltpu.VMEM((2, 2, TOK_TILE * _lines_per_row(d), LANES), F32),
            pltpu.SemaphoreType.DMA((2, 2)),
        ],
        compiler_params=_cparams(("arbitrary",)),
        name="combine",
    )(slots3, slots3, x1, wts, g2g, ln2_g, ln2_b, ys)


def _routing_tables(meta, cnt_row, n_sorted):
    n_tok = meta.shape[0] * meta.shape[2]
    experts = jnp.transpose(meta[:, 0:2, :], (0, 2, 1)).reshape(n_tok, 2)
    ranks = jnp.transpose(meta[:, 2:4, :], (0, 2, 1)).reshape(n_tok, 2)
    cnt = cnt_row[0, N_GROUPS:N_GROUPS + N_EXPERTS]
    padded = ((cnt + EXP_TILE - 1) // EXP_TILE) * EXP_TILE
    ends = jnp.cumsum(padded)
    offs = ends - padded
    slots = offs[experts] + ranks
    total = ends[-1]
    n_tiles = n_sorted // EXP_TILE
    tile_ids = jnp.arange(n_tiles, dtype=I32)
    tile_expert = jnp.minimum(
        jnp.sum((tile_ids[:, None] >= (ends // EXP_TILE)[None, :]).astype(I32), axis=1), N_EXPERTS - 1)
    n_valid = (total // EXP_TILE).reshape(1).astype(I32)
    ids = jnp.arange(N_EXPERTS, dtype=I32)
    later = (ids[None, :] > tile_expert[:, None]) & (padded[None, :] > 0)
    next_expert = jnp.min(jnp.where(later, ids[None, :], N_EXPERTS), axis=1)
    next_expert = jnp.where(next_expert == N_EXPERTS, tile_expert, next_expert).astype(I32)
    starts_run = jnp.concatenate([jnp.zeros((1,), I32), (tile_expert[1:] != tile_expert[:-1]).astype(I32)])
    run_id = jnp.cumsum(starts_run).astype(I32)
    gap_start = jnp.concatenate([offs + cnt, total.reshape(1)]).astype(I32)
    gap_len = jnp.concatenate([padded - cnt, (n_sorted - total).reshape(1)]).astype(I32)
    return slots.astype(I32), (tile_expert, n_valid, next_expert, run_id), gap_start, gap_len


def _layer(layer, n_layers, xp, xs, s0_l, c_all, p):
    (w_ada, b_ada, w_in, ln_v_g, ln_v_b, w_s, b_s, hgrn_lb, gnorm_g, w_out, ln1_g, ln1_b,
     w_rg, b_rg, w_re, b_re, w1, w3, w2, ln2_g, ln2_b) = p
    bsz, t, d = xp.shape
    n_streams, seq, _ = xs.shape
    alpha = float((2.0 * n_layers) ** 0.25)
    n_p = bsz * t
    n_s = n_streams * seq
    n_tok = n_p + n_s

    n_c = c_all.shape[0]
    c_pad = jnp.pad(c_all, ((0, (-n_c) % 8), (0, 0)))
    mod = _ada_call(c_pad, w_ada, b_ada)[:n_c]
    sh1, sc1, g1, sh2, sc2, g2 = [mod[:, m * d:(m + 1) * d] for m in range(6)]

    w_in_b = w_in.astype(BF16)
    w_out_b = w_out.astype(BF16)
    lvg = ln_v_g.reshape(N_HEADS, 1, HEAD_DIM)
    lvb = ln_v_b.reshape(N_HEADS, 1, HEAD_DIM)
    b_s3 = b_s.reshape(N_HEADS, GMLP_CHUNK, 1)
    lb_h = jnp.transpose(hgrn_lb.reshape(hgrn_lb.shape[0], N_HEADS, HEAD_DIM), (1, 0, 2))
    gn = gnorm_g.reshape(N_HEADS, 1, HEAD_DIM)

    ca_p, co_p, state_p = _mixer_prompt_call(xp, sh1[:bsz], sc1[:bsz], w_in_b, lvg, lvb, w_s, b_s3, lb_h, gn, layer)
    ca_s, co_s, state_s, vrows = _mixer_sample_call(xs.reshape(n_s, d), sh1[bsz:], sc1[bsz:], w_in_b, lvg, lvb, w_s,
                                               b_s3, lb_h, gn, s0_l, layer, seq)

    group_stream = jnp.concatenate([
        jnp.repeat(jnp.arange(bsz, dtype=I32), t // MOD_GROUP),
        bsz + jnp.repeat(jnp.arange(n_streams, dtype=I32), seq // MOD_GROUP)])
    g1g, sh2g, sc2g, g2g = [m[group_stream] for m in (g1, sh2, sc2, g2)]

    wr = jnp.concatenate([w_rg, w_re], axis=1)
    wr_hi = wr.astype(BF16)
    wr_lo = (wr - wr_hi.astype(F32)).astype(BF16)
    n_r = wr.shape[1]
    wr_cat = jnp.concatenate([wr_hi, wr_lo, jnp.zeros((d, LANES - 2 * n_r), BF16)], axis=1)
    br = jnp.pad(jnp.concatenate([b_rg, b_re]), (0, LANES - n_r)).reshape(1, LANES)
    tri = jnp.triu(jnp.ones((POST_TILE, POST_TILE), F32), 1).astype(BF16)

    x1, h2, meta, wts, cnt_row = _post_call((ca_p, co_p), (ca_s, co_s), xp.reshape(n_p, d), xs.reshape(n_s, d),
                                            g1g, sh2g, sc2g, w_out_b, ln1_g.reshape(1, d), ln1_b.reshape(1, d),
                                            wr_cat, br, tri, alpha)

    n_sorted = 2 * n_tok + N_EXPERTS * EXP_TILE
    slots, tile_tables, gap_start, gap_len = _routing_tables(meta, cnt_row, n_sorted)
    n_tiles = n_tok // TOK_TILE
    slots3 = slots.reshape(n_tiles, 1, 2 * TOK_TILE)

    xs_sorted = _dispatch_call(slots3, gap_start, gap_len, h2, n_sorted)
    ys_sorted = _expert_call(*tile_tables, xs_sorted, w1, w3, w2)
    yp, ys_out = _combine_call(slots3, x1, wts, g2g, ln2_g.reshape(1, d), ln2_b.reshape(1, d), ys_sorted, n_p, alpha)

    v_rows = vrows.reshape(n_streams, seq, N_HEADS, HEAD_DIM)
    return yp.reshape(bsz, t, d), ys_out.reshape(n_streams, seq, d), state_p, state_s, v_rows


def kernel(x_prompt, x_sample, state_hgrn, c_prompt, c_sample, w_ada, b_ada, w_in, ln_v_g, ln_v_b, w_s, b_s, hgrn_lb, gnorm_g, w_out, ln1_g, ln1_b, w_router_g, b_router_g, w_router_e, b_router_e, w1, w3, w2, ln2_g, ln2_b):
    n_layers = w_ada.shape[0]
    assert x_prompt.shape[1] % MIX_TILE == 0 and x_prompt.shape[2] == 2 * N_HEADS * HEAD_DIM
    assert x_sample.shape[1] % MOD_GROUP == 0 and x_sample.shape[1] <= SUB_CHUNK
    assert (x_sample.shape[0] * x_sample.shape[1]) % TOK_TILE == 0 and TOK_TILE % POST_TILE == 0
    c_all = jnp.concatenate([c_prompt, c_sample], axis=0)
    xp, xs = x_prompt, x_sample
    sp_list, ss_list, vs_list = [], [], []
    for l in range(n_layers):
        p = (w_ada[l], b_ada[l], w_in[l], ln_v_g[l], ln_v_b[l], w_s[l], b_s[l], hgrn_lb, gnorm_g[l], w_out[l],
             ln1_g[l], ln1_b[l], w_router_g[l], b_router_g[l], w_router_e[l], b_router_e[l],
             w1[l], w3[l], w2[l], ln2_g[l], ln2_b[l])
        xp, xs, sp, ss, vs = _layer(l, n_layers, xp, xs, state_hgrn[l], c_all, p)
        sp_list.append(sp.astype(state_hgrn.dtype))
        ss_list.append(ss.astype(state_hgrn.dtype))
        vs_list.append(vs)
    return (xp, xs, jnp.stack(sp_list, axis=0), jnp.stack(ss_list, axis=0), jnp.stack(vs_list, axis=0))
```

```python
import functools

import jax
import jax.numpy as jnp
from jax import lax
from jax.experimental import pallas as pl
from jax.experimental.pallas import tpu as pltpu

F32 = jnp.float32
BF16 = jnp.bfloat16
I32 = jnp.int32

N_HEADS = 8
HEAD_DIM = 128
GMLP_CHUNK = 128
SUB_CHUNK = 64
N_GROUPS = 4
EXPERTS_PER_GROUP = 8
N_EXPERTS = N_GROUPS * EXPERTS_PER_GROUP
LN_EPS = 1e-5
N_KINDS = 6

LANES = 128
MIX_TILE = 512
HEADS_PER_STEP = 2
HGRN_BLOCK = 64
DIAG_BLOCK = 16
SAFE_BLOCK_DECAY = 60.0
TOK_TILE = 512
POST_TILE = 512
META_ROWS = 8
MOD_GROUP = 32
SLAB_UNROLL = 4
EXP_TILE = 256
X_AHEAD = 2
ROW_DMA_UNROLL = 8
ADA_TILE = 1024
VMEM_LIMIT = 56 * 1024 * 1024


def _cparams(sem):
    return pltpu.CompilerParams(dimension_semantics=sem, vmem_limit_bytes=VMEM_LIMIT)


def _ln_plain(x):
    mu = jnp.mean(x, axis=-1, keepdims=True)
    xc = x - mu
    var = jnp.mean(xc * xc, axis=-1, keepdims=True)
    return xc * lax.rsqrt(var + LN_EPS)


def _dot(a, b):
    return jnp.dot(a, b, preferred_element_type=F32)


def _dot_nt(a, b):
    return lax.dot_general(a, b, (((1,), (1,)), ((), ())), preferred_element_type=F32)


def _lines_per_row(d):
    return d // LANES


def _load_rows(ref, lead, row0, n_rows, d):
    lpr = _lines_per_row(d)
    parts = [ref[lead + (pl.ds(row0 * lpr + c, n_rows, stride=lpr), slice(None))] for c in range(lpr)]
    return jnp.concatenate(parts, axis=1)


def _store_rows(ref, lead, row0, val):
    n_rows, d = val.shape
    lpr = _lines_per_row(d)
    for c in range(lpr):
        ref[lead + (pl.ds(row0 * lpr + c, n_rows, stride=lpr), slice(None))] = val[:, c * LANES:(c + 1) * LANES]


def _ada_kernel(c_ref, w_ref, b_ref, o_ref):
    s = jax.nn.silu(c_ref[...]).astype(BF16)
    o_ref[...] = _dot(s, w_ref[...].astype(BF16)) + b_ref[...]


def _ada_call(c_pad, w_ada, b_ada):
    rows, d = c_pad.shape
    n_out = w_ada.shape[1]
    return pl.pallas_call(
        _ada_kernel,
        grid=(n_out // ADA_TILE,),
        in_specs=[
            pl.BlockSpec((rows, d), lambda n: (0, 0)),
            pl.BlockSpec((d, ADA_TILE), lambda n: (0, n)),
            pl.BlockSpec((1, ADA_TILE), lambda n: (0, n)),
        ],
        out_specs=pl.BlockSpec((rows, ADA_TILE), lambda n: (0, n)),
        out_shape=jax.ShapeDtypeStruct((rows, n_out), F32),
        compiler_params=_cparams(("arbitrary",)),
        name="adaln",
    )(c_pad, w_ada, b_ada.reshape(1, n_out))


def _lower_bound(lb_ref, hd, layer):
    raw = lb_ref[hd]
    m = jnp.max(raw, axis=0, keepdims=True)
    e = jnp.exp(raw - m)
    p = e / jnp.sum(e, axis=0, keepdims=True)
    return jnp.sum(p[: layer + 1], axis=0, keepdims=True)


def _gmlp_rows(u_pre, v_pre, ln_g, ln_b):
    u = jax.nn.gelu(u_pre)
    v = jax.nn.gelu(v_pre)
    mu = jnp.mean(v, axis=-1, keepdims=True)
    vc = v - mu
    var = jnp.mean(vc * vc, axis=-1, keepdims=True)
    vn = vc * lax.rsqrt(var + LN_EPS) * ln_g + ln_b
    return u, vn


def _row_bcast(a, row, n):
    return jnp.broadcast_to(a[row:row + 1, :], (n, a.shape[1]))


def _block_id(idx, size):
    return lax.shift_right_logical(idx, I32(size.bit_length() - 1))


def _hgrn_masks(c):
    row = lax.broadcasted_iota(I32, (c, c), 0)
    col = lax.broadcasted_iota(I32, (c, c), 1)
    masks = []
    half = c // 2
    while half >= DIAG_BLOCK:
        span = 2 * half
        same = _block_id(row, span) == _block_id(col, span)
        masks.append(same & ((row & (span - 1)) >= half) & ((col & (span - 1)) < half))
        half //= 2
    diag = (_block_id(row, DIAG_BLOCK) == _block_id(col, DIAG_BLOCK)) & (col <= row)
    return masks, diag


def _tri_ones(c):
    row = lax.broadcasted_iota(I32, (c, c), 0)
    col = lax.broadcasted_iota(I32, (c, c), 1)
    return jnp.where(col <= row, 1.0, 0.0).astype(BF16)


def _level_halves(c):
    halves = []
    half = c // 2
    while half >= 1:
        halves.append(half)
        half //= 2
    return halves


def _level_masks(c):
    row = lax.broadcasted_iota(I32, (c, c), 0)
    col = lax.broadcasted_iota(I32, (c, c), 1)
    masks = []
    for half in _level_halves(c):
        span = 2 * half
        same = _block_id(row, span) == _block_id(col, span)
        masks.append(same & ((row & (span - 1)) >= half) & ((col & (span - 1)) < half))
    return masks, row == col


def _level_ref_sums(c):
    row = lax.broadcasted_iota(I32, (c, c), 0)
    col = lax.broadcasted_iota(I32, (c, c), 1)
    pieces = [col <= _block_id(row, 2 * half) * (2 * half) + half for half in _level_halves(c)]
    return jnp.concatenate([jnp.where(p, 1.0, 0.0).astype(BF16) for p in pieces], axis=0)


def _hgrn_block_decay(units, c):
    worst = None
    for u in units:
        a = u["a"]
        for b in range(c // DIAG_BLOCK):
            span = a[b * DIAG_BLOCK:b * DIAG_BLOCK + 1, :] - a[(b + 1) * DIAG_BLOCK - 1:(b + 1) * DIAG_BLOCK, :]
            worst = span if worst is None else jnp.maximum(worst, span)
    return jnp.max(worst)


def _hgrn_prepare(chains, c):
    tri = _tri_ones(c)
    units = []
    for st0, lbv, gn, blocks in chains:
        for q_pre, f_pre, i_pre, g_pre in blocks:
            q = jax.nn.silu(q_pre)
            fg = lbv + (1.0 - lbv) * jax.nn.sigmoid(f_pre)
            logf = jnp.log(fg)
            hi = logf.astype(BF16)
            lo = (logf - hi.astype(F32)).astype(BF16)
            units.append(dict(q=q, k=1.0 - fg, hilo=jnp.concatenate([hi, lo], axis=1),
                              v=i_pre, g=g_pre, gn=gn))

    for u in units:
        p = _dot(tri, u["hilo"])
        u["a"] = p[:, 0:HEAD_DIM] + p[:, HEAD_DIM:2 * HEAD_DIM]
    return units


def _hgrn_scores(units, c, exact):
    units = [dict(u) for u in units]
    if exact:
        masks, eye = _level_masks(c)
        sums = _level_ref_sums(c)
    else:
        masks, diag_mask = _hgrn_masks(c)
    for u in units:
        q, k, a = u["q"], u["k"], u["a"]
        parts = []
        if exact:
            p = _dot(sums, u["hilo"])
            p = p[:, 0:HEAD_DIM] + p[:, HEAD_DIM:2 * HEAD_DIM]
            for n, mask in enumerate(masks):
                d = a - p[n * c:(n + 1) * c]
                e = jnp.exp(jnp.minimum(d, -d))
                parts.append((mask, _dot_nt((q * e).astype(BF16), (k * e).astype(BF16))))
            parts.append((eye, jnp.sum(q * k, axis=-1, keepdims=True)))
        else:
            half = c // 2
            level = 0
            while half >= DIAG_BLOCK:
                span = 2 * half
                ref = jnp.concatenate([_row_bcast(a, b * span + half, span) for b in range(c // span)], axis=0)
                qs = (q * jnp.exp(jnp.minimum(a - ref, 0.0))).astype(BF16)
                ks = (k * jnp.exp(jnp.minimum(ref - a, 0.0))).astype(BF16)
                parts.append((masks[level], _dot_nt(qs, ks)))
                half //= 2
                level += 1
            ref = jnp.concatenate([_row_bcast(a, b * DIAG_BLOCK, DIAG_BLOCK) for b in range(c // DIAG_BLOCK)], axis=0)
            qd = (q * jnp.exp(a - ref)).astype(BF16)
            kd = (k * jnp.exp(jnp.minimum(ref - a, SAFE_BLOCK_DECAY))).astype(BF16)
            parts.append((diag_mask, _dot_nt(qd, kd)))
        u["parts"] = parts
        a_last = a[c - 1:c, :]
        u["decay"] = jnp.exp(a_last)
        kl = (k * jnp.exp(a_last - a)).astype(BF16)
        v_t = jnp.transpose(u["v"]).astype(BF16)
        u["upd"] = _dot(v_t, kl)
        u["qe"] = (q * jnp.exp(a)).astype(BF16)
    return units


def _hgrn_finish(chains, units):
    finals = []
    n = 0
    for st0, lbv, gn, blocks in chains:
        st = st0
        for _ in blocks:
            units[n]["st_in"] = st.astype(BF16)
            st = st * units[n]["decay"] + units[n]["upd"]
            n += 1
        finals.append(st)

    outs = []
    n = 0
    for st0, lbv, gn, blocks in chains:
        chain_out = []
        for _ in blocks:
            u = units[n]
            scores = None
            for mask, part in u["parts"]:
                part = jnp.where(mask, part, 0.0)
                scores = part if scores is None else scores + part
            o = _dot(scores.astype(BF16), u["v"].astype(BF16)) + _dot_nt(u["qe"], u["st_in"])
            o = o * lax.rsqrt(jnp.mean(o * o, axis=-1, keepdims=True) + LN_EPS) * u["gn"]
            chain_out.append(o * jax.nn.silu(u["g"]))
            n += 1
        outs.append(chain_out)
    return outs, finals


def _proj_cols(kind, hd):
    c0 = (kind * HEADS_PER_STEP + hd) * HEAD_DIM
    return slice(c0, c0 + HEAD_DIM)


def _in_projection(h_scr, w_refs, proj_scr):
    width = HEADS_PER_STEP * HEAD_DIM
    for kind, w_ref in enumerate(w_refs):
        proj_scr[:, kind * width:(kind + 1) * width] = _dot(h_scr[...], w_ref[...])


def _mixer_prompt_kernel(x_ref, sh_ref, sc_ref, wu_ref, wv_ref, wq_ref, wf_ref, wi_ref, wg_ref,
                         lvg_ref, lvb_ref, ws_ref, bs_ref, lb_ref, gn_ref,
                         cata_ref, cato_ref, state_ref, h_scr, proj_scr, st_scr, *, layer):
    i = pl.program_id(1)
    j = pl.program_id(2)
    tm = h_scr.shape[0]

    @pl.when(j == 0)
    def _():
        def slab(r, carry):
            rows = pl.ds(pl.multiple_of(r * HGRN_BLOCK, HGRN_BLOCK), HGRN_BLOCK)
            h = _ln_plain(x_ref[0, rows, :]) * (1.0 + sc_ref[0]) + sh_ref[0]
            h_scr[rows, :] = h.astype(BF16)
            return carry
        lax.fori_loop(0, tm // HGRN_BLOCK, slab, 0)

    @pl.when((i == 0) & (j == 0))
    def _():
        st_scr[...] = jnp.zeros_like(st_scr)

    _in_projection(h_scr, (wu_ref, wv_ref, wq_ref, wf_ref, wi_ref, wg_ref), proj_scr)

    row = lax.broadcasted_iota(I32, (GMLP_CHUNK, GMLP_CHUNK), 0)
    col = lax.broadcasted_iota(I32, (GMLP_CHUNK, GMLP_CHUNK), 1)
    causal = _block_id(row, SUB_CHUNK) >= _block_id(col, SUB_CHUNK)
    for hd in range(HEADS_PER_STEP):
        wm = jnp.where(causal, ws_ref[hd], 0.0).astype(BF16)
        b_col = bs_ref[hd]
        for c in range(tm // GMLP_CHUNK):
            rows = slice(c * GMLP_CHUNK, (c + 1) * GMLP_CHUNK)
            u, vn = _gmlp_rows(proj_scr[rows, _proj_cols(0, hd)], proj_scr[rows, _proj_cols(1, hd)],
                               lvg_ref[hd], lvb_ref[hd])
            sp = _dot(wm, vn.astype(BF16)) + b_col
            cata_ref[rows, hd * HEAD_DIM:(hd + 1) * HEAD_DIM] = (u * sp).astype(BF16)

    chains = []
    for hd in range(HEADS_PER_STEP):
        blocks = [tuple(proj_scr[c * HGRN_BLOCK:(c + 1) * HGRN_BLOCK, _proj_cols(kind, hd)] for kind in range(2, N_KINDS))
                  for c in range(tm // HGRN_BLOCK)]
        chains.append((st_scr[HEADS_PER_STEP * j + hd], _lower_bound(lb_ref, hd, layer), gn_ref[hd], blocks))
    units = _hgrn_prepare(chains, HGRN_BLOCK)
    wide_decay = _hgrn_block_decay(units, HGRN_BLOCK) > SAFE_BLOCK_DECAY

    def finish(exact):
        outs, finals = _hgrn_finish(chains, _hgrn_scores(units, HGRN_BLOCK, exact))
        for hd in range(HEADS_PER_STEP):
            for c, out in enumerate(outs[hd]):
                cato_ref[c * HGRN_BLOCK:(c + 1) * HGRN_BLOCK, hd * HEAD_DIM:(hd + 1) * HEAD_DIM] = out.astype(BF16)
            st_scr[HEADS_PER_STEP * j + hd] = finals[hd]
            state_ref[0, HEADS_PER_STEP * j + hd] = jnp.transpose(finals[hd])

    @pl.when(wide_decay)
    def _():
        finish(True)

    @pl.when(jnp.logical_not(wide_decay))
    def _():
        finish(False)


def _w_in_specs(d):
    width = HEADS_PER_STEP * HEAD_DIM
    blocks_per_kind = N_HEADS // HEADS_PER_STEP
    return [pl.BlockSpec((d, width), lambda j, kind=kind: (0, kind * blocks_per_kind + j)) for kind in range(N_KINDS)]


def _mixer_prompt_call(x, sh1, sc1, w_in_b, ln_v_g, ln_v_b, w_s, b_s, lb_h, gnorm_g, layer):
    bsz, t, d = x.shape
    n_t = t // MIX_TILE
    hp = HEADS_PER_STEP
    kern = functools.partial(_mixer_prompt_kernel, layer=layer)
    head_vec = lambda b, i, j: (j, 0, 0)
    return pl.pallas_call(
        kern,
        grid=(bsz, n_t, N_HEADS // hp),
        in_specs=[
            pl.BlockSpec((1, MIX_TILE, d), lambda b, i, j: (b, i, 0)),
            pl.BlockSpec((1, 1, d), lambda b, i, j: (b, 0, 0)),
            pl.BlockSpec((1, 1, d), lambda b, i, j: (b, 0, 0)),
            *[pl.BlockSpec((d, hp * HEAD_DIM), lambda b, i, j, kind=kind: (0, kind * (N_HEADS // hp) + j))
              for kind in range(N_KINDS)],
            pl.BlockSpec((hp, 1, HEAD_DIM), head_vec),
            pl.BlockSpec((hp, 1, HEAD_DIM), head_vec),
            pl.BlockSpec((hp, GMLP_CHUNK, GMLP_CHUNK), head_vec),
            pl.BlockSpec((hp, GMLP_CHUNK, 1), head_vec),
            pl.BlockSpec((hp, lb_h.shape[1], HEAD_DIM), head_vec),
            pl.BlockSpec((hp, 1, HEAD_DIM), head_vec),
        ],
        out_specs=[
            pl.BlockSpec((MIX_TILE, hp * HEAD_DIM), lambda b, i, j: (b * n_t + i, j)),
            pl.BlockSpec((MIX_TILE, hp * HEAD_DIM), lambda b, i, j: (b * n_t + i, j)),
            pl.BlockSpec((1, N_HEADS, HEAD_DIM, HEAD_DIM), lambda b, i, j: (b, 0, 0, 0)),
        ],
        out_shape=[
            jax.ShapeDtypeStruct((bsz * t, N_HEADS * HEAD_DIM), BF16),
            jax.ShapeDtypeStruct((bsz * t, N_HEADS * HEAD_DIM), BF16),
            jax.ShapeDtypeStruct((bsz, N_HEADS, HEAD_DIM, HEAD_DIM), F32),
        ],
        scratch_shapes=[
            pltpu.VMEM((MIX_TILE, d), BF16),
            pltpu.VMEM((MIX_TILE, N_KINDS * hp * HEAD_DIM), F32),
            pltpu.VMEM((N_HEADS, HEAD_DIM, HEAD_DIM), F32),
        ],
        compiler_params=_cparams(("arbitrary", "arbitrary", "arbitrary")),
        name="mixer_prompt",
    )(x, sh1.reshape(bsz, 1, d), sc1.reshape(bsz, 1, d), *([w_in_b] * N_KINDS), ln_v_g, ln_v_b, w_s, b_s, lb_h, gnorm_g)


def _mixer_sample_kernel(x_ref, sh_ref, sc_ref, wu_ref, wv_ref, wq_ref, wf_ref, wi_ref, wg_ref,
                         lvg_ref, lvb_ref, ws_ref, bs_ref, lb_ref, gn_ref, s0_ref,
                         cata_ref, cato_ref, state_ref, vrows_ref, h_scr, proj_scr, *, layer, seq):
    j = pl.program_id(0)
    n_streams = x_ref.shape[0] // seq

    @pl.when(j == 0)
    def _():
        def slab(r, carry):
            rows = pl.ds(pl.multiple_of(r * seq, seq), seq)
            h = _ln_plain(x_ref[rows, :]) * (1.0 + sc_ref[pl.ds(r, 1), :]) + sh_ref[pl.ds(r, 1), :]
            h_scr[rows, :] = h.astype(BF16)
            return carry
        lax.fori_loop(0, n_streams, slab, 0)

    _in_projection(h_scr, (wu_ref, wv_ref, wq_ref, wf_ref, wi_ref, wg_ref), proj_scr)

    chains = []
    for hd in range(HEADS_PER_STEP):
        wm = ws_ref[hd][0:seq, 0:seq].astype(BF16)
        b_col = bs_ref[hd][0:seq, :]
        lbv = _lower_bound(lb_ref, hd, layer)
        for n in range(n_streams):
            rows = slice(n * seq, (n + 1) * seq)
            u, vn = _gmlp_rows(proj_scr[rows, _proj_cols(0, hd)], proj_scr[rows, _proj_cols(1, hd)],
                               lvg_ref[hd], lvb_ref[hd])
            vrows_ref[rows, hd * HEAD_DIM:(hd + 1) * HEAD_DIM] = vn
            sp = _dot(wm, vn.astype(BF16)) + b_col
            cata_ref[rows, hd * HEAD_DIM:(hd + 1) * HEAD_DIM] = (u * sp).astype(BF16)
            chains.append((jnp.transpose(s0_ref[n, hd]), lbv, gn_ref[hd],
                           [tuple(proj_scr[rows, _proj_cols(kind, hd)] for kind in range(2, N_KINDS))]))

    units = _hgrn_prepare(chains, seq)
    wide_decay = _hgrn_block_decay(units, seq) > SAFE_BLOCK_DECAY

    def finish(exact):
        outs, finals = _hgrn_finish(chains, _hgrn_scores(units, seq, exact))
        for hd in range(HEADS_PER_STEP):
            for n in range(n_streams):
                r0 = n * seq
                m = hd * n_streams + n
                cato_ref[r0:r0 + seq, hd * HEAD_DIM:(hd + 1) * HEAD_DIM] = outs[m][0].astype(BF16)
                state_ref[n, hd] = jnp.transpose(finals[m])

    @pl.when(wide_decay)
    def _():
        finish(True)

    @pl.when(jnp.logical_not(wide_decay))
    def _():
        finish(False)


def _mixer_sample_call(x2d, sh1, sc1, w_in_b, ln_v_g, ln_v_b, w_s, b_s, lb_h, gnorm_g, s0, layer, seq):
    rows, d = x2d.shape
    n_streams = rows // seq
    hp = HEADS_PER_STEP
    kern = functools.partial(_mixer_sample_kernel, layer=layer, seq=seq)
    head_vec = lambda j: (j, 0, 0)
    return pl.pallas_call(
        kern,
        grid=(N_HEADS // hp,),
        in_specs=[
            pl.BlockSpec((rows, d), lambda j: (0, 0)),
            pl.BlockSpec((n_streams, d), lambda j: (0, 0)),
            pl.BlockSpec((n_streams, d), lambda j: (0, 0)),
            *_w_in_specs(d),
            pl.BlockSpec((hp, 1, HEAD_DIM), head_vec),
            pl.BlockSpec((hp, 1, HEAD_DIM), head_vec),
            pl.BlockSpec((hp, GMLP_CHUNK, GMLP_CHUNK), head_vec),
            pl.BlockSpec((hp, GMLP_CHUNK, 1), head_vec),
            pl.BlockSpec((hp, lb_h.shape[1], HEAD_DIM), head_vec),
            pl.BlockSpec((hp, 1, HEAD_DIM), head_vec),
            pl.BlockSpec((n_streams, hp, HEAD_DIM, HEAD_DIM), lambda j: (0, j, 0, 0)),
        ],
        out_specs=[
            pl.BlockSpec((rows, hp * HEAD_DIM), lambda j: (0, j)),
            pl.BlockSpec((rows, hp * HEAD_DIM), lambda j: (0, j)),
            pl.BlockSpec((n_streams, hp, HEAD_DIM, HEAD_DIM), lambda j: (0, j, 0, 0)),
            pl.BlockSpec((rows, hp * HEAD_DIM), lambda j: (0, j)),
        ],
        out_shape=[
            jax.ShapeDtypeStruct((rows, N_HEADS * HEAD_DIM), BF16),
            jax.ShapeDtypeStruct((rows, N_HEADS * HEAD_DIM), BF16),
            jax.ShapeDtypeStruct((n_streams, N_HEADS, HEAD_DIM, HEAD_DIM), F32),
            jax.ShapeDtypeStruct((rows, N_HEADS * HEAD_DIM), F32),
        ],
        scratch_shapes=[
            pltpu.VMEM((rows, d), BF16),
            pltpu.VMEM((rows, N_KINDS * hp * HEAD_DIM), F32),
        ],
        compiler_params=_cparams(("arbitrary",)),
        name="mixer_sample",
    )(x2d, sh1, sc1, *([w_in_b] * N_KINDS), ln_v_g, ln_v_b, w_s, b_s, lb_h, gnorm_g, s0)


def _post_kernel(cap_ref, cop_ref, cas_ref, cos_ref, xp_ref, xs_ref, g1_ref, sh2_ref, sc2_ref, wout_ref, l1g_ref, l1b_ref,
                 wr_ref, br_ref, tri_ref,
                 x1_ref, h2_ref, meta_ref, wts_ref, cnt_ref,
                 mix_scr, hhi_scr, hlo_scr, run_scr, *, n_prompt_tiles, alpha):
    i = pl.program_id(0)
    is_s = i >= n_prompt_tiles
    tm = mix_scr.shape[0]

    @pl.when(is_s)
    def _():
        mix_scr[...] = _dot(jnp.concatenate([cas_ref[...], cos_ref[...]], axis=1), wout_ref[...])

    @pl.when(jnp.logical_not(is_s))
    def _():
        mix_scr[...] = _dot(jnp.concatenate([cap_ref[...], cop_ref[...]], axis=1), wout_ref[...])

    @pl.when(i == 0)
    def _():
        run_scr[...] = jnp.zeros_like(run_scr)

    def slab(g, carry):
        rows = pl.ds(pl.multiple_of(g * MOD_GROUP, MOD_GROUP), MOD_GROUP)
        x = jnp.where(is_s, xs_ref[rows, :], xp_ref[rows, :])
        z = alpha * x + g1_ref[pl.ds(g, 1), :] * mix_scr[rows, :]
        x1 = _ln_plain(z) * l1g_ref[...] + l1b_ref[...]
        x1_ref[rows, :] = x1
        h2 = _ln_plain(x1) * (1.0 + sc2_ref[pl.ds(g, 1), :]) + sh2_ref[pl.ds(g, 1), :]
        h2_ref[rows, :] = h2
        hi = h2.astype(BF16)
        hhi_scr[rows, :] = hi
        hlo_scr[rows, :] = (h2 - hi.astype(F32)).astype(BF16)
        return carry
    lax.fori_loop(0, tm // MOD_GROUP, slab, 0, unroll=SLAB_UNROLL)

    _route(hhi_scr, hlo_scr, wr_ref, br_ref, tri_ref, run_scr, meta_ref, wts_ref, cnt_ref)


def _route(hhi_scr, hlo_scr, wr_ref, br_ref, before_ref, run_scr, meta_ref, wts_ref, cnt_ref):
    tm = hhi_scr.shape[0]
    n_r = N_GROUPS + N_EXPERTS
    n_rows = -(-n_r // 8) * 8
    s = _dot(hhi_scr[...], wr_ref[...]) + _dot(hlo_scr[...], wr_ref[...])
    logits = s + pltpu.roll(s, LANES - n_r, axis=1) + br_ref[...]
    lt = jnp.transpose(logits)[0:n_rows, :]

    row = lax.broadcasted_iota(I32, (n_rows, tm), 0)
    row_f = row.astype(F32)
    neg = jnp.float32(-jnp.inf)
    big = jnp.float32(LANES)

    def first_row_of(vals, top):
        return jnp.min(jnp.where(vals == top, row_f, big), axis=0, keepdims=True)

    gl = jnp.where(row < N_GROUPS, lt, neg)
    gmax = jnp.max(gl, axis=0, keepdims=True)
    gsel = first_row_of(gl, gmax)
    p_group = 1.0 / jnp.sum(jnp.exp(gl - gmax), axis=0, keepdims=True)

    e_lo = N_GROUPS + EXPERTS_PER_GROUP * gsel.astype(I32)
    emask = (row >= e_lo) & (row < e_lo + EXPERTS_PER_GROUP)
    el = jnp.where(emask, lt, neg)
    t1 = jnp.max(el, axis=0, keepdims=True)
    i1 = first_row_of(el, t1)
    el2 = jnp.where(row_f == i1, neg, el)
    t2 = jnp.max(el2, axis=0, keepdims=True)
    i2 = first_row_of(el2, t2)
    e2 = jnp.exp(t2 - t1)
    den = 1.0 + e2
    w1 = (1.0 / den) * p_group
    w2 = (e2 / den) * p_group

    sel1 = row_f == i1
    sel2 = row_f == i2
    onehot = jnp.where(sel1 | sel2, 1.0, 0.0)
    run_col = jnp.transpose(jnp.broadcast_to(run_scr[...], (LANES, LANES)))[0:n_rows, 0:1]
    before = _dot(onehot.astype(BF16), before_ref[...]) + run_col
    r1 = jnp.sum(jnp.where(sel1, before, 0.0), axis=0, keepdims=True)
    r2 = jnp.sum(jnp.where(sel2, before, 0.0), axis=0, keepdims=True)
    added = jnp.concatenate([jnp.sum(onehot, axis=1, keepdims=True), jnp.zeros((LANES - n_rows, 1), F32)], axis=0)
    run_scr[...] = run_scr[...] + jnp.transpose(jnp.broadcast_to(added, (LANES, LANES)))[0:1, :]
    cnt_ref[...] = run_scr[...].astype(I32)

    base = jnp.float32(N_GROUPS)
    mrow = lax.broadcasted_iota(I32, (META_ROWS, tm), 0)
    meta = jnp.where(mrow == 0, i1 - base, jnp.where(mrow == 1, i2 - base, jnp.where(mrow == 2, r1, jnp.where(mrow == 3, r2, 0.0))))
    meta_ref[0] = meta.astype(I32)
    wrow = lax.broadcasted_iota(I32, (LANES, tm), 0)
    wts_ref[...] = jnp.transpose(jnp.where(wrow == 0, w1, jnp.where(wrow == 1, w2, 0.0)))


def _post_call(cats_p, cats_s, x_p, x_s, g1g, sh2g, sc2g, w_out_b, ln1_g, ln1_b, wr, br, tri, alpha):
    n_p, d = x_p.shape
    n_s = x_s.shape[0]
    d_a = cats_p[0].shape[1]
    tm = POST_TILE
    n_pt = n_p // tm
    n_tiles = n_pt + n_s // tm
    n_tok = n_p + n_s
    groups = tm // MOD_GROUP
    kern = functools.partial(_post_kernel, n_prompt_tiles=n_pt, alpha=alpha)
    p_idx = lambda i: (jnp.minimum(i, n_pt - 1), 0)
    s_idx = lambda i: (jnp.maximum(i - n_pt, 0), 0)
    tile = lambda i: (i, 0)
    const = lambda i: (0, 0)
    once = pl.Buffered(1)
    s_mode = once if n_tiles - n_pt == 1 else None
    return pl.pallas_call(
        kern,
        grid=(n_tiles,),
        in_specs=[
            pl.BlockSpec((tm, d_a), p_idx),
            pl.BlockSpec((tm, d_a), p_idx),
            pl.BlockSpec((tm, d_a), s_idx, pipeline_mode=s_mode),
            pl.BlockSpec((tm, d_a), s_idx, pipeline_mode=s_mode),
            pl.BlockSpec((tm, d), p_idx),
            pl.BlockSpec((tm, d), s_idx, pipeline_mode=s_mode),
            pl.BlockSpec((groups, d), tile),
            pl.BlockSpec((groups, d), tile),
            pl.BlockSpec((groups, d), tile),
            pl.BlockSpec((d, d), const, pipeline_mode=once),
            pl.BlockSpec((1, d), const),
            pl.BlockSpec((1, d), const),
            pl.BlockSpec((d, LANES), const, pipeline_mode=once),
            pl.BlockSpec((1, LANES), const),
            pl.BlockSpec((tm, tm), const, pipeline_mode=once),
        ],
        out_specs=[
            pl.BlockSpec((tm, d), tile),
            pl.BlockSpec((tm, d), tile),
            pl.BlockSpec((1, META_ROWS, tm), lambda i: (i, 0, 0)),
            pl.BlockSpec((tm, LANES), tile),
            pl.BlockSpec((1, LANES), const),
        ],
        out_shape=[
            jax.ShapeDtypeStruct((n_tok, d), F32),
            jax.ShapeDtypeStruct((n_tok, d), F32),
            jax.ShapeDtypeStruct((n_tiles, META_ROWS, tm), I32),
            jax.ShapeDtypeStruct((n_tok, LANES), F32),
            jax.ShapeDtypeStruct((1, LANES), I32),
        ],
        scratch_shapes=[
            pltpu.VMEM((tm, d), F32),
            pltpu.VMEM((tm, d), BF16),
            pltpu.VMEM((tm, d), BF16),
            pltpu.VMEM((1, LANES), F32),
        ],
        compiler_params=_cparams(("arbitrary",)),
        name="post_router",
    )(*cats_p, *cats_s, x_p, x_s, g1g, sh2g, sc2g, w_out_b, ln1_g, ln1_b, wr, br, tri)


def _dispatch_kernel(slot_ref, gap_start_ref, gap_len_ref, h_ref, xs_ref, zero_scr, sem, tile_sem, *, gaps_per_step):
    i = pl.program_id(0)
    tm = h_ref.shape[0]
    n_gaps = gap_len_ref.shape[0]
    tile_rows = zero_scr.shape[0]

    @pl.when(i == 0)
    def _():
        zero_scr[...] = jnp.zeros_like(zero_scr)

    def row_copy(r, k):
        return pltpu.make_async_copy(h_ref.at[pl.ds(r, 1)], xs_ref.at[pl.ds(slot_ref[0, 0, 2 * r + k], 1)], sem)

    def start_rows(r, carry):
        row_copy(r, 0).start(priority=0)
        row_copy(r, 1).start(priority=1)
        return carry
    lax.fori_loop(0, tm, start_rows, 0, unroll=ROW_DMA_UNROLL)

    def gap_rows(g):
        return jnp.where(g < n_gaps - 1, gap_len_ref[jnp.minimum(g, n_gaps - 1)], 0)

    def gap_tiles(g):
        return jnp.where(g == n_gaps - 1, gap_len_ref[n_gaps - 1] // tile_rows, 0)

    def zero_row_copy(g, r):
        return pltpu.make_async_copy(zero_scr.at[pl.ds(0, 1)], xs_ref.at[pl.ds(gap_start_ref[g] + r, 1)], sem)

    def zero_tile_copy(g, t):
        start = pl.multiple_of(gap_start_ref[g] + t * tile_rows, tile_rows)
        return pltpu.make_async_copy(zero_scr, xs_ref.at[pl.ds(start, tile_rows)], tile_sem)

    for j in range(gaps_per_step):
        g = jnp.minimum(i * gaps_per_step + j, n_gaps - 1)
        live = i * gaps_per_step + j < n_gaps
        n_rows = jnp.where(live, gap_rows(g), 0)
        n_til = jnp.where(live, gap_tiles(g), 0)
        lax.fori_loop(0, n_rows, lambda r, c, g=g: (zero_row_copy(g, r).start(), c)[1], 0)
        lax.fori_loop(0, n_til, lambda t, c, g=g: (zero_tile_copy(g, t).start(), c)[1], 0)

    def wait_rows(r, carry):
        row_copy(r, 0).wait()
        row_copy(r, 1).wait()
        return carry
    lax.fori_loop(0, tm, wait_rows, 0, unroll=ROW_DMA_UNROLL)

    for j in range(gaps_per_step):
        g = jnp.minimum(i * gaps_per_step + j, n_gaps - 1)
        live = i * gaps_per_step + j < n_gaps
        n_rows = jnp.where(live, gap_rows(g), 0)
        n_til = jnp.where(live, gap_tiles(g), 0)
        lax.fori_loop(0, n_rows, lambda r, c, g=g: (zero_row_copy(g, r).wait(), c)[1], 0)
        lax.fori_loop(0, n_til, lambda t, c, g=g: (zero_tile_copy(g, t).wait(), c)[1], 0)


def _dispatch_call(slots3, gap_start, gap_len, h2, n_sorted):
    n_tok, d = h2.shape
    n_tiles = n_tok // TOK_TILE
    n_gaps = gap_len.shape[0]
    kern = functools.partial(_dispatch_kernel, gaps_per_step=-(-n_gaps // n_tiles))
    return pl.pallas_call(
        kern,
        grid=(n_tiles,),
        in_specs=[
            pl.BlockSpec((1, 1, slots3.shape[2]), lambda i: (i, 0, 0), memory_space=pltpu.SMEM),
            pl.BlockSpec((n_gaps,), lambda i: (0,), memory_space=pltpu.SMEM),
            pl.BlockSpec((n_gaps,), lambda i: (0,), memory_space=pltpu.SMEM),
            pl.BlockSpec((TOK_TILE, d), lambda i: (i, 0)),
        ],
        out_specs=pl.BlockSpec(memory_space=pl.ANY),
        out_shape=jax.ShapeDtypeStruct((n_sorted, d), F32),
        scratch_shapes=[pltpu.VMEM((EXP_TILE, d), F32), pltpu.SemaphoreType.DMA(()), pltpu.SemaphoreType.DMA(())],
        compiler_params=_cparams(("arbitrary",)),
        name="dispatch",
    )(slots3, gap_start, gap_len, h2)


def _expert_kernel(te_ref, nt_ref, nxt_ref, run_ref, xs_ref, w1_ref, w3_ref, w2_ref, ys_ref,
                   x_scr, w1_scr, w3_scr, w2f_scr, w13_scr, w2_scr, xsem, wsem):
    t = pl.program_id(0)
    n_t = pl.num_programs(0)
    rows = x_scr.shape[1]
    d_exp = w1_scr.shape[2]
    x_slots = x_scr.shape[0]
    xs = lax.rem(t, x_slots)
    e = te_ref[t]
    prev = te_ref[jnp.maximum(t - 1, 0)]
    changed = (t == 0) | (e != prev)
    valid = t < nt_ref[0]
    wslot = lax.rem(run_ref[t], 2)

    def x_copy(tile, slot):
        start = pl.multiple_of(tile * rows, rows)
        return pltpu.make_async_copy(xs_ref.at[pl.ds(start, rows)], x_scr.at[slot], xsem.at[slot])

    def w_copies(expert, slot):
        return (pltpu.make_async_copy(w1_ref.at[expert], w1_scr.at[slot], wsem.at[slot]),
                pltpu.make_async_copy(w3_ref.at[expert], w3_scr.at[slot], wsem.at[slot]),
                pltpu.make_async_copy(w2_ref.at[expert], w2f_scr.at[slot], wsem.at[slot]))

    @pl.when(t == 0)
    def _():
        for ahead in range(X_AHEAD):
            x_copy(ahead, ahead).start()
        for c in w_copies(e, wslot):
            c.start()

    @pl.when(t + X_AHEAD < n_t)
    def _():
        x_copy(t + X_AHEAD, lax.rem(t + X_AHEAD, x_slots)).start()

    @pl.when(changed)
    def _():
        for c in w_copies(e, wslot):
            c.wait()
        nxt = nxt_ref[t]

        @pl.when(nxt != e)
        def _():
            for c in w_copies(nxt, 1 - wslot):
                c.start()
        w13_scr[:, 0:d_exp] = w1_scr[wslot].astype(BF16)
        w13_scr[:, d_exp:2 * d_exp] = w3_scr[wslot].astype(BF16)
        w2_scr[...] = w2f_scr[wslot].astype(BF16)

    x_copy(t, xs).wait()

    @pl.when(valid)
    def _():
        h13 = _dot(x_scr[xs].astype(BF16), w13_scr[...])
        hm = jax.nn.silu(h13[:, 0:d_exp]) * h13[:, d_exp:2 * d_exp]
        _store_rows(ys_ref, (), 0, _dot(hm.astype(BF16), w2_scr[...]))

    @pl.when(jnp.logical_not(valid))
    def _():
        ys_ref[...] = jnp.zeros_like(ys_ref)


def _expert_call(tile_expert, n_valid, next_expert, run_id, xs, w1, w3, w2):
    n_sorted, d = xs.shape
    n_exp, _, d_exp = w1.shape
    n_tiles = n_sorted // EXP_TILE
    assert n_tiles > X_AHEAD
    grid_spec = pltpu.PrefetchScalarGridSpec(
        num_scalar_prefetch=4,
        grid=(n_tiles,),
        in_specs=[pl.BlockSpec(memory_space=pl.ANY)] * 4,
        out_specs=pl.BlockSpec((EXP_TILE * _lines_per_row(d), LANES), lambda t, te, nt, nx, rn: (t, 0)),
        scratch_shapes=[
            pltpu.VMEM((X_AHEAD + 1, EXP_TILE, d), F32),
            pltpu.VMEM((2, d, d_exp), F32),
            pltpu.VMEM((2, d, d_exp), F32),
            pltpu.VMEM((2, d_exp, d), F32),
            pltpu.VMEM((d, 2 * d_exp), BF16),
            pltpu.VMEM((d_exp, d), BF16),
            pltpu.SemaphoreType.DMA((X_AHEAD + 1,)),
            pltpu.SemaphoreType.DMA((2,)),
        ],
    )
    return pl.pallas_call(
        _expert_kernel,
        grid_spec=grid_spec,
        out_shape=jax.ShapeDtypeStruct((n_sorted * _lines_per_row(d), LANES), F32),
        compiler_params=_cparams(("arbitrary",)),
        name="experts",
    )(tile_expert, n_valid, next_expert, run_id, xs, w1, w3, w2)


def _combine_kernel(scur_ref, snext_ref, x1_ref, wts_ref, g2_ref, l2g_ref, l2b_ref, ys_ref, outp_ref, outs_ref,
                    y_scr, sem, *, n_prompt_tiles, alpha):
    i = pl.program_id(0)
    n_i = pl.num_programs(0)
    is_s = i >= n_prompt_tiles
    tm, d = x1_ref.shape
    lpr = _lines_per_row(d)
    cur = lax.rem(i, 2)

    def row_lines(r):
        return pl.ds(pl.multiple_of(r * lpr, lpr), lpr)

    def start_gathers(s_ref, buf):
        def body(r, carry):
            for k in range(2):
                pltpu.make_async_copy(ys_ref.at[row_lines(s_ref[0, 0, 2 * r + k])], y_scr.at[buf, k, row_lines(r)],
                                      sem.at[buf, k]).start(priority=k)
            return carry
        lax.fori_loop(0, tm, body, 0, unroll=ROW_DMA_UNROLL // 2)

    @pl.when(i == 0)
    def _():
        start_gathers(scur_ref, 0)

    @pl.when(i + 1 < n_i)
    def _():
        start_gathers(snext_ref, 1 - cur)

    for k in range(2):
        pltpu.make_async_copy(y_scr.at[cur, k], y_scr.at[cur, k], sem.at[cur, k]).wait()

    def slab(g, carry, out_ref):
        rows = pl.ds(pl.multiple_of(g * MOD_GROUP, MOD_GROUP), MOD_GROUP)
        w = wts_ref[rows, :]
        moe = (w[:, 0:1] * _load_rows(y_scr, (cur, 0), g * MOD_GROUP, MOD_GROUP, d)
               + w[:, 1:2] * _load_rows(y_scr, (cur, 1), g * MOD_GROUP, MOD_GROUP, d))
        z = alpha * x1_ref[rows, :] + g2_ref[pl.ds(g, 1), :] * moe
        out_ref[rows, :] = _ln_plain(z) * l2g_ref[...] + l2b_ref[...]
        return carry

    @pl.when(is_s)
    def _():
        lax.fori_loop(0, tm // MOD_GROUP, functools.partial(slab, out_ref=outs_ref), 0, unroll=SLAB_UNROLL)

    @pl.when(jnp.logical_not(is_s))
    def _():
        lax.fori_loop(0, tm // MOD_GROUP, functools.partial(slab, out_ref=outp_ref), 0, unroll=SLAB_UNROLL)


def _combine_call(slots3, x1, wts, g2g, ln2_g, ln2_b, ys, n_p, alpha):
    n_tok, d = x1.shape
    n_tiles = n_tok // TOK_TILE
    n_pt = n_p // TOK_TILE
    groups = TOK_TILE // MOD_GROUP
    kern = functools.partial(_combine_kernel, n_prompt_tiles=n_pt, alpha=alpha)
    slot_tile = lambda ahead: pl.BlockSpec(
        (1, 1, slots3.shape[2]), lambda i: (jnp.minimum(i + ahead, n_tiles - 1), 0, 0), memory_space=pltpu.SMEM)
    return pl.pallas_call(
        kern,
        grid=(n_tiles,),
        in_specs=[
            slot_tile(0), slot_tile(1),
            pl.BlockSpec((TOK_TILE, d), lambda i: (i, 0)),
            pl.BlockSpec((TOK_TILE, LANES), lambda i: (i, 0)),
            pl.BlockSpec((groups, d), lambda i: (i, 0)),
            pl.BlockSpec((1, d), lambda i: (0, 0)),
            pl.BlockSpec((1, d), lambda i: (0, 0)),
            pl.BlockSpec(memory_space=pl.ANY),
        ],
        out_specs=[
            pl.BlockSpec((TOK_TILE, d), lambda i: (jnp.minimum(i, n_pt - 1), 0)),
            pl.BlockSpec((TOK_TILE, d), lambda i: (jnp.maximum(i - n_pt, 0), 0)),
        ],
        out_shape=[
            jax.ShapeDtypeStruct((n_p, d), F32),
            jax.ShapeDtypeStruct((n_tok - n_p, d), F32),
        ],
        scratch_shapes=[
            pltpu.VMEM((2, 2, TOK_TILE * _lines_per_row(d), LANES), F32),
            pltpu.SemaphoreType.DMA((2, 2)),
        ],
        compiler_params=_cparams(("arbitrary",)),
        name="combine",
    )(slots3, slots3, x1, wts, g2g, ln2_g, ln2_b, ys)


def _routing_tables(meta, cnt_row, n_sorted):
    n_tok = meta.shape[0] * meta.shape[2]
    experts = jnp.transpose(meta[:, 0:2, :], (0, 2, 1)).reshape(n_tok, 2)
    ranks = jnp.transpose(meta[:, 2:4, :], (0, 2, 1)).reshape(n_tok, 2)
    cnt = cnt_row[0, N_GROUPS:N_GROUPS + N_EXPERTS]
    padded = ((cnt + EXP_TILE - 1) // EXP_TILE) * EXP_TILE
    ends = jnp.cumsum(padded)
    offs = ends - padded
    slots = offs[experts] + ranks
    total = ends[-1]
    n_tiles = n_sorted // EXP_TILE
    tile_ids = jnp.arange(n_tiles, dtype=I32)
    tile_expert = jnp.minimum(
        jnp.sum((tile_ids[:, None] >= (ends // EXP_TILE)[None, :]).astype(I32), axis=1), N_EXPERTS - 1)
    n_valid = (total // EXP_TILE).reshape(1).astype(I32)
    ids = jnp.arange(N_EXPERTS, dtype=I32)
    later = (ids[None, :] > tile_expert[:, None]) & (padded[None, :] > 0)
    next_expert = jnp.min(jnp.where(later, ids[None, :], N_EXPERTS), axis=1)
    next_expert = jnp.where(next_expert == N_EXPERTS, tile_expert, next_expert).astype(I32)
    starts_run = jnp.concatenate([jnp.zeros((1,), I32), (tile_expert[1:] != tile_expert[:-1]).astype(I32)])
    run_id = jnp.cumsum(starts_run).astype(I32)
    gap_start = jnp.concatenate([offs + cnt, total.reshape(1)]).astype(I32)
    gap_len = jnp.concatenate([padded - cnt, (n_sorted - total).reshape(1)]).astype(I32)
    return slots.astype(I32), (tile_expert, n_valid, next_expert, run_id), gap_start, gap_len


def _layer(layer, n_layers, xp, xs, s0_l, c_all, p):
    (w_ada, b_ada, w_in, ln_v_g, ln_v_b, w_s, b_s, hgrn_lb, gnorm_g, w_out, ln1_g, ln1_b,
     w_rg, b_rg, w_re, b_re, w1, w3, w2, ln2_g, ln2_b) = p
    bsz, t, d = xp.shape
    n_streams, seq, _ = xs.shape
    alpha = float((2.0 * n_layers) ** 0.25)
    n_p = bsz * t
    n_s = n_streams * seq
    n_tok = n_p + n_s

    n_c = c_all.shape[0]
    c_pad = jnp.pad(c_all, ((0, (-n_c) % 8), (0, 0)))
    mod = _ada_call(c_pad, w_ada, b_ada)[:n_c]
    sh1, sc1, g1, sh2, sc2, g2 = [mod[:, m * d:(m + 1) * d] for m in range(6)]

    w_in_b = w_in.astype(BF16)
    w_out_b = w_out.astype(BF16)
    lvg = ln_v_g.reshape(N_HEADS, 1, HEAD_DIM)
    lvb = ln_v_b.reshape(N_HEADS, 1, HEAD_DIM)
    b_s3 = b_s.reshape(N_HEADS, GMLP_CHUNK, 1)
    lb_h = jnp.transpose(hgrn_lb.reshape(hgrn_lb.shape[0], N_HEADS, HEAD_DIM), (1, 0, 2))
    gn = gnorm_g.reshape(N_HEADS, 1, HEAD_DIM)

    ca_p, co_p, state_p = _mixer_prompt_call(xp, sh1[:bsz], sc1[:bsz], w_in_b, lvg, lvb, w_s, b_s3, lb_h, gn, layer)
    ca_s, co_s, state_s, vrows = _mixer_sample_call(xs.reshape(n_s, d), sh1[bsz:], sc1[bsz:], w_in_b, lvg, lvb, w_s,
                                               b_s3, lb_h, gn, s0_l, layer, seq)

    group_stream = jnp.concatenate([
        jnp.repeat(jnp.arange(bsz, dtype=I32), t // MOD_GROUP),
        bsz + jnp.repeat(jnp.arange(n_streams, dtype=I32), seq // MOD_GROUP)])
    g1g, sh2g, sc2g, g2g = [m[group_stream] for m in (g1, sh2, sc2, g2)]

    wr = jnp.concatenate([w_rg, w_re], axis=1)
    wr_hi = wr.astype(BF16)
    wr_lo = (wr - wr_hi.astype(F32)).astype(BF16)
    n_r = wr.shape[1]
    wr_cat = jnp.concatenate([wr_hi, wr_lo, jnp.zeros((d, LANES - 2 * n_r), BF16)], axis=1)
    br = jnp.pad(jnp.concatenate([b_rg, b_re]), (0, LANES - n_r)).reshape(1, LANES)
    tri = jnp.triu(jnp.ones((POST_TILE, POST_TILE), F32), 1).astype(BF16)

    x1, h2, meta, wts, cnt_row = _post_call((ca_p, co_p), (ca_s, co_s), xp.reshape(n_p, d), xs.reshape(n_s, d),
                                            g1g, sh2g, sc2g, w_out_b, ln1_g.reshape(1, d), ln1_b.reshape(1, d),
                                            wr_cat, br, tri, alpha)

    n_sorted = 2 * n_tok + N_EXPERTS * EXP_TILE
    slots, tile_tables, gap_start, gap_len = _routing_tables(meta, cnt_row, n_sorted)
    n_tiles = n_tok // TOK_TILE
    slots3 = slots.reshape(n_tiles, 1, 2 * TOK_TILE)

    xs_sorted = _dispatch_call(slots3, gap_start, gap_len, h2, n_sorted)
    ys_sorted = _expert_call(*tile_tables, xs_sorted, w1, w3, w2)
    yp, ys_out = _combine_call(slots3, x1, wts, g2g, ln2_g.reshape(1, d), ln2_b.reshape(1, d), ys_sorted, n_p, alpha)

    v_rows = vrows.reshape(n_streams, seq, N_HEADS, HEAD_DIM)
    return yp.reshape(bsz, t, d), ys_out.reshape(n_streams, seq, d), state_p, state_s, v_rows


def kernel(x_prompt, x_sample, state_hgrn, c_prompt, c_sample, w_ada, b_ada, w_in, ln_v_g, ln_v_b, w_s, b_s, hgrn_lb, gnorm_g, w_out, ln1_g, ln1_b, w_router_g, b_router_g, w_router_e, b_router_e, w1, w3, w2, ln2_g, ln2_b):
    n_layers = w_ada.shape[0]
    assert x_prompt.shape[1] % MIX_TILE == 0 and x_prompt.shape[2] == 2 * N_HEADS * HEAD_DIM
    assert x_sample.shape[1] % MOD_GROUP == 0 and x_sample.shape[1] <= SUB_CHUNK
    assert (x_sample.shape[0] * x_sample.shape[1]) % TOK_TILE == 0 and TOK_TILE % POST_TILE == 0
    c_all = jnp.concatenate([c_prompt, c_sample], axis=0)
    xp, xs = x_prompt, x_sample
    sp_list, ss_list, vs_list = [], [], []
    for l in range(n_layers):
        p = (w_ada[l], b_ada[l], w_in[l], ln_v_g[l], ln_v_b[l], w_s[l], b_s[l], hgrn_lb, gnorm_g[l], w_out[l],
             ln1_g[l], ln1_b[l], w_router_g[l], b_router_g[l], w_router_e[l], b_router_e[l],
             w1[l], w3[l], w2[l], ln2_g[l], ln2_b[l])
        xp, xs, sp, ss, vs = _layer(l, n_layers, xp, xs, state_hgrn[l], c_all, p)
        sp_list.append(sp.astype(state_hgrn.dtype))
        ss_list.append(ss.astype(state_hgrn.dtype))
        vs_list.append(vs)
    return (xp, xs, jnp.stack(sp_list, axis=0), jnp.stack(ss_list, axis=0), jnp.stack(vs_list, axis=0))
```

```python
import functools

import jax
import jax.numpy as jnp
from jax import lax
from jax.experimental import pallas as pl
from jax.experimental.pallas import tpu as pltpu

F32 = jnp.float32
BF16 = jnp.bfloat16
I32 = jnp.int32

N_HEADS = 8
HEAD_DIM = 128
GMLP_CHUNK = 128
SUB_CHUNK = 64
N_GROUPS = 4
EXPERTS_PER_GROUP = 8
N_EXPERTS = N_GROUPS * EXPERTS_PER_GROUP
LN_EPS = 1e-5
N_KINDS = 6

LANES = 128
MIX_TILE = 512
HEADS_PER_STEP = 2
HGRN_BLOCK = 64
DIAG_BLOCK = 16
SAFE_BLOCK_DECAY = 60.0
TOK_TILE = 512
POST_TILE = 512
META_ROWS = 8
MOD_GROUP = 32
SLAB_UNROLL = 4
EXP_TILE = 256
X_AHEAD = 2
ROW_DMA_UNROLL = 8
ADA_TILE = 1024
VMEM_LIMIT = 56 * 1024 * 1024


def _cparams(sem):
    return pltpu.CompilerParams(dimension_semantics=sem, vmem_limit_bytes=VMEM_LIMIT)


def _ln_plain(x):
    mu = jnp.mean(x, axis=-1, keepdims=True)
    xc = x - mu
    var = jnp.mean(xc * xc, axis=-1, keepdims=True)
    return xc * lax.rsqrt(var + LN_EPS)


def _dot(a, b):
    return jnp.dot(a, b, preferred_element_type=F32)


def _dot_nt(a, b):
    return lax.dot_general(a, b, (((1,), (1,)), ((), ())), preferred_element_type=F32)


def _lines_per_row(d):
    return d // LANES


def _load_rows(ref, lead, row0, n_rows, d):
    lpr = _lines_per_row(d)
    parts = [ref[lead + (pl.ds(row0 * lpr + c, n_rows, stride=lpr), slice(None))] for c in range(lpr)]
    return jnp.concatenate(parts, axis=1)


def _store_rows(ref, lead, row0, val):
    n_rows, d = val.shape
    lpr = _lines_per_row(d)
    for c in range(lpr):
        ref[lead + (pl.ds(row0 * lpr + c, n_rows, stride=lpr), slice(None))] = val[:, c * LANES:(c + 1) * LANES]


def _ada_kernel(c_ref, w_ref, b_ref, o_ref):
    s = jax.nn.silu(c_ref[...]).astype(BF16)
    o_ref[...] = _dot(s, w_ref[...].astype(BF16)) + b_ref[...]


def _ada_call(c_pad, w_ada, b_ada):
    rows, d = c_pad.shape
    n_out = w_ada.shape[1]
    return pl.pallas_call(
        _ada_kernel,
        grid=(n_out // ADA_TILE,),
        in_specs=[
            pl.BlockSpec((rows, d), lambda n: (0, 0)),
            pl.BlockSpec((d, ADA_TILE), lambda n: (0, n)),
            pl.BlockSpec((1, ADA_TILE), lambda n: (0, n)),
        ],
        out_specs=pl.BlockSpec((rows, ADA_TILE), lambda n: (0, n)),
        out_shape=jax.ShapeDtypeStruct((rows, n_out), F32),
        compiler_params=_cparams(("arbitrary",)),
        name="adaln",
    )(c_pad, w_ada, b_ada.reshape(1, n_out))


def _lower_bound(lb_ref, hd, layer):
    raw = lb_ref[hd]
    m = jnp.max(raw, axis=0, keepdims=True)
    e = jnp.exp(raw - m)
    p = e / jnp.sum(e, axis=0, keepdims=True)
    return jnp.sum(p[: layer + 1], axis=0, keepdims=True)


def _gmlp_rows(u_pre, v_pre, ln_g, ln_b):
    u = jax.nn.gelu(u_pre)
    v = jax.nn.gelu(v_pre)
    mu = jnp.mean(v, axis=-1, keepdims=True)
    vc = v - mu
    var = jnp.mean(vc * vc, axis=-1, keepdims=True)
    vn = vc * lax.rsqrt(var + LN_EPS) * ln_g + ln_b
    return u, vn


def _row_bcast(a, row, n):
    return jnp.broadcast_to(a[row:row + 1, :], (n, a.shape[1]))


def _block_id(idx, size):
    return lax.shift_right_logical(idx, I32(size.bit_length() - 1))


def _hgrn_masks(c):
    row = lax.broadcasted_iota(I32, (c, c), 0)
    col = lax.broadcasted_iota(I32, (c, c), 1)
    masks = []
    half = c // 2
    while half >= DIAG_BLOCK:
        span = 2 * half
        same = _block_id(row, span) == _block_id(col, span)
        masks.append(same & ((row & (span - 1)) >= half) & ((col & (span - 1)) < half))
        half //= 2
    diag = (_block_id(row, DIAG_BLOCK) == _block_id(col, DIAG_BLOCK)) & (col <= row)
    return masks, diag


def _tri_ones(c):
    row = lax.broadcasted_iota(I32, (c, c), 0)
    col = lax.broadcasted_iota(I32, (c, c), 1)
    return jnp.where(col <= row, 1.0, 0.0).astype(BF16)


def _level_halves(c):
    halves = []
    half = c // 2
    while half >= 1:
        halves.append(half)
        half //= 2
    return halves


def _level_masks(c):
    row = lax.broadcasted_iota(I32, (c, c), 0)
    col = lax.broadcasted_iota(I32, (c, c), 1)
    masks = []
    for half in _level_halves(c):
        span = 2 * half
        same = _block_id(row, span) == _block_id(col, span)
        masks.append(same & ((row & (span - 1)) >= half) & ((col & (span - 1)) < half))
    return masks, row == col


def _level_ref_sums(c):
    row = lax.broadcasted_iota(I32, (c, c), 0)
    col = lax.broadcasted_iota(I32, (c, c), 1)
    pieces = [col <= _block_id(row, 2 * half) * (2 * half) + half for half in _level_halves(c)]
    return jnp.concatenate([jnp.where(p, 1.0, 0.0).astype(BF16) for p in pieces], axis=0)


def _hgrn_block_decay(units, c):
    worst = None
    for u in units:
        a = u["a"]
        for b in range(c // DIAG_BLOCK):
            span = a[b * DIAG_BLOCK:b * DIAG_BLOCK + 1, :] - a[(b + 1) * DIAG_BLOCK - 1:(b + 1) * DIAG_BLOCK, :]
            worst = span if worst is None else jnp.maximum(worst, span)
    return jnp.max(worst)


def _hgrn_prepare(chains, c):
    tri = _tri_ones(c)
    units = []
    for st0, lbv, gn, blocks in chains:
        for q_pre, f_pre, i_pre, g_pre in blocks:
            q = jax.nn.silu(q_pre)
            fg = lbv + (1.0 - lbv) * jax.nn.sigmoid(f_pre)
            logf = jnp.log(fg)
            hi = logf.astype(BF16)
            lo = (logf - hi.astype(F32)).astype(BF16)
            units.append(dict(q=q, k=1.0 - fg, hilo=jnp.concatenate([hi, lo], axis=1),
                              v=i_pre, g=g_pre, gn=gn))

    for u in units:
        p = _dot(tri, u["hilo"])
        u["a"] = p[:, 0:HEAD_DIM] + p[:, HEAD_DIM:2 * HEAD_DIM]
    return units


def _hgrn_scores(units, c, exact):
    units = [dict(u) for u in units]
    if exact:
        masks, eye = _level_masks(c)
        sums = _level_ref_sums(c)
    else:
        masks, diag_mask = _hgrn_masks(c)
    for u in units:
        q, k, a = u["q"], u["k"], u["a"]
        parts = []
        if exact:
            p = _dot(sums, u["hilo"])
            p = p[:, 0:HEAD_DIM] + p[:, HEAD_DIM:2 * HEAD_DIM]
            for n, mask in enumerate(masks):
                d = a - p[n * c:(n + 1) * c]
                e = jnp.exp(jnp.minimum(d, -d))
                parts.append((mask, _dot_nt((q * e).astype(BF16), (k * e).astype(BF16))))
            parts.append((eye, jnp.sum(q * k, axis=-1, keepdims=True)))
        else:
            half = c // 2
            level = 0
            while half >= DIAG_BLOCK:
                span = 2 * half
                ref = jnp.concatenate([_row_bcast(a, b * span + half, span) for b in range(c // span)], axis=0)
                qs = (q * jnp.exp(jnp.minimum(a - ref, 0.0))).astype(BF16)
                ks = (k * jnp.exp(jnp.minimum(ref - a, 0.0))).astype(BF16)
                parts.append((masks[level], _dot_nt(qs, ks)))
                half //= 2
                level += 1
            ref = jnp.concatenate([_row_bcast(a, b * DIAG_BLOCK, DIAG_BLOCK) for b in range(c // DIAG_BLOCK)], axis=0)
            qd = (q * jnp.exp(a - ref)).astype(BF16)
            kd = (k * jnp.exp(jnp.minimum(ref - a, SAFE_BLOCK_DECAY))).astype(BF16)
            parts.append((diag_mask, _dot_nt(qd, kd)))
        u["parts"] = parts
        a_last = a[c - 1:c, :]
        u["decay"] = jnp.exp(a_last)
        kl = (k * jnp.exp(a_last - a)).astype(BF16)
        v_t = jnp.transpose(u["v"]).astype(BF16)
        u["upd"] = _dot(v_t, kl)
        u["qe"] = (q * jnp.exp(a)).astype(BF16)
    return units


def _hgrn_finish(chains, units):
    finals = []
    n = 0
    for st0, lbv, gn, blocks in chains:
        st = st0
        for _ in blocks:
            units[n]["st_in"] = st.astype(BF16)
            st = st * units[n]["decay"] + units[n]["upd"]
            n += 1
        finals.append(st)

    outs = []
    n = 0
    for st0, lbv, gn, blocks in chains:
        chain_out = []
        for _ in blocks:
            u = units[n]
            scores = None
            for mask, part in u["parts"]:
                part = jnp.where(mask, part, 0.0)
                scores = part if scores is None else scores + part
            o = _dot(scores.astype(BF16), u["v"].astype(BF16)) + _dot_nt(u["qe"], u["st_in"])
            o = o * lax.rsqrt(jnp.mean(o * o, axis=-1, keepdims=True) + LN_EPS) * u["gn"]
            chain_out.append(o * jax.nn.silu(u["g"]))
            n += 1
        outs.append(chain_out)
    return outs, finals


def _proj_cols(kind, hd):
    c0 = (kind * HEADS_PER_STEP + hd) * HEAD_DIM
    return slice(c0, c0 + HEAD_DIM)


def _in_projection(h_scr, w_refs, proj_scr):
    width = HEADS_PER_STEP * HEAD_DIM
    for kind, w_ref in enumerate(w_refs):
        proj_scr[:, kind * width:(kind + 1) * width] = _dot(h_scr[...], w_ref[...])


def _mixer_prompt_kernel(x_ref, sh_ref, sc_ref, wu_ref, wv_ref, wq_ref, wf_ref, wi_ref, wg_ref,
                         lvg_ref, lvb_ref, ws_ref, bs_ref, lb_ref, gn_ref,
                         cata_ref, cato_ref, state_ref, h_scr, proj_scr, st_scr, *, layer):
    i = pl.program_id(1)
    j = pl.program_id(2)
    tm = h_scr.shape[0]

    @pl.when(j == 0)
    def _():
        def slab(r, carry):
            rows = pl.ds(pl.multiple_of(r * HGRN_BLOCK, HGRN_BLOCK), HGRN_BLOCK)
            h = _ln_plain(x_ref[0, rows, :]) * (1.0 + sc_ref[0]) + sh_ref[0]
            h_scr[rows, :] = h.astype(BF16)
            return carry
        lax.fori_loop(0, tm // HGRN_BLOCK, slab, 0, unroll=2)

    @pl.when((i == 0) & (j == 0))
    def _():
        st_scr[...] = jnp.zeros_like(st_scr)

    _in_projection(h_scr, (wu_ref, wv_ref, wq_ref, wf_ref, wi_ref, wg_ref), proj_scr)

    row = lax.broadcasted_iota(I32, (GMLP_CHUNK, GMLP_CHUNK), 0)
    col = lax.broadcasted_iota(I32, (GMLP_CHUNK, GMLP_CHUNK), 1)
    causal = _block_id(row, SUB_CHUNK) >= _block_id(col, SUB_CHUNK)
    for hd in range(HEADS_PER_STEP):
        wm = jnp.where(causal, ws_ref[hd], 0.0).astype(BF16)
        b_col = bs_ref[hd]
        for c in range(tm // GMLP_CHUNK):
            rows = slice(c * GMLP_CHUNK, (c + 1) * GMLP_CHUNK)
            u, vn = _gmlp_rows(proj_scr[rows, _proj_cols(0, hd)], proj_scr[rows, _proj_cols(1, hd)],
                               lvg_ref[hd], lvb_ref[hd])
            sp = _dot(wm, vn.astype(BF16)) + b_col
            cata_ref[rows, hd * HEAD_DIM:(hd + 1) * HEAD_DIM] = (u * sp).astype(BF16)

    chains = []
    for hd in range(HEADS_PER_STEP):
        blocks = [tuple(proj_scr[c * HGRN_BLOCK:(c + 1) * HGRN_BLOCK, _proj_cols(kind, hd)] for kind in range(2, N_KINDS))
                  for c in range(tm // HGRN_BLOCK)]
        chains.append((st_scr[HEADS_PER_STEP * j + hd], _lower_bound(lb_ref, hd, layer), gn_ref[hd], blocks))
    units = _hgrn_prepare(chains, HGRN_BLOCK)
    wide_decay = _hgrn_block_decay(units, HGRN_BLOCK) > SAFE_BLOCK_DECAY

    def finish(exact):
        outs, finals = _hgrn_finish(chains, _hgrn_scores(units, HGRN_BLOCK, exact))
        for hd in range(HEADS_PER_STEP):
            for c, out in enumerate(outs[hd]):
                cato_ref[c * HGRN_BLOCK:(c + 1) * HGRN_BLOCK, hd * HEAD_DIM:(hd + 1) * HEAD_DIM] = out.astype(BF16)
            st_scr[HEADS_PER_STEP * j + hd] = finals[hd]
            state_ref[0, HEADS_PER_STEP * j + hd] = jnp.transpose(finals[hd])

    @pl.when(wide_decay)
    def _():
        finish(True)

    @pl.when(jnp.logical_not(wide_decay))
    def _():
        finish(False)


def _w_in_specs(d):
    width = HEADS_PER_STEP * HEAD_DIM
    blocks_per_kind = N_HEADS // HEADS_PER_STEP
    return [pl.BlockSpec((d, width), lambda j, kind=kind: (0, kind * blocks_per_kind + j)) for kind in range(N_KINDS)]


def _mixer_prompt_call(x, sh1, sc1, w_in_b, ln_v_g, ln_v_b, w_s, b_s, lb_h, gnorm_g, layer):
    bsz, t, d = x.shape
    n_t = t // MIX_TILE
    hp = HEADS_PER_STEP
    kern = functools.partial(_mixer_prompt_kernel, layer=layer)
    head_vec = lambda b, i, j: (j, 0, 0)
    return pl.pallas_call(
        kern,
        grid=(bsz, n_t, N_HEADS // hp),
        in_specs=[
            pl.BlockSpec((1, MIX_TILE, d), lambda b, i, j: (b, i, 0)),
            pl.BlockSpec((1, 1, d), lambda b, i, j: (b, 0, 0)),
            pl.BlockSpec((1, 1, d), lambda b, i, j: (b, 0, 0)),
            *[pl.BlockSpec((d, hp * HEAD_DIM), lambda b, i, j, kind=kind: (0, kind * (N_HEADS // hp) + j))
              for kind in range(N_KINDS)],
            pl.BlockSpec((hp, 1, HEAD_DIM), head_vec),
            pl.BlockSpec((hp, 1, HEAD_DIM), head_vec),
            pl.BlockSpec((hp, GMLP_CHUNK, GMLP_CHUNK), head_vec),
            pl.BlockSpec((hp, GMLP_CHUNK, 1), head_vec),
            pl.BlockSpec((hp, lb_h.shape[1], HEAD_DIM), head_vec),
            pl.BlockSpec((hp, 1, HEAD_DIM), head_vec),
        ],
        out_specs=[
            pl.BlockSpec((MIX_TILE, hp * HEAD_DIM), lambda b, i, j: (b * n_t + i, j)),
            pl.BlockSpec((MIX_TILE, hp * HEAD_DIM), lambda b, i, j: (b * n_t + i, j)),
            pl.BlockSpec((1, N_HEADS, HEAD_DIM, HEAD_DIM), lambda b, i, j: (b, 0, 0, 0)),
        ],
        out_shape=[
            jax.ShapeDtypeStruct((bsz * t, N_HEADS * HEAD_DIM), BF16),
            jax.ShapeDtypeStruct((bsz * t, N_HEADS * HEAD_DIM), BF16),
            jax.ShapeDtypeStruct((bsz, N_HEADS, HEAD_DIM, HEAD_DIM), F32),
        ],
        scratch_shapes=[
            pltpu.VMEM((MIX_TILE, d), BF16),
            pltpu.VMEM((MIX_TILE, N_KINDS * hp * HEAD_DIM), F32),
            pltpu.VMEM((N_HEADS, HEAD_DIM, HEAD_DIM), F32),
        ],
        compiler_params=_cparams(("arbitrary", "arbitrary", "arbitrary")),
        name="mixer_prompt",
    )(x, sh1.reshape(bsz, 1, d), sc1.reshape(bsz, 1, d), *([w_in_b] * N_KINDS), ln_v_g, ln_v_b, w_s, b_s, lb_h, gnorm_g)


def _mixer_sample_kernel(x_ref, sh_ref, sc_ref, wu_ref, wv_ref, wq_ref, wf_ref, wi_ref, wg_ref,
                         lvg_ref, lvb_ref, ws_ref, bs_ref, lb_ref, gn_ref, s0_ref,
                         cata_ref, cato_ref, state_ref, vrows_ref, h_scr, proj_scr, *, layer, seq):
    j = pl.program_id(0)
    n_streams = x_ref.shape[0] // seq

    @pl.when(j == 0)
    def _():
        def slab(r, carry):
            rows = pl.ds(pl.multiple_of(r * seq, seq), seq)
            h = _ln_plain(x_ref[rows, :]) * (1.0 + sc_ref[pl.ds(r, 1), :]) + sh_ref[pl.ds(r, 1), :]
            h_scr[rows, :] = h.astype(BF16)
            return carry
        lax.fori_loop(0, n_streams, slab, 0)

    _in_projection(h_scr, (wu_ref, wv_ref, wq_ref, wf_ref, wi_ref, wg_ref), proj_scr)

    chains = []
    for hd in range(HEADS_PER_STEP):
        wm = ws_ref[hd][0:seq, 0:seq].astype(BF16)
        b_col = bs_ref[hd][0:seq, :]
        lbv = _lower_bound(lb_ref, hd, layer)
        for n in range(n_streams):
            rows = slice(n * seq, (n + 1) * seq)
            u, vn = _gmlp_rows(proj_scr[rows, _proj_cols(0, hd)], proj_scr[rows, _proj_cols(1, hd)],
                               lvg_ref[hd], lvb_ref[hd])
            vrows_ref[rows, hd * HEAD_DIM:(hd + 1) * HEAD_DIM] = vn
            sp = _dot(wm, vn.astype(BF16)) + b_col
            cata_ref[rows, hd * HEAD_DIM:(hd + 1) * HEAD_DIM] = (u * sp).astype(BF16)
            chains.append((jnp.transpose(s0_ref[n, hd]), lbv, gn_ref[hd],
                           [tuple(proj_scr[rows, _proj_cols(kind, hd)] for kind in range(2, N_KINDS))]))

    units = _hgrn_prepare(chains, seq)
    wide_decay = _hgrn_block_decay(units, seq) > SAFE_BLOCK_DECAY

    def finish(exact):
        outs, finals = _hgrn_finish(chains, _hgrn_scores(units, seq, exact))
        for hd in range(HEADS_PER_STEP):
            for n in range(n_streams):
                r0 = n * seq
                m = hd * n_streams + n
                cato_ref[r0:r0 + seq, hd * HEAD_DIM:(hd + 1) * HEAD_DIM] = outs[m][0].astype(BF16)
                state_ref[n, hd] = jnp.transpose(finals[m])

    @pl.when(wide_decay)
    def _():
        finish(True)

    @pl.when(jnp.logical_not(wide_decay))
    def _():
        finish(False)


def _mixer_sample_call(x2d, sh1, sc1, w_in_b, ln_v_g, ln_v_b, w_s, b_s, lb_h, gnorm_g, s0, layer, seq):
    rows, d = x2d.shape
    n_streams = rows // seq
    hp = HEADS_PER_STEP
    kern = functools.partial(_mixer_sample_kernel, layer=layer, seq=seq)
    head_vec = lambda j: (j, 0, 0)
    return pl.pallas_call(
        kern,
        grid=(N_HEADS // hp,),
        in_specs=[
            pl.BlockSpec((rows, d), lambda j: (0, 0)),
            pl.BlockSpec((n_streams, d), lambda j: (0, 0)),
            pl.BlockSpec((n_streams, d), lambda j: (0, 0)),
            *_w_in_specs(d),
            pl.BlockSpec((hp, 1, HEAD_DIM), head_vec),
            pl.BlockSpec((hp, 1, HEAD_DIM), head_vec),
            pl.BlockSpec((hp, GMLP_CHUNK, GMLP_CHUNK), head_vec),
            pl.BlockSpec((hp, GMLP_CHUNK, 1), head_vec),
            pl.BlockSpec((hp, lb_h.shape[1], HEAD_DIM), head_vec),
            pl.BlockSpec((hp, 1, HEAD_DIM), head_vec),
            pl.BlockSpec((n_streams, hp, HEAD_DIM, HEAD_DIM), lambda j: (0, j, 0, 0)),
        ],
        out_specs=[
            pl.BlockSpec((rows, hp * HEAD_DIM), lambda j: (0, j)),
            pl.BlockSpec((rows, hp * HEAD_DIM), lambda j: (0, j)),
            pl.BlockSpec((n_streams, hp, HEAD_DIM, HEAD_DIM), lambda j: (0, j, 0, 0)),
            pl.BlockSpec((rows, hp * HEAD_DIM), lambda j: (0, j)),
        ],
        out_shape=[
            jax.ShapeDtypeStruct((rows, N_HEADS * HEAD_DIM), BF16),
            jax.ShapeDtypeStruct((rows, N_HEADS * HEAD_DIM), BF16),
            jax.ShapeDtypeStruct((n_streams, N_HEADS, HEAD_DIM, HEAD_DIM), F32),
            jax.ShapeDtypeStruct((rows, N_HEADS * HEAD_DIM), F32),
        ],
        scratch_shapes=[
            pltpu.VMEM((rows, d), BF16),
            pltpu.VMEM((rows, N_KINDS * hp * HEAD_DIM), F32),
        ],
        compiler_params=_cparams(("arbitrary",)),
        name="mixer_sample",
    )(x2d, sh1, sc1, *([w_in_b] * N_KINDS), ln_v_g, ln_v_b, w_s, b_s, lb_h, gnorm_g, s0)


def _post_kernel(grp_ref, cap_ref, cop_ref, cas_ref, cos_ref, xp_ref, xs_ref, g1_ref, sh2_ref, sc2_ref, wout_ref, l1g_ref, l1b_ref,
                 wr_ref, br_ref, tri_ref,
                 x1_ref, h2_ref, meta_ref, wts_ref, cnt_ref,
                 mix_scr, hhi_scr, hlo_scr, run_scr, *, n_prompt_tiles, alpha):
    i = pl.program_id(0)
    is_s = i >= n_prompt_tiles
    tm = mix_scr.shape[0]

    @pl.when(is_s)
    def _():
        mix_scr[...] = _dot(jnp.concatenate([cas_ref[...], cos_ref[...]], axis=1), wout_ref[...])

    @pl.when(jnp.logical_not(is_s))
    def _():
        mix_scr[...] = _dot(jnp.concatenate([cap_ref[...], cop_ref[...]], axis=1), wout_ref[...])

    @pl.when(i == 0)
    def _():
        run_scr[...] = jnp.zeros_like(run_scr)

    def slab(g, carry):
        rows = pl.ds(pl.multiple_of(g * MOD_GROUP, MOD_GROUP), MOD_GROUP)
        x = jnp.where(is_s, xs_ref[rows, :], xp_ref[rows, :])
        stream = pl.ds(grp_ref[0, 0, g], 1)
        z = alpha * x + g1_ref[stream, :] * mix_scr[rows, :]
        x1 = _ln_plain(z) * l1g_ref[...] + l1b_ref[...]
        x1_ref[rows, :] = x1
        h2 = _ln_plain(x1) * (1.0 + sc2_ref[stream, :]) + sh2_ref[stream, :]
        h2_ref[rows, :] = h2
        hi = h2.astype(BF16)
        hhi_scr[rows, :] = hi
        hlo_scr[rows, :] = (h2 - hi.astype(F32)).astype(BF16)
        return carry
    lax.fori_loop(0, tm // MOD_GROUP, slab, 0, unroll=SLAB_UNROLL)

    _route(hhi_scr, hlo_scr, wr_ref, br_ref, tri_ref, run_scr, meta_ref, wts_ref, cnt_ref)


def _route(hhi_scr, hlo_scr, wr_ref, br_ref, before_ref, run_scr, meta_ref, wts_ref, cnt_ref):
    tm = hhi_scr.shape[0]
    n_r = N_GROUPS + N_EXPERTS
    n_rows = -(-n_r // 8) * 8
    s = _dot(hhi_scr[...], wr_ref[...]) + _dot(hlo_scr[...], wr_ref[...])
    logits = s + pltpu.roll(s, LANES - n_r, axis=1) + br_ref[...]
    lt = jnp.transpose(logits)[0:n_rows, :]

    row = lax.broadcasted_iota(I32, (n_rows, tm), 0)
    row_f = row.astype(F32)
    neg = jnp.float32(-jnp.inf)
    big = jnp.float32(LANES)

    def first_row_of(vals, top):
        return jnp.min(jnp.where(vals == top, row_f, big), axis=0, keepdims=True)

    gl = jnp.where(row < N_GROUPS, lt, neg)
    gmax = jnp.max(gl, axis=0, keepdims=True)
    gsel = first_row_of(gl, gmax)
    p_group = 1.0 / jnp.sum(jnp.exp(gl - gmax), axis=0, keepdims=True)

    e_lo = N_GROUPS + EXPERTS_PER_GROUP * gsel.astype(I32)
    emask = (row >= e_lo) & (row < e_lo + EXPERTS_PER_GROUP)
    el = jnp.where(emask, lt, neg)
    t1 = jnp.max(el, axis=0, keepdims=True)
    i1 = first_row_of(el, t1)
    el2 = jnp.where(row_f == i1, neg, el)
    t2 = jnp.max(el2, axis=0, keepdims=True)
    i2 = first_row_of(el2, t2)
    e2 = jnp.exp(t2 - t1)
    den = 1.0 + e2
    w1 = (1.0 / den) * p_group
    w2 = (e2 / den) * p_group

    sel1 = row_f == i1
    sel2 = row_f == i2
    onehot = jnp.where(sel1 | sel2, 1.0, 0.0)
    run_col = jnp.transpose(jnp.broadcast_to(run_scr[...], (LANES, LANES)))[0:n_rows, 0:1]
    before = _dot(onehot.astype(BF16), before_ref[...]) + run_col
    r1 = jnp.sum(jnp.where(sel1, before, 0.0), axis=0, keepdims=True)
    r2 = jnp.sum(jnp.where(sel2, before, 0.0), axis=0, keepdims=True)
    added = jnp.concatenate([jnp.sum(onehot, axis=1, keepdims=True), jnp.zeros((LANES - n_rows, 1), F32)], axis=0)
    run_scr[...] = run_scr[...] + jnp.transpose(jnp.broadcast_to(added, (LANES, LANES)))[0:1, :]
    cnt_ref[...] = run_scr[...].astype(I32)

    base = jnp.float32(N_GROUPS)
    mrow = lax.broadcasted_iota(I32, (META_ROWS, tm), 0)
    meta = jnp.where(mrow == 0, i1 - base, jnp.where(mrow == 1, i2 - base, jnp.where(mrow == 2, r1, jnp.where(mrow == 3, r2, 0.0))))
    meta_ref[0] = meta.astype(I32)
    wrow = lax.broadcasted_iota(I32, (LANES, tm), 0)
    wts_ref[...] = jnp.transpose(jnp.where(wrow == 0, w1, jnp.where(wrow == 1, w2, 0.0)))


def _post_call(groups3, cats_p, cats_s, x_p, x_s, mod, w_out_b, ln1_g, ln1_b, wr, br, tri, alpha):
    n_p, d = x_p.shape
    n_s = x_s.shape[0]
    d_a = cats_p[0].shape[1]
    tm = POST_TILE
    n_pt = n_p // tm
    n_tiles = n_pt + n_s // tm
    n_tok = n_p + n_s
    groups = tm // MOD_GROUP
    kern = functools.partial(_post_kernel, n_prompt_tiles=n_pt, alpha=alpha)
    p_idx = lambda i: (jnp.minimum(i, n_pt - 1), 0)
    s_idx = lambda i: (jnp.maximum(i - n_pt, 0), 0)
    tile = lambda i: (i, 0)
    const = lambda i: (0, 0)
    once = pl.Buffered(1)
    s_mode = once if n_tiles - n_pt == 1 else None
    mod_spec = lambda m: pl.BlockSpec((mod.shape[0], d), lambda i: (0, m), pipeline_mode=once)
    return pl.pallas_call(
        kern,
        grid=(n_tiles,),
        in_specs=[
            pl.BlockSpec((1, 1, groups), lambda i: (i, 0, 0), memory_space=pltpu.SMEM),
            pl.BlockSpec((tm, d_a), p_idx),
            pl.BlockSpec((tm, d_a), p_idx),
            pl.BlockSpec((tm, d_a), s_idx, pipeline_mode=s_mode),
            pl.BlockSpec((tm, d_a), s_idx, pipeline_mode=s_mode),
            pl.BlockSpec((tm, d), p_idx),
            pl.BlockSpec((tm, d), s_idx, pipeline_mode=s_mode),
            mod_spec(2), mod_spec(3), mod_spec(4),
            pl.BlockSpec((d, d), const, pipeline_mode=once),
            pl.BlockSpec((1, d), const),
            pl.BlockSpec((1, d), const),
            pl.BlockSpec((d, LANES), const, pipeline_mode=once),
            pl.BlockSpec((1, LANES), const),
            pl.BlockSpec((tm, tm), const, pipeline_mode=once),
        ],
        out_specs=[
            pl.BlockSpec((tm, d), tile),
            pl.BlockSpec((tm, d), tile),
            pl.BlockSpec((1, META_ROWS, tm), lambda i: (i, 0, 0)),
            pl.BlockSpec((tm, LANES), tile),
            pl.BlockSpec((1, LANES), const),
        ],
        out_shape=[
            jax.ShapeDtypeStruct((n_tok, d), F32),
            jax.ShapeDtypeStruct((n_tok, d), F32),
            jax.ShapeDtypeStruct((n_tiles, META_ROWS, tm), I32),
            jax.ShapeDtypeStruct((n_tok, LANES), F32),
            jax.ShapeDtypeStruct((1, LANES), I32),
        ],
        scratch_shapes=[
            pltpu.VMEM((tm, d), F32),
            pltpu.VMEM((tm, d), BF16),
            pltpu.VMEM((tm, d), BF16),
            pltpu.VMEM((1, LANES), F32),
        ],
        compiler_params=_cparams(("arbitrary",)),
        name="post_router",
    )(groups3, *cats_p, *cats_s, x_p, x_s, mod, mod, mod, w_out_b, ln1_g, ln1_b, wr, br, tri)


def _dispatch_kernel(slot_ref, gap_start_ref, gap_len_ref, h_ref, xs_ref, zero_scr, sem, tile_sem, *, gaps_per_step):
    i = pl.program_id(0)
    tm = h_ref.shape[0]
    n_gaps = gap_len_ref.shape[0]
    tile_rows = zero_scr.shape[0]

    @pl.when(i == 0)
    def _():
        zero_scr[...] = jnp.zeros_like(zero_scr)

    def row_copy(r, k):
        return pltpu.make_async_copy(h_ref.at[pl.ds(r, 1)], xs_ref.at[pl.ds(slot_ref[0, 0, 2 * r + k], 1)], sem)

    def start_rows(r, carry):
        row_copy(r, 0).start(priority=0)
        row_copy(r, 1).start(priority=1)
        return carry
    lax.fori_loop(0, tm, start_rows, 0, unroll=ROW_DMA_UNROLL)

    def gap_rows(g):
        return jnp.where(g < n_gaps - 1, gap_len_ref[jnp.minimum(g, n_gaps - 1)], 0)

    def gap_tiles(g):
        return jnp.where(g == n_gaps - 1, gap_len_ref[n_gaps - 1] // tile_rows, 0)

    def zero_row_copy(g, r):
        return pltpu.make_async_copy(zero_scr.at[pl.ds(0, 1)], xs_ref.at[pl.ds(gap_start_ref[g] + r, 1)], sem)

    def zero_tile_copy(g, t):
        start = pl.multiple_of(gap_start_ref[g] + t * tile_rows, tile_rows)
        return pltpu.make_async_copy(zero_scr, xs_ref.at[pl.ds(start, tile_rows)], tile_sem)

    for j in range(gaps_per_step):
        g = jnp.minimum(i * gaps_per_step + j, n_gaps - 1)
        live = i * gaps_per_step + j < n_gaps
        n_rows = jnp.where(live, gap_rows(g), 0)
        n_til = jnp.where(live, gap_tiles(g), 0)
        lax.fori_loop(0, n_rows, lambda r, c, g=g: (zero_row_copy(g, r).start(), c)[1], 0)
        lax.fori_loop(0, n_til, lambda t, c, g=g: (zero_tile_copy(g, t).start(), c)[1], 0)

    def wait_rows(r, carry):
        row_copy(r, 0).wait()
        row_copy(r, 1).wait()
        return carry
    lax.fori_loop(0, tm, wait_rows, 0, unroll=ROW_DMA_UNROLL)

    for j in range(gaps_per_step):
        g = jnp.minimum(i * gaps_per_step + j, n_gaps - 1)
        live = i * gaps_per_step + j < n_gaps
        n_rows = jnp.where(live, gap_rows(g), 0)
        n_til = jnp.where(live, gap_tiles(g), 0)
        lax.fori_loop(0, n_rows, lambda r, c, g=g: (zero_row_copy(g, r).wait(), c)[1], 0)
        lax.fori_loop(0, n_til, lambda t, c, g=g: (zero_tile_copy(g, t).wait(), c)[1], 0)


def _dispatch_call(slots3, gap_start, gap_len, h2, n_sorted):
    n_tok, d = h2.shape
    n_tiles = n_tok // TOK_TILE
    n_gaps = gap_len.shape[0]
    kern = functools.partial(_dispatch_kernel, gaps_per_step=-(-n_gaps // n_tiles))
    return pl.pallas_call(
        kern,
        grid=(n_tiles,),
        in_specs=[
            pl.BlockSpec((1, 1, slots3.shape[2]), lambda i: (i, 0, 0), memory_space=pltpu.SMEM),
            pl.BlockSpec((n_gaps,), lambda i: (0,), memory_space=pltpu.SMEM),
            pl.BlockSpec((n_gaps,), lambda i: (0,), memory_space=pltpu.SMEM),
            pl.BlockSpec((TOK_TILE, d), lambda i: (i, 0)),
        ],
        out_specs=pl.BlockSpec(memory_space=pl.ANY),
        out_shape=jax.ShapeDtypeStruct((n_sorted, d), F32),
        scratch_shapes=[pltpu.VMEM((EXP_TILE, d), F32), pltpu.SemaphoreType.DMA(()), pltpu.SemaphoreType.DMA(())],
        compiler_params=_cparams(("arbitrary",)),
        name="dispatch",
    )(slots3, gap_start, gap_len, h2)


def _expert_kernel(te_ref, nt_ref, nxt_ref, run_ref, xs_ref, w1_ref, w3_ref, w2_ref, ys_ref,
                   x_scr, w1_scr, w3_scr, w2f_scr, w13_scr, w2_scr, xsem, wsem):
    t = pl.program_id(0)
    n_t = pl.num_programs(0)
    rows = x_scr.shape[1]
    d_exp = w1_scr.shape[2]
    x_slots = x_scr.shape[0]
    xs = lax.rem(t, x_slots)
    e = te_ref[t]
    prev = te_ref[jnp.maximum(t - 1, 0)]
    changed = (t == 0) | (e != prev)
    valid = t < nt_ref[0]
    wslot = lax.rem(run_ref[t], 2)

    def x_copy(tile, slot):
        start = pl.multiple_of(tile * rows, rows)
        return pltpu.make_async_copy(xs_ref.at[pl.ds(start, rows)], x_scr.at[slot], xsem.at[slot])

    def w_copies(expert, slot):
        return (pltpu.make_async_copy(w1_ref.at[expert], w1_scr.at[slot], wsem.at[slot]),
                pltpu.make_async_copy(w3_ref.at[expert], w3_scr.at[slot], wsem.at[slot]),
                pltpu.make_async_copy(w2_ref.at[expert], w2f_scr.at[slot], wsem.at[slot]))

    @pl.when(t == 0)
    def _():
        for ahead in range(X_AHEAD):
            x_copy(ahead, ahead).start()
        for c in w_copies(e, wslot):
            c.start()

    @pl.when(t + X_AHEAD < n_t)
    def _():
        x_copy(t + X_AHEAD, lax.rem(t + X_AHEAD, x_slots)).start()

    @pl.when(changed)
    def _():
        for c in w_copies(e, wslot):
            c.wait()
        nxt = nxt_ref[t]

        @pl.when(nxt != e)
        def _():
            for c in w_copies(nxt, 1 - wslot):
                c.start()
        w13_scr[:, 0:d_exp] = w1_scr[wslot].astype(BF16)
        w13_scr[:, d_exp:2 * d_exp] = w3_scr[wslot].astype(BF16)
        w2_scr[...] = w2f_scr[wslot].astype(BF16)

    x_copy(t, xs).wait()

    @pl.when(valid)
    def _():
        h13 = _dot(x_scr[xs].astype(BF16), w13_scr[...])
        hm = jax.nn.silu(h13[:, 0:d_exp]) * h13[:, d_exp:2 * d_exp]
        _store_rows(ys_ref, (), 0, _dot(hm.astype(BF16), w2_scr[...]))

    @pl.when(jnp.logical_not(valid))
    def _():
        ys_ref[...] = jnp.zeros_like(ys_ref)


def _expert_call(tile_expert, n_valid, next_expert, run_id, xs, w1, w3, w2):
    n_sorted, d = xs.shape
    n_exp, _, d_exp = w1.shape
    n_tiles = n_sorted // EXP_TILE
    assert n_tiles > X_AHEAD
    grid_spec = pltpu.PrefetchScalarGridSpec(
        num_scalar_prefetch=4,
        grid=(n_tiles,),
        in_specs=[pl.BlockSpec(memory_space=pl.ANY)] * 4,
        out_specs=pl.BlockSpec((EXP_TILE * _lines_per_row(d), LANES), lambda t, te, nt, nx, rn: (t, 0)),
        scratch_shapes=[
            pltpu.VMEM((X_AHEAD + 1, EXP_TILE, d), F32),
            pltpu.VMEM((2, d, d_exp), F32),
            pltpu.VMEM((2, d, d_exp), F32),
            pltpu.VMEM((2, d_exp, d), F32),
            pltpu.VMEM((d, 2 * d_exp), BF16),
            pltpu.VMEM((d_exp, d), BF16),
            pltpu.SemaphoreType.DMA((X_AHEAD + 1,)),
            pltpu.SemaphoreType.DMA((2,)),
        ],
    )
    return pl.pallas_call(
        _expert_kernel,
        grid_spec=grid_spec,
        out_shape=jax.ShapeDtypeStruct((n_sorted * _lines_per_row(d), LANES), F32),
        compiler_params=_cparams(("arbitrary",)),
        name="experts",
    )(tile_expert, n_valid, next_expert, run_id, xs, w1, w3, w2)


def _combine_kernel(scur_ref, snext_ref, grp_ref, x1_ref, wts_ref, g2_ref, l2g_ref, l2b_ref, ys_ref, outp_ref, outs_ref,
                    y_scr, sem, *, n_prompt_tiles, alpha):
    i = pl.program_id(0)
    n_i = pl.num_programs(0)
    is_s = i >= n_prompt_tiles
    tm, d = x1_ref.shape
    lpr = _lines_per_row(d)
    cur = lax.rem(i, 2)

    def row_lines(r):
        return pl.ds(pl.multiple_of(r * lpr, lpr), lpr)

    def start_gathers(s_ref, buf):
        def body(r, carry):
            for k in range(2):
                pltpu.make_async_copy(ys_ref.at[row_lines(s_ref[0, 0, 2 * r + k])], y_scr.at[buf, k, row_lines(r)],
                                      sem.at[buf, k]).start(priority=k)
            return carry
        lax.fori_loop(0, tm, body, 0, unroll=ROW_DMA_UNROLL // 2)

    @pl.when(i == 0)
    def _():
        start_gathers(scur_ref, 0)

    @pl.when(i + 1 < n_i)
    def _():
        start_gathers(snext_ref, 1 - cur)

    for k in range(2):
        pltpu.make_async_copy(y_scr.at[cur, k], y_scr.at[cur, k], sem.at[cur, k]).wait()

    def slab(g, carry, out_ref):
        rows = pl.ds(pl.multiple_of(g * MOD_GROUP, MOD_GROUP), MOD_GROUP)
        w = wts_ref[rows, :]
        moe = (w[:, 0:1] * _load_rows(y_scr, (cur, 0), g * MOD_GROUP, MOD_GROUP, d)
               + w[:, 1:2] * _load_rows(y_scr, (cur, 1), g * MOD_GROUP, MOD_GROUP, d))
        z = alpha * x1_ref[rows, :] + g2_ref[pl.ds(grp_ref[0, 0, g], 1), :] * moe
        out_ref[rows, :] = _ln_plain(z) * l2g_ref[...] + l2b_ref[...]
        return carry

    @pl.when(is_s)
    def _():
        lax.fori_loop(0, tm // MOD_GROUP, functools.partial(slab, out_ref=outs_ref), 0, unroll=SLAB_UNROLL)

    @pl.when(jnp.logical_not(is_s))
    def _():
        lax.fori_loop(0, tm // MOD_GROUP, functools.partial(slab, out_ref=outp_ref), 0, unroll=SLAB_UNROLL)


def _combine_call(slots3, groups3, x1, wts, mod, ln2_g, ln2_b, ys, n_p, alpha):
    n_tok, d = x1.shape
    n_tiles = n_tok // TOK_TILE
    n_pt = n_p // TOK_TILE
    groups = TOK_TILE // MOD_GROUP
    kern = functools.partial(_combine_kernel, n_prompt_tiles=n_pt, alpha=alpha)
    slot_tile = lambda ahead: pl.BlockSpec(
        (1, 1, slots3.shape[2]), lambda i: (jnp.minimum(i + ahead, n_tiles - 1), 0, 0), memory_space=pltpu.SMEM)
    return pl.pallas_call(
        kern,
        grid=(n_tiles,),
        in_specs=[
            slot_tile(0), slot_tile(1),
            pl.BlockSpec((1, 1, groups), lambda i: (i, 0, 0), memory_space=pltpu.SMEM),
            pl.BlockSpec((TOK_TILE, d), lambda i: (i, 0)),
            pl.BlockSpec((TOK_TILE, LANES), lambda i: (i, 0)),
            pl.BlockSpec((mod.shape[0], d), lambda i: (0, 5)),
            pl.BlockSpec((1, d), lambda i: (0, 0)),
            pl.BlockSpec((1, d), lambda i: (0, 0)),
            pl.BlockSpec(memory_space=pl.ANY),
        ],
        out_specs=[
            pl.BlockSpec((TOK_TILE, d), lambda i: (jnp.minimum(i, n_pt - 1), 0)),
            pl.BlockSpec((TOK_TILE, d), lambda i: (jnp.maximum(i - n_pt, 0), 0)),
        ],
        out_shape=[
            jax.ShapeDtypeStruct((n_p, d), F32),
            jax.ShapeDtypeStruct((n_tok - n_p, d), F32),
        ],
        scratch_shapes=[
            pltpu.VMEM((2, 2, TOK_TILE * _lines_per_row(d), LANES), F32),
            pltpu.SemaphoreType.DMA((2, 2)),
        ],
        compiler_params=_cparams(("arbitrary",)),
        name="combine",
    )(slots3, slots3, groups3, x1, wts, mod, ln2_g, ln2_b, ys)


def _routing_tables(meta, cnt_row, n_sorted):
    n_tok = meta.shape[0] * meta.shape[2]
    experts = jnp.transpose(meta[:, 0:2, :], (0, 2, 1)).reshape(n_tok, 2)
    ranks = jnp.transpose(meta[:, 2:4, :], (0, 2, 1)).reshape(n_tok, 2)
    cnt = cnt_row[0, N_GROUPS:N_GROUPS + N_EXPERTS]
    padded = ((cnt + EXP_TILE - 1) // EXP_TILE) * EXP_TILE
    ends = jnp.cumsum(padded)
    offs = ends - padded
    slots = offs[experts] + ranks
    total = ends[-1]
    n_tiles = n_sorted // EXP_TILE
    tile_ids = jnp.arange(n_tiles, dtype=I32)
    tile_expert = jnp.minimum(
        jnp.sum((tile_ids[:, None] >= (ends // EXP_TILE)[None, :]).astype(I32), axis=1), N_EXPERTS - 1)
    n_valid = (total // EXP_TILE).reshape(1).astype(I32)
    ids = jnp.arange(N_EXPERTS, dtype=I32)
    later = (ids[None, :] > tile_expert[:, None]) & (padded[None, :] > 0)
    next_expert = jnp.min(jnp.where(later, ids[None, :], N_EXPERTS), axis=1)
    next_expert = jnp.where(next_expert == N_EXPERTS, tile_expert, next_expert).astype(I32)
    starts_run = jnp.concatenate([jnp.zeros((1,), I32), (tile_expert[1:] != tile_expert[:-1]).astype(I32)])
    run_id = jnp.cumsum(starts_run).astype(I32)
    gap_start = jnp.concatenate([offs + cnt, total.reshape(1)]).astype(I32)
    gap_len = jnp.concatenate([padded - cnt, (n_sorted - total).reshape(1)]).astype(I32)
    return slots.astype(I32), (tile_expert, n_valid, next_expert, run_id), gap_start, gap_len


def _layer(layer, n_layers, xp, xs, s0_l, c_all, p):
    (w_ada, b_ada, w_in, ln_v_g, ln_v_b, w_s, b_s, hgrn_lb, gnorm_g, w_out, ln1_g, ln1_b,
     w_rg, b_rg, w_re, b_re, w1, w3, w2, ln2_g, ln2_b) = p
    bsz, t, d = xp.shape
    n_streams, seq, _ = xs.shape
    alpha = float((2.0 * n_layers) ** 0.25)
    n_p = bsz * t
    n_s = n_streams * seq
    n_tok = n_p + n_s

    n_c = c_all.shape[0]
    c_pad = jnp.pad(c_all, ((0, (-n_c) % 8), (0, 0)))
    mod = _ada_call(c_pad, w_ada, b_ada)
    sh1, sc1 = mod[:n_c, 0:d], mod[:n_c, d:2 * d]

    w_in_b = w_in.astype(BF16)
    w_out_b = w_out.astype(BF16)
    lvg = ln_v_g.reshape(N_HEADS, 1, HEAD_DIM)
    lvb = ln_v_b.reshape(N_HEADS, 1, HEAD_DIM)
    b_s3 = b_s.reshape(N_HEADS, GMLP_CHUNK, 1)
    lb_h = jnp.transpose(hgrn_lb.reshape(hgrn_lb.shape[0], N_HEADS, HEAD_DIM), (1, 0, 2))
    gn = gnorm_g.reshape(N_HEADS, 1, HEAD_DIM)

    ca_p, co_p, state_p = _mixer_prompt_call(xp, sh1[:bsz], sc1[:bsz], w_in_b, lvg, lvb, w_s, b_s3, lb_h, gn, layer)
    ca_s, co_s, state_s, vrows = _mixer_sample_call(xs.reshape(n_s, d), sh1[bsz:], sc1[bsz:], w_in_b, lvg, lvb, w_s,
                                               b_s3, lb_h, gn, s0_l, layer, seq)

    group_stream = jnp.concatenate([
        jnp.repeat(jnp.arange(bsz, dtype=I32), t // MOD_GROUP),
        bsz + jnp.repeat(jnp.arange(n_streams, dtype=I32), seq // MOD_GROUP)])

    wr = jnp.concatenate([w_rg, w_re], axis=1)
    wr_hi = wr.astype(BF16)
    wr_lo = (wr - wr_hi.astype(F32)).astype(BF16)
    n_r = wr.shape[1]
    wr_cat = jnp.concatenate([wr_hi, wr_lo, jnp.zeros((d, LANES - 2 * n_r), BF16)], axis=1)
    br = jnp.pad(jnp.concatenate([b_rg, b_re]), (0, LANES - n_r)).reshape(1, LANES)
    tri = jnp.triu(jnp.ones((POST_TILE, POST_TILE), F32), 1).astype(BF16)

    x1, h2, meta, wts, cnt_row = _post_call(group_stream.reshape(n_tok // POST_TILE, 1, POST_TILE // MOD_GROUP),
                                            (ca_p, co_p), (ca_s, co_s), xp.reshape(n_p, d), xs.reshape(n_s, d),
                                            mod, w_out_b, ln1_g.reshape(1, d), ln1_b.reshape(1, d), wr_cat, br, tri, alpha)

    n_sorted = 2 * n_tok + N_EXPERTS * EXP_TILE
    slots, tile_tables, gap_start, gap_len = _routing_tables(meta, cnt_row, n_sorted)
    n_tiles = n_tok // TOK_TILE
    slots3 = slots.reshape(n_tiles, 1, 2 * TOK_TILE)

    xs_sorted = _dispatch_call(slots3, gap_start, gap_len, h2, n_sorted)
    ys_sorted = _expert_call(*tile_tables, xs_sorted, w1, w3, w2)
    yp, ys_out = _combine_call(slots3, group_stream.reshape(n_tiles, 1, TOK_TILE // MOD_GROUP), x1, wts, mod,
                               ln2_g.reshape(1, d), ln2_b.reshape(1, d), ys_sorted, n_p, alpha)

    v_rows = vrows.reshape(n_streams, seq, N_HEADS, HEAD_DIM)
    return yp.reshape(bsz, t, d), ys_out.reshape(n_streams, seq, d), state_p, state_s, v_rows


def kernel(x_prompt, x_sample, state_hgrn, c_prompt, c_sample, w_ada, b_ada, w_in, ln_v_g, ln_v_b, w_s, b_s, hgrn_lb, gnorm_g, w_out, ln1_g, ln1_b, w_router_g, b_router_g, w_router_e, b_router_e, w1, w3, w2, ln2_g, ln2_b):
    n_layers = w_ada.shape[0]
    assert x_prompt.shape[1] % MIX_TILE == 0 and x_prompt.shape[2] == 2 * N_HEADS * HEAD_DIM
    assert x_sample.shape[1] % MOD_GROUP == 0 and x_sample.shape[1] <= SUB_CHUNK
    assert (x_sample.shape[0] * x_sample.shape[1]) % TOK_TILE == 0 and TOK_TILE % POST_TILE == 0
    c_all = jnp.concatenate([c_prompt, c_sample], axis=0)
    xp, xs = x_prompt, x_sample
    sp_list, ss_list, vs_list = [], [], []
    for l in range(n_layers):
        p = (w_ada[l], b_ada[l], w_in[l], ln_v_g[l], ln_v_b[l], w_s[l], b_s[l], hgrn_lb, gnorm_g[l], w_out[l],
             ln1_g[l], ln1_b[l], w_router_g[l], b_router_g[l], w_router_e[l], b_router_e[l],
             w1[l], w3[l], w2[l], ln2_g[l], ln2_b[l])
        xp, xs, sp, ss, vs = _layer(l, n_layers, xp, xs, state_hgrn[l], c_all, p)
        sp_list.append(sp.astype(state_hgrn.dtype))
        ss_list.append(ss.astype(state_hgrn.dtype))
        vs_list.append(vs)
    return (xp, xs, jnp.stack(sp_list, axis=0), jnp.stack(ss_list, axis=0), jnp.stack(vs_list, axis=0))
```

```python
import functools

import jax
import jax.numpy as jnp
from jax import lax
from jax.experimental import pallas as pl
from jax.experimental.pallas import tpu as pltpu

F32 = jnp.float32
BF16 = jnp.bfloat16
I32 = jnp.int32

N_HEADS = 8
HEAD_DIM = 128
GMLP_CHUNK = 128
SUB_CHUNK = 64
N_GROUPS = 4
EXPERTS_PER_GROUP = 8
N_EXPERTS = N_GROUPS * EXPERTS_PER_GROUP
LN_EPS = 1e-5
N_KINDS = 6

LANES = 128
MIX_TILE = 512
HEADS_PER_STEP = 2
HGRN_BLOCK = 64
DIAG_BLOCK = 16
SAFE_BLOCK_DECAY = 60.0
TOK_TILE = 512
POST_TILE = 512
META_ROWS = 8
MOD_GROUP = 32
SLAB_UNROLL = 4
EXP_TILE = 256
X_AHEAD = 2
ROW_DMA_UNROLL = 8
GAP_ROWS = 8
ADA_TILE = 1024
VMEM_LIMIT = 56 * 1024 * 1024


def _cparams(sem):
    return pltpu.CompilerParams(dimension_semantics=sem, vmem_limit_bytes=VMEM_LIMIT)


def _ln_plain(x):
    mu = jnp.mean(x, axis=-1, keepdims=True)
    xc = x - mu
    var = jnp.mean(xc * xc, axis=-1, keepdims=True)
    return xc * lax.rsqrt(var + LN_EPS)


def _dot(a, b):
    return jnp.dot(a, b, preferred_element_type=F32)


def _dot_nt(a, b):
    return lax.dot_general(a, b, (((1,), (1,)), ((), ())), preferred_element_type=F32)


def _lines_per_row(d):
    return d // LANES


def _load_rows(ref, lead, row0, n_rows, d):
    lpr = _lines_per_row(d)
    parts = [ref[lead + (pl.ds(row0 * lpr + c, n_rows, stride=lpr), slice(None))] for c in range(lpr)]
    return jnp.concatenate(parts, axis=1)


def _store_rows(ref, lead, row0, val):
    n_rows, d = val.shape
    lpr = _lines_per_row(d)
    for c in range(lpr):
        ref[lead + (pl.ds(row0 * lpr + c, n_rows, stride=lpr), slice(None))] = val[:, c * LANES:(c + 1) * LANES]


def _ada_kernel(c_ref, w_ref, b_ref, o_ref):
    s = jax.nn.silu(c_ref[...]).astype(BF16)
    o_ref[...] = _dot(s, w_ref[...].astype(BF16)) + b_ref[...]


def _ada_call(c_pad, w_ada, b_ada):
    rows, d = c_pad.shape
    n_out = w_ada.shape[1]
    return pl.pallas_call(
        _ada_kernel,
        grid=(n_out // ADA_TILE,),
        in_specs=[
            pl.BlockSpec((rows, d), lambda n: (0, 0)),
            pl.BlockSpec((d, ADA_TILE), lambda n: (0, n)),
            pl.BlockSpec((1, ADA_TILE), lambda n: (0, n)),
        ],
        out_specs=pl.BlockSpec((rows, ADA_TILE), lambda n: (0, n)),
        out_shape=jax.ShapeDtypeStruct((rows, n_out), F32),
        compiler_params=_cparams(("arbitrary",)),
        name="adaln",
    )(c_pad, w_ada, b_ada.reshape(1, n_out))


def _lower_bound(lb_ref, hd, layer):
    raw = lb_ref[hd]
    m = jnp.max(raw, axis=0, keepdims=True)
    e = jnp.exp(raw - m)
    p = e / jnp.sum(e, axis=0, keepdims=True)
    return jnp.sum(p[: layer + 1], axis=0, keepdims=True)


def _gmlp_rows(u_pre, v_pre, ln_g, ln_b):
    u = jax.nn.gelu(u_pre)
    v = jax.nn.gelu(v_pre)
    mu = jnp.mean(v, axis=-1, keepdims=True)
    vc = v - mu
    var = jnp.mean(vc * vc, axis=-1, keepdims=True)
    vn = vc * lax.rsqrt(var + LN_EPS) * ln_g + ln_b
    return u, vn


def _row_bcast(a, row, n):
    return jnp.broadcast_to(a[row:row + 1, :], (n, a.shape[1]))


def _block_id(idx, size):
    return lax.shift_right_logical(idx, I32(size.bit_length() - 1))


def _hgrn_masks(c):
    row = lax.broadcasted_iota(I32, (c, c), 0)
    col = lax.broadcasted_iota(I32, (c, c), 1)
    masks = []
    half = c // 2
    while half >= DIAG_BLOCK:
        span = 2 * half
        same = _block_id(row, span) == _block_id(col, span)
        masks.append(same & ((row & (span - 1)) >= half) & ((col & (span - 1)) < half))
        half //= 2
    diag = (_block_id(row, DIAG_BLOCK) == _block_id(col, DIAG_BLOCK)) & (col <= row)
    return masks, diag


def _tri_ones(c):
    row = lax.broadcasted_iota(I32, (c, c), 0)
    col = lax.broadcasted_iota(I32, (c, c), 1)
    return jnp.where(col <= row, 1.0, 0.0).astype(BF16)


def _level_halves(c):
    halves = []
    half = c // 2
    while half >= 1:
        halves.append(half)
        half //= 2
    return halves


def _level_masks(c):
    row = lax.broadcasted_iota(I32, (c, c), 0)
    col = lax.broadcasted_iota(I32, (c, c), 1)
    masks = []
    for half in _level_halves(c):
        span = 2 * half
        same = _block_id(row, span) == _block_id(col, span)
        masks.append(same & ((row & (span - 1)) >= half) & ((col & (span - 1)) < half))
    return masks, row == col


def _level_ref_sums(c):
    row = lax.broadcasted_iota(I32, (c, c), 0)
    col = lax.broadcasted_iota(I32, (c, c), 1)
    pieces = [col <= _block_id(row, 2 * half) * (2 * half) + half for half in _level_halves(c)]
    return jnp.concatenate([jnp.where(p, 1.0, 0.0).astype(BF16) for p in pieces], axis=0)


def _hgrn_block_decay(units, c):
    worst = None
    for u in units:
        a = u["a"]
        for b in range(c // DIAG_BLOCK):
            span = a[b * DIAG_BLOCK:b * DIAG_BLOCK + 1, :] - a[(b + 1) * DIAG_BLOCK - 1:(b + 1) * DIAG_BLOCK, :]
            worst = span if worst is None else jnp.maximum(worst, span)
    return jnp.max(worst)


def _hgrn_prepare(chains, c):
    tri = _tri_ones(c)
    units = []
    for st0, lbv, gn, blocks in chains:
        for q_pre, f_pre, i_pre, g_pre in blocks:
            q = jax.nn.silu(q_pre)
            fg = lbv + (1.0 - lbv) * jax.nn.sigmoid(f_pre)
            logf = jnp.log(fg)
            hi = logf.astype(BF16)
            lo = (logf - hi.astype(F32)).astype(BF16)
            units.append(dict(q=q, k=1.0 - fg, hilo=jnp.concatenate([hi, lo], axis=1),
                              v=i_pre, g=g_pre, gn=gn))

    for u in units:
        p = _dot(tri, u["hilo"])
        u["a"] = p[:, 0:HEAD_DIM] + p[:, HEAD_DIM:2 * HEAD_DIM]
    return units


def _hgrn_scores(units, c, exact):
    units = [dict(u) for u in units]
    if exact:
        masks, eye = _level_masks(c)
        sums = _level_ref_sums(c)
    else:
        masks, diag_mask = _hgrn_masks(c)
    for u in units:
        q, k, a = u["q"], u["k"], u["a"]
        parts = []
        if exact:
            p = _dot(sums, u["hilo"])
            p = p[:, 0:HEAD_DIM] + p[:, HEAD_DIM:2 * HEAD_DIM]
            for n, mask in enumerate(masks):
                d = a - p[n * c:(n + 1) * c]
                e = jnp.exp(jnp.minimum(d, -d))
                parts.append((mask, _dot_nt((q * e).astype(BF16), (k * e).astype(BF16))))
            parts.append((eye, jnp.sum(q * k, axis=-1, keepdims=True)))
        else:
            half = c // 2
            level = 0
            while half >= DIAG_BLOCK:
                span = 2 * half
                ref = jnp.concatenate([_row_bcast(a, b * span + half, span) for b in range(c // span)], axis=0)
                qs = (q * jnp.exp(jnp.minimum(a - ref, 0.0))).astype(BF16)
                ks = (k * jnp.exp(jnp.minimum(ref - a, 0.0))).astype(BF16)
                parts.append((masks[level], _dot_nt(qs, ks)))
                half //= 2
                level += 1
            ref = jnp.concatenate([_row_bcast(a, b * DIAG_BLOCK, DIAG_BLOCK) for b in range(c // DIAG_BLOCK)], axis=0)
            qd = (q * jnp.exp(a - ref)).astype(BF16)
            kd = (k * jnp.exp(jnp.minimum(ref - a, SAFE_BLOCK_DECAY))).astype(BF16)
            parts.append((diag_mask, _dot_nt(qd, kd)))
        u["parts"] = parts
        a_last = a[c - 1:c, :]
        u["decay"] = jnp.exp(a_last)
        kl = (k * jnp.exp(a_last - a)).astype(BF16)
        v_t = jnp.transpose(u["v"]).astype(BF16)
        u["upd"] = _dot(v_t, kl)
        u["qe"] = (q * jnp.exp(a)).astype(BF16)
    return units


def _hgrn_finish(chains, units):
    finals = []
    n = 0
    for st0, lbv, gn, blocks in chains:
        st = st0
        for _ in blocks:
            units[n]["st_in"] = st.astype(BF16)
            st = st * units[n]["decay"] + units[n]["upd"]
            n += 1
        finals.append(st)

    outs = []
    n = 0
    for st0, lbv, gn, blocks in chains:
        chain_out = []
        for _ in blocks:
            u = units[n]
            scores = None
            for mask, part in u["parts"]:
                part = jnp.where(mask, part, 0.0)
                scores = part if scores is None else scores + part
            o = _dot(scores.astype(BF16), u["v"].astype(BF16)) + _dot_nt(u["qe"], u["st_in"])
            o = o * lax.rsqrt(jnp.mean(o * o, axis=-1, keepdims=True) + LN_EPS) * u["gn"]
            chain_out.append(o * jax.nn.silu(u["g"]))
            n += 1
        outs.append(chain_out)
    return outs, finals


def _proj_cols(kind, hd):
    c0 = (kind * HEADS_PER_STEP + hd) * HEAD_DIM
    return slice(c0, c0 + HEAD_DIM)


def _in_projection(h_scr, w_refs, proj_scr):
    width = HEADS_PER_STEP * HEAD_DIM
    for kind, w_ref in enumerate(w_refs):
        proj_scr[:, kind * width:(kind + 1) * width] = _dot(h_scr[...], w_ref[...])


def _mixer_prompt_kernel(x_ref, sh_ref, sc_ref, wu_ref, wv_ref, wq_ref, wf_ref, wi_ref, wg_ref,
                         lvg_ref, lvb_ref, ws_ref, bs_ref, lb_ref, gn_ref,
                         cata_ref, cato_ref, state_ref, h_scr, proj_scr, st_scr, *, layer):
    i = pl.program_id(1)
    j = pl.program_id(2)
    tm = h_scr.shape[0]

    @pl.when(j == 0)
    def _():
        def slab(r, carry):
            rows = pl.ds(pl.multiple_of(r * HGRN_BLOCK, HGRN_BLOCK), HGRN_BLOCK)
            h = _ln_plain(x_ref[0, rows, :]) * (1.0 + sc_ref[0]) + sh_ref[0]
            h_scr[rows, :] = h.astype(BF16)
            return carry
        lax.fori_loop(0, tm // HGRN_BLOCK, slab, 0, unroll=2)

    @pl.when((i == 0) & (j == 0))
    def _():
        st_scr[...] = jnp.zeros_like(st_scr)

    _in_projection(h_scr, (wu_ref, wv_ref, wq_ref, wf_ref, wi_ref, wg_ref), proj_scr)

    row = lax.broadcasted_iota(I32, (GMLP_CHUNK, GMLP_CHUNK), 0)
    col = lax.broadcasted_iota(I32, (GMLP_CHUNK, GMLP_CHUNK), 1)
    causal = _block_id(row, SUB_CHUNK) >= _block_id(col, SUB_CHUNK)
    for hd in range(HEADS_PER_STEP):
        wm = jnp.where(causal, ws_ref[hd], 0.0).astype(BF16)
        b_col = bs_ref[hd]
        for c in range(tm // GMLP_CHUNK):
            rows = slice(c * GMLP_CHUNK, (c + 1) * GMLP_CHUNK)
            u, vn = _gmlp_rows(proj_scr[rows, _proj_cols(0, hd)], proj_scr[rows, _proj_cols(1, hd)],
                               lvg_ref[hd], lvb_ref[hd])
            sp = _dot(wm, vn.astype(BF16)) + b_col
            cata_ref[rows, hd * HEAD_DIM:(hd + 1) * HEAD_DIM] = (u * sp).astype(BF16)

    chains = []
    for hd in range(HEADS_PER_STEP):
        blocks = [tuple(proj_scr[c * HGRN_BLOCK:(c + 1) * HGRN_BLOCK, _proj_cols(kind, hd)] for kind in range(2, N_KINDS))
                  for c in range(tm // HGRN_BLOCK)]
        chains.append((st_scr[HEADS_PER_STEP * j + hd], _lower_bound(lb_ref, hd, layer), gn_ref[hd], blocks))
    units = _hgrn_prepare(chains, HGRN_BLOCK)
    wide_decay = _hgrn_block_decay(units, HGRN_BLOCK) > SAFE_BLOCK_DECAY

    def finish(exact):
        outs, finals = _hgrn_finish(chains, _hgrn_scores(units, HGRN_BLOCK, exact))
        for hd in range(HEADS_PER_STEP):
            for c, out in enumerate(outs[hd]):
                cato_ref[c * HGRN_BLOCK:(c + 1) * HGRN_BLOCK, hd * HEAD_DIM:(hd + 1) * HEAD_DIM] = out.astype(BF16)
            st_scr[HEADS_PER_STEP * j + hd] = finals[hd]
            state_ref[0, HEADS_PER_STEP * j + hd] = jnp.transpose(finals[hd])

    @pl.when(wide_decay)
    def _():
        finish(True)

    @pl.when(jnp.logical_not(wide_decay))
    def _():
        finish(False)


def _w_in_specs(d):
    width = HEADS_PER_STEP * HEAD_DIM
    blocks_per_kind = N_HEADS // HEADS_PER_STEP
    return [pl.BlockSpec((d, width), lambda j, kind=kind: (0, kind * blocks_per_kind + j)) for kind in range(N_KINDS)]


def _mixer_prompt_call(x, sh1, sc1, w_in_b, ln_v_g, ln_v_b, w_s, b_s, lb_h, gnorm_g, layer):
    bsz, t, d = x.shape
    n_t = t // MIX_TILE
    hp = HEADS_PER_STEP
    kern = functools.partial(_mixer_prompt_kernel, layer=layer)
    head_vec = lambda b, i, j: (j, 0, 0)
    return pl.pallas_call(
        kern,
        grid=(bsz, n_t, N_HEADS // hp),
        in_specs=[
            pl.BlockSpec((1, MIX_TILE, d), lambda b, i, j: (b, i, 0)),
            pl.BlockSpec((1, 1, d), lambda b, i, j: (b, 0, 0)),
            pl.BlockSpec((1, 1, d), lambda b, i, j: (b, 0, 0)),
            *[pl.BlockSpec((d, hp * HEAD_DIM), lambda b, i, j, kind=kind: (0, kind * (N_HEADS // hp) + j))
              for kind in range(N_KINDS)],
            pl.BlockSpec((hp, 1, HEAD_DIM), head_vec),
            pl.BlockSpec((hp, 1, HEAD_DIM), head_vec),
            pl.BlockSpec((hp, GMLP_CHUNK, GMLP_CHUNK), head_vec),
            pl.BlockSpec((hp, GMLP_CHUNK, 1), head_vec),
            pl.BlockSpec((hp, lb_h.shape[1], HEAD_DIM), head_vec),
            pl.BlockSpec((hp, 1, HEAD_DIM), head_vec),
        ],
        out_specs=[
            pl.BlockSpec((MIX_TILE, hp * HEAD_DIM), lambda b, i, j: (b * n_t + i, j)),
            pl.BlockSpec((MIX_TILE, hp * HEAD_DIM), lambda b, i, j: (b * n_t + i, j)),
            pl.BlockSpec((1, N_HEADS, HEAD_DIM, HEAD_DIM), lambda b, i, j: (b, 0, 0, 0)),
        ],
        out_shape=[
            jax.ShapeDtypeStruct((bsz * t, N_HEADS * HEAD_DIM), BF16),
            jax.ShapeDtypeStruct((bsz * t, N_HEADS * HEAD_DIM), BF16),
            jax.ShapeDtypeStruct((bsz, N_HEADS, HEAD_DIM, HEAD_DIM), F32),
        ],
        scratch_shapes=[
            pltpu.VMEM((MIX_TILE, d), BF16),
            pltpu.VMEM((MIX_TILE, N_KINDS * hp * HEAD_DIM), F32),
            pltpu.VMEM((N_HEADS, HEAD_DIM, HEAD_DIM), F32),
        ],
        compiler_params=_cparams(("arbitrary", "arbitrary", "arbitrary")),
        name="mixer_prompt",
    )(x, sh1.reshape(bsz, 1, d), sc1.reshape(bsz, 1, d), *([w_in_b] * N_KINDS), ln_v_g, ln_v_b, w_s, b_s, lb_h, gnorm_g)


def _mixer_sample_kernel(x_ref, sh_ref, sc_ref, wu_ref, wv_ref, wq_ref, wf_ref, wi_ref, wg_ref,
                         lvg_ref, lvb_ref, ws_ref, bs_ref, lb_ref, gn_ref, s0_ref,
                         cata_ref, cato_ref, state_ref, vrows_ref, h_scr, proj_scr, *, layer, seq):
    j = pl.program_id(0)
    n_streams = x_ref.shape[0] // seq

    @pl.when(j == 0)
    def _():
        def slab(r, carry):
            rows = pl.ds(pl.multiple_of(r * seq, seq), seq)
            h = _ln_plain(x_ref[rows, :]) * (1.0 + sc_ref[pl.ds(r, 1), :]) + sh_ref[pl.ds(r, 1), :]
            h_scr[rows, :] = h.astype(BF16)
            return carry
        lax.fori_loop(0, n_streams, slab, 0)

    _in_projection(h_scr, (wu_ref, wv_ref, wq_ref, wf_ref, wi_ref, wg_ref), proj_scr)

    chains = []
    for hd in range(HEADS_PER_STEP):
        wm = ws_ref[hd][0:seq, 0:seq].astype(BF16)
        b_col = bs_ref[hd][0:seq, :]
        lbv = _lower_bound(lb_ref, hd, layer)
        for n in range(n_streams):
            rows = slice(n * seq, (n + 1) * seq)
            u, vn = _gmlp_rows(proj_scr[rows, _proj_cols(0, hd)], proj_scr[rows, _proj_cols(1, hd)],
                               lvg_ref[hd], lvb_ref[hd])
            vrows_ref[rows, hd * HEAD_DIM:(hd + 1) * HEAD_DIM] = vn
            sp = _dot(wm, vn.astype(BF16)) + b_col
            cata_ref[rows, hd * HEAD_DIM:(hd + 1) * HEAD_DIM] = (u * sp).astype(BF16)
            chains.append((jnp.transpose(s0_ref[n, hd]), lbv, gn_ref[hd],
                           [tuple(proj_scr[rows, _proj_cols(kind, hd)] for kind in range(2, N_KINDS))]))

    units = _hgrn_prepare(chains, seq)
    wide_decay = _hgrn_block_decay(units, seq) > SAFE_BLOCK_DECAY

    def finish(exact):
        outs, finals = _hgrn_finish(chains, _hgrn_scores(units, seq, exact))
        for hd in range(HEADS_PER_STEP):
            for n in range(n_streams):
                r0 = n * seq
                m = hd * n_streams + n
                cato_ref[r0:r0 + seq, hd * HEAD_DIM:(hd + 1) * HEAD_DIM] = outs[m][0].astype(BF16)
                state_ref[n, hd] = jnp.transpose(finals[m])

    @pl.when(wide_decay)
    def _():
        finish(True)

    @pl.when(jnp.logical_not(wide_decay))
    def _():
        finish(False)


def _mixer_sample_call(x2d, sh1, sc1, w_in_b, ln_v_g, ln_v_b, w_s, b_s, lb_h, gnorm_g, s0, layer, seq):
    rows, d = x2d.shape
    n_streams = rows // seq
    hp = HEADS_PER_STEP
    kern = functools.partial(_mixer_sample_kernel, layer=layer, seq=seq)
    head_vec = lambda j: (j, 0, 0)
    return pl.pallas_call(
        kern,
        grid=(N_HEADS // hp,),
        in_specs=[
            pl.BlockSpec((rows, d), lambda j: (0, 0)),
            pl.BlockSpec((n_streams, d), lambda j: (0, 0)),
            pl.BlockSpec((n_streams, d), lambda j: (0, 0)),
            *_w_in_specs(d),
            pl.BlockSpec((hp, 1, HEAD_DIM), head_vec),
            pl.BlockSpec((hp, 1, HEAD_DIM), head_vec),
            pl.BlockSpec((hp, GMLP_CHUNK, GMLP_CHUNK), head_vec),
            pl.BlockSpec((hp, GMLP_CHUNK, 1), head_vec),
            pl.BlockSpec((hp, lb_h.shape[1], HEAD_DIM), head_vec),
            pl.BlockSpec((hp, 1, HEAD_DIM), head_vec),
            pl.BlockSpec((n_streams, hp, HEAD_DIM, HEAD_DIM), lambda j: (0, j, 0, 0)),
        ],
        out_specs=[
            pl.BlockSpec((rows, hp * HEAD_DIM), lambda j: (0, j)),
            pl.BlockSpec((rows, hp * HEAD_DIM), lambda j: (0, j)),
            pl.BlockSpec((n_streams, hp, HEAD_DIM, HEAD_DIM), lambda j: (0, j, 0, 0)),
            pl.BlockSpec((rows, hp * HEAD_DIM), lambda j: (0, j)),
        ],
        out_shape=[
            jax.ShapeDtypeStruct((rows, N_HEADS * HEAD_DIM), BF16),
            jax.ShapeDtypeStruct((rows, N_HEADS * HEAD_DIM), BF16),
            jax.ShapeDtypeStruct((n_streams, N_HEADS, HEAD_DIM, HEAD_DIM), F32),
            jax.ShapeDtypeStruct((rows, N_HEADS * HEAD_DIM), F32),
        ],
        scratch_shapes=[
            pltpu.VMEM((rows, d), BF16),
            pltpu.VMEM((rows, N_KINDS * hp * HEAD_DIM), F32),
        ],
        compiler_params=_cparams(("arbitrary",)),
        name="mixer_sample",
    )(x2d, sh1, sc1, *([w_in_b] * N_KINDS), ln_v_g, ln_v_b, w_s, b_s, lb_h, gnorm_g, s0)


def _post_kernel(grp_ref, cap_ref, cop_ref, cas_ref, cos_ref, xp_ref, xs_ref, g1_ref, sh2_ref, sc2_ref, wout_ref, l1g_ref, l1b_ref,
                 wr_ref, br_ref, tri_ref,
                 x1_ref, h2_ref, meta_ref, wts_ref, cnt_ref,
                 mix_scr, hhi_scr, hlo_scr, run_scr, *, n_prompt_tiles, alpha):
    i = pl.program_id(0)
    is_s = i >= n_prompt_tiles
    tm = mix_scr.shape[0]

    @pl.when(is_s)
    def _():
        mix_scr[...] = _dot(jnp.concatenate([cas_ref[...], cos_ref[...]], axis=1), wout_ref[...])

    @pl.when(jnp.logical_not(is_s))
    def _():
        mix_scr[...] = _dot(jnp.concatenate([cap_ref[...], cop_ref[...]], axis=1), wout_ref[...])

    @pl.when(i == 0)
    def _():
        run_scr[...] = jnp.zeros_like(run_scr)

    def slab(g, carry):
        rows = pl.ds(pl.multiple_of(g * MOD_GROUP, MOD_GROUP), MOD_GROUP)
        x = jnp.where(is_s, xs_ref[rows, :], xp_ref[rows, :])
        stream = pl.ds(grp_ref[0, 0, g], 1)
        z = alpha * x + g1_ref[stream, :] * mix_scr[rows, :]
        x1 = _ln_plain(z) * l1g_ref[...] + l1b_ref[...]
        x1_ref[rows, :] = x1
        h2 = _ln_plain(x1) * (1.0 + sc2_ref[stream, :]) + sh2_ref[stream, :]
        h2_ref[rows, :] = h2
        hi = h2.astype(BF16)
        hhi_scr[rows, :] = hi
        hlo_scr[rows, :] = (h2 - hi.astype(F32)).astype(BF16)
        return carry
    lax.fori_loop(0, tm // MOD_GROUP, slab, 0, unroll=SLAB_UNROLL)

    _route(hhi_scr, hlo_scr, wr_ref, br_ref, tri_ref, run_scr, meta_ref, wts_ref, cnt_ref)


def _route(hhi_scr, hlo_scr, wr_ref, br_ref, before_ref, run_scr, meta_ref, wts_ref, cnt_ref):
    tm = hhi_scr.shape[0]
    n_r = N_GROUPS + N_EXPERTS
    n_rows = -(-n_r // 8) * 8
    s = _dot(hhi_scr[...], wr_ref[...]) + _dot(hlo_scr[...], wr_ref[...])
    logits = s + pltpu.roll(s, LANES - n_r, axis=1) + br_ref[...]
    lt = jnp.transpose(logits)[0:n_rows, :]

    row = lax.broadcasted_iota(I32, (n_rows, tm), 0)
    row_f = row.astype(F32)
    neg = jnp.float32(-jnp.inf)
    big = jnp.float32(LANES)

    def first_row_of(vals, top):
        return jnp.min(jnp.where(vals == top, row_f, big), axis=0, keepdims=True)

    gl = jnp.where(row < N_GROUPS, lt, neg)
    gmax = jnp.max(gl, axis=0, keepdims=True)
    gsel = first_row_of(gl, gmax)
    p_group = 1.0 / jnp.sum(jnp.exp(gl - gmax), axis=0, keepdims=True)

    e_lo = N_GROUPS + EXPERTS_PER_GROUP * gsel.astype(I32)
    emask = (row >= e_lo) & (row < e_lo + EXPERTS_PER_GROUP)
    el = jnp.where(emask, lt, neg)
    t1 = jnp.max(el, axis=0, keepdims=True)
    i1 = first_row_of(el, t1)
    el2 = jnp.where(row_f == i1, neg, el)
    t2 = jnp.max(el2, axis=0, keepdims=True)
    i2 = first_row_of(el2, t2)
    e2 = jnp.exp(t2 - t1)
    den = 1.0 + e2
    w1 = (1.0 / den) * p_group
    w2 = (e2 / den) * p_group

    sel1 = row_f == i1
    sel2 = row_f == i2
    onehot = jnp.where(sel1 | sel2, 1.0, 0.0)
    run_col = jnp.transpose(jnp.broadcast_to(run_scr[...], (LANES, LANES)))[0:n_rows, 0:1]
    before = _dot(onehot.astype(BF16), before_ref[...]) + run_col
    r1 = jnp.sum(jnp.where(sel1, before, 0.0), axis=0, keepdims=True)
    r2 = jnp.sum(jnp.where(sel2, before, 0.0), axis=0, keepdims=True)
    added = jnp.concatenate([jnp.sum(onehot, axis=1, keepdims=True), jnp.zeros((LANES - n_rows, 1), F32)], axis=0)
    run_scr[...] = run_scr[...] + jnp.transpose(jnp.broadcast_to(added, (LANES, LANES)))[0:1, :]
    cnt_ref[...] = run_scr[...].astype(I32)

    base = jnp.float32(N_GROUPS)
    mrow = lax.broadcasted_iota(I32, (META_ROWS, tm), 0)
    meta = jnp.where(mrow == 0, i1 - base, jnp.where(mrow == 1, i2 - base, jnp.where(mrow == 2, r1, jnp.where(mrow == 3, r2, 0.0))))
    meta_ref[0] = meta.astype(I32)
    wrow = lax.broadcasted_iota(I32, (LANES, tm), 0)
    wts_ref[...] = jnp.transpose(jnp.where(wrow == 0, w1, jnp.where(wrow == 1, w2, 0.0)))


def _post_call(groups3, cats_p, cats_s, x_p, x_s, mod, w_out_b, ln1_g, ln1_b, wr, br, tri, alpha):
    n_p, d = x_p.shape
    n_s = x_s.shape[0]
    d_a = cats_p[0].shape[1]
    tm = POST_TILE
    n_pt = n_p // tm
    n_tiles = n_pt + n_s // tm
    n_tok = n_p + n_s
    groups = tm // MOD_GROUP
    kern = functools.partial(_post_kernel, n_prompt_tiles=n_pt, alpha=alpha)
    p_idx = lambda i: (jnp.minimum(i, n_pt - 1), 0)
    s_idx = lambda i: (jnp.maximum(i - n_pt, 0), 0)
    tile = lambda i: (i, 0)
    const = lambda i: (0, 0)
    once = pl.Buffered(1)
    s_mode = once if n_tiles - n_pt == 1 else None
    mod_spec = lambda m: pl.BlockSpec((mod.shape[0], d), lambda i: (0, m), pipeline_mode=once)
    return pl.pallas_call(
        kern,
        grid=(n_tiles,),
        in_specs=[
            pl.BlockSpec((1, 1, groups), lambda i: (i, 0, 0), memory_space=pltpu.SMEM),
            pl.BlockSpec((tm, d_a), p_idx),
            pl.BlockSpec((tm, d_a), p_idx),
            pl.BlockSpec((tm, d_a), s_idx, pipeline_mode=s_mode),
            pl.BlockSpec((tm, d_a), s_idx, pipeline_mode=s_mode),
            pl.BlockSpec((tm, d), p_idx),
            pl.BlockSpec((tm, d), s_idx, pipeline_mode=s_mode),
            mod_spec(2), mod_spec(3), mod_spec(4),
            pl.BlockSpec((d, d), const, pipeline_mode=once),
            pl.BlockSpec((1, d), const),
            pl.BlockSpec((1, d), const),
            pl.BlockSpec((d, LANES), const, pipeline_mode=once),
            pl.BlockSpec((1, LANES), const),
            pl.BlockSpec((tm, tm), const, pipeline_mode=once),
        ],
        out_specs=[
            pl.BlockSpec((tm, d), tile),
            pl.BlockSpec((tm, d), tile),
            pl.BlockSpec((1, META_ROWS, tm), lambda i: (i, 0, 0)),
            pl.BlockSpec((tm, LANES), tile),
            pl.BlockSpec((1, LANES), const),
        ],
        out_shape=[
            jax.ShapeDtypeStruct((n_tok, d), F32),
            jax.ShapeDtypeStruct((n_tok, d), F32),
            jax.ShapeDtypeStruct((n_tiles, META_ROWS, tm), I32),
            jax.ShapeDtypeStruct((n_tok, LANES), F32),
            jax.ShapeDtypeStruct((1, LANES), I32),
        ],
        scratch_shapes=[
            pltpu.VMEM((tm, d), F32),
            pltpu.VMEM((tm, d), BF16),
            pltpu.VMEM((tm, d), BF16),
            pltpu.VMEM((1, LANES), F32),
        ],
        compiler_params=_cparams(("arbitrary",)),
        name="post_router",
    )(groups3, *cats_p, *cats_s, x_p, x_s, mod, mod, mod, w_out_b, ln1_g, ln1_b, wr, br, tri)


def _dispatch_kernel(slot_ref, gap_start_ref, gap_len_ref, h_ref, xs_ref, zero_scr, sem, gap_sem, tile_sem, *,
                     gaps_per_step):
    i = pl.program_id(0)
    tm = h_ref.shape[0]
    n_gaps = gap_len_ref.shape[0]
    tile_rows = zero_scr.shape[0]

    @pl.when(i == 0)
    def _():
        zero_scr[...] = jnp.zeros_like(zero_scr)

    def row_copy(r, k):
        return pltpu.make_async_copy(h_ref.at[pl.ds(r, 1)], xs_ref.at[pl.ds(slot_ref[0, 0, k * tm + r], 1)], sem)

    def start_rows(r, carry):
        row_copy(r, 0).start(priority=0)
        row_copy(r, 1).start(priority=1)
        return carry
    lax.fori_loop(0, tm, start_rows, 0, unroll=ROW_DMA_UNROLL)

    def gap_head(g):
        return jnp.where(g < n_gaps - 1, (-gap_start_ref[g]) & (GAP_ROWS - 1), 0)

    def gap_chunks(g):
        return jnp.where(g < n_gaps - 1, (gap_len_ref[g] - gap_head(g)) // GAP_ROWS, 0)

    def gap_tiles(g):
        return jnp.where(g == n_gaps - 1, gap_len_ref[n_gaps - 1] // tile_rows, 0)

    def zero_row_copy(g, r):
        return pltpu.make_async_copy(zero_scr.at[pl.ds(0, 1)], xs_ref.at[pl.ds(gap_start_ref[g] + r, 1)], gap_sem)

    def zero_chunk_copy(g, q):
        start = pl.multiple_of(gap_start_ref[g] + gap_head(g) + q * GAP_ROWS, GAP_ROWS)
        return pltpu.make_async_copy(zero_scr.at[pl.ds(0, GAP_ROWS)], xs_ref.at[pl.ds(start, GAP_ROWS)], gap_sem)

    def zero_tile_copy(g, t):
        start = pl.multiple_of(gap_start_ref[g] + t * tile_rows, tile_rows)
        return pltpu.make_async_copy(zero_scr, xs_ref.at[pl.ds(start, tile_rows)], tile_sem)

    gap_copies = ((gap_head, zero_row_copy), (gap_chunks, zero_chunk_copy), (gap_tiles, zero_tile_copy))

    def for_gap_copies(act):
        for j in range(gaps_per_step):
            g = jnp.minimum(i * gaps_per_step + j, n_gaps - 1)
            live = i * gaps_per_step + j < n_gaps
            for count, copy in gap_copies:
                lax.fori_loop(0, jnp.where(live, count(g), 0), lambda q, c, g=g, copy=copy: (act(copy(g, q)), c)[1], 0)

    for_gap_copies(lambda copy: copy.start())
    for _ in range(2 * tm // tile_rows):
        pltpu.make_async_copy(zero_scr, zero_scr, sem).wait()
    for_gap_copies(lambda copy: copy.wait())


def _dispatch_call(slots3, gap_start, gap_len, h2, n_sorted):
    n_tok, d = h2.shape
    n_tiles = n_tok // TOK_TILE
    n_gaps = gap_len.shape[0]
    kern = functools.partial(_dispatch_kernel, gaps_per_step=-(-n_gaps // n_tiles))
    return pl.pallas_call(
        kern,
        grid=(n_tiles,),
        in_specs=[
            pl.BlockSpec((1, 1, slots3.shape[2]), lambda i: (i, 0, 0), memory_space=pltpu.SMEM),
            pl.BlockSpec((n_gaps,), lambda i: (0,), memory_space=pltpu.SMEM),
            pl.BlockSpec((n_gaps,), lambda i: (0,), memory_space=pltpu.SMEM),
            pl.BlockSpec((TOK_TILE, d), lambda i: (i, 0)),
        ],
        out_specs=pl.BlockSpec(memory_space=pl.ANY),
        out_shape=jax.ShapeDtypeStruct((n_sorted, d), F32),
        scratch_shapes=[pltpu.VMEM((EXP_TILE, d), F32)] + [pltpu.SemaphoreType.DMA(())] * 3,
        compiler_params=_cparams(("arbitrary",)),
        name="dispatch",
    )(slots3, gap_start, gap_len, h2)


def _expert_kernel(te_ref, nt_ref, nxt_ref, run_ref, xs_ref, w1_ref, w3_ref, w2_ref, ys_ref,
                   x_scr, w1_scr, w3_scr, w2f_scr, w13_scr, w2_scr, xsem, wsem):
    t = pl.program_id(0)
    n_t = pl.num_programs(0)
    rows = x_scr.shape[1]
    d_exp = w1_scr.shape[2]
    x_slots = x_scr.shape[0]
    xs = lax.rem(t, x_slots)
    e = te_ref[t]
    prev = te_ref[jnp.maximum(t - 1, 0)]
    changed = (t == 0) | (e != prev)
    valid = t < nt_ref[0]
    wslot = lax.rem(run_ref[t], 2)

    def x_copy(tile, slot):
        start = pl.multiple_of(tile * rows, rows)
        return pltpu.make_async_copy(xs_ref.at[pl.ds(start, rows)], x_scr.at[slot], xsem.at[slot])

    def w_copies(expert, slot):
        return (pltpu.make_async_copy(w1_ref.at[expert], w1_scr.at[slot], wsem.at[slot]),
                pltpu.make_async_copy(w3_ref.at[expert], w3_scr.at[slot], wsem.at[slot]),
                pltpu.make_async_copy(w2_ref.at[expert], w2f_scr.at[slot], wsem.at[slot]))

    @pl.when(t == 0)
    def _():
        for ahead in range(X_AHEAD):
            x_copy(ahead, ahead).start()
        for c in w_copies(e, wslot):
            c.start()

    @pl.when(t + X_AHEAD < n_t)
    def _():
        x_copy(t + X_AHEAD, lax.rem(t + X_AHEAD, x_slots)).start()

    @pl.when(changed)
    def _():
        for c in w_copies(e, wslot):
            c.wait()
        nxt = nxt_ref[t]

        @pl.when(nxt != e)
        def _():
            for c in w_copies(nxt, 1 - wslot):
                c.start()
        w13_scr[:, 0:d_exp] = w1_scr[wslot].astype(BF16)
        w13_scr[:, d_exp:2 * d_exp] = w3_scr[wslot].astype(BF16)
        w2_scr[...] = w2f_scr[wslot].astype(BF16)

    x_copy(t, xs).wait()

    @pl.when(valid)
    def _():
        h13 = _dot(x_scr[xs].astype(BF16), w13_scr[...])
        hm = jax.nn.silu(h13[:, 0:d_exp]) * h13[:, d_exp:2 * d_exp]
        _store_rows(ys_ref, (), 0, _dot(hm.astype(BF16), w2_scr[...]))

    @pl.when(jnp.logical_not(valid))
    def _():
        ys_ref[...] = jnp.zeros_like(ys_ref)


def _expert_call(tile_expert, n_valid, next_expert, run_id, xs, w1, w3, w2):
    n_sorted, d = xs.shape
    n_exp, _, d_exp = w1.shape
    n_tiles = n_sorted // EXP_TILE
    assert n_tiles > X_AHEAD
    grid_spec = pltpu.PrefetchScalarGridSpec(
        num_scalar_prefetch=4,
        grid=(n_tiles,),
        in_specs=[pl.BlockSpec(memory_space=pl.ANY)] * 4,
        out_specs=pl.BlockSpec((EXP_TILE * _lines_per_row(d), LANES), lambda t, te, nt, nx, rn: (t, 0)),
        scratch_shapes=[
            pltpu.VMEM((X_AHEAD + 1, EXP_TILE, d), F32),
            pltpu.VMEM((2, d, d_exp), F32),
            pltpu.VMEM((2, d, d_exp), F32),
            pltpu.VMEM((2, d_exp, d), F32),
            pltpu.VMEM((d, 2 * d_exp), BF16),
            pltpu.VMEM((d_exp, d), BF16),
            pltpu.SemaphoreType.DMA((X_AHEAD + 1,)),
            pltpu.SemaphoreType.DMA((2,)),
        ],
    )
    return pl.pallas_call(
        _expert_kernel,
        grid_spec=grid_spec,
        out_shape=jax.ShapeDtypeStruct((n_sorted * _lines_per_row(d), LANES), F32),
        compiler_params=_cparams(("arbitrary",)),
        name="experts",
    )(tile_expert, n_valid, next_expert, run_id, xs, w1, w3, w2)


def _combine_kernel(scur_ref, snext_ref, grp_ref, x1_ref, wts_ref, g2_ref, l2g_ref, l2b_ref, ys_ref, outp_ref, outs_ref,
                    y_scr, sem, *, n_prompt_tiles, alpha):
    i = pl.program_id(0)
    n_i = pl.num_programs(0)
    is_s = i >= n_prompt_tiles
    tm, d = x1_ref.shape
    lpr = _lines_per_row(d)
    cur = lax.rem(i, 2)

    def row_lines(r):
        return pl.ds(pl.multiple_of(r * lpr, lpr), lpr)

    def start_gathers(s_ref, buf):
        def body(r, carry):
            for k in range(2):
                pltpu.make_async_copy(ys_ref.at[row_lines(s_ref[0, 0, k * tm + r])], y_scr.at[buf, k, row_lines(r)],
                                      sem.at[buf, k]).start(priority=k)
            return carry
        lax.fori_loop(0, tm, body, 0, unroll=ROW_DMA_UNROLL // 2)

    @pl.when(i == 0)
    def _():
        start_gathers(scur_ref, 0)

    @pl.when(i + 1 < n_i)
    def _():
        start_gathers(snext_ref, 1 - cur)

    for k in range(2):
        pltpu.make_async_copy(y_scr.at[cur, k], y_scr.at[cur, k], sem.at[cur, k]).wait()

    def slab(g, carry, out_ref):
        rows = pl.ds(pl.multiple_of(g * MOD_GROUP, MOD_GROUP), MOD_GROUP)
        w = wts_ref[rows, :]
        moe = (w[:, 0:1] * _load_rows(y_scr, (cur, 0), g * MOD_GROUP, MOD_GROUP, d)
               + w[:, 1:2] * _load_rows(y_scr, (cur, 1), g * MOD_GROUP, MOD_GROUP, d))
        z = alpha * x1_ref[rows, :] + g2_ref[pl.ds(grp_ref[0, 0, g], 1), :] * moe
        out_ref[rows, :] = _ln_plain(z) * l2g_ref[...] + l2b_ref[...]
        return carry

    @pl.when(is_s)
    def _():
        lax.fori_loop(0, tm // MOD_GROUP, functools.partial(slab, out_ref=outs_ref), 0, unroll=SLAB_UNROLL)

    @pl.when(jnp.logical_not(is_s))
    def _():
        lax.fori_loop(0, tm // MOD_GROUP, functools.partial(slab, out_ref=outp_ref), 0, unroll=SLAB_UNROLL)


def _combine_call(slots3, groups3, x1, wts, mod, ln2_g, ln2_b, ys, n_p, alpha):
    n_tok, d = x1.shape
    n_tiles = n_tok // TOK_TILE
    n_pt = n_p // TOK_TILE
    groups = TOK_TILE // MOD_GROUP
    kern = functools.partial(_combine_kernel, n_prompt_tiles=n_pt, alpha=alpha)
    slot_tile = lambda ahead: pl.BlockSpec(
        (1, 1, slots3.shape[2]), lambda i: (jnp.minimum(i + ahead, n_tiles - 1), 0, 0), memory_space=pltpu.SMEM)
    return pl.pallas_call(
        kern,
        grid=(n_tiles,),
        in_specs=[
            slot_tile(0), slot_tile(1),
            pl.BlockSpec((1, 1, groups), lambda i: (i, 0, 0), memory_space=pltpu.SMEM),
            pl.BlockSpec((TOK_TILE, d), lambda i: (i, 0)),
            pl.BlockSpec((TOK_TILE, LANES), lambda i: (i, 0)),
            pl.BlockSpec((mod.shape[0], d), lambda i: (0, 5)),
            pl.BlockSpec((1, d), lambda i: (0, 0)),
            pl.BlockSpec((1, d), lambda i: (0, 0)),
            pl.BlockSpec(memory_space=pl.ANY),
        ],
        out_specs=[
            pl.BlockSpec((TOK_TILE, d), lambda i: (jnp.minimum(i, n_pt - 1), 0)),
            pl.BlockSpec((TOK_TILE, d), lambda i: (jnp.maximum(i - n_pt, 0), 0)),
        ],
        out_shape=[
            jax.ShapeDtypeStruct((n_p, d), F32),
            jax.ShapeDtypeStruct((n_tok - n_p, d), F32),
        ],
        scratch_shapes=[
            pltpu.VMEM((2, 2, TOK_TILE * _lines_per_row(d), LANES), F32),
            pltpu.SemaphoreType.DMA((2, 2)),
        ],
        compiler_params=_cparams(("arbitrary",)),
        name="combine",
    )(slots3, slots3, groups3, x1, wts, mod, ln2_g, ln2_b, ys)


def _routing_tables(meta, cnt_row, n_sorted):
    experts = meta[:, 0:2, :]
    ranks = meta[:, 2:4, :]
    cnt = cnt_row[0, N_GROUPS:N_GROUPS + N_EXPERTS]
    padded = ((cnt + EXP_TILE - 1) // EXP_TILE) * EXP_TILE
    ends = jnp.cumsum(padded)
    offs = ends - padded
    slots = offs[experts] + ranks
    total = ends[-1]
    n_tiles = n_sorted // EXP_TILE
    tile_ids = jnp.arange(n_tiles, dtype=I32)
    tile_expert = jnp.minimum(
        jnp.sum((tile_ids[:, None] >= (ends // EXP_TILE)[None, :]).astype(I32), axis=1), N_EXPERTS - 1)
    n_valid = (total // EXP_TILE).reshape(1).astype(I32)
    ids = jnp.arange(N_EXPERTS, dtype=I32)
    later = (ids[None, :] > tile_expert[:, None]) & (padded[None, :] > 0)
    next_expert = jnp.min(jnp.where(later, ids[None, :], N_EXPERTS), axis=1)
    next_expert = jnp.where(next_expert == N_EXPERTS, tile_expert, next_expert).astype(I32)
    starts_run = jnp.concatenate([jnp.zeros((1,), I32), (tile_expert[1:] != tile_expert[:-1]).astype(I32)])
    run_id = jnp.cumsum(starts_run).astype(I32)
    gap_start = jnp.concatenate([offs + cnt, total.reshape(1)]).astype(I32)
    gap_len = jnp.concatenate([padded - cnt, (n_sorted - total).reshape(1)]).astype(I32)
    return slots.astype(I32), (tile_expert, n_valid, next_expert, run_id), gap_start, gap_len


def _layer(layer, n_layers, xp, xs, s0_l, c_all, p):
    (w_ada, b_ada, w_in, ln_v_g, ln_v_b, w_s, b_s, hgrn_lb, gnorm_g, w_out, ln1_g, ln1_b,
     w_rg, b_rg, w_re, b_re, w1, w3, w2, ln2_g, ln2_b) = p
    bsz, t, d = xp.shape
    n_streams, seq, _ = xs.shape
    alpha = float((2.0 * n_layers) ** 0.25)
    n_p = bsz * t
    n_s = n_streams * seq
    n_tok = n_p + n_s

    n_c = c_all.shape[0]
    c_pad = jnp.pad(c_all, ((0, (-n_c) % 8), (0, 0)))
    mod = _ada_call(c_pad, w_ada, b_ada)
    sh1, sc1 = mod[:n_c, 0:d], mod[:n_c, d:2 * d]

    w_in_b = w_in.astype(BF16)
    w_out_b = w_out.astype(BF16)
    lvg = ln_v_g.reshape(N_HEADS, 1, HEAD_DIM)
    lvb = ln_v_b.reshape(N_HEADS, 1, HEAD_DIM)
    b_s3 = b_s.reshape(N_HEADS, GMLP_CHUNK, 1)
    lb_h = jnp.transpose(hgrn_lb.reshape(hgrn_lb.shape[0], N_HEADS, HEAD_DIM), (1, 0, 2))
    gn = gnorm_g.reshape(N_HEADS, 1, HEAD_DIM)

    ca_p, co_p, state_p = _mixer_prompt_call(xp, sh1[:bsz], sc1[:bsz], w_in_b, lvg, lvb, w_s, b_s3, lb_h, gn, layer)
    ca_s, co_s, state_s, vrows = _mixer_sample_call(xs.reshape(n_s, d), sh1[bsz:], sc1[bsz:], w_in_b, lvg, lvb, w_s,
                                               b_s3, lb_h, gn, s0_l, layer, seq)

    group_stream = jnp.concatenate([
        jnp.repeat(jnp.arange(bsz, dtype=I32), t // MOD_GROUP),
        bsz + jnp.repeat(jnp.arange(n_streams, dtype=I32), seq // MOD_GROUP)])

    wr = jnp.concatenate([w_rg, w_re], axis=1)
    wr_hi = wr.astype(BF16)
    wr_lo = (wr - wr_hi.astype(F32)).astype(BF16)
    n_r = wr.shape[1]
    wr_cat = jnp.concatenate([wr_hi, wr_lo, jnp.zeros((d, LANES - 2 * n_r), BF16)], axis=1)
    br = jnp.pad(jnp.concatenate([b_rg, b_re]), (0, LANES - n_r)).reshape(1, LANES)
    tri = jnp.triu(jnp.ones((POST_TILE, POST_TILE), F32), 1).astype(BF16)

    x1, h2, meta, wts, cnt_row = _post_call(group_stream.reshape(n_tok // POST_TILE, 1, POST_TILE // MOD_GROUP),
                                            (ca_p, co_p), (ca_s, co_s), xp.reshape(n_p, d), xs.reshape(n_s, d),
                                            mod, w_out_b, ln1_g.reshape(1, d), ln1_b.reshape(1, d), wr_cat, br, tri, alpha)

    n_sorted = 2 * n_tok + N_EXPERTS * EXP_TILE
    slots, tile_tables, gap_start, gap_len = _routing_tables(meta, cnt_row, n_sorted)
    n_tiles = n_tok // TOK_TILE
    assert POST_TILE == TOK_TILE
    slots3 = slots.reshape(n_tiles, 1, 2 * TOK_TILE)

    xs_sorted = _dispatch_call(slots3, gap_start, gap_len, h2, n_sorted)
    ys_sorted = _expert_call(*tile_tables, xs_sorted, w1, w3, w2)
    yp, ys_out = _combine_call(slots3, group_stream.reshape(n_tiles, 1, TOK_TILE // MOD_GROUP), x1, wts, mod,
                               ln2_g.reshape(1, d), ln2_b.reshape(1, d), ys_sorted, n_p, alpha)

    v_rows = vrows.reshape(n_streams, seq, N_HEADS, HEAD_DIM)
    return yp.reshape(bsz, t, d), ys_out.reshape(n_streams, seq, d), state_p, state_s, v_rows


def kernel(x_prompt, x_sample, state_hgrn, c_prompt, c_sample, w_ada, b_ada, w_in, ln_v_g, ln_v_b, w_s, b_s, hgrn_lb, gnorm_g, w_out, ln1_g, ln1_b, w_router_g, b_router_g, w_router_e, b_router_e, w1, w3, w2, ln2_g, ln2_b):
    n_layers = w_ada.shape[0]
    assert x_prompt.shape[1] % MIX_TILE == 0 and x_prompt.shape[2] == 2 * N_HEADS * HEAD_DIM
    assert x_sample.shape[1] % MOD_GROUP == 0 and x_sample.shape[1] <= SUB_CHUNK
    assert (x_sample.shape[0] * x_sample.shape[1]) % TOK_TILE == 0 and TOK_TILE % POST_TILE == 0
    c_all = jnp.concatenate([c_prompt, c_sample], axis=0)
    xp, xs = x_prompt, x_sample
    sp_list, ss_list, vs_list = [], [], []
    for l in range(n_layers):
        p = (w_ada[l], b_ada[l], w_in[l], ln_v_g[l], ln_v_b[l], w_s[l], b_s[l], hgrn_lb, gnorm_g[l], w_out[l],
             ln1_g[l], ln1_b[l], w_router_g[l], b_router_g[l], w_router_e[l], b_router_e[l],
             w1[l], w3[l], w2[l], ln2_g[l], ln2_b[l])
        xp, xs, sp, ss, vs = _layer(l, n_layers, xp, xs, state_hgrn[l], c_all, p)
        sp_list.append(sp.astype(state_hgrn.dtype))
        ss_list.append(ss.astype(state_hgrn.dtype))
        vs_list.append(vs)
    return (xp, xs, jnp.stack(sp_list, axis=0), jnp.stack(ss_list, axis=0), jnp.stack(vs_list, axis=0))
```

```python
import functools

import jax
import jax.numpy as jnp
from jax import lax
from jax.experimental import pallas as pl
from jax.experimental.pallas import tpu as pltpu

F32 = jnp.float32
BF16 = jnp.bfloat16
I32 = jnp.int32

N_HEADS = 8
HEAD_DIM = 128
GMLP_CHUNK = 128
SUB_CHUNK = 64
N_GROUPS = 4
EXPERTS_PER_GROUP = 8
N_EXPERTS = N_GROUPS * EXPERTS_PER_GROUP
LN_EPS = 1e-5
N_KINDS = 6

LANES = 128
MIX_TILE = 512
HEADS_PER_STEP = 2
HGRN_BLOCK = 64
DIAG_BLOCK = 16
SAFE_BLOCK_DECAY = 60.0
TOK_TILE = 512
POST_TILE = 512
META_ROWS = 8
MOD_GROUP = 32
SLAB_UNROLL = 4
EXP_TILE = 256
X_AHEAD = 2
ROW_DMA_UNROLL = 8
GAP_ROWS = 8
ADA_TILE = 1024
VMEM_LIMIT = 56 * 1024 * 1024


def _cparams(sem):
    return pltpu.CompilerParams(dimension_semantics=sem, vmem_limit_bytes=VMEM_LIMIT)


def _ln_plain(x):
    mu = jnp.mean(x, axis=-1, keepdims=True)
    xc = x - mu
    var = jnp.mean(xc * xc, axis=-1, keepdims=True)
    return xc * lax.rsqrt(var + LN_EPS)


def _dot(a, b):
    return jnp.dot(a, b, preferred_element_type=F32)


def _dot_nt(a, b):
    return lax.dot_general(a, b, (((1,), (1,)), ((), ())), preferred_element_type=F32)


def _lines_per_row(d):
    return d // LANES


def _load_rows(ref, lead, row0, n_rows, d):
    lpr = _lines_per_row(d)
    parts = [ref[lead + (pl.ds(row0 * lpr + c, n_rows, stride=lpr), slice(None))] for c in range(lpr)]
    return jnp.concatenate(parts, axis=1)


def _store_rows(ref, lead, row0, val):
    n_rows, d = val.shape
    lpr = _lines_per_row(d)
    for c in range(lpr):
        ref[lead + (pl.ds(row0 * lpr + c, n_rows, stride=lpr), slice(None))] = val[:, c * LANES:(c + 1) * LANES]


def _ada_kernel(c_ref, w_ref, b_ref, o_ref):
    s = jax.nn.silu(c_ref[...]).astype(BF16)
    o_ref[...] = _dot(s, w_ref[...].astype(BF16)) + b_ref[...]


def _ada_call(c_pad, w_ada, b_ada):
    rows, d = c_pad.shape
    n_out = w_ada.shape[1]
    return pl.pallas_call(
        _ada_kernel,
        grid=(n_out // ADA_TILE,),
        in_specs=[
            pl.BlockSpec((rows, d), lambda n: (0, 0)),
            pl.BlockSpec((d, ADA_TILE), lambda n: (0, n)),
            pl.BlockSpec((1, ADA_TILE), lambda n: (0, n)),
        ],
        out_specs=pl.BlockSpec((rows, ADA_TILE), lambda n: (0, n)),
        out_shape=jax.ShapeDtypeStruct((rows, n_out), F32),
        compiler_params=_cparams(("arbitrary",)),
        name="adaln",
    )(c_pad, w_ada, b_ada.reshape(1, n_out))


def _lower_bound(lb_ref, hd, layer):
    raw = lb_ref[hd]
    m = jnp.max(raw, axis=0, keepdims=True)
    e = jnp.exp(raw - m)
    p = e / jnp.sum(e, axis=0, keepdims=True)
    return jnp.sum(p[: layer + 1], axis=0, keepdims=True)


def _gmlp_rows(u_pre, v_pre, ln_g, ln_b):
    u = jax.nn.gelu(u_pre)
    v = jax.nn.gelu(v_pre)
    mu = jnp.mean(v, axis=-1, keepdims=True)
    vc = v - mu
    var = jnp.mean(vc * vc, axis=-1, keepdims=True)
    vn = vc * lax.rsqrt(var + LN_EPS) * ln_g + ln_b
    return u, vn


def _row_bcast(a, row, n):
    return jnp.broadcast_to(a[row:row + 1, :], (n, a.shape[1]))


def _block_id(idx, size):
    return lax.shift_right_logical(idx, I32(size.bit_length() - 1))


def _hgrn_masks(c):
    row = lax.broadcasted_iota(I32, (c, c), 0)
    col = lax.broadcasted_iota(I32, (c, c), 1)
    masks = []
    half = c // 2
    while half >= DIAG_BLOCK:
        span = 2 * half
        same = _block_id(row, span) == _block_id(col, span)
        masks.append(same & ((row & (span - 1)) >= half) & ((col & (span - 1)) < half))
        half //= 2
    diag = (_block_id(row, DIAG_BLOCK) == _block_id(col, DIAG_BLOCK)) & (col <= row)
    return masks, diag


def _tri_ones(c):
    row = lax.broadcasted_iota(I32, (c, c), 0)
    col = lax.broadcasted_iota(I32, (c, c), 1)
    return jnp.where(col <= row, 1.0, 0.0).astype(BF16)


def _level_halves(c):
    halves = []
    half = c // 2
    while half >= 1:
        halves.append(half)
        half //= 2
    return halves


def _level_masks(c):
    row = lax.broadcasted_iota(I32, (c, c), 0)
    col = lax.broadcasted_iota(I32, (c, c), 1)
    masks = []
    for half in _level_halves(c):
        span = 2 * half
        same = _block_id(row, span) == _block_id(col, span)
        masks.append(same & ((row & (span - 1)) >= half) & ((col & (span - 1)) < half))
    return masks, row == col


def _level_ref_sums(c):
    row = lax.broadcasted_iota(I32, (c, c), 0)
    col = lax.broadcasted_iota(I32, (c, c), 1)
    pieces = [col <= _block_id(row, 2 * half) * (2 * half) + half for half in _level_halves(c)]
    return jnp.concatenate([jnp.where(p, 1.0, 0.0).astype(BF16) for p in pieces], axis=0)


def _hgrn_block_decay(units, c):
    worst = None
    for u in units:
        a = u["a"]
        for b in range(c // DIAG_BLOCK):
            span = a[b * DIAG_BLOCK:b * DIAG_BLOCK + 1, :] - a[(b + 1) * DIAG_BLOCK - 1:(b + 1) * DIAG_BLOCK, :]
            worst = span if worst is None else jnp.maximum(worst, span)
    return jnp.max(worst)


def _hgrn_prepare(chains, c):
    tri = _tri_ones(c)
    units = []
    for st0, lbv, gn, blocks in chains:
        for q_pre, f_pre, i_pre, g_pre in blocks:
            q = jax.nn.silu(q_pre)
            fg = lbv + (1.0 - lbv) * jax.nn.sigmoid(f_pre)
            logf = jnp.log(fg)
            hi = logf.astype(BF16)
            lo = (logf - hi.astype(F32)).astype(BF16)
            units.append(dict(q=q, k=1.0 - fg, hilo=jnp.concatenate([hi, lo], axis=1),
                              v=i_pre, g=g_pre, gn=gn))

    for u in units:
        p = _dot(tri, u["hilo"])
        u["a"] = p[:, 0:HEAD_DIM] + p[:, HEAD_DIM:2 * HEAD_DIM]
    return units


def _hgrn_scores(units, c, exact):
    units = [dict(u) for u in units]
    if exact:
        masks, eye = _level_masks(c)
        sums = _level_ref_sums(c)
    else:
        masks, diag_mask = _hgrn_masks(c)
    for u in units:
        q, k, a = u["q"], u["k"], u["a"]
        parts = []
        if exact:
            p = _dot(sums, u["hilo"])
            p = p[:, 0:HEAD_DIM] + p[:, HEAD_DIM:2 * HEAD_DIM]
            for n, mask in enumerate(masks):
                d = a - p[n * c:(n + 1) * c]
                e = jnp.exp(jnp.minimum(d, -d))
                parts.append((mask, _dot_nt((q * e).astype(BF16), (k * e).astype(BF16))))
            parts.append((eye, jnp.sum(q * k, axis=-1, keepdims=True)))
        else:
            half = c // 2
            level = 0
            while half >= DIAG_BLOCK:
                span = 2 * half
                ref = jnp.concatenate([_row_bcast(a, b * span + half, span) for b in range(c // span)], axis=0)
                qs = (q * jnp.exp(jnp.minimum(a - ref, 0.0))).astype(BF16)
                ks = (k * jnp.exp(jnp.minimum(ref - a, 0.0))).astype(BF16)
                parts.append((masks[level], _dot_nt(qs, ks)))
                half //= 2
                level += 1
            ref = jnp.concatenate([_row_bcast(a, b * DIAG_BLOCK, DIAG_BLOCK) for b in range(c // DIAG_BLOCK)], axis=0)
            qd = (q * jnp.exp(a - ref)).astype(BF16)
            kd = (k * jnp.exp(jnp.minimum(ref - a, SAFE_BLOCK_DECAY))).astype(BF16)
            parts.append((diag_mask, _dot_nt(qd, kd)))
        u["parts"] = parts
        a_last = a[c - 1:c, :]
        u["decay"] = jnp.exp(a_last)
        kl = (k * jnp.exp(a_last - a)).astype(BF16)
        v_t = jnp.transpose(u["v"]).astype(BF16)
        u["upd"] = _dot(v_t, kl)
        u["qe"] = (q * jnp.exp(a)).astype(BF16)
    return units


def _hgrn_finish(chains, units):
    finals = []
    n = 0
    for st0, lbv, gn, blocks in chains:
        st = st0
        for _ in blocks:
            units[n]["st_in"] = st.astype(BF16)
            st = st * units[n]["decay"] + units[n]["upd"]
            n += 1
        finals.append(st)

    outs = []
    n = 0
    for st0, lbv, gn, blocks in chains:
        chain_out = []
        for _ in blocks:
            u = units[n]
            scores = None
            for mask, part in u["parts"]:
                part = jnp.where(mask, part, 0.0)
                scores = part if scores is None else scores + part
            o = _dot(scores.astype(BF16), u["v"].astype(BF16)) + _dot_nt(u["qe"], u["st_in"])
            o = o * lax.rsqrt(jnp.mean(o * o, axis=-1, keepdims=True) + LN_EPS) * u["gn"]
            chain_out.append(o * jax.nn.silu(u["g"]))
            n += 1
        outs.append(chain_out)
    return outs, finals


def _proj_cols(kind, hd):
    c0 = (kind * HEADS_PER_STEP + hd) * HEAD_DIM
    return slice(c0, c0 + HEAD_DIM)


def _in_projection(h_scr, w_refs, proj_scr):
    width = HEADS_PER_STEP * HEAD_DIM
    for kind, w_ref in enumerate(w_refs):
        proj_scr[:, kind * width:(kind + 1) * width] = _dot(h_scr[...], w_ref[...])


def _mixer_prompt_kernel(x_ref, sh_ref, sc_ref, wu_ref, wv_ref, wq_ref, wf_ref, wi_ref, wg_ref,
                         lvg_ref, lvb_ref, ws_ref, bs_ref, lb_ref, gn_ref,
                         cata_ref, cato_ref, state_ref, h_scr, proj_scr, st_scr, *, layer):
    i = pl.program_id(1)
    j = pl.program_id(2)
    tm = h_scr.shape[0]

    @pl.when(j == 0)
    def _():
        def slab(r, carry):
            rows = pl.ds(pl.multiple_of(r * HGRN_BLOCK, HGRN_BLOCK), HGRN_BLOCK)
            h = _ln_plain(x_ref[0, rows, :]) * (1.0 + sc_ref[0]) + sh_ref[0]
            h_scr[rows, :] = h.astype(BF16)
            return carry
        lax.fori_loop(0, tm // HGRN_BLOCK, slab, 0, unroll=2)

    @pl.when((i == 0) & (j == 0))
    def _():
        st_scr[...] = jnp.zeros_like(st_scr)

    _in_projection(h_scr, (wu_ref, wv_ref, wq_ref, wf_ref, wi_ref, wg_ref), proj_scr)

    row = lax.broadcasted_iota(I32, (GMLP_CHUNK, GMLP_CHUNK), 0)
    col = lax.broadcasted_iota(I32, (GMLP_CHUNK, GMLP_CHUNK), 1)
    causal = _block_id(row, SUB_CHUNK) >= _block_id(col, SUB_CHUNK)
    for hd in range(HEADS_PER_STEP):
        wm = jnp.where(causal, ws_ref[hd], 0.0).astype(BF16)
        b_col = bs_ref[hd]
        for c in range(tm // GMLP_CHUNK):
            rows = slice(c * GMLP_CHUNK, (c + 1) * GMLP_CHUNK)
            u, vn = _gmlp_rows(proj_scr[rows, _proj_cols(0, hd)], proj_scr[rows, _proj_cols(1, hd)],
                               lvg_ref[hd], lvb_ref[hd])
            sp = _dot(wm, vn.astype(BF16)) + b_col
            cata_ref[rows, hd * HEAD_DIM:(hd + 1) * HEAD_DIM] = (u * sp).astype(BF16)

    chains = []
    for hd in range(HEADS_PER_STEP):
        blocks = [tuple(proj_scr[c * HGRN_BLOCK:(c + 1) * HGRN_BLOCK, _proj_cols(kind, hd)] for kind in range(2, N_KINDS))
                  for c in range(tm // HGRN_BLOCK)]
        chains.append((st_scr[HEADS_PER_STEP * j + hd], _lower_bound(lb_ref, hd, layer), gn_ref[hd], blocks))
    units = _hgrn_prepare(chains, HGRN_BLOCK)
    wide_decay = _hgrn_block_decay(units, HGRN_BLOCK) > SAFE_BLOCK_DECAY

    def finish(exact):
        outs, finals = _hgrn_finish(chains, _hgrn_scores(units, HGRN_BLOCK, exact))
        for hd in range(HEADS_PER_STEP):
            for c, out in enumerate(outs[hd]):
                cato_ref[c * HGRN_BLOCK:(c + 1) * HGRN_BLOCK, hd * HEAD_DIM:(hd + 1) * HEAD_DIM] = out.astype(BF16)
            st_scr[HEADS_PER_STEP * j + hd] = finals[hd]
            state_ref[0, HEADS_PER_STEP * j + hd] = jnp.transpose(finals[hd])

    @pl.when(wide_decay)
    def _():
        finish(True)

    @pl.when(jnp.logical_not(wide_decay))
    def _():
        finish(False)


def _w_in_specs(d):
    width = HEADS_PER_STEP * HEAD_DIM
    blocks_per_kind = N_HEADS // HEADS_PER_STEP
    return [pl.BlockSpec((d, width), lambda j, kind=kind: (0, kind * blocks_per_kind + j)) for kind in range(N_KINDS)]


def _mixer_prompt_call(x, sh1, sc1, w_in_b, ln_v_g, ln_v_b, w_s, b_s, lb_h, gnorm_g, layer):
    bsz, t, d = x.shape
    n_t = t // MIX_TILE
    hp = HEADS_PER_STEP
    kern = functools.partial(_mixer_prompt_kernel, layer=layer)
    head_vec = lambda b, i, j: (j, 0, 0)
    return pl.pallas_call(
        kern,
        grid=(bsz, n_t, N_HEADS // hp),
        in_specs=[
            pl.BlockSpec((1, MIX_TILE, d), lambda b, i, j: (b, i, 0)),
            pl.BlockSpec((1, 1, d), lambda b, i, j: (b, 0, 0)),
            pl.BlockSpec((1, 1, d), lambda b, i, j: (b, 0, 0)),
            *[pl.BlockSpec((d, hp * HEAD_DIM), lambda b, i, j, kind=kind: (0, kind * (N_HEADS // hp) + j))
              for kind in range(N_KINDS)],
            pl.BlockSpec((hp, 1, HEAD_DIM), head_vec),
            pl.BlockSpec((hp, 1, HEAD_DIM), head_vec),
            pl.BlockSpec((hp, GMLP_CHUNK, GMLP_CHUNK), head_vec),
            pl.BlockSpec((hp, GMLP_CHUNK, 1), head_vec),
            pl.BlockSpec((hp, lb_h.shape[1], HEAD_DIM), head_vec),
            pl.BlockSpec((hp, 1, HEAD_DIM), head_vec),
        ],
        out_specs=[
            pl.BlockSpec((MIX_TILE, hp * HEAD_DIM), lambda b, i, j: (b * n_t + i, j)),
            pl.BlockSpec((MIX_TILE, hp * HEAD_DIM), lambda b, i, j: (b * n_t + i, j)),
            pl.BlockSpec((1, N_HEADS, HEAD_DIM, HEAD_DIM), lambda b, i, j: (b, 0, 0, 0)),
        ],
        out_shape=[
            jax.ShapeDtypeStruct((bsz * t, N_HEADS * HEAD_DIM), BF16),
            jax.ShapeDtypeStruct((bsz * t, N_HEADS * HEAD_DIM), BF16),
            jax.ShapeDtypeStruct((bsz, N_HEADS, HEAD_DIM, HEAD_DIM), F32),
        ],
        scratch_shapes=[
            pltpu.VMEM((MIX_TILE, d), BF16),
            pltpu.VMEM((MIX_TILE, N_KINDS * hp * HEAD_DIM), F32),
            pltpu.VMEM((N_HEADS, HEAD_DIM, HEAD_DIM), F32),
        ],
        compiler_params=_cparams(("arbitrary", "arbitrary", "arbitrary")),
        name="mixer_prompt",
    )(x, sh1.reshape(bsz, 1, d), sc1.reshape(bsz, 1, d), *([w_in_b] * N_KINDS), ln_v_g, ln_v_b, w_s, b_s, lb_h, gnorm_g)


def _mixer_sample_kernel(x_ref, sh_ref, sc_ref, wu_ref, wv_ref, wq_ref, wf_ref, wi_ref, wg_ref,
                         lvg_ref, lvb_ref, ws_ref, bs_ref, lb_ref, gn_ref, s0_ref,
                         cata_ref, cato_ref, state_ref, vrows_ref, h_scr, proj_scr, *, layer, seq):
    j = pl.program_id(0)
    n_streams = x_ref.shape[0] // seq

    @pl.when(j == 0)
    def _():
        def slab(r, carry):
            rows = pl.ds(pl.multiple_of(r * seq, seq), seq)
            h = _ln_plain(x_ref[rows, :]) * (1.0 + sc_ref[pl.ds(r, 1), :]) + sh_ref[pl.ds(r, 1), :]
            h_scr[rows, :] = h.astype(BF16)
            return carry
        lax.fori_loop(0, n_streams, slab, 0)

    _in_projection(h_scr, (wu_ref, wv_ref, wq_ref, wf_ref, wi_ref, wg_ref), proj_scr)

    chains = []
    for hd in range(HEADS_PER_STEP):
        wm = ws_ref[hd][0:seq, 0:seq].astype(BF16)
        b_col = bs_ref[hd][0:seq, :]
        lbv = _lower_bound(lb_ref, hd, layer)
        for n in range(n_streams):
            rows = slice(n * seq, (n + 1) * seq)
            u, vn = _gmlp_rows(proj_scr[rows, _proj_cols(0, hd)], proj_scr[rows, _proj_cols(1, hd)],
                               lvg_ref[hd], lvb_ref[hd])
            vrows_ref[rows, hd * HEAD_DIM:(hd + 1) * HEAD_DIM] = vn
            sp = _dot(wm, vn.astype(BF16)) + b_col
            cata_ref[rows, hd * HEAD_DIM:(hd + 1) * HEAD_DIM] = (u * sp).astype(BF16)
            chains.append((jnp.transpose(s0_ref[n, hd]), lbv, gn_ref[hd],
                           [tuple(proj_scr[rows, _proj_cols(kind, hd)] for kind in range(2, N_KINDS))]))

    units = _hgrn_prepare(chains, seq)
    wide_decay = _hgrn_block_decay(units, seq) > SAFE_BLOCK_DECAY

    def finish(exact):
        outs, finals = _hgrn_finish(chains, _hgrn_scores(units, seq, exact))
        for hd in range(HEADS_PER_STEP):
            for n in range(n_streams):
                r0 = n * seq
                m = hd * n_streams + n
                cato_ref[r0:r0 + seq, hd * HEAD_DIM:(hd + 1) * HEAD_DIM] = outs[m][0].astype(BF16)
                state_ref[n, hd] = jnp.transpose(finals[m])

    @pl.when(wide_decay)
    def _():
        finish(True)

    @pl.when(jnp.logical_not(wide_decay))
    def _():
        finish(False)


def _mixer_sample_call(x2d, sh1, sc1, w_in_b, ln_v_g, ln_v_b, w_s, b_s, lb_h, gnorm_g, s0, layer, seq):
    rows, d = x2d.shape
    n_streams = rows // seq
    hp = HEADS_PER_STEP
    kern = functools.partial(_mixer_sample_kernel, layer=layer, seq=seq)
    head_vec = lambda j: (j, 0, 0)
    return pl.pallas_call(
        kern,
        grid=(N_HEADS // hp,),
        in_specs=[
            pl.BlockSpec((rows, d), lambda j: (0, 0)),
            pl.BlockSpec((n_streams, d), lambda j: (0, 0)),
            pl.BlockSpec((n_streams, d), lambda j: (0, 0)),
            *_w_in_specs(d),
            pl.BlockSpec((hp, 1, HEAD_DIM), head_vec),
            pl.BlockSpec((hp, 1, HEAD_DIM), head_vec),
            pl.BlockSpec((hp, GMLP_CHUNK, GMLP_CHUNK), head_vec),
            pl.BlockSpec((hp, GMLP_CHUNK, 1), head_vec),
            pl.BlockSpec((hp, lb_h.shape[1], HEAD_DIM), head_vec),
            pl.BlockSpec((hp, 1, HEAD_DIM), head_vec),
            pl.BlockSpec((n_streams, hp, HEAD_DIM, HEAD_DIM), lambda j: (0, j, 0, 0)),
        ],
        out_specs=[
            pl.BlockSpec((rows, hp * HEAD_DIM), lambda j: (0, j)),
            pl.BlockSpec((rows, hp * HEAD_DIM), lambda j: (0, j)),
            pl.BlockSpec((n_streams, hp, HEAD_DIM, HEAD_DIM), lambda j: (0, j, 0, 0)),
            pl.BlockSpec((rows, hp * HEAD_DIM), lambda j: (0, j)),
        ],
        out_shape=[
            jax.ShapeDtypeStruct((rows, N_HEADS * HEAD_DIM), BF16),
            jax.ShapeDtypeStruct((rows, N_HEADS * HEAD_DIM), BF16),
            jax.ShapeDtypeStruct((n_streams, N_HEADS, HEAD_DIM, HEAD_DIM), F32),
            jax.ShapeDtypeStruct((rows, N_HEADS * HEAD_DIM), F32),
        ],
        scratch_shapes=[
            pltpu.VMEM((rows, d), BF16),
            pltpu.VMEM((rows, N_KINDS * hp * HEAD_DIM), F32),
        ],
        compiler_params=_cparams(("arbitrary",)),
        name="mixer_sample",
    )(x2d, sh1, sc1, *([w_in_b] * N_KINDS), ln_v_g, ln_v_b, w_s, b_s, lb_h, gnorm_g, s0)


def _post_kernel(grp_ref, cap_ref, cop_ref, cas_ref, cos_ref, xp_ref, xs_ref, g1_ref, sh2_ref, sc2_ref, wout_ref, l1g_ref, l1b_ref,
                 wr_ref, br_ref, tri_ref,
                 x1_ref, h2_ref, meta_ref, wts_ref, cnt_ref,
                 mix_scr, hhi_scr, hlo_scr, run_scr, *, n_prompt_tiles, alpha):
    i = pl.program_id(0)
    is_s = i >= n_prompt_tiles
    tm = mix_scr.shape[0]

    @pl.when(is_s)
    def _():
        mix_scr[...] = _dot(jnp.concatenate([cas_ref[...], cos_ref[...]], axis=1), wout_ref[...])

    @pl.when(jnp.logical_not(is_s))
    def _():
        mix_scr[...] = _dot(jnp.concatenate([cap_ref[...], cop_ref[...]], axis=1), wout_ref[...])

    @pl.when(i == 0)
    def _():
        run_scr[...] = jnp.zeros_like(run_scr)

    def slab(g, carry):
        rows = pl.ds(pl.multiple_of(g * MOD_GROUP, MOD_GROUP), MOD_GROUP)
        x = jnp.where(is_s, xs_ref[rows, :], xp_ref[rows, :])
        stream = pl.ds(grp_ref[0, 0, g], 1)
        z = alpha * x + g1_ref[stream, :] * mix_scr[rows, :]
        x1 = _ln_plain(z) * l1g_ref[...] + l1b_ref[...]
        x1_ref[rows, :] = x1
        h2 = _ln_plain(x1) * (1.0 + sc2_ref[stream, :]) + sh2_ref[stream, :]
        h2_ref[rows, :] = h2
        hi = h2.astype(BF16)
        hhi_scr[rows, :] = hi
        hlo_scr[rows, :] = (h2 - hi.astype(F32)).astype(BF16)
        return carry
    lax.fori_loop(0, tm // MOD_GROUP, slab, 0, unroll=SLAB_UNROLL)

    _route(hhi_scr, hlo_scr, wr_ref, br_ref, tri_ref, run_scr, meta_ref, wts_ref, cnt_ref)


def _route(hhi_scr, hlo_scr, wr_ref, br_ref, before_ref, run_scr, meta_ref, wts_ref, cnt_ref):
    tm = hhi_scr.shape[0]
    n_r = N_GROUPS + N_EXPERTS
    n_rows = -(-n_r // 8) * 8
    s = _dot(hhi_scr[...], wr_ref[...]) + _dot(hlo_scr[...], wr_ref[...])
    logits = s + pltpu.roll(s, LANES - n_r, axis=1) + br_ref[...]
    lt = jnp.transpose(logits)[0:n_rows, :]

    row = lax.broadcasted_iota(I32, (n_rows, tm), 0)
    row_f = row.astype(F32)
    neg = jnp.float32(-jnp.inf)
    big = jnp.float32(LANES)

    def first_row_of(vals, top):
        return jnp.min(jnp.where(vals == top, row_f, big), axis=0, keepdims=True)

    gl = jnp.where(row < N_GROUPS, lt, neg)
    gmax = jnp.max(gl, axis=0, keepdims=True)
    gsel = first_row_of(gl, gmax)
    p_group = 1.0 / jnp.sum(jnp.exp(gl - gmax), axis=0, keepdims=True)

    e_lo = N_GROUPS + EXPERTS_PER_GROUP * gsel.astype(I32)
    emask = (row >= e_lo) & (row < e_lo + EXPERTS_PER_GROUP)
    el = jnp.where(emask, lt, neg)
    t1 = jnp.max(el, axis=0, keepdims=True)
    i1 = first_row_of(el, t1)
    el2 = jnp.where(row_f == i1, neg, el)
    t2 = jnp.max(el2, axis=0, keepdims=True)
    i2 = first_row_of(el2, t2)
    e2 = jnp.exp(t2 - t1)
    den = 1.0 + e2
    w1 = (1.0 / den) * p_group
    w2 = (e2 / den) * p_group

    sel1 = row_f == i1
    sel2 = row_f == i2
    onehot = jnp.where(sel1 | sel2, 1.0, 0.0)
    run_col = jnp.transpose(jnp.broadcast_to(run_scr[...], (LANES, LANES)))[0:n_rows, 0:1]
    before = _dot(onehot.astype(BF16), before_ref[...]) + run_col
    r1 = jnp.sum(jnp.where(sel1, before, 0.0), axis=0, keepdims=True)
    r2 = jnp.sum(jnp.where(sel2, before, 0.0), axis=0, keepdims=True)
    added = jnp.concatenate([jnp.sum(onehot, axis=1, keepdims=True), jnp.zeros((LANES - n_rows, 1), F32)], axis=0)
    run_scr[...] = run_scr[...] + jnp.transpose(jnp.broadcast_to(added, (LANES, LANES)))[0:1, :]
    cnt_ref[...] = run_scr[...].astype(I32)

    base = jnp.float32(N_GROUPS)
    mrow = lax.broadcasted_iota(I32, (META_ROWS, tm), 0)
    meta = jnp.where(mrow == 0, i1 - base, jnp.where(mrow == 1, i2 - base, jnp.where(mrow == 2, r1, jnp.where(mrow == 3, r2, 0.0))))
    meta_ref[0] = meta.astype(I32)
    wrow = lax.broadcasted_iota(I32, (LANES, tm), 0)
    wts_ref[...] = jnp.transpose(jnp.where(wrow == 0, w1, jnp.where(wrow == 1, w2, 0.0)))


def _post_call(groups3, cats_p, cats_s, x_p, x_s, mod, w_out_b, ln1_g, ln1_b, wr, br, tri, alpha):
    n_p, d = x_p.shape
    n_s = x_s.shape[0]
    d_a = cats_p[0].shape[1]
    tm = POST_TILE
    n_pt = n_p // tm
    n_tiles = n_pt + n_s // tm
    n_tok = n_p + n_s
    groups = tm // MOD_GROUP
    kern = functools.partial(_post_kernel, n_prompt_tiles=n_pt, alpha=alpha)
    p_idx = lambda i: (jnp.minimum(i, n_pt - 1), 0)
    s_idx = lambda i: (jnp.maximum(i - n_pt, 0), 0)
    tile = lambda i: (i, 0)
    const = lambda i: (0, 0)
    once = pl.Buffered(1)
    s_mode = once if n_tiles - n_pt == 1 else None
    mod_spec = lambda m: pl.BlockSpec((mod.shape[0], d), lambda i: (0, m), pipeline_mode=once)
    return pl.pallas_call(
        kern,
        grid=(n_tiles,),
        in_specs=[
            pl.BlockSpec((1, 1, groups), lambda i: (i, 0, 0), memory_space=pltpu.SMEM),
            pl.BlockSpec((tm, d_a), p_idx),
            pl.BlockSpec((tm, d_a), p_idx),
            pl.BlockSpec((tm, d_a), s_idx, pipeline_mode=s_mode),
            pl.BlockSpec((tm, d_a), s_idx, pipeline_mode=s_mode),
            pl.BlockSpec((tm, d), p_idx),
            pl.BlockSpec((tm, d), s_idx, pipeline_mode=s_mode),
            mod_spec(2), mod_spec(3), mod_spec(4),
            pl.BlockSpec((d, d), const, pipeline_mode=once),
            pl.BlockSpec((1, d), const),
            pl.BlockSpec((1, d), const),
            pl.BlockSpec((d, LANES), const, pipeline_mode=once),
            pl.BlockSpec((1, LANES), const),
            pl.BlockSpec((tm, tm), const, pipeline_mode=once),
        ],
        out_specs=[
            pl.BlockSpec((tm, d), tile),
            pl.BlockSpec((tm, d), tile),
            pl.BlockSpec((1, META_ROWS, tm), lambda i: (i, 0, 0)),
            pl.BlockSpec((tm, LANES), tile),
            pl.BlockSpec((1, LANES), const),
        ],
        out_shape=[
            jax.ShapeDtypeStruct((n_tok, d), F32),
            jax.ShapeDtypeStruct((n_tok, d), F32),
            jax.ShapeDtypeStruct((n_tiles, META_ROWS, tm), I32),
            jax.ShapeDtypeStruct((n_tok, LANES), F32),
            jax.ShapeDtypeStruct((1, LANES), I32),
        ],
        scratch_shapes=[
            pltpu.VMEM((tm, d), F32),
            pltpu.VMEM((tm, d), BF16),
            pltpu.VMEM((tm, d), BF16),
            pltpu.VMEM((1, LANES), F32),
        ],
        compiler_params=_cparams(("arbitrary",)),
        name="post_router",
    )(groups3, *cats_p, *cats_s, x_p, x_s, mod, mod, mod, w_out_b, ln1_g, ln1_b, wr, br, tri)


def _dispatch_kernel(slot_ref, gap_start_ref, gap_len_ref, h_ref, xs_ref, zero_scr, sem, gap_sem, tile_sem, *,
                     gaps_per_step):
    i = pl.program_id(0)
    tm = h_ref.shape[0]
    n_gaps = gap_len_ref.shape[0]
    tile_rows = zero_scr.shape[0]

    @pl.when(i == 0)
    def _():
        zero_scr[...] = jnp.zeros_like(zero_scr)

    def row_copy(r, k):
        return pltpu.make_async_copy(h_ref.at[pl.ds(r, 1)], xs_ref.at[pl.ds(slot_ref[0, 0, k * tm + r], 1)], sem)

    def start_rows(r, carry):
        row_copy(r, 0).start(priority=0)
        row_copy(r, 1).start(priority=1)
        return carry
    lax.fori_loop(0, tm, start_rows, 0, unroll=ROW_DMA_UNROLL)

    def gap_head(g):
        return jnp.where(g < n_gaps - 1, (-gap_start_ref[g]) & (GAP_ROWS - 1), 0)

    def gap_chunks(g):
        return jnp.where(g < n_gaps - 1, (gap_len_ref[g] - gap_head(g)) // GAP_ROWS, 0)

    def gap_tiles(g):
        return jnp.where(g == n_gaps - 1, gap_len_ref[n_gaps - 1] // tile_rows, 0)

    def zero_row_copy(g, r):
        return pltpu.make_async_copy(zero_scr.at[pl.ds(0, 1)], xs_ref.at[pl.ds(gap_start_ref[g] + r, 1)], gap_sem)

    def zero_chunk_copy(g, q):
        start = pl.multiple_of(gap_start_ref[g] + gap_head(g) + q * GAP_ROWS, GAP_ROWS)
        return pltpu.make_async_copy(zero_scr.at[pl.ds(0, GAP_ROWS)], xs_ref.at[pl.ds(start, GAP_ROWS)], gap_sem)

    def zero_tile_copy(g, t):
        start = pl.multiple_of(gap_start_ref[g] + t * tile_rows, tile_rows)
        return pltpu.make_async_copy(zero_scr, xs_ref.at[pl.ds(start, tile_rows)], tile_sem)

    gap_copies = ((gap_head, zero_row_copy), (gap_chunks, zero_chunk_copy), (gap_tiles, zero_tile_copy))

    def for_gap_copies(act):
        for j in range(gaps_per_step):
            g = jnp.minimum(i * gaps_per_step + j, n_gaps - 1)
            live = i * gaps_per_step + j < n_gaps
            for count, copy in gap_copies:
                lax.fori_loop(0, jnp.where(live, count(g), 0), lambda q, c, g=g, copy=copy: (act(copy(g, q)), c)[1], 0)

    for_gap_copies(lambda copy: copy.start())
    for _ in range(2 * tm // tile_rows):
        pltpu.make_async_copy(zero_scr, zero_scr, sem).wait()
    for_gap_copies(lambda copy: copy.wait())


def _dispatch_call(slots3, gap_start, gap_len, h2, n_sorted):
    n_tok, d = h2.shape
    n_tiles = n_tok // TOK_TILE
    n_gaps = gap_len.shape[0]
    kern = functools.partial(_dispatch_kernel, gaps_per_step=-(-n_gaps // n_tiles))
    return pl.pallas_call(
        kern,
        grid=(n_tiles,),
        in_specs=[
            pl.BlockSpec((1, 1, slots3.shape[2]), lambda i: (i, 0, 0), memory_space=pltpu.SMEM),
            pl.BlockSpec((n_gaps,), lambda i: (0,), memory_space=pltpu.SMEM),
            pl.BlockSpec((n_gaps,), lambda i: (0,), memory_space=pltpu.SMEM),
            pl.BlockSpec((TOK_TILE, d), lambda i: (i, 0)),
        ],
        out_specs=pl.BlockSpec(memory_space=pl.ANY),
        out_shape=jax.ShapeDtypeStruct((n_sorted, d), F32),
        scratch_shapes=[pltpu.VMEM((EXP_TILE, d), F32)] + [pltpu.SemaphoreType.DMA(())] * 3,
        compiler_params=_cparams(("arbitrary",)),
        name="dispatch",
    )(slots3, gap_start, gap_len, h2)


def _expert_kernel(te_ref, nt_ref, nxt_ref, run_ref, xs_ref, w1_ref, w3_ref, w2_ref, ys_ref,
                   x_scr, w1_scr, w3_scr, w2f_scr, w13_scr, w2_scr, xsem, wsem):
    t = pl.program_id(0)
    n_t = pl.num_programs(0)
    rows = x_scr.shape[1]
    d_exp = w1_scr.shape[2]
    x_slots = x_scr.shape[0]
    xs = lax.rem(t, x_slots)
    e = te_ref[t]
    prev = te_ref[jnp.maximum(t - 1, 0)]
    changed = (t == 0) | (e != prev)
    valid = t < nt_ref[0]
    wslot = lax.rem(run_ref[t], 2)

    def x_copy(tile, slot):
        start = pl.multiple_of(tile * rows, rows)
        return pltpu.make_async_copy(xs_ref.at[pl.ds(start, rows)], x_scr.at[slot], xsem.at[slot])

    def w_copies(expert, slot):
        return (pltpu.make_async_copy(w1_ref.at[expert], w1_scr.at[slot], wsem.at[slot]),
                pltpu.make_async_copy(w3_ref.at[expert], w3_scr.at[slot], wsem.at[slot]),
                pltpu.make_async_copy(w2_ref.at[expert], w2f_scr.at[slot], wsem.at[slot]))

    @pl.when(t == 0)
    def _():
        for ahead in range(X_AHEAD):
            x_copy(ahead, ahead).start()
        for c in w_copies(e, wslot):
            c.start()

    @pl.when(t + X_AHEAD < n_t)
    def _():
        x_copy(t + X_AHEAD, lax.rem(t + X_AHEAD, x_slots)).start()

    @pl.when(changed)
    def _():
        for c in w_copies(e, wslot):
            c.wait()
        nxt = nxt_ref[t]

        @pl.when(nxt != e)
        def _():
            for c in w_copies(nxt, 1 - wslot):
                c.start()
        w13_scr[:, 0:d_exp] = w1_scr[wslot].astype(BF16)
        w13_scr[:, d_exp:2 * d_exp] = w3_scr[wslot].astype(BF16)
        w2_scr[...] = w2f_scr[wslot].astype(BF16)

    x_copy(t, xs).wait()

    @pl.when(valid)
    def _():
        h13 = _dot(x_scr[xs].astype(BF16), w13_scr[...])
        hm = jax.nn.silu(h13[:, 0:d_exp]) * h13[:, d_exp:2 * d_exp]
        _store_rows(ys_ref, (), 0, _dot(hm.astype(BF16), w2_scr[...]))

    @pl.when(jnp.logical_not(valid))
    def _():
        ys_ref[...] = jnp.zeros_like(ys_ref)


def _expert_call(tile_expert, n_valid, next_expert, run_id, xs, w1, w3, w2):
    n_sorted, d = xs.shape
    n_exp, _, d_exp = w1.shape
    n_tiles = n_sorted // EXP_TILE
    assert n_tiles > X_AHEAD
    grid_spec = pltpu.PrefetchScalarGridSpec(
        num_scalar_prefetch=4,
        grid=(n_tiles,),
        in_specs=[pl.BlockSpec(memory_space=pl.ANY)] * 4,
        out_specs=pl.BlockSpec((EXP_TILE * _lines_per_row(d), LANES), lambda t, te, nt, nx, rn: (t, 0)),
        scratch_shapes=[
            pltpu.VMEM((X_AHEAD + 1, EXP_TILE, d), F32),
            pltpu.VMEM((2, d, d_exp), F32),
            pltpu.VMEM((2, d, d_exp), F32),
            pltpu.VMEM((2, d_exp, d), F32),
            pltpu.VMEM((d, 2 * d_exp), BF16),
            pltpu.VMEM((d_exp, d), BF16),
            pltpu.SemaphoreType.DMA((X_AHEAD + 1,)),
            pltpu.SemaphoreType.DMA((2,)),
        ],
    )
    return pl.pallas_call(
        _expert_kernel,
        grid_spec=grid_spec,
        out_shape=jax.ShapeDtypeStruct((n_sorted * _lines_per_row(d), LANES), F32),
        compiler_params=_cparams(("arbitrary",)),
        name="experts",
    )(tile_expert, n_valid, next_expert, run_id, xs, w1, w3, w2)


def _combine_kernel(scur_ref, snext_ref, grp_ref, x1_ref, wts_ref, g2_ref, l2g_ref, l2b_ref, ys_ref, outp_ref, outs_ref,
                    y_scr, sem, *, n_prompt_tiles, alpha):
    i = pl.program_id(0)
    n_i = pl.num_programs(0)
    is_s = i >= n_prompt_tiles
    tm, d = x1_ref.shape
    lpr = _lines_per_row(d)
    cur = lax.rem(i, 2)

    def row_lines(r):
        return pl.ds(pl.multiple_of(r * lpr, lpr), lpr)

    def start_gathers(s_ref, buf):
        def body(r, carry):
            for k in range(2):
                pltpu.make_async_copy(ys_ref.at[row_lines(s_ref[0, 0, k * tm + r])], y_scr.at[buf, k, row_lines(r)],
                                      sem.at[buf, k]).start(priority=k)
            return carry
        lax.fori_loop(0, tm, body, 0, unroll=ROW_DMA_UNROLL // 2)

    @pl.when(i == 0)
    def _():
        start_gathers(scur_ref, 0)

    @pl.when(i + 1 < n_i)
    def _():
        start_gathers(snext_ref, 1 - cur)

    for k in range(2):
        pltpu.make_async_copy(y_scr.at[cur, k], y_scr.at[cur, k], sem.at[cur, k]).wait()

    def slab(g, carry, out_ref):
        rows = pl.ds(pl.multiple_of(g * MOD_GROUP, MOD_GROUP), MOD_GROUP)
        w = wts_ref[rows, :]
        moe = (w[:, 0:1] * _load_rows(y_scr, (cur, 0), g * MOD_GROUP, MOD_GROUP, d)
               + w[:, 1:2] * _load_rows(y_scr, (cur, 1), g * MOD_GROUP, MOD_GROUP, d))
        z = alpha * x1_ref[rows, :] + g2_ref[pl.ds(grp_ref[0, 0, g], 1), :] * moe
        out_ref[rows, :] = _ln_plain(z) * l2g_ref[...] + l2b_ref[...]
        return carry

    @pl.when(is_s)
    def _():
        lax.fori_loop(0, tm // MOD_GROUP, functools.partial(slab, out_ref=outs_ref), 0, unroll=SLAB_UNROLL)

    @pl.when(jnp.logical_not(is_s))
    def _():
        lax.fori_loop(0, tm // MOD_GROUP, functools.partial(slab, out_ref=outp_ref), 0, unroll=SLAB_UNROLL)


def _combine_call(slots3, groups3, x1, wts, mod, ln2_g, ln2_b, ys, n_p, alpha):
    n_tok, d = x1.shape
    n_tiles = n_tok // TOK_TILE
    n_pt = n_p // TOK_TILE
    groups = TOK_TILE // MOD_GROUP
    kern = functools.partial(_combine_kernel, n_prompt_tiles=n_pt, alpha=alpha)
    slot_tile = lambda ahead: pl.BlockSpec(
        (1, 1, slots3.shape[2]), lambda i: (jnp.minimum(i + ahead, n_tiles - 1), 0, 0), memory_space=pltpu.SMEM)
    return pl.pallas_call(
        kern,
        grid=(n_tiles,),
        in_specs=[
            slot_tile(0), slot_tile(1),
            pl.BlockSpec((1, 1, groups), lambda i: (i, 0, 0), memory_space=pltpu.SMEM),
            pl.BlockSpec((TOK_TILE, d), lambda i: (i, 0)),
            pl.BlockSpec((TOK_TILE, LANES), lambda i: (i, 0)),
            pl.BlockSpec((mod.shape[0], d), lambda i: (0, 5)),
            pl.BlockSpec((1, d), lambda i: (0, 0)),
            pl.BlockSpec((1, d), lambda i: (0, 0)),
            pl.BlockSpec(memory_space=pl.ANY),
        ],
        out_specs=[
            pl.BlockSpec((TOK_TILE, d), lambda i: (jnp.minimum(i, n_pt - 1), 0)),
            pl.BlockSpec((TOK_TILE, d), lambda i: (jnp.maximum(i - n_pt, 0), 0)),
        ],
        out_shape=[
            jax.ShapeDtypeStruct((n_p, d), F32),
            jax.ShapeDtypeStruct((n_tok - n_p, d), F32),
        ],
        scratch_shapes=[
            pltpu.VMEM((2, 2, TOK_TILE * _lines_per_row(d), LANES), F32),
            pltpu.SemaphoreType.DMA((2, 2)),
        ],
        compiler_params=_cparams(("arbitrary",)),
        name="combine",
    )(slots3, slots3, groups3, x1, wts, mod, ln2_g, ln2_b, ys)


def _routing_tables(meta, cnt_row, n_sorted):
    experts = meta[:, 0:2, :]
    ranks = meta[:, 2:4, :]
    cnt = cnt_row[0, N_GROUPS:N_GROUPS + N_EXPERTS]
    padded = ((cnt + EXP_TILE - 1) // EXP_TILE) * EXP_TILE
    ends = jnp.cumsum(padded)
    offs = ends - padded
    ids = jnp.arange(N_EXPERTS, dtype=I32)
    is_expert = experts[None] == ids[:, None, None, None]
    slots = jnp.sum(jnp.where(is_expert, offs[:, None, None, None], 0), axis=0) + ranks
    total = ends[-1]
    n_tiles = n_sorted // EXP_TILE
    tile_ids = jnp.arange(n_tiles, dtype=I32)
    tile_expert = jnp.minimum(
        jnp.sum((tile_ids[:, None] >= (ends // EXP_TILE)[None, :]).astype(I32), axis=1), N_EXPERTS - 1)
    n_valid = (total // EXP_TILE).reshape(1).astype(I32)
    later = (ids[None, :] > tile_expert[:, None]) & (padded[None, :] > 0)
    next_expert = jnp.min(jnp.where(later, ids[None, :], N_EXPERTS), axis=1)
    next_expert = jnp.where(next_expert == N_EXPERTS, tile_expert, next_expert).astype(I32)
    starts_run = jnp.concatenate([jnp.zeros((1,), I32), (tile_expert[1:] != tile_expert[:-1]).astype(I32)])
    run_id = jnp.cumsum(starts_run).astype(I32)
    gap_start = jnp.concatenate([offs + cnt, total.reshape(1)]).astype(I32)
    gap_len = jnp.concatenate([padded - cnt, (n_sorted - total).reshape(1)]).astype(I32)
    return slots.astype(I32), (tile_expert, n_valid, next_expert, run_id), gap_start, gap_len


def _layer(layer, n_layers, xp, xs, s0_l, c_all, p):
    (w_ada, b_ada, w_in, ln_v_g, ln_v_b, w_s, b_s, hgrn_lb, gnorm_g, w_out, ln1_g, ln1_b,
     w_rg, b_rg, w_re, b_re, w1, w3, w2, ln2_g, ln2_b) = p
    bsz, t, d = xp.shape
    n_streams, seq, _ = xs.shape
    alpha = float((2.0 * n_layers) ** 0.25)
    n_p = bsz * t
    n_s = n_streams * seq
    n_tok = n_p + n_s

    n_c = c_all.shape[0]
    c_pad = jnp.pad(c_all, ((0, (-n_c) % 8), (0, 0)))
    mod = _ada_call(c_pad, w_ada, b_ada)
    sh1, sc1 = mod[:n_c, 0:d], mod[:n_c, d:2 * d]

    w_in_b = w_in.astype(BF16)
    w_out_b = w_out.astype(BF16)
    lvg = ln_v_g.reshape(N_HEADS, 1, HEAD_DIM)
    lvb = ln_v_b.reshape(N_HEADS, 1, HEAD_DIM)
    b_s3 = b_s.reshape(N_HEADS, GMLP_CHUNK, 1)
    lb_h = jnp.transpose(hgrn_lb.reshape(hgrn_lb.shape[0], N_HEADS, HEAD_DIM), (1, 0, 2))
    gn = gnorm_g.reshape(N_HEADS, 1, HEAD_DIM)

    ca_p, co_p, state_p = _mixer_prompt_call(xp, sh1[:bsz], sc1[:bsz], w_in_b, lvg, lvb, w_s, b_s3, lb_h, gn, layer)
    ca_s, co_s, state_s, vrows = _mixer_sample_call(xs.reshape(n_s, d), sh1[bsz:], sc1[bsz:], w_in_b, lvg, lvb, w_s,
                                               b_s3, lb_h, gn, s0_l, layer, seq)

    group_stream = jnp.concatenate([
        jnp.repeat(jnp.arange(bsz, dtype=I32), t // MOD_GROUP),
        bsz + jnp.repeat(jnp.arange(n_streams, dtype=I32), seq // MOD_GROUP)])

    wr = jnp.concatenate([w_rg, w_re], axis=1)
    wr_hi = wr.astype(BF16)
    wr_lo = (wr - wr_hi.astype(F32)).astype(BF16)
    n_r = wr.shape[1]
    wr_cat = jnp.concatenate([wr_hi, wr_lo, jnp.zeros((d, LANES - 2 * n_r), BF16)], axis=1)
    br = jnp.pad(jnp.concatenate([b_rg, b_re]), (0, LANES - n_r)).reshape(1, LANES)
    tri = jnp.triu(jnp.ones((POST_TILE, POST_TILE), F32), 1).astype(BF16)

    x1, h2, meta, wts, cnt_row = _post_call(group_stream.reshape(n_tok // POST_TILE, 1, POST_TILE // MOD_GROUP),
                                            (ca_p, co_p), (ca_s, co_s), xp.reshape(n_p, d), xs.reshape(n_s, d),
                                            mod, w_out_b, ln1_g.reshape(1, d), ln1_b.reshape(1, d), wr_cat, br, tri, alpha)

    n_sorted = 2 * n_tok + N_EXPERTS * EXP_TILE
    slots, tile_tables, gap_start, gap_len = _routing_tables(meta, cnt_row, n_sorted)
    n_tiles = n_tok // TOK_TILE
    assert POST_TILE == TOK_TILE
    slots3 = slots.reshape(n_tiles, 1, 2 * TOK_TILE)

    xs_sorted = _dispatch_call(slots3, gap_start, gap_len, h2, n_sorted)
    ys_sorted = _expert_call(*tile_tables, xs_sorted, w1, w3, w2)
    yp, ys_out = _combine_call(slots3, group_stream.reshape(n_tiles, 1, TOK_TILE // MOD_GROUP), x1, wts, mod,
                               ln2_g.reshape(1, d), ln2_b.reshape(1, d), ys_sorted, n_p, alpha)

    v_rows = vrows.reshape(n_streams, seq, N_HEADS, HEAD_DIM)
    return yp.reshape(bsz, t, d), ys_out.reshape(n_streams, seq, d), state_p, state_s, v_rows


def kernel(x_prompt, x_sample, state_hgrn, c_prompt, c_sample, w_ada, b_ada, w_in, ln_v_g, ln_v_b, w_s, b_s, hgrn_lb, gnorm_g, w_out, ln1_g, ln1_b, w_router_g, b_router_g, w_router_e, b_router_e, w1, w3, w2, ln2_g, ln2_b):
    n_layers = w_ada.shape[0]
    assert x_prompt.shape[1] % MIX_TILE == 0 and x_prompt.shape[2] == 2 * N_HEADS * HEAD_DIM
    assert x_sample.shape[1] % MOD_GROUP == 0 and x_sample.shape[1] <= SUB_CHUNK
    assert (x_sample.shape[0] * x_sample.shape[1]) % TOK_TILE == 0 and TOK_TILE % POST_TILE == 0
    c_all = jnp.concatenate([c_prompt, c_sample], axis=0)
    xp, xs = x_prompt, x_sample
    sp_list, ss_list, vs_list = [], [], []
    for l in range(n_layers):
        p = (w_ada[l], b_ada[l], w_in[l], ln_v_g[l], ln_v_b[l], w_s[l], b_s[l], hgrn_lb, gnorm_g[l], w_out[l],
             ln1_g[l], ln1_b[l], w_router_g[l], b_router_g[l], w_router_e[l], b_router_e[l],
             w1[l], w3[l], w2[l], ln2_g[l], ln2_b[l])
        xp, xs, sp, ss, vs = _layer(l, n_layers, xp, xs, state_hgrn[l], c_all, p)
        sp_list.append(sp.astype(state_hgrn.dtype))
        ss_list.append(ss.astype(state_hgrn.dtype))
        vs_list.append(vs)
    return (xp, xs, jnp.stack(sp_list, axis=0), jnp.stack(ss_list, axis=0), jnp.stack(vs_list, axis=0))
```

```python
import functools

import jax
import jax.numpy as jnp
from jax import lax
from jax.experimental import pallas as pl
from jax.experimental.pallas import tpu as pltpu

F32 = jnp.float32
BF16 = jnp.bfloat16
I32 = jnp.int32

N_HEADS = 8
HEAD_DIM = 128
GMLP_CHUNK = 128
SUB_CHUNK = 64
N_GROUPS = 4
EXPERTS_PER_GROUP = 8
N_EXPERTS = N_GROUPS * EXPERTS_PER_GROUP
LN_EPS = 1e-5
N_KINDS = 6

LANES = 128
MIX_TILE = 512
HEADS_PER_STEP = 2
HGRN_BLOCK = 64
DIAG_BLOCK = 16
SAFE_BLOCK_DECAY = 60.0
TOK_TILE = 512
POST_TILE = 512
META_ROWS = 8
MOD_GROUP = 32
SLAB_UNROLL = 4
EXP_TILE = 256
X_AHEAD = 2
ROW_DMA_UNROLL = 8
GAP_ROWS = 8
ADA_TILE = 1024
VMEM_LIMIT = 56 * 1024 * 1024


def _cparams(sem):
    return pltpu.CompilerParams(dimension_semantics=sem, vmem_limit_bytes=VMEM_LIMIT)


def _ln_plain(x):
    mu = jnp.mean(x, axis=-1, keepdims=True)
    xc = x - mu
    var = jnp.mean(xc * xc, axis=-1, keepdims=True)
    return xc * lax.rsqrt(var + LN_EPS)


def _dot(a, b):
    return jnp.dot(a, b, preferred_element_type=F32)


def _dot_nt(a, b):
    return lax.dot_general(a, b, (((1,), (1,)), ((), ())), preferred_element_type=F32)


def _lines_per_row(d):
    return d // LANES


def _load_rows(ref, lead, row0, n_rows, d):
    lpr = _lines_per_row(d)
    parts = [ref[lead + (pl.ds(row0 * lpr + c, n_rows, stride=lpr), slice(None))] for c in range(lpr)]
    return jnp.concatenate(parts, axis=1)


def _store_rows(ref, lead, row0, val):
    n_rows, d = val.shape
    lpr = _lines_per_row(d)
    for c in range(lpr):
        ref[lead + (pl.ds(row0 * lpr + c, n_rows, stride=lpr), slice(None))] = val[:, c * LANES:(c + 1) * LANES]


def _ada_kernel(c_ref, w_ref, b_ref, o_ref):
    s = jax.nn.silu(c_ref[...]).astype(BF16)
    o_ref[...] = _dot(s, w_ref[...].astype(BF16)) + b_ref[...]


def _ada_call(c_pad, w_ada, b_ada):
    rows, d = c_pad.shape
    n_out = w_ada.shape[1]
    return pl.pallas_call(
        _ada_kernel,
        grid=(n_out // ADA_TILE,),
        in_specs=[
            pl.BlockSpec((rows, d), lambda n: (0, 0)),
            pl.BlockSpec((d, ADA_TILE), lambda n: (0, n)),
            pl.BlockSpec((1, ADA_TILE), lambda n: (0, n)),
        ],
        out_specs=pl.BlockSpec((rows, ADA_TILE), lambda n: (0, n)),
        out_shape=jax.ShapeDtypeStruct((rows, n_out), F32),
        compiler_params=_cparams(("arbitrary",)),
        name="adaln",
    )(c_pad, w_ada, b_ada.reshape(1, n_out))


def _lower_bound(lb_ref, hd, layer):
    raw = lb_ref[hd]
    m = jnp.max(raw, axis=0, keepdims=True)
    e = jnp.exp(raw - m)
    p = e / jnp.sum(e, axis=0, keepdims=True)
    return jnp.sum(p[: layer + 1], axis=0, keepdims=True)


def _gmlp_rows(u_pre, v_pre, ln_g, ln_b):
    u = jax.nn.gelu(u_pre)
    v = jax.nn.gelu(v_pre)
    mu = jnp.mean(v, axis=-1, keepdims=True)
    vc = v - mu
    var = jnp.mean(vc * vc, axis=-1, keepdims=True)
    vn = vc * lax.rsqrt(var + LN_EPS) * ln_g + ln_b
    return u, vn


def _row_bcast(a, row, n):
    return jnp.broadcast_to(a[row:row + 1, :], (n, a.shape[1]))


def _block_id(idx, size):
    return lax.shift_right_logical(idx, I32(size.bit_length() - 1))


def _hgrn_masks(c):
    row = lax.broadcasted_iota(I32, (c, c), 0)
    col = lax.broadcasted_iota(I32, (c, c), 1)
    masks = []
    half = c // 2
    while half >= DIAG_BLOCK:
        span = 2 * half
        same = _block_id(row, span) == _block_id(col, span)
        masks.append(same & ((row & (span - 1)) >= half) & ((col & (span - 1)) < half))
        half //= 2
    diag = (_block_id(row, DIAG_BLOCK) == _block_id(col, DIAG_BLOCK)) & (col <= row)
    return masks, diag


def _tri_ones(c):
    row = lax.broadcasted_iota(I32, (c, c), 0)
    col = lax.broadcasted_iota(I32, (c, c), 1)
    return jnp.where(col <= row, 1.0, 0.0).astype(BF16)


def _level_halves(c):
    halves = []
    half = c // 2
    while half >= 1:
        halves.append(half)
        half //= 2
    return halves


def _level_masks(c):
    row = lax.broadcasted_iota(I32, (c, c), 0)
    col = lax.broadcasted_iota(I32, (c, c), 1)
    masks = []
    for half in _level_halves(c):
        span = 2 * half
        same = _block_id(row, span) == _block_id(col, span)
        masks.append(same & ((row & (span - 1)) >= half) & ((col & (span - 1)) < half))
    return masks, row == col


def _level_ref_sums(c):
    row = lax.broadcasted_iota(I32, (c, c), 0)
    col = lax.broadcasted_iota(I32, (c, c), 1)
    pieces = [col <= _block_id(row, 2 * half) * (2 * half) + half for half in _level_halves(c)]
    return jnp.concatenate([jnp.where(p, 1.0, 0.0).astype(BF16) for p in pieces], axis=0)


def _hgrn_block_decay(units, c):
    worst = None
    for u in units:
        a = u["a"]
        for b in range(c // DIAG_BLOCK):
            span = a[b * DIAG_BLOCK:b * DIAG_BLOCK + 1, :] - a[(b + 1) * DIAG_BLOCK - 1:(b + 1) * DIAG_BLOCK, :]
            worst = span if worst is None else jnp.maximum(worst, span)
    return jnp.max(worst)


def _hgrn_prepare(chains, c):
    tri = _tri_ones(c)
    units = []
    for st0, lbv, gn, blocks in chains:
        for q_pre, f_pre, i_pre, g_pre in blocks:
            q = jax.nn.silu(q_pre)
            fg = lbv + (1.0 - lbv) * jax.nn.sigmoid(f_pre)
            logf = jnp.log(fg)
            hi = logf.astype(BF16)
            lo = (logf - hi.astype(F32)).astype(BF16)
            units.append(dict(q=q, k=1.0 - fg, hilo=jnp.concatenate([hi, lo], axis=1),
                              v=i_pre, g=g_pre, gn=gn))

    for u in units:
        p = _dot(tri, u["hilo"])
        u["a"] = p[:, 0:HEAD_DIM] + p[:, HEAD_DIM:2 * HEAD_DIM]
    return units


def _hgrn_scores(units, c, exact):
    units = [dict(u) for u in units]
    if exact:
        masks, eye = _level_masks(c)
        sums = _level_ref_sums(c)
    else:
        masks, diag_mask = _hgrn_masks(c)
    for u in units:
        q, k, a = u["q"], u["k"], u["a"]
        parts = []
        if exact:
            p = _dot(sums, u["hilo"])
            p = p[:, 0:HEAD_DIM] + p[:, HEAD_DIM:2 * HEAD_DIM]
            for n, mask in enumerate(masks):
                d = a - p[n * c:(n + 1) * c]
                e = jnp.exp(jnp.minimum(d, -d))
                parts.append((mask, _dot_nt((q * e).astype(BF16), (k * e).astype(BF16))))
            parts.append((eye, jnp.sum(q * k, axis=-1, keepdims=True)))
        else:
            half = c // 2
            level = 0
            while half >= DIAG_BLOCK:
                span = 2 * half
                ref = jnp.concatenate([_row_bcast(a, b * span + half, span) for b in range(c // span)], axis=0)
                qs = (q * jnp.exp(jnp.minimum(a - ref, 0.0))).astype(BF16)
                ks = (k * jnp.exp(jnp.minimum(ref - a, 0.0))).astype(BF16)
                parts.append((masks[level], _dot_nt(qs, ks)))
                half //= 2
                level += 1
            ref = jnp.concatenate([_row_bcast(a, b * DIAG_BLOCK, DIAG_BLOCK) for b in range(c // DIAG_BLOCK)], axis=0)
            qd = (q * jnp.exp(a - ref)).astype(BF16)
            kd = (k * jnp.exp(jnp.minimum(ref - a, SAFE_BLOCK_DECAY))).astype(BF16)
            parts.append((diag_mask, _dot_nt(qd, kd)))
        u["parts"] = parts
        a_last = a[c - 1:c, :]
        u["decay"] = jnp.exp(a_last)
        kl = (k * jnp.exp(a_last - a)).astype(BF16)
        v_t = jnp.transpose(u["v"]).astype(BF16)
        u["upd"] = _dot(v_t, kl)
        u["qe"] = (q * jnp.exp(a)).astype(BF16)
    return units


def _hgrn_finish(chains, units):
    finals = []
    n = 0
    for st0, lbv, gn, blocks in chains:
        st = st0
        for _ in blocks:
            units[n]["st_in"] = st.astype(BF16)
            st = st * units[n]["decay"] + units[n]["upd"]
            n += 1
        finals.append(st)

    outs = []
    n = 0
    for st0, lbv, gn, blocks in chains:
        chain_out = []
        for _ in blocks:
            u = units[n]
            scores = None
            for mask, part in u["parts"]:
                part = jnp.where(mask, part, 0.0)
                scores = part if scores is None else scores + part
            o = _dot(scores.astype(BF16), u["v"].astype(BF16)) + _dot_nt(u["qe"], u["st_in"])
            o = o * lax.rsqrt(jnp.mean(o * o, axis=-1, keepdims=True) + LN_EPS) * u["gn"]
            chain_out.append(o * jax.nn.silu(u["g"]))
            n += 1
        outs.append(chain_out)
    return outs, finals


def _proj_cols(kind, hd):
    c0 = (kind * HEADS_PER_STEP + hd) * HEAD_DIM
    return slice(c0, c0 + HEAD_DIM)


def _in_projection(h_scr, w_refs, proj_scr):
    width = HEADS_PER_STEP * HEAD_DIM
    for kind, w_ref in enumerate(w_refs):
        proj_scr[:, kind * width:(kind + 1) * width] = _dot(h_scr[...], w_ref[...])


def _mixer_prompt_kernel(x_ref, sh_ref, sc_ref, wu_ref, wv_ref, wq_ref, wf_ref, wi_ref, wg_ref,
                         lvg_ref, lvb_ref, ws_ref, bs_ref, lb_ref, gn_ref,
                         cata_ref, cato_ref, state_ref, h_scr, proj_scr, st_scr, *, layer):
    i = pl.program_id(1)
    j = pl.program_id(2)
    tm = h_scr.shape[0]

    @pl.when(j == 0)
    def _():
        def slab(r, carry):
            rows = pl.ds(pl.multiple_of(r * HGRN_BLOCK, HGRN_BLOCK), HGRN_BLOCK)
            h = _ln_plain(x_ref[0, rows, :]) * (1.0 + sc_ref[0]) + sh_ref[0]
            h_scr[rows, :] = h.astype(BF16)
            return carry
        lax.fori_loop(0, tm // HGRN_BLOCK, slab, 0, unroll=2)

    @pl.when((i == 0) & (j == 0))
    def _():
        st_scr[...] = jnp.zeros_like(st_scr)

    _in_projection(h_scr, (wu_ref, wv_ref, wq_ref, wf_ref, wi_ref, wg_ref), proj_scr)

    row = lax.broadcasted_iota(I32, (GMLP_CHUNK, GMLP_CHUNK), 0)
    col = lax.broadcasted_iota(I32, (GMLP_CHUNK, GMLP_CHUNK), 1)
    causal = _block_id(row, SUB_CHUNK) >= _block_id(col, SUB_CHUNK)
    for hd in range(HEADS_PER_STEP):
        wm = jnp.where(causal, ws_ref[hd], 0.0).astype(BF16)
        b_col = bs_ref[hd]
        for c in range(tm // GMLP_CHUNK):
            rows = slice(c * GMLP_CHUNK, (c + 1) * GMLP_CHUNK)
            u, vn = _gmlp_rows(proj_scr[rows, _proj_cols(0, hd)], proj_scr[rows, _proj_cols(1, hd)],
                               lvg_ref[hd], lvb_ref[hd])
            sp = _dot(wm, vn.astype(BF16)) + b_col
            cata_ref[rows, hd * HEAD_DIM:(hd + 1) * HEAD_DIM] = (u * sp).astype(BF16)

    chains = []
    for hd in range(HEADS_PER_STEP):
        blocks = [tuple(proj_scr[c * HGRN_BLOCK:(c + 1) * HGRN_BLOCK, _proj_cols(kind, hd)] for kind in range(2, N_KINDS))
                  for c in range(tm // HGRN_BLOCK)]
        chains.append((st_scr[HEADS_PER_STEP * j + hd], _lower_bound(lb_ref, hd, layer), gn_ref[hd], blocks))
    units = _hgrn_prepare(chains, HGRN_BLOCK)
    wide_decay = _hgrn_block_decay(units, HGRN_BLOCK) > SAFE_BLOCK_DECAY

    def finish(exact):
        outs, finals = _hgrn_finish(chains, _hgrn_scores(units, HGRN_BLOCK, exact))
        for hd in range(HEADS_PER_STEP):
            for c, out in enumerate(outs[hd]):
                cato_ref[c * HGRN_BLOCK:(c + 1) * HGRN_BLOCK, hd * HEAD_DIM:(hd + 1) * HEAD_DIM] = out.astype(BF16)
            st_scr[HEADS_PER_STEP * j + hd] = finals[hd]
            state_ref[0, HEADS_PER_STEP * j + hd] = jnp.transpose(finals[hd])

    @pl.when(wide_decay)
    def _():
        finish(True)

    @pl.when(jnp.logical_not(wide_decay))
    def _():
        finish(False)


def _w_in_specs(d):
    width = HEADS_PER_STEP * HEAD_DIM
    blocks_per_kind = N_HEADS // HEADS_PER_STEP
    return [pl.BlockSpec((d, width), lambda j, kind=kind: (0, kind * blocks_per_kind + j)) for kind in range(N_KINDS)]


def _mixer_prompt_call(x, sh1, sc1, w_in_b, ln_v_g, ln_v_b, w_s, b_s, lb_h, gnorm_g, layer):
    bsz, t, d = x.shape
    n_t = t // MIX_TILE
    hp = HEADS_PER_STEP
    kern = functools.partial(_mixer_prompt_kernel, layer=layer)
    head_vec = lambda b, i, j: (j, 0, 0)
    return pl.pallas_call(
        kern,
        grid=(bsz, n_t, N_HEADS // hp),
        in_specs=[
            pl.BlockSpec((1, MIX_TILE, d), lambda b, i, j: (b, i, 0)),
            pl.BlockSpec((1, 1, d), lambda b, i, j: (b, 0, 0)),
            pl.BlockSpec((1, 1, d), lambda b, i, j: (b, 0, 0)),
            *[pl.BlockSpec((d, hp * HEAD_DIM), lambda b, i, j, kind=kind: (0, kind * (N_HEADS // hp) + j))
              for kind in range(N_KINDS)],
            pl.BlockSpec((hp, 1, HEAD_DIM), head_vec),
            pl.BlockSpec((hp, 1, HEAD_DIM), head_vec),
            pl.BlockSpec((hp, GMLP_CHUNK, GMLP_CHUNK), head_vec),
            pl.BlockSpec((hp, GMLP_CHUNK, 1), head_vec),
            pl.BlockSpec((hp, lb_h.shape[1], HEAD_DIM), head_vec),
            pl.BlockSpec((hp, 1, HEAD_DIM), head_vec),
        ],
        out_specs=[
            pl.BlockSpec((MIX_TILE, hp * HEAD_DIM), lambda b, i, j: (b * n_t + i, j)),
            pl.BlockSpec((MIX_TILE, hp * HEAD_DIM), lambda b, i, j: (b * n_t + i, j)),
            pl.BlockSpec((1, N_HEADS, HEAD_DIM, HEAD_DIM), lambda b, i, j: (b, 0, 0, 0)),
        ],
        out_shape=[
            jax.ShapeDtypeStruct((bsz * t, N_HEADS * HEAD_DIM), BF16),
            jax.ShapeDtypeStruct((bsz * t, N_HEADS * HEAD_DIM), BF16),
            jax.ShapeDtypeStruct((bsz, N_HEADS, HEAD_DIM, HEAD_DIM), F32),
        ],
        scratch_shapes=[
            pltpu.VMEM((MIX_TILE, d), BF16),
            pltpu.VMEM((MIX_TILE, N_KINDS * hp * HEAD_DIM), F32),
            pltpu.VMEM((N_HEADS, HEAD_DIM, HEAD_DIM), F32),
        ],
        compiler_params=_cparams(("arbitrary", "arbitrary", "arbitrary")),
        name="mixer_prompt",
    )(x, sh1.reshape(bsz, 1, d), sc1.reshape(bsz, 1, d), *([w_in_b] * N_KINDS), ln_v_g, ln_v_b, w_s, b_s, lb_h, gnorm_g)


def _mixer_sample_kernel(x_ref, sh_ref, sc_ref, wu_ref, wv_ref, wq_ref, wf_ref, wi_ref, wg_ref,
                         lvg_ref, lvb_ref, ws_ref, bs_ref, lb_ref, gn_ref, s0_ref,
                         cata_ref, cato_ref, state_ref, vrows_ref, h_scr, proj_scr, *, layer, seq):
    j = pl.program_id(0)
    n_streams = x_ref.shape[0] // seq

    @pl.when(j == 0)
    def _():
        def slab(r, carry):
            rows = pl.ds(pl.multiple_of(r * seq, seq), seq)
            h = _ln_plain(x_ref[rows, :]) * (1.0 + sc_ref[pl.ds(r, 1), :]) + sh_ref[pl.ds(r, 1), :]
            h_scr[rows, :] = h.astype(BF16)
            return carry
        lax.fori_loop(0, n_streams, slab, 0)

    _in_projection(h_scr, (wu_ref, wv_ref, wq_ref, wf_ref, wi_ref, wg_ref), proj_scr)

    chains = []
    for hd in range(HEADS_PER_STEP):
        wm = ws_ref[hd][0:seq, 0:seq].astype(BF16)
        b_col = bs_ref[hd][0:seq, :]
        lbv = _lower_bound(lb_ref, hd, layer)
        for n in range(n_streams):
            rows = slice(n * seq, (n + 1) * seq)
            u, vn = _gmlp_rows(proj_scr[rows, _proj_cols(0, hd)], proj_scr[rows, _proj_cols(1, hd)],
                               lvg_ref[hd], lvb_ref[hd])
            vrows_ref[rows, hd * HEAD_DIM:(hd + 1) * HEAD_DIM] = vn
            sp = _dot(wm, vn.astype(BF16)) + b_col
            cata_ref[rows, hd * HEAD_DIM:(hd + 1) * HEAD_DIM] = (u * sp).astype(BF16)
            chains.append((jnp.transpose(s0_ref[n, hd]), lbv, gn_ref[hd],
                           [tuple(proj_scr[rows, _proj_cols(kind, hd)] for kind in range(2, N_KINDS))]))

    units = _hgrn_prepare(chains, seq)
    wide_decay = _hgrn_block_decay(units, seq) > SAFE_BLOCK_DECAY

    def finish(exact):
        outs, finals = _hgrn_finish(chains, _hgrn_scores(units, seq, exact))
        for hd in range(HEADS_PER_STEP):
            for n in range(n_streams):
                r0 = n * seq
                m = hd * n_streams + n
                cato_ref[r0:r0 + seq, hd * HEAD_DIM:(hd + 1) * HEAD_DIM] = outs[m][0].astype(BF16)
                state_ref[n, hd] = jnp.transpose(finals[m])

    @pl.when(wide_decay)
    def _():
        finish(True)

    @pl.when(jnp.logical_not(wide_decay))
    def _():
        finish(False)


def _mixer_sample_call(x2d, sh1, sc1, w_in_b, ln_v_g, ln_v_b, w_s, b_s, lb_h, gnorm_g, s0, layer, seq):
    rows, d = x2d.shape
    n_streams = rows // seq
    hp = HEADS_PER_STEP
    kern = functools.partial(_mixer_sample_kernel, layer=layer, seq=seq)
    head_vec = lambda j: (j, 0, 0)
    return pl.pallas_call(
        kern,
        grid=(N_HEADS // hp,),
        in_specs=[
            pl.BlockSpec((rows, d), lambda j: (0, 0)),
            pl.BlockSpec((n_streams, d), lambda j: (0, 0)),
            pl.BlockSpec((n_streams, d), lambda j: (0, 0)),
            *_w_in_specs(d),
            pl.BlockSpec((hp, 1, HEAD_DIM), head_vec),
            pl.BlockSpec((hp, 1, HEAD_DIM), head_vec),
            pl.BlockSpec((hp, GMLP_CHUNK, GMLP_CHUNK), head_vec),
            pl.BlockSpec((hp, GMLP_CHUNK, 1), head_vec),
            pl.BlockSpec((hp, lb_h.shape[1], HEAD_DIM), head_vec),
            pl.BlockSpec((hp, 1, HEAD_DIM), head_vec),
            pl.BlockSpec((n_streams, hp, HEAD_DIM, HEAD_DIM), lambda j: (0, j, 0, 0)),
        ],
        out_specs=[
            pl.BlockSpec((rows, hp * HEAD_DIM), lambda j: (0, j)),
            pl.BlockSpec((rows, hp * HEAD_DIM), lambda j: (0, j)),
            pl.BlockSpec((n_streams, hp, HEAD_DIM, HEAD_DIM), lambda j: (0, j, 0, 0)),
            pl.BlockSpec((rows, hp * HEAD_DIM), lambda j: (0, j)),
        ],
        out_shape=[
            jax.ShapeDtypeStruct((rows, N_HEADS * HEAD_DIM), BF16),
            jax.ShapeDtypeStruct((rows, N_HEADS * HEAD_DIM), BF16),
            jax.ShapeDtypeStruct((n_streams, N_HEADS, HEAD_DIM, HEAD_DIM), F32),
            jax.ShapeDtypeStruct((rows, N_HEADS * HEAD_DIM), F32),
        ],
        scratch_shapes=[
            pltpu.VMEM((rows, d), BF16),
            pltpu.VMEM((rows, N_KINDS * hp * HEAD_DIM), F32),
        ],
        compiler_params=_cparams(("arbitrary",)),
        name="mixer_sample",
    )(x2d, sh1, sc1, *([w_in_b] * N_KINDS), ln_v_g, ln_v_b, w_s, b_s, lb_h, gnorm_g, s0)


def _post_kernel(grp_ref, cap_ref, cop_ref, cas_ref, cos_ref, xp_ref, xs_ref, g1_ref, sh2_ref, sc2_ref, wout_ref, l1g_ref, l1b_ref,
                 wr_ref, br_ref, tri_ref,
                 x1_ref, h2_ref, meta_ref, wts_ref, cnt_ref,
                 mix_scr, hhi_scr, hlo_scr, run_scr, *, n_prompt_tiles, alpha):
    i = pl.program_id(0)
    is_s = i >= n_prompt_tiles
    tm = mix_scr.shape[0]

    @pl.when(i == 0)
    def _():
        run_scr[...] = jnp.zeros_like(run_scr)

    def slab(g):
        rows = slice(g * MOD_GROUP, (g + 1) * MOD_GROUP)
        x = jnp.where(is_s, xs_ref[rows, :], xp_ref[rows, :])
        stream = pl.ds(grp_ref[0, 0, g], 1)
        z = alpha * x + g1_ref[stream, :] * mix_scr[rows, :]
        x1 = _ln_plain(z) * l1g_ref[...] + l1b_ref[...]
        x1_ref[rows, :] = x1
        h2 = _ln_plain(x1) * (1.0 + sc2_ref[stream, :]) + sh2_ref[stream, :]
        h2_ref[rows, :] = h2
        hi = h2.astype(BF16)
        hhi_scr[rows, :] = hi
        hlo_scr[rows, :] = (h2 - hi.astype(F32)).astype(BF16)

    def project(rows):
        cat = jnp.where(is_s, jnp.concatenate([cas_ref[rows, :], cos_ref[rows, :]], axis=1),
                        jnp.concatenate([cap_ref[rows, :], cop_ref[rows, :]], axis=1))
        mix_scr[rows, :] = _dot(cat, wout_ref[...])

    half = tm // 2
    project(slice(0, half))
    project(slice(half, tm))
    for g in range(tm // MOD_GROUP):
        slab(g)

    _route(hhi_scr, hlo_scr, wr_ref, br_ref, tri_ref, run_scr, meta_ref, wts_ref, cnt_ref)


def _route(hhi_scr, hlo_scr, wr_ref, br_ref, before_ref, run_scr, meta_ref, wts_ref, cnt_ref):
    tm = hhi_scr.shape[0]
    n_r = N_GROUPS + N_EXPERTS
    n_rows = -(-n_r // 8) * 8
    s = _dot(hhi_scr[...], wr_ref[...]) + _dot(hlo_scr[...], wr_ref[...])
    logits = s + pltpu.roll(s, LANES - n_r, axis=1) + br_ref[...]
    lt = jnp.transpose(logits)[0:n_rows, :]

    row = lax.broadcasted_iota(I32, (n_rows, tm), 0)
    row_f = row.astype(F32)
    neg = jnp.float32(-jnp.inf)
    big = jnp.float32(LANES)

    def first_row_of(vals, top):
        return jnp.min(jnp.where(vals == top, row_f, big), axis=0, keepdims=True)

    gl = jnp.where(row < N_GROUPS, lt, neg)
    gmax = jnp.max(gl, axis=0, keepdims=True)
    gsel = first_row_of(gl, gmax)
    p_group = 1.0 / jnp.sum(jnp.exp(gl - gmax), axis=0, keepdims=True)

    e_lo = N_GROUPS + EXPERTS_PER_GROUP * gsel.astype(I32)
    emask = (row >= e_lo) & (row < e_lo + EXPERTS_PER_GROUP)
    el = jnp.where(emask, lt, neg)
    t1 = jnp.max(el, axis=0, keepdims=True)
    i1 = first_row_of(el, t1)
    el2 = jnp.where(row_f == i1, neg, el)
    t2 = jnp.max(el2, axis=0, keepdims=True)
    i2 = first_row_of(el2, t2)
    e2 = jnp.exp(t2 - t1)
    den = 1.0 + e2
    w1 = (1.0 / den) * p_group
    w2 = (e2 / den) * p_group

    sel1 = row_f == i1
    sel2 = row_f == i2
    onehot = jnp.where(sel1 | sel2, 1.0, 0.0)
    run_col = jnp.transpose(jnp.broadcast_to(run_scr[...], (LANES, LANES)))[0:n_rows, 0:1]
    before = _dot(onehot.astype(BF16), before_ref[...]) + run_col
    r1 = jnp.sum(jnp.where(sel1, before, 0.0), axis=0, keepdims=True)
    r2 = jnp.sum(jnp.where(sel2, before, 0.0), axis=0, keepdims=True)
    added = jnp.concatenate([jnp.sum(onehot, axis=1, keepdims=True), jnp.zeros((LANES - n_rows, 1), F32)], axis=0)
    run_scr[...] = run_scr[...] + jnp.transpose(jnp.broadcast_to(added, (LANES, LANES)))[0:1, :]
    cnt_ref[...] = run_scr[...].astype(I32)

    base = jnp.float32(N_GROUPS)
    mrow = lax.broadcasted_iota(I32, (META_ROWS, tm), 0)
    meta = jnp.where(mrow == 0, i1 - base, jnp.where(mrow == 1, i2 - base, jnp.where(mrow == 2, r1, jnp.where(mrow == 3, r2, 0.0))))
    meta_ref[0] = meta.astype(I32)
    wrow = lax.broadcasted_iota(I32, (LANES, tm), 0)
    wts_ref[...] = jnp.transpose(jnp.where(wrow == 0, w1, jnp.where(wrow == 1, w2, 0.0)))


def _post_call(groups3, cats_p, cats_s, x_p, x_s, mod, w_out_b, ln1_g, ln1_b, wr, br, tri, alpha):
    n_p, d = x_p.shape
    n_s = x_s.shape[0]
    d_a = cats_p[0].shape[1]
    tm = POST_TILE
    n_pt = n_p // tm
    n_tiles = n_pt + n_s // tm
    n_tok = n_p + n_s
    groups = tm // MOD_GROUP
    kern = functools.partial(_post_kernel, n_prompt_tiles=n_pt, alpha=alpha)
    p_idx = lambda i: (jnp.minimum(i, n_pt - 1), 0)
    s_idx = lambda i: (jnp.maximum(i - n_pt, 0), 0)
    tile = lambda i: (i, 0)
    const = lambda i: (0, 0)
    once = pl.Buffered(1)
    s_mode = once if n_tiles - n_pt == 1 else None
    mod_spec = lambda m: pl.BlockSpec((mod.shape[0], d), lambda i: (0, m), pipeline_mode=once)
    return pl.pallas_call(
        kern,
        grid=(n_tiles,),
        in_specs=[
            pl.BlockSpec((1, 1, groups), lambda i: (i, 0, 0), memory_space=pltpu.SMEM),
            pl.BlockSpec((tm, d_a), p_idx),
            pl.BlockSpec((tm, d_a), p_idx),
            pl.BlockSpec((tm, d_a), s_idx, pipeline_mode=s_mode),
            pl.BlockSpec((tm, d_a), s_idx, pipeline_mode=s_mode),
            pl.BlockSpec((tm, d), p_idx),
            pl.BlockSpec((tm, d), s_idx, pipeline_mode=s_mode),
            mod_spec(2), mod_spec(3), mod_spec(4),
            pl.BlockSpec((d, d), const, pipeline_mode=once),
            pl.BlockSpec((1, d), const),
            pl.BlockSpec((1, d), const),
            pl.BlockSpec((d, LANES), const, pipeline_mode=once),
            pl.BlockSpec((1, LANES), const),
            pl.BlockSpec((tm, tm), const, pipeline_mode=once),
        ],
        out_specs=[
            pl.BlockSpec((tm, d), tile),
            pl.BlockSpec((tm, d), tile),
            pl.BlockSpec((1, META_ROWS, tm), lambda i: (i, 0, 0)),
            pl.BlockSpec((tm, LANES), tile),
            pl.BlockSpec((1, LANES), const),
        ],
        out_shape=[
            jax.ShapeDtypeStruct((n_tok, d), F32),
            jax.ShapeDtypeStruct((n_tok, d), F32),
            jax.ShapeDtypeStruct((n_tiles, META_ROWS, tm), I32),
            jax.ShapeDtypeStruct((n_tok, LANES), F32),
            jax.ShapeDtypeStruct((1, LANES), I32),
        ],
        scratch_shapes=[
            pltpu.VMEM((tm, d), F32),
            pltpu.VMEM((tm, d), BF16),
            pltpu.VMEM((tm, d), BF16),
            pltpu.VMEM((1, LANES), F32),
        ],
        compiler_params=_cparams(("arbitrary",)),
        name="post_router",
    )(groups3, *cats_p, *cats_s, x_p, x_s, mod, mod, mod, w_out_b, ln1_g, ln1_b, wr, br, tri)


def _dispatch_kernel(slot_ref, gap_start_ref, gap_len_ref, h_ref, xs_ref, zero_scr, sem, gap_sem, tile_sem, *,
                     gaps_per_step):
    i = pl.program_id(0)
    tm = h_ref.shape[0]
    n_gaps = gap_len_ref.shape[0]
    tile_rows = zero_scr.shape[0]

    @pl.when(i == 0)
    def _():
        zero_scr[...] = jnp.zeros_like(zero_scr)

    def row_copy(r, k):
        return pltpu.make_async_copy(h_ref.at[pl.ds(r, 1)], xs_ref.at[pl.ds(slot_ref[0, 0, k * tm + r], 1)], sem)

    def start_rows(r, carry):
        row_copy(r, 0).start(priority=0)
        row_copy(r, 1).start(priority=1)
        return carry
    lax.fori_loop(0, tm, start_rows, 0, unroll=ROW_DMA_UNROLL)

    def gap_head(g):
        return jnp.where(g < n_gaps - 1, (-gap_start_ref[g]) & (GAP_ROWS - 1), 0)

    def gap_chunks(g):
        return jnp.where(g < n_gaps - 1, (gap_len_ref[g] - gap_head(g)) // GAP_ROWS, 0)

    def gap_tiles(g):
        return jnp.where(g == n_gaps - 1, gap_len_ref[n_gaps - 1] // tile_rows, 0)

    def zero_row_copy(g, r):
        return pltpu.make_async_copy(zero_scr.at[pl.ds(0, 1)], xs_ref.at[pl.ds(gap_start_ref[g] + r, 1)], gap_sem)

    def zero_chunk_copy(g, q):
        start = pl.multiple_of(gap_start_ref[g] + gap_head(g) + q * GAP_ROWS, GAP_ROWS)
        return pltpu.make_async_copy(zero_scr.at[pl.ds(0, GAP_ROWS)], xs_ref.at[pl.ds(start, GAP_ROWS)], gap_sem)

    def zero_tile_copy(g, t):
        start = pl.multiple_of(gap_start_ref[g] + t * tile_rows, tile_rows)
        return pltpu.make_async_copy(zero_scr, xs_ref.at[pl.ds(start, tile_rows)], tile_sem)

    gap_copies = ((gap_head, zero_row_copy), (gap_chunks, zero_chunk_copy), (gap_tiles, zero_tile_copy))

    def for_gap_copies(act):
        for j in range(gaps_per_step):
            g = jnp.minimum(i * gaps_per_step + j, n_gaps - 1)
            live = i * gaps_per_step + j < n_gaps
            for count, copy in gap_copies:
                lax.fori_loop(0, jnp.where(live, count(g), 0), lambda q, c, g=g, copy=copy: (act(copy(g, q)), c)[1], 0)

    for_gap_copies(lambda copy: copy.start())
    for _ in range(2 * tm // tile_rows):
        pltpu.make_async_copy(zero_scr, zero_scr, sem).wait()
    for_gap_copies(lambda copy: copy.wait())


def _dispatch_call(slots3, gap_start, gap_len, h2, n_sorted):
    n_tok, d = h2.shape
    n_tiles = n_tok // TOK_TILE
    n_gaps = gap_len.shape[0]
    kern = functools.partial(_dispatch_kernel, gaps_per_step=-(-n_gaps // n_tiles))
    return pl.pallas_call(
        kern,
        grid=(n_tiles,),
        in_specs=[
            pl.BlockSpec((1, 1, slots3.shape[2]), lambda i: (i, 0, 0), memory_space=pltpu.SMEM),
            pl.BlockSpec((n_gaps,), lambda i: (0,), memory_space=pltpu.SMEM),
            pl.BlockSpec((n_gaps,), lambda i: (0,), memory_space=pltpu.SMEM),
            pl.BlockSpec((TOK_TILE, d), lambda i: (i, 0)),
        ],
        out_specs=pl.BlockSpec(memory_space=pl.ANY),
        out_shape=jax.ShapeDtypeStruct((n_sorted, d), F32),
        scratch_shapes=[pltpu.VMEM((EXP_TILE, d), F32)] + [pltpu.SemaphoreType.DMA(())] * 3,
        compiler_params=_cparams(("arbitrary",)),
        name="dispatch",
    )(slots3, gap_start, gap_len, h2)


def _expert_kernel(te_ref, nt_ref, nxt_ref, run_ref, xs_ref, w1_ref, w3_ref, w2_ref, ys_ref,
                   x_scr, w1_scr, w3_scr, w2f_scr, w13_scr, w2_scr, xsem, wsem):
    t = pl.program_id(0)
    n_t = pl.num_programs(0)
    rows = x_scr.shape[1]
    d_exp = w1_scr.shape[2]
    x_slots = x_scr.shape[0]
    xs = lax.rem(t, x_slots)
    e = te_ref[t]
    prev = te_ref[jnp.maximum(t - 1, 0)]
    changed = (t == 0) | (e != prev)
    valid = t < nt_ref[0]
    wslot = lax.rem(run_ref[t], 2)

    def x_copy(tile, slot):
        start = pl.multiple_of(tile * rows, rows)
        return pltpu.make_async_copy(xs_ref.at[pl.ds(start, rows)], x_scr.at[slot], xsem.at[slot])

    def w_copies(expert, slot):
        return (pltpu.make_async_copy(w1_ref.at[expert], w1_scr.at[slot], wsem.at[slot]),
                pltpu.make_async_copy(w3_ref.at[expert], w3_scr.at[slot], wsem.at[slot]),
                pltpu.make_async_copy(w2_ref.at[expert], w2f_scr.at[slot], wsem.at[slot]))

    @pl.when(t == 0)
    def _():
        for ahead in range(X_AHEAD):
            x_copy(ahead, ahead).start()
        for c in w_copies(e, wslot):
            c.start()

    @pl.when(t + X_AHEAD < n_t)
    def _():
        x_copy(t + X_AHEAD, lax.rem(t + X_AHEAD, x_slots)).start()

    @pl.when(changed)
    def _():
        for c in w_copies(e, wslot):
            c.wait()
        nxt = nxt_ref[t]

        @pl.when(nxt != e)
        def _():
            for c in w_copies(nxt, 1 - wslot):
                c.start()
        w13_scr[:, 0:d_exp] = w1_scr[wslot].astype(BF16)
        w13_scr[:, d_exp:2 * d_exp] = w3_scr[wslot].astype(BF16)
        w2_scr[...] = w2f_scr[wslot].astype(BF16)

    x_copy(t, xs).wait()

    @pl.when(valid)
    def _():
        h13 = _dot(x_scr[xs].astype(BF16), w13_scr[...])
        hm = jax.nn.silu(h13[:, 0:d_exp]) * h13[:, d_exp:2 * d_exp]
        _store_rows(ys_ref, (), 0, _dot(hm.astype(BF16), w2_scr[...]))

    @pl.when(jnp.logical_not(valid))
    def _():
        ys_ref[...] = jnp.zeros_like(ys_ref)


def _expert_call(tile_expert, n_valid, next_expert, run_id, xs, w1, w3, w2):
    n_sorted, d = xs.shape
    n_exp, _, d_exp = w1.shape
    n_tiles = n_sorted // EXP_TILE
    assert n_tiles > X_AHEAD
    grid_spec = pltpu.PrefetchScalarGridSpec(
        num_scalar_prefetch=4,
        grid=(n_tiles,),
        in_specs=[pl.BlockSpec(memory_space=pl.ANY)] * 4,
        out_specs=pl.BlockSpec((EXP_TILE * _lines_per_row(d), LANES), lambda t, te, nt, nx, rn: (t, 0)),
        scratch_shapes=[
            pltpu.VMEM((X_AHEAD + 1, EXP_TILE, d), F32),
            pltpu.VMEM((2, d, d_exp), F32),
            pltpu.VMEM((2, d, d_exp), F32),
            pltpu.VMEM((2, d_exp, d), F32),
            pltpu.VMEM((d, 2 * d_exp), BF16),
            pltpu.VMEM((d_exp, d), BF16),
            pltpu.SemaphoreType.DMA((X_AHEAD + 1,)),
            pltpu.SemaphoreType.DMA((2,)),
        ],
    )
    return pl.pallas_call(
        _expert_kernel,
        grid_spec=grid_spec,
        out_shape=jax.ShapeDtypeStruct((n_sorted * _lines_per_row(d), LANES), F32),
        compiler_params=_cparams(("arbitrary",)),
        name="experts",
    )(tile_expert, n_valid, next_expert, run_id, xs, w1, w3, w2)


def _combine_kernel(scur_ref, snext_ref, grp_ref, x1_ref, wts_ref, g2_ref, l2g_ref, l2b_ref, ys_ref, outp_ref, outs_ref,
                    y_scr, sem, *, n_prompt_tiles, alpha):
    i = pl.program_id(0)
    n_i = pl.num_programs(0)
    is_s = i >= n_prompt_tiles
    tm, d = x1_ref.shape
    lpr = _lines_per_row(d)
    cur = lax.rem(i, 2)

    def row_lines(r):
        return pl.ds(pl.multiple_of(r * lpr, lpr), lpr)

    def start_gathers(s_ref, buf):
        def body(r, carry):
            for k in range(2):
                pltpu.make_async_copy(ys_ref.at[row_lines(s_ref[0, 0, k * tm + r])], y_scr.at[buf, k, row_lines(r)],
                                      sem.at[buf, k]).start(priority=k)
            return carry
        lax.fori_loop(0, tm, body, 0, unroll=ROW_DMA_UNROLL // 2)

    @pl.when(i == 0)
    def _():
        start_gathers(scur_ref, 0)

    @pl.when(i + 1 < n_i)
    def _():
        start_gathers(snext_ref, 1 - cur)

    for k in range(2):
        pltpu.make_async_copy(y_scr.at[cur, k], y_scr.at[cur, k], sem.at[cur, k]).wait()

    def slab(g, carry, out_ref):
        rows = pl.ds(pl.multiple_of(g * MOD_GROUP, MOD_GROUP), MOD_GROUP)
        w = wts_ref[rows, :]
        moe = (w[:, 0:1] * _load_rows(y_scr, (cur, 0), g * MOD_GROUP, MOD_GROUP, d)
               + w[:, 1:2] * _load_rows(y_scr, (cur, 1), g * MOD_GROUP, MOD_GROUP, d))
        z = alpha * x1_ref[rows, :] + g2_ref[pl.ds(grp_ref[0, 0, g], 1), :] * moe
        out_ref[rows, :] = _ln_plain(z) * l2g_ref[...] + l2b_ref[...]
        return carry

    @pl.when(is_s)
    def _():
        lax.fori_loop(0, tm // MOD_GROUP, functools.partial(slab, out_ref=outs_ref), 0, unroll=SLAB_UNROLL)

    @pl.when(jnp.logical_not(is_s))
    def _():
        lax.fori_loop(0, tm // MOD_GROUP, functools.partial(slab, out_ref=outp_ref), 0, unroll=SLAB_UNROLL)


def _combine_call(slots3, groups3, x1, wts, mod, ln2_g, ln2_b, ys, n_p, alpha):
    n_tok, d = x1.shape
    n_tiles = n_tok // TOK_TILE
    n_pt = n_p // TOK_TILE
    groups = TOK_TILE // MOD_GROUP
    kern = functools.partial(_combine_kernel, n_prompt_tiles=n_pt, alpha=alpha)
    slot_tile = lambda ahead: pl.BlockSpec(
        (1, 1, slots3.shape[2]), lambda i: (jnp.minimum(i + ahead, n_tiles - 1), 0, 0), memory_space=pltpu.SMEM)
    return pl.pallas_call(
        kern,
        grid=(n_tiles,),
        in_specs=[
            slot_tile(0), slot_tile(1),
            pl.BlockSpec((1, 1, groups), lambda i: (i, 0, 0), memory_space=pltpu.SMEM),
            pl.BlockSpec((TOK_TILE, d), lambda i: (i, 0)),
            pl.BlockSpec((TOK_TILE, LANES), lambda i: (i, 0)),
            pl.BlockSpec((mod.shape[0], d), lambda i: (0, 5)),
            pl.BlockSpec((1, d), lambda i: (0, 0)),
            pl.BlockSpec((1, d), lambda i: (0, 0)),
            pl.BlockSpec(memory_space=pl.ANY),
        ],
        out_specs=[
            pl.BlockSpec((TOK_TILE, d), lambda i: (jnp.minimum(i, n_pt - 1), 0)),
            pl.BlockSpec((TOK_TILE, d), lambda i: (jnp.maximum(i - n_pt, 0), 0)),
        ],
        out_shape=[
            jax.ShapeDtypeStruct((n_p, d), F32),
            jax.ShapeDtypeStruct((n_tok - n_p, d), F32),
        ],
        scratch_shapes=[
            pltpu.VMEM((2, 2, TOK_TILE * _lines_per_row(d), LANES), F32),
            pltpu.SemaphoreType.DMA((2, 2)),
        ],
        compiler_params=_cparams(("arbitrary",)),
        name="combine",
    )(slots3, slots3, groups3, x1, wts, mod, ln2_g, ln2_b, ys)


def _routing_tables(meta, cnt_row, n_sorted):
    experts = meta[:, 0:2, :]
    ranks = meta[:, 2:4, :]
    cnt = cnt_row[0, N_GROUPS:N_GROUPS + N_EXPERTS]
    padded = ((cnt + EXP_TILE - 1) // EXP_TILE) * EXP_TILE
    ends = jnp.cumsum(padded)
    offs = ends - padded
    ids = jnp.arange(N_EXPERTS, dtype=I32)
    is_expert = experts[None] == ids[:, None, None, None]
    slots = jnp.sum(jnp.where(is_expert, offs[:, None, None, None], 0), axis=0) + ranks
    total = ends[-1]
    n_tiles = n_sorted // EXP_TILE
    tile_ids = jnp.arange(n_tiles, dtype=I32)
    tile_expert = jnp.minimum(
        jnp.sum((tile_ids[:, None] >= (ends // EXP_TILE)[None, :]).astype(I32), axis=1), N_EXPERTS - 1)
    n_valid = (total // EXP_TILE).reshape(1).astype(I32)
    later = (ids[None, :] > tile_expert[:, None]) & (padded[None, :] > 0)
    next_expert = jnp.min(jnp.where(later, ids[None, :], N_EXPERTS), axis=1)
    next_expert = jnp.where(next_expert == N_EXPERTS, tile_expert, next_expert).astype(I32)
    starts_run = jnp.concatenate([jnp.zeros((1,), I32), (tile_expert[1:] != tile_expert[:-1]).astype(I32)])
    run_id = jnp.cumsum(starts_run).astype(I32)
    gap_start = jnp.concatenate([offs + cnt, total.reshape(1)]).astype(I32)
    gap_len = jnp.concatenate([padded - cnt, (n_sorted - total).reshape(1)]).astype(I32)
    return slots.astype(I32), (tile_expert, n_valid, next_expert, run_id), gap_start, gap_len


def _layer(layer, n_layers, xp, xs, s0_l, c_all, p):
    (w_ada, b_ada, w_in, ln_v_g, ln_v_b, w_s, b_s, hgrn_lb, gnorm_g, w_out, ln1_g, ln1_b,
     w_rg, b_rg, w_re, b_re, w1, w3, w2, ln2_g, ln2_b) = p
    bsz, t, d = xp.shape
    n_streams, seq, _ = xs.shape
    alpha = float((2.0 * n_layers) ** 0.25)
    n_p = bsz * t
    n_s = n_streams * seq
    n_tok = n_p + n_s

    n_c = c_all.shape[0]
    c_pad = jnp.pad(c_all, ((0, (-n_c) % 8), (0, 0)))
    mod = _ada_call(c_pad, w_ada, b_ada)
    sh1, sc1 = mod[:n_c, 0:d], mod[:n_c, d:2 * d]

    w_in_b = w_in.astype(BF16)
    w_out_b = w_out.astype(BF16)
    lvg = ln_v_g.reshape(N_HEADS, 1, HEAD_DIM)
    lvb = ln_v_b.reshape(N_HEADS, 1, HEAD_DIM)
    b_s3 = b_s.reshape(N_HEADS, GMLP_CHUNK, 1)
    lb_h = jnp.transpose(hgrn_lb.reshape(hgrn_lb.shape[0], N_HEADS, HEAD_DIM), (1, 0, 2))
    gn = gnorm_g.reshape(N_HEADS, 1, HEAD_DIM)

    ca_p, co_p, state_p = _mixer_prompt_call(xp, sh1[:bsz], sc1[:bsz], w_in_b, lvg, lvb, w_s, b_s3, lb_h, gn, layer)
    ca_s, co_s, state_s, vrows = _mixer_sample_call(xs.reshape(n_s, d), sh1[bsz:], sc1[bsz:], w_in_b, lvg, lvb, w_s,
                                               b_s3, lb_h, gn, s0_l, layer, seq)

    group_stream = jnp.concatenate([
        jnp.repeat(jnp.arange(bsz, dtype=I32), t // MOD_GROUP),
        bsz + jnp.repeat(jnp.arange(n_streams, dtype=I32), seq // MOD_GROUP)])

    wr = jnp.concatenate([w_rg, w_re], axis=1)
    wr_hi = wr.astype(BF16)
    wr_lo = (wr - wr_hi.astype(F32)).astype(BF16)
    n_r = wr.shape[1]
    wr_cat = jnp.concatenate([wr_hi, wr_lo, jnp.zeros((d, LANES - 2 * n_r), BF16)], axis=1)
    br = jnp.pad(jnp.concatenate([b_rg, b_re]), (0, LANES - n_r)).reshape(1, LANES)
    tri = jnp.triu(jnp.ones((POST_TILE, POST_TILE), F32), 1).astype(BF16)

    x1, h2, meta, wts, cnt_row = _post_call(group_stream.reshape(n_tok // POST_TILE, 1, POST_TILE // MOD_GROUP),
                                            (ca_p, co_p), (ca_s, co_s), xp.reshape(n_p, d), xs.reshape(n_s, d),
                                            mod, w_out_b, ln1_g.reshape(1, d), ln1_b.reshape(1, d), wr_cat, br, tri, alpha)

    n_sorted = 2 * n_tok + N_EXPERTS * EXP_TILE
    slots, tile_tables, gap_start, gap_len = _routing_tables(meta, cnt_row, n_sorted)
    n_tiles = n_tok // TOK_TILE
    assert POST_TILE == TOK_TILE
    slots3 = slots.reshape(n_tiles, 1, 2 * TOK_TILE)

    xs_sorted = _dispatch_call(slots3, gap_start, gap_len, h2, n_sorted)
    ys_sorted = _expert_call(*tile_tables, xs_sorted, w1, w3, w2)
    yp, ys_out = _combine_call(slots3, group_stream.reshape(n_tiles, 1, TOK_TILE // MOD_GROUP), x1, wts, mod,
                               ln2_g.reshape(1, d), ln2_b.reshape(1, d), ys_sorted, n_p, alpha)

    v_rows = vrows.reshape(n_streams, seq, N_HEADS, HEAD_DIM)
    return yp.reshape(bsz, t, d), ys_out.reshape(n_streams, seq, d), state_p, state_s, v_rows


def kernel(x_prompt, x_sample, state_hgrn, c_prompt, c_sample, w_ada, b_ada, w_in, ln_v_g, ln_v_b, w_s, b_s, hgrn_lb, gnorm_g, w_out, ln1_g, ln1_b, w_router_g, b_router_g, w_router_e, b_router_e, w1, w3, w2, ln2_g, ln2_b):
    n_layers = w_ada.shape[0]
    assert x_prompt.shape[1] % MIX_TILE == 0 and x_prompt.shape[2] == 2 * N_HEADS * HEAD_DIM
    assert x_sample.shape[1] % MOD_GROUP == 0 and x_sample.shape[1] <= SUB_CHUNK
    assert (x_sample.shape[0] * x_sample.shape[1]) % TOK_TILE == 0 and TOK_TILE % POST_TILE == 0
    c_all = jnp.concatenate([c_prompt, c_sample], axis=0)
    xp, xs = x_prompt, x_sample
    sp_list, ss_list, vs_list = [], [], []
    for l in range(n_layers):
        p = (w_ada[l], b_ada[l], w_in[l], ln_v_g[l], ln_v_b[l], w_s[l], b_s[l], hgrn_lb, gnorm_g[l], w_out[l],
             ln1_g[l], ln1_b[l], w_router_g[l], b_router_g[l], w_router_e[l], b_router_e[l],
             w1[l], w3[l], w2[l], ln2_g[l], ln2_b[l])
        xp, xs, sp, ss, vs = _layer(l, n_layers, xp, xs, state_hgrn[l], c_all, p)
        sp_list.append(sp.astype(state_hgrn.dtype))
        ss_list.append(ss.astype(state_hgrn.dtype))
        vs_list.append(vs)
    return (xp, xs, jnp.stack(sp_list, axis=0), jnp.stack(ss_list, axis=0), jnp.stack(vs_list, axis=0))
```

```python
import functools

import jax
import jax.numpy as jnp
from jax import lax
from jax.experimental import pallas as pl
from jax.experimental.pallas import tpu as pltpu

F32 = jnp.float32
BF16 = jnp.bfloat16
I32 = jnp.int32

N_HEADS = 8
HEAD_DIM = 128
GMLP_CHUNK = 128
SUB_CHUNK = 64
N_GROUPS = 4
EXPERTS_PER_GROUP = 8
N_EXPERTS = N_GROUPS * EXPERTS_PER_GROUP
LN_EPS = 1e-5
N_KINDS = 6

LANES = 128
MIX_TILE = 512
HEADS_PER_STEP = 2
HGRN_BLOCK = 64
DIAG_BLOCK = 16
SAFE_BLOCK_DECAY = 60.0
TOK_TILE = 512
POST_TILE = 512
META_ROWS = 8
MOD_GROUP = 32
SLAB_UNROLL = 4
EXP_TILE = 256
X_AHEAD = 2
ROW_DMA_UNROLL = 8
GAP_ROWS = 8
ADA_TILE = 1024
VMEM_LIMIT = 56 * 1024 * 1024


def _cparams(sem):
    return pltpu.CompilerParams(dimension_semantics=sem, vmem_limit_bytes=VMEM_LIMIT)


def _ln_plain(x):
    mu = jnp.mean(x, axis=-1, keepdims=True)
    xc = x - mu
    var = jnp.mean(xc * xc, axis=-1, keepdims=True)
    return xc * lax.rsqrt(var + LN_EPS)


def _dot(a, b):
    return jnp.dot(a, b, preferred_element_type=F32)


def _dot_nt(a, b):
    return lax.dot_general(a, b, (((1,), (1,)), ((), ())), preferred_element_type=F32)


def _lines_per_row(d):
    return d // LANES


def _load_rows(ref, lead, row0, n_rows, d):
    lpr = _lines_per_row(d)
    parts = [ref[lead + (pl.ds(row0 * lpr + c, n_rows, stride=lpr), slice(None))] for c in range(lpr)]
    return jnp.concatenate(parts, axis=1)


def _store_rows(ref, lead, row0, val):
    n_rows, d = val.shape
    lpr = _lines_per_row(d)
    for c in range(lpr):
        ref[lead + (pl.ds(row0 * lpr + c, n_rows, stride=lpr), slice(None))] = val[:, c * LANES:(c + 1) * LANES]


def _ada_kernel(c_ref, w_ref, b_ref, o_ref):
    s = jax.nn.silu(c_ref[...]).astype(BF16)
    o_ref[...] = _dot(s, w_ref[...].astype(BF16)) + b_ref[...]


def _ada_call(c_pad, w_ada, b_ada):
    rows, d = c_pad.shape
    n_out = w_ada.shape[1]
    return pl.pallas_call(
        _ada_kernel,
        grid=(n_out // ADA_TILE,),
        in_specs=[
            pl.BlockSpec((rows, d), lambda n: (0, 0)),
            pl.BlockSpec((d, ADA_TILE), lambda n: (0, n)),
            pl.BlockSpec((1, ADA_TILE), lambda n: (0, n)),
        ],
        out_specs=pl.BlockSpec((rows, ADA_TILE), lambda n: (0, n)),
        out_shape=jax.ShapeDtypeStruct((rows, n_out), F32),
        compiler_params=_cparams(("arbitrary",)),
        name="adaln",
    )(c_pad, w_ada, b_ada.reshape(1, n_out))


def _lower_bound(lb_ref, hd, layer):
    raw = lb_ref[hd]
    m = jnp.max(raw, axis=0, keepdims=True)
    e = jnp.exp(raw - m)
    p = e / jnp.sum(e, axis=0, keepdims=True)
    return jnp.sum(p[: layer + 1], axis=0, keepdims=True)


def _gmlp_rows(u_pre, v_pre, ln_g, ln_b):
    u = jax.nn.gelu(u_pre)
    v = jax.nn.gelu(v_pre)
    mu = jnp.mean(v, axis=-1, keepdims=True)
    vc = v - mu
    var = jnp.mean(vc * vc, axis=-1, keepdims=True)
    vn = vc * lax.rsqrt(var + LN_EPS) * ln_g + ln_b
    return u, vn


def _row_bcast(a, row, n):
    return jnp.broadcast_to(a[row:row + 1, :], (n, a.shape[1]))


def _block_id(idx, size):
    return lax.shift_right_logical(idx, I32(size.bit_length() - 1))


def _hgrn_masks(c):
    row = lax.broadcasted_iota(I32, (c, c), 0)
    col = lax.broadcasted_iota(I32, (c, c), 1)
    masks = []
    half = c // 2
    while half >= DIAG_BLOCK:
        span = 2 * half
        same = _block_id(row, span) == _block_id(col, span)
        masks.append(same & ((row & (span - 1)) >= half) & ((col & (span - 1)) < half))
        half //= 2
    diag = (_block_id(row, DIAG_BLOCK) == _block_id(col, DIAG_BLOCK)) & (col <= row)
    return masks, diag


def _tri_ones(c):
    row = lax.broadcasted_iota(I32, (c, c), 0)
    col = lax.broadcasted_iota(I32, (c, c), 1)
    return jnp.where(col <= row, 1.0, 0.0).astype(BF16)


def _level_halves(c):
    halves = []
    half = c // 2
    while half >= 1:
        halves.append(half)
        half //= 2
    return halves


def _level_masks(c):
    row = lax.broadcasted_iota(I32, (c, c), 0)
    col = lax.broadcasted_iota(I32, (c, c), 1)
    masks = []
    for half in _level_halves(c):
        span = 2 * half
        same = _block_id(row, span) == _block_id(col, span)
        masks.append(same & ((row & (span - 1)) >= half) & ((col & (span - 1)) < half))
    return masks, row == col


def _level_ref_sums(c):
    row = lax.broadcasted_iota(I32, (c, c), 0)
    col = lax.broadcasted_iota(I32, (c, c), 1)
    pieces = [col <= _block_id(row, 2 * half) * (2 * half) + half for half in _level_halves(c)]
    return jnp.concatenate([jnp.where(p, 1.0, 0.0).astype(BF16) for p in pieces], axis=0)


def _hgrn_block_decay(units, c):
    worst = None
    for u in units:
        a = u["a"]
        for b in range(c // DIAG_BLOCK):
            span = a[b * DIAG_BLOCK:b * DIAG_BLOCK + 1, :] - a[(b + 1) * DIAG_BLOCK - 1:(b + 1) * DIAG_BLOCK, :]
            worst = span if worst is None else jnp.maximum(worst, span)
    return jnp.max(worst)


def _hgrn_prepare(chains, c):
    tri = _tri_ones(c)
    units = []
    for st0, lbv, gn, blocks in chains:
        for q_pre, f_pre, i_pre, g_pre in blocks:
            q = jax.nn.silu(q_pre)
            fg = lbv + (1.0 - lbv) * jax.nn.sigmoid(f_pre)
            logf = jnp.log(fg)
            hi = logf.astype(BF16)
            lo = (logf - hi.astype(F32)).astype(BF16)
            units.append(dict(q=q, k=1.0 - fg, hilo=jnp.concatenate([hi, lo], axis=1),
                              v=i_pre, g=g_pre, gn=gn))

    for u in units:
        p = _dot(tri, u["hilo"])
        u["a"] = p[:, 0:HEAD_DIM] + p[:, HEAD_DIM:2 * HEAD_DIM]
    return units


def _hgrn_scores(units, c, exact):
    units = [dict(u) for u in units]
    if exact:
        masks, eye = _level_masks(c)
        sums = _level_ref_sums(c)
    else:
        masks, diag_mask = _hgrn_masks(c)
    for u in units:
        q, k, a = u["q"], u["k"], u["a"]
        parts = []
        if exact:
            p = _dot(sums, u["hilo"])
            p = p[:, 0:HEAD_DIM] + p[:, HEAD_DIM:2 * HEAD_DIM]
            for n, mask in enumerate(masks):
                d = a - p[n * c:(n + 1) * c]
                e = jnp.exp(jnp.minimum(d, -d))
                parts.append((mask, _dot_nt((q * e).astype(BF16), (k * e).astype(BF16))))
            parts.append((eye, jnp.sum(q * k, axis=-1, keepdims=True)))
        else:
            half = c // 2
            level = 0
            while half >= DIAG_BLOCK:
                span = 2 * half
                ref = jnp.concatenate([_row_bcast(a, b * span + half, span) for b in range(c // span)], axis=0)
                qs = (q * jnp.exp(jnp.minimum(a - ref, 0.0))).astype(BF16)
                ks = (k * jnp.exp(jnp.minimum(ref - a, 0.0))).astype(BF16)
                parts.append((masks[level], _dot_nt(qs, ks)))
                half //= 2
                level += 1
            ref = jnp.concatenate([_row_bcast(a, b * DIAG_BLOCK, DIAG_BLOCK) for b in range(c // DIAG_BLOCK)], axis=0)
            qd = (q * jnp.exp(a - ref)).astype(BF16)
            kd = (k * jnp.exp(jnp.minimum(ref - a, SAFE_BLOCK_DECAY))).astype(BF16)
            parts.append((diag_mask, _dot_nt(qd, kd)))
        u["parts"] = parts
        a_last = a[c - 1:c, :]
        u["decay"] = jnp.exp(a_last)
        kl = (k * jnp.exp(a_last - a)).astype(BF16)
        v_t = jnp.transpose(u["v"]).astype(BF16)
        u["upd"] = _dot(v_t, kl)
        u["qe"] = (q * jnp.exp(a)).astype(BF16)
    return units


def _hgrn_finish(chains, units):
    finals = []
    n = 0
    for st0, lbv, gn, blocks in chains:
        st = st0
        for _ in blocks:
            units[n]["st_in"] = st.astype(BF16)
            st = st * units[n]["decay"] + units[n]["upd"]
            n += 1
        finals.append(st)

    outs = []
    n = 0
    for st0, lbv, gn, blocks in chains:
        chain_out = []
        for _ in blocks:
            u = units[n]
            scores = None
            for mask, part in u["parts"]:
                part = jnp.where(mask, part, 0.0)
                scores = part if scores is None else scores + part
            o = _dot(scores.astype(BF16), u["v"].astype(BF16)) + _dot_nt(u["qe"], u["st_in"])
            o = o * lax.rsqrt(jnp.mean(o * o, axis=-1, keepdims=True) + LN_EPS) * u["gn"]
            chain_out.append(o * jax.nn.silu(u["g"]))
            n += 1
        outs.append(chain_out)
    return outs, finals


def _proj_cols(kind, hd):
    c0 = (kind * HEADS_PER_STEP + hd) * HEAD_DIM
    return slice(c0, c0 + HEAD_DIM)


def _in_projection(h_scr, w_refs, proj_scr):
    width = HEADS_PER_STEP * HEAD_DIM
    for kind, w_ref in enumerate(w_refs):
        proj_scr[:, kind * width:(kind + 1) * width] = _dot(h_scr[...], w_ref[...])


def _mixer_prompt_kernel(x_ref, sh_ref, sc_ref, wu_ref, wv_ref, wq_ref, wf_ref, wi_ref, wg_ref,
                         lvg_ref, lvb_ref, ws_ref, bs_ref, lb_ref, gn_ref,
                         cata_ref, cato_ref, state_ref, h_scr, proj_scr, st_scr, *, layer):
    i = pl.program_id(1)
    j = pl.program_id(2)
    tm = h_scr.shape[0]

    @pl.when(j == 0)
    def _():
        def slab(r, carry):
            rows = pl.ds(pl.multiple_of(r * HGRN_BLOCK, HGRN_BLOCK), HGRN_BLOCK)
            h = _ln_plain(x_ref[0, rows, :]) * (1.0 + sc_ref[0]) + sh_ref[0]
            h_scr[rows, :] = h.astype(BF16)
            return carry
        lax.fori_loop(0, tm // HGRN_BLOCK, slab, 0, unroll=2)

    @pl.when((i == 0) & (j == 0))
    def _():
        st_scr[...] = jnp.zeros_like(st_scr)

    _in_projection(h_scr, (wu_ref, wv_ref, wq_ref, wf_ref, wi_ref, wg_ref), proj_scr)

    row = lax.broadcasted_iota(I32, (GMLP_CHUNK, GMLP_CHUNK), 0)
    col = lax.broadcasted_iota(I32, (GMLP_CHUNK, GMLP_CHUNK), 1)
    causal = _block_id(row, SUB_CHUNK) >= _block_id(col, SUB_CHUNK)
    for hd in range(HEADS_PER_STEP):
        wm = jnp.where(causal, ws_ref[hd], 0.0).astype(BF16)
        b_col = bs_ref[hd]
        for c in range(tm // GMLP_CHUNK):
            rows = slice(c * GMLP_CHUNK, (c + 1) * GMLP_CHUNK)
            u, vn = _gmlp_rows(proj_scr[rows, _proj_cols(0, hd)], proj_scr[rows, _proj_cols(1, hd)],
                               lvg_ref[hd], lvb_ref[hd])
            sp = _dot(wm, vn.astype(BF16)) + b_col
            cata_ref[rows, hd * HEAD_DIM:(hd + 1) * HEAD_DIM] = (u * sp).astype(BF16)

    chains = []
    for hd in range(HEADS_PER_STEP):
        blocks = [tuple(proj_scr[c * HGRN_BLOCK:(c + 1) * HGRN_BLOCK, _proj_cols(kind, hd)] for kind in range(2, N_KINDS))
                  for c in range(tm // HGRN_BLOCK)]
        chains.append((st_scr[HEADS_PER_STEP * j + hd], _lower_bound(lb_ref, hd, layer), gn_ref[hd], blocks))
    units = _hgrn_prepare(chains, HGRN_BLOCK)
    wide_decay = _hgrn_block_decay(units, HGRN_BLOCK) > SAFE_BLOCK_DECAY

    def finish(exact):
        outs, finals = _hgrn_finish(chains, _hgrn_scores(units, HGRN_BLOCK, exact))
        for hd in range(HEADS_PER_STEP):
            for c, out in enumerate(outs[hd]):
                cato_ref[c * HGRN_BLOCK:(c + 1) * HGRN_BLOCK, hd * HEAD_DIM:(hd + 1) * HEAD_DIM] = out.astype(BF16)
            st_scr[HEADS_PER_STEP * j + hd] = finals[hd]
            state_ref[0, HEADS_PER_STEP * j + hd] = jnp.transpose(finals[hd])

    @pl.when(wide_decay)
    def _():
        finish(True)

    @pl.when(jnp.logical_not(wide_decay))
    def _():
        finish(False)


def _w_in_specs(d):
    width = HEADS_PER_STEP * HEAD_DIM
    blocks_per_kind = N_HEADS // HEADS_PER_STEP
    return [pl.BlockSpec((d, width), lambda j, kind=kind: (0, kind * blocks_per_kind + j)) for kind in range(N_KINDS)]


def _mixer_prompt_call(x, sh1, sc1, w_in_b, ln_v_g, ln_v_b, w_s, b_s, lb_h, gnorm_g, layer):
    bsz, t, d = x.shape
    n_t = t // MIX_TILE
    hp = HEADS_PER_STEP
    kern = functools.partial(_mixer_prompt_kernel, layer=layer)
    head_vec = lambda b, i, j: (j, 0, 0)
    return pl.pallas_call(
        kern,
        grid=(bsz, n_t, N_HEADS // hp),
        in_specs=[
            pl.BlockSpec((1, MIX_TILE, d), lambda b, i, j: (b, i, 0)),
            pl.BlockSpec((1, 1, d), lambda b, i, j: (b, 0, 0)),
            pl.BlockSpec((1, 1, d), lambda b, i, j: (b, 0, 0)),
            *[pl.BlockSpec((d, hp * HEAD_DIM), lambda b, i, j, kind=kind: (0, kind * (N_HEADS // hp) + j))
              for kind in range(N_KINDS)],
            pl.BlockSpec((hp, 1, HEAD_DIM), head_vec),
            pl.BlockSpec((hp, 1, HEAD_DIM), head_vec),
            pl.BlockSpec((hp, GMLP_CHUNK, GMLP_CHUNK), head_vec),
            pl.BlockSpec((hp, GMLP_CHUNK, 1), head_vec),
            pl.BlockSpec((hp, lb_h.shape[1], HEAD_DIM), head_vec),
            pl.BlockSpec((hp, 1, HEAD_DIM), head_vec),
        ],
        out_specs=[
            pl.BlockSpec((MIX_TILE, hp * HEAD_DIM), lambda b, i, j: (b * n_t + i, j)),
            pl.BlockSpec((MIX_TILE, hp * HEAD_DIM), lambda b, i, j: (b * n_t + i, j)),
            pl.BlockSpec((1, N_HEADS, HEAD_DIM, HEAD_DIM), lambda b, i, j: (b, 0, 0, 0)),
        ],
        out_shape=[
            jax.ShapeDtypeStruct((bsz * t, N_HEADS * HEAD_DIM), BF16),
            jax.ShapeDtypeStruct((bsz * t, N_HEADS * HEAD_DIM), BF16),
            jax.ShapeDtypeStruct((bsz, N_HEADS, HEAD_DIM, HEAD_DIM), F32),
        ],
        scratch_shapes=[
            pltpu.VMEM((MIX_TILE, d), BF16),
            pltpu.VMEM((MIX_TILE, N_KINDS * hp * HEAD_DIM), F32),
            pltpu.VMEM((N_HEADS, HEAD_DIM, HEAD_DIM), F32),
        ],
        compiler_params=_cparams(("arbitrary", "arbitrary", "arbitrary")),
        name="mixer_prompt",
    )(x, sh1.reshape(bsz, 1, d), sc1.reshape(bsz, 1, d), *([w_in_b] * N_KINDS), ln_v_g, ln_v_b, w_s, b_s, lb_h, gnorm_g)


def _mixer_sample_kernel(x_ref, sh_ref, sc_ref, wu_ref, wv_ref, wq_ref, wf_ref, wi_ref, wg_ref,
                         lvg_ref, lvb_ref, ws_ref, bs_ref, lb_ref, gn_ref, s0_ref,
                         cata_ref, cato_ref, state_ref, vrows_ref, h_scr, proj_scr, *, layer, seq):
    j = pl.program_id(0)
    n_streams = x_ref.shape[0] // seq

    @pl.when(j == 0)
    def _():
        def slab(r, carry):
            rows = pl.ds(pl.multiple_of(r * seq, seq), seq)
            h = _ln_plain(x_ref[rows, :]) * (1.0 + sc_ref[pl.ds(r, 1), :]) + sh_ref[pl.ds(r, 1), :]
            h_scr[rows, :] = h.astype(BF16)
            return carry
        lax.fori_loop(0, n_streams, slab, 0)

    _in_projection(h_scr, (wu_ref, wv_ref, wq_ref, wf_ref, wi_ref, wg_ref), proj_scr)

    chains = []
    for hd in range(HEADS_PER_STEP):
        wm = ws_ref[hd][0:seq, 0:seq].astype(BF16)
        b_col = bs_ref[hd][0:seq, :]
        lbv = _lower_bound(lb_ref, hd, layer)
        for n in range(n_streams):
            rows = slice(n * seq, (n + 1) * seq)
            u, vn = _gmlp_rows(proj_scr[rows, _proj_cols(0, hd)], proj_scr[rows, _proj_cols(1, hd)],
                               lvg_ref[hd], lvb_ref[hd])
            vrows_ref[rows, hd * HEAD_DIM:(hd + 1) * HEAD_DIM] = vn
            sp = _dot(wm, vn.astype(BF16)) + b_col
            cata_ref[rows, hd * HEAD_DIM:(hd + 1) * HEAD_DIM] = (u * sp).astype(BF16)
            chains.append((jnp.transpose(s0_ref[n, hd]), lbv, gn_ref[hd],
                           [tuple(proj_scr[rows, _proj_cols(kind, hd)] for kind in range(2, N_KINDS))]))

    units = _hgrn_prepare(chains, seq)
    wide_decay = _hgrn_block_decay(units, seq) > SAFE_BLOCK_DECAY

    def finish(exact):
        outs, finals = _hgrn_finish(chains, _hgrn_scores(units, seq, exact))
        for hd in range(HEADS_PER_STEP):
            for n in range(n_streams):
                r0 = n * seq
                m = hd * n_streams + n
                cato_ref[r0:r0 + seq, hd * HEAD_DIM:(hd + 1) * HEAD_DIM] = outs[m][0].astype(BF16)
                state_ref[n, hd] = jnp.transpose(finals[m])

    @pl.when(wide_decay)
    def _():
        finish(True)

    @pl.when(jnp.logical_not(wide_decay))
    def _():
        finish(False)


def _mixer_sample_call(x2d, sh1, sc1, w_in_b, ln_v_g, ln_v_b, w_s, b_s, lb_h, gnorm_g, s0, layer, seq):
    rows, d = x2d.shape
    n_streams = rows // seq
    hp = HEADS_PER_STEP
    kern = functools.partial(_mixer_sample_kernel, layer=layer, seq=seq)
    head_vec = lambda j: (j, 0, 0)
    return pl.pallas_call(
        kern,
        grid=(N_HEADS // hp,),
        in_specs=[
            pl.BlockSpec((rows, d), lambda j: (0, 0)),
            pl.BlockSpec((n_streams, d), lambda j: (0, 0)),
            pl.BlockSpec((n_streams, d), lambda j: (0, 0)),
            *_w_in_specs(d),
            pl.BlockSpec((hp, 1, HEAD_DIM), head_vec),
            pl.BlockSpec((hp, 1, HEAD_DIM), head_vec),
            pl.BlockSpec((hp, GMLP_CHUNK, GMLP_CHUNK), head_vec),
            pl.BlockSpec((hp, GMLP_CHUNK, 1), head_vec),
            pl.BlockSpec((hp, lb_h.shape[1], HEAD_DIM), head_vec),
            pl.BlockSpec((hp, 1, HEAD_DIM), head_vec),
            pl.BlockSpec((n_streams, hp, HEAD_DIM, HEAD_DIM), lambda j: (0, j, 0, 0)),
        ],
        out_specs=[
            pl.BlockSpec((rows, hp * HEAD_DIM), lambda j: (0, j)),
            pl.BlockSpec((rows, hp * HEAD_DIM), lambda j: (0, j)),
            pl.BlockSpec((n_streams, hp, HEAD_DIM, HEAD_DIM), lambda j: (0, j, 0, 0)),
            pl.BlockSpec((rows, hp * HEAD_DIM), lambda j: (0, j)),
        ],
        out_shape=[
            jax.ShapeDtypeStruct((rows, N_HEADS * HEAD_DIM), BF16),
            jax.ShapeDtypeStruct((rows, N_HEADS * HEAD_DIM), BF16),
            jax.ShapeDtypeStruct((n_streams, N_HEADS, HEAD_DIM, HEAD_DIM), F32),
            jax.ShapeDtypeStruct((rows, N_HEADS * HEAD_DIM), F32),
        ],
        scratch_shapes=[
            pltpu.VMEM((rows, d), BF16),
            pltpu.VMEM((rows, N_KINDS * hp * HEAD_DIM), F32),
        ],
        compiler_params=_cparams(("arbitrary",)),
        name="mixer_sample",
    )(x2d, sh1, sc1, *([w_in_b] * N_KINDS), ln_v_g, ln_v_b, w_s, b_s, lb_h, gnorm_g, s0)


def _post_kernel(grp_ref, cap_ref, cop_ref, cas_ref, cos_ref, xp_ref, xs_ref, g1_ref, sh2_ref, sc2_ref, wout_ref, l1g_ref, l1b_ref,
                 wr_ref, br_ref, tri_ref,
                 x1_ref, h2_ref, meta_ref, wts_ref, cnt_ref,
                 mix_scr, hhi_scr, hlo_scr, run_scr, *, n_prompt_tiles, alpha):
    i = pl.program_id(0)
    is_s = i >= n_prompt_tiles
    tm = mix_scr.shape[0]

    @pl.when(i == 0)
    def _():
        run_scr[...] = jnp.zeros_like(run_scr)

    def slab(g):
        rows = slice(g * MOD_GROUP, (g + 1) * MOD_GROUP)
        x = jnp.where(is_s, xs_ref[rows, :], xp_ref[rows, :])
        stream = pl.ds(grp_ref[0, 0, g], 1)
        z = alpha * x + g1_ref[stream, :] * mix_scr[rows, :]
        x1 = _ln_plain(z) * l1g_ref[...] + l1b_ref[...]
        x1_ref[rows, :] = x1
        h2 = _ln_plain(x1) * (1.0 + sc2_ref[stream, :]) + sh2_ref[stream, :]
        _store_rows(h2_ref, (), g * MOD_GROUP, h2)
        hi = h2.astype(BF16)
        hhi_scr[rows, :] = hi
        hlo_scr[rows, :] = (h2 - hi.astype(F32)).astype(BF16)

    def project(rows):
        cat = jnp.where(is_s, jnp.concatenate([cas_ref[rows, :], cos_ref[rows, :]], axis=1),
                        jnp.concatenate([cap_ref[rows, :], cop_ref[rows, :]], axis=1))
        mix_scr[rows, :] = _dot(cat, wout_ref[...])

    half = tm // 2
    project(slice(0, half))
    project(slice(half, tm))
    for g in range(tm // MOD_GROUP):
        slab(g)

    _route(hhi_scr, hlo_scr, wr_ref, br_ref, tri_ref, run_scr, meta_ref, wts_ref, cnt_ref)


def _route(hhi_scr, hlo_scr, wr_ref, br_ref, before_ref, run_scr, meta_ref, wts_ref, cnt_ref):
    tm = hhi_scr.shape[0]
    n_r = N_GROUPS + N_EXPERTS
    n_rows = -(-n_r // 8) * 8
    s = _dot(hhi_scr[...], wr_ref[...]) + _dot(hlo_scr[...], wr_ref[...])
    logits = s + pltpu.roll(s, LANES - n_r, axis=1) + br_ref[...]
    lt = jnp.transpose(logits)[0:n_rows, :]

    row = lax.broadcasted_iota(I32, (n_rows, tm), 0)
    row_f = row.astype(F32)
    neg = jnp.float32(-jnp.inf)
    big = jnp.float32(LANES)

    def first_row_of(vals, top):
        return jnp.min(jnp.where(vals == top, row_f, big), axis=0, keepdims=True)

    gl = jnp.where(row < N_GROUPS, lt, neg)
    gmax = jnp.max(gl, axis=0, keepdims=True)
    gsel = first_row_of(gl, gmax)
    p_group = 1.0 / jnp.sum(jnp.exp(gl - gmax), axis=0, keepdims=True)

    e_lo = N_GROUPS + EXPERTS_PER_GROUP * gsel.astype(I32)
    emask = (row >= e_lo) & (row < e_lo + EXPERTS_PER_GROUP)
    el = jnp.where(emask, lt, neg)
    t1 = jnp.max(el, axis=0, keepdims=True)
    i1 = first_row_of(el, t1)
    el2 = jnp.where(row_f == i1, neg, el)
    t2 = jnp.max(el2, axis=0, keepdims=True)
    i2 = first_row_of(el2, t2)
    e2 = jnp.exp(t2 - t1)
    den = 1.0 + e2
    w1 = (1.0 / den) * p_group
    w2 = (e2 / den) * p_group

    sel1 = row_f == i1
    sel2 = row_f == i2
    onehot = jnp.where(sel1 | sel2, 1.0, 0.0)
    run_col = jnp.transpose(jnp.broadcast_to(run_scr[...], (LANES, LANES)))[0:n_rows, 0:1]
    before = _dot(onehot.astype(BF16), before_ref[...]) + run_col
    r1 = jnp.sum(jnp.where(sel1, before, 0.0), axis=0, keepdims=True)
    r2 = jnp.sum(jnp.where(sel2, before, 0.0), axis=0, keepdims=True)
    added = jnp.concatenate([jnp.sum(onehot, axis=1, keepdims=True), jnp.zeros((LANES - n_rows, 1), F32)], axis=0)
    run_scr[...] = run_scr[...] + jnp.transpose(jnp.broadcast_to(added, (LANES, LANES)))[0:1, :]
    cnt_ref[...] = run_scr[...].astype(I32)

    base = jnp.float32(N_GROUPS)
    mrow = lax.broadcasted_iota(I32, (META_ROWS, tm), 0)
    meta = jnp.where(mrow == 0, i1 - base, jnp.where(mrow == 1, i2 - base, jnp.where(mrow == 2, r1, jnp.where(mrow == 3, r2, 0.0))))
    meta_ref[0] = meta.astype(I32)
    wrow = lax.broadcasted_iota(I32, (LANES, tm), 0)
    wts_ref[...] = jnp.transpose(jnp.where(wrow == 0, w1, jnp.where(wrow == 1, w2, 0.0)))


def _post_call(groups3, cats_p, cats_s, x_p, x_s, mod, w_out_b, ln1_g, ln1_b, wr, br, tri, alpha):
    n_p, d = x_p.shape
    n_s = x_s.shape[0]
    d_a = cats_p[0].shape[1]
    tm = POST_TILE
    n_pt = n_p // tm
    n_tiles = n_pt + n_s // tm
    n_tok = n_p + n_s
    groups = tm // MOD_GROUP
    kern = functools.partial(_post_kernel, n_prompt_tiles=n_pt, alpha=alpha)
    p_idx = lambda i: (jnp.minimum(i, n_pt - 1), 0)
    s_idx = lambda i: (jnp.maximum(i - n_pt, 0), 0)
    tile = lambda i: (i, 0)
    const = lambda i: (0, 0)
    once = pl.Buffered(1)
    s_mode = once if n_tiles - n_pt == 1 else None
    mod_spec = lambda m: pl.BlockSpec((mod.shape[0], d), lambda i: (0, m), pipeline_mode=once)
    return pl.pallas_call(
        kern,
        grid=(n_tiles,),
        in_specs=[
            pl.BlockSpec((1, 1, groups), lambda i: (i, 0, 0), memory_space=pltpu.SMEM),
            pl.BlockSpec((tm, d_a), p_idx),
            pl.BlockSpec((tm, d_a), p_idx),
            pl.BlockSpec((tm, d_a), s_idx, pipeline_mode=s_mode),
            pl.BlockSpec((tm, d_a), s_idx, pipeline_mode=s_mode),
            pl.BlockSpec((tm, d), p_idx),
            pl.BlockSpec((tm, d), s_idx, pipeline_mode=s_mode),
            mod_spec(2), mod_spec(3), mod_spec(4),
            pl.BlockSpec((d, d), const, pipeline_mode=once),
            pl.BlockSpec((1, d), const),
            pl.BlockSpec((1, d), const),
            pl.BlockSpec((d, LANES), const, pipeline_mode=once),
            pl.BlockSpec((1, LANES), const),
            pl.BlockSpec((tm, tm), const, pipeline_mode=once),
        ],
        out_specs=[
            pl.BlockSpec((tm, d), tile),
            pl.BlockSpec((tm * _lines_per_row(d), LANES), tile),
            pl.BlockSpec((1, META_ROWS, tm), lambda i: (i, 0, 0)),
            pl.BlockSpec((tm, LANES), tile),
            pl.BlockSpec((1, LANES), const),
        ],
        out_shape=[
            jax.ShapeDtypeStruct((n_tok, d), F32),
            jax.ShapeDtypeStruct((n_tok * _lines_per_row(d), LANES), F32),
            jax.ShapeDtypeStruct((n_tiles, META_ROWS, tm), I32),
            jax.ShapeDtypeStruct((n_tok, LANES), F32),
            jax.ShapeDtypeStruct((1, LANES), I32),
        ],
        scratch_shapes=[
            pltpu.VMEM((tm, d), F32),
            pltpu.VMEM((tm, d), BF16),
            pltpu.VMEM((tm, d), BF16),
            pltpu.VMEM((1, LANES), F32),
        ],
        compiler_params=_cparams(("arbitrary",)),
        name="post_router",
    )(groups3, *cats_p, *cats_s, x_p, x_s, mod, mod, mod, w_out_b, ln1_g, ln1_b, wr, br, tri)


def _dispatch_kernel(slot_ref, gap_start_ref, gap_len_ref, h_ref, xs_ref, zero_scr, sem, gap_sem, tile_sem, *,
                     gaps_per_step):
    i = pl.program_id(0)
    tm = slot_ref.shape[2] // 2
    lpr = h_ref.shape[0] // tm
    n_gaps = gap_len_ref.shape[0]
    tile_rows = zero_scr.shape[0] // lpr

    def lines(row, n_rows=1):
        return pl.ds(pl.multiple_of(row * lpr, lpr), n_rows * lpr)

    @pl.when(i == 0)
    def _():
        zero_scr[...] = jnp.zeros_like(zero_scr)

    def row_copy(r, k):
        return pltpu.make_async_copy(h_ref.at[lines(r)], xs_ref.at[lines(slot_ref[0, 0, k * tm + r])], sem)

    def start_rows(r, carry):
        row_copy(r, 0).start(priority=0)
        row_copy(r, 1).start(priority=1)
        return carry
    lax.fori_loop(0, tm, start_rows, 0, unroll=ROW_DMA_UNROLL)

    def gap_head(g):
        return jnp.where(g < n_gaps - 1, (-gap_start_ref[g]) & (GAP_ROWS - 1), 0)

    def gap_chunks(g):
        return jnp.where(g < n_gaps - 1, (gap_len_ref[g] - gap_head(g)) // GAP_ROWS, 0)

    def gap_tiles(g):
        return jnp.where(g == n_gaps - 1, gap_len_ref[n_gaps - 1] // tile_rows, 0)

    def zero_row_copy(g, r):
        return pltpu.make_async_copy(zero_scr.at[lines(0)], xs_ref.at[lines(gap_start_ref[g] + r)], gap_sem)

    def zero_chunk_copy(g, q):
        start = gap_start_ref[g] + gap_head(g) + q * GAP_ROWS
        return pltpu.make_async_copy(zero_scr.at[lines(0, GAP_ROWS)], xs_ref.at[lines(start, GAP_ROWS)], gap_sem)

    def zero_tile_copy(g, t):
        return pltpu.make_async_copy(zero_scr, xs_ref.at[lines(gap_start_ref[g] + t * tile_rows, tile_rows)], tile_sem)

    gap_copies = ((gap_head, zero_row_copy), (gap_chunks, zero_chunk_copy), (gap_tiles, zero_tile_copy))

    def for_gap_copies(act):
        for j in range(gaps_per_step):
            g = jnp.minimum(i * gaps_per_step + j, n_gaps - 1)
            live = i * gaps_per_step + j < n_gaps
            for count, copy in gap_copies:
                lax.fori_loop(0, jnp.where(live, count(g), 0), lambda q, c, g=g, copy=copy: (act(copy(g, q)), c)[1], 0)

    for_gap_copies(lambda copy: copy.start())
    for _ in range(2 * tm // tile_rows):
        pltpu.make_async_copy(zero_scr, zero_scr, sem).wait()
    for_gap_copies(lambda copy: copy.wait())


def _dispatch_call(slots3, gap_start, gap_len, h2, n_sorted):
    n_tiles = slots3.shape[0]
    lpr = h2.shape[0] // (n_tiles * TOK_TILE)
    n_gaps = gap_len.shape[0]
    kern = functools.partial(_dispatch_kernel, gaps_per_step=-(-n_gaps // n_tiles))
    return pl.pallas_call(
        kern,
        grid=(n_tiles,),
        in_specs=[
            pl.BlockSpec((1, 1, slots3.shape[2]), lambda i: (i, 0, 0), memory_space=pltpu.SMEM),
            pl.BlockSpec((n_gaps,), lambda i: (0,), memory_space=pltpu.SMEM),
            pl.BlockSpec((n_gaps,), lambda i: (0,), memory_space=pltpu.SMEM),
            pl.BlockSpec((TOK_TILE * lpr, LANES), lambda i: (i, 0)),
        ],
        out_specs=pl.BlockSpec(memory_space=pl.ANY),
        out_shape=jax.ShapeDtypeStruct((n_sorted * lpr, LANES), F32),
        scratch_shapes=[pltpu.VMEM((EXP_TILE * lpr, LANES), F32)] + [pltpu.SemaphoreType.DMA(())] * 3,
        compiler_params=_cparams(("arbitrary",)),
        name="dispatch",
    )(slots3, gap_start, gap_len, h2)


def _expert_kernel(te_ref, nt_ref, nxt_ref, run_ref, xs_ref, w1_ref, w3_ref, w2_ref, ys_ref,
                   x_scr, w1_scr, w3_scr, w2f_scr, w13_scr, w2_scr, xsem, wsem):
    t = pl.program_id(0)
    n_t = pl.num_programs(0)
    tile_lines = x_scr.shape[1]
    d, d_exp = w1_scr.shape[1:]
    x_slots = x_scr.shape[0]
    xs = lax.rem(t, x_slots)
    e = te_ref[t]
    prev = te_ref[jnp.maximum(t - 1, 0)]
    changed = (t == 0) | (e != prev)
    valid = t < nt_ref[0]
    wslot = lax.rem(run_ref[t], 2)

    def x_copy(tile, slot):
        start = pl.multiple_of(tile * tile_lines, tile_lines)
        return pltpu.make_async_copy(xs_ref.at[pl.ds(start, tile_lines)], x_scr.at[slot], xsem.at[slot])

    def w_copies(expert, slot):
        return (pltpu.make_async_copy(w1_ref.at[expert], w1_scr.at[slot], wsem.at[slot]),
                pltpu.make_async_copy(w3_ref.at[expert], w3_scr.at[slot], wsem.at[slot]),
                pltpu.make_async_copy(w2_ref.at[expert], w2f_scr.at[slot], wsem.at[slot]))

    @pl.when(t == 0)
    def _():
        for ahead in range(X_AHEAD):
            x_copy(ahead, ahead).start()
        for c in w_copies(e, wslot):
            c.start()

    @pl.when(t + X_AHEAD < n_t)
    def _():
        x_copy(t + X_AHEAD, lax.rem(t + X_AHEAD, x_slots)).start()

    @pl.when(changed)
    def _():
        for c in w_copies(e, wslot):
            c.wait()
        nxt = nxt_ref[t]

        @pl.when(nxt != e)
        def _():
            for c in w_copies(nxt, 1 - wslot):
                c.start()
        w13_scr[:, 0:d_exp] = w1_scr[wslot].astype(BF16)
        w13_scr[:, d_exp:2 * d_exp] = w3_scr[wslot].astype(BF16)
        w2_scr[...] = w2f_scr[wslot].astype(BF16)

    x_copy(t, xs).wait()

    @pl.when(valid)
    def _():
        x = _load_rows(x_scr, (xs,), 0, tile_lines // _lines_per_row(d), d)
        h13 = _dot(x.astype(BF16), w13_scr[...])
        hm = jax.nn.silu(h13[:, 0:d_exp]) * h13[:, d_exp:2 * d_exp]
        _store_rows(ys_ref, (), 0, _dot(hm.astype(BF16), w2_scr[...]))

    @pl.when(jnp.logical_not(valid))
    def _():
        ys_ref[...] = jnp.zeros_like(ys_ref)


def _expert_call(tile_expert, n_valid, next_expert, run_id, xs, w1, w3, w2):
    n_exp, d, d_exp = w1.shape
    n_sorted = xs.shape[0] // _lines_per_row(d)
    n_tiles = n_sorted // EXP_TILE
    assert n_tiles > X_AHEAD
    grid_spec = pltpu.PrefetchScalarGridSpec(
        num_scalar_prefetch=4,
        grid=(n_tiles,),
        in_specs=[pl.BlockSpec(memory_space=pl.ANY)] * 4,
        out_specs=pl.BlockSpec((EXP_TILE * _lines_per_row(d), LANES), lambda t, te, nt, nx, rn: (t, 0)),
        scratch_shapes=[
            pltpu.VMEM((X_AHEAD + 1, EXP_TILE * _lines_per_row(d), LANES), F32),
            pltpu.VMEM((2, d, d_exp), F32),
            pltpu.VMEM((2, d, d_exp), F32),
            pltpu.VMEM((2, d_exp, d), F32),
            pltpu.VMEM((d, 2 * d_exp), BF16),
            pltpu.VMEM((d_exp, d), BF16),
            pltpu.SemaphoreType.DMA((X_AHEAD + 1,)),
            pltpu.SemaphoreType.DMA((2,)),
        ],
    )
    return pl.pallas_call(
        _expert_kernel,
        grid_spec=grid_spec,
        out_shape=jax.ShapeDtypeStruct((n_sorted * _lines_per_row(d), LANES), F32),
        compiler_params=_cparams(("arbitrary",)),
        name="experts",
    )(tile_expert, n_valid, next_expert, run_id, xs, w1, w3, w2)


def _combine_kernel(scur_ref, snext_ref, grp_ref, x1_ref, wts_ref, g2_ref, l2g_ref, l2b_ref, ys_ref, outp_ref, outs_ref,
                    y_scr, sem, *, n_prompt_tiles, alpha):
    i = pl.program_id(0)
    n_i = pl.num_programs(0)
    is_s = i >= n_prompt_tiles
    tm, d = x1_ref.shape
    lpr = _lines_per_row(d)
    cur = lax.rem(i, 2)

    def row_lines(r):
        return pl.ds(pl.multiple_of(r * lpr, lpr), lpr)

    def start_gathers(s_ref, buf):
        def body(r, carry):
            for k in range(2):
                pltpu.make_async_copy(ys_ref.at[row_lines(s_ref[0, 0, k * tm + r])], y_scr.at[buf, k, row_lines(r)],
                                      sem.at[buf, k]).start(priority=k)
            return carry
        lax.fori_loop(0, tm, body, 0, unroll=ROW_DMA_UNROLL // 2)

    @pl.when(i == 0)
    def _():
        start_gathers(scur_ref, 0)

    @pl.when(i + 1 < n_i)
    def _():
        start_gathers(snext_ref, 1 - cur)

    for k in range(2):
        pltpu.make_async_copy(y_scr.at[cur, k], y_scr.at[cur, k], sem.at[cur, k]).wait()

    def slab(g, carry, out_ref):
        rows = pl.ds(pl.multiple_of(g * MOD_GROUP, MOD_GROUP), MOD_GROUP)
        w = wts_ref[rows, :]
        moe = (w[:, 0:1] * _load_rows(y_scr, (cur, 0), g * MOD_GROUP, MOD_GROUP, d)
               + w[:, 1:2] * _load_rows(y_scr, (cur, 1), g * MOD_GROUP, MOD_GROUP, d))
        z = alpha * x1_ref[rows, :] + g2_ref[pl.ds(grp_ref[0, 0, g], 1), :] * moe
        out_ref[rows, :] = _ln_plain(z) * l2g_ref[...] + l2b_ref[...]
        return carry

    @pl.when(is_s)
    def _():
        lax.fori_loop(0, tm // MOD_GROUP, functools.partial(slab, out_ref=outs_ref), 0, unroll=SLAB_UNROLL)

    @pl.when(jnp.logical_not(is_s))
    def _():
        lax.fori_loop(0, tm // MOD_GROUP, functools.partial(slab, out_ref=outp_ref), 0, unroll=SLAB_UNROLL)


def _combine_call(slots3, groups3, x1, wts, mod, ln2_g, ln2_b, ys, n_p, alpha):
    n_tok, d = x1.shape
    n_tiles = n_tok // TOK_TILE
    n_pt = n_p // TOK_TILE
    groups = TOK_TILE // MOD_GROUP
    kern = functools.partial(_combine_kernel, n_prompt_tiles=n_pt, alpha=alpha)
    slot_tile = lambda ahead: pl.BlockSpec(
        (1, 1, slots3.shape[2]), lambda i: (jnp.minimum(i + ahead, n_tiles - 1), 0, 0), memory_space=pltpu.SMEM)
    return pl.pallas_call(
        kern,
        grid=(n_tiles,),
        in_specs=[
            slot_tile(0), slot_tile(1),
            pl.BlockSpec((1, 1, groups), lambda i: (i, 0, 0), memory_space=pltpu.SMEM),
            pl.BlockSpec((TOK_TILE, d), lambda i: (i, 0)),
            pl.BlockSpec((TOK_TILE, LANES), lambda i: (i, 0)),
            pl.BlockSpec((mod.shape[0], d), lambda i: (0, 5)),
            pl.BlockSpec((1, d), lambda i: (0, 0)),
            pl.BlockSpec((1, d), lambda i: (0, 0)),
            pl.BlockSpec(memory_space=pl.ANY),
        ],
        out_specs=[
            pl.BlockSpec((TOK_TILE, d), lambda i: (jnp.minimum(i, n_pt - 1), 0)),
            pl.BlockSpec((TOK_TILE, d), lambda i: (jnp.maximum(i - n_pt, 0), 0)),
        ],
        out_shape=[
            jax.ShapeDtypeStruct((n_p, d), F32),
            jax.ShapeDtypeStruct((n_tok - n_p, d), F32),
        ],
        scratch_shapes=[
            pltpu.VMEM((2, 2, TOK_TILE * _lines_per_row(d), LANES), F32),
            pltpu.SemaphoreType.DMA((2, 2)),
        ],
        compiler_params=_cparams(("arbitrary",)),
        name="combine",
    )(slots3, slots3, groups3, x1, wts, mod, ln2_g, ln2_b, ys)


def _routing_tables(meta, cnt_row, n_sorted):
    experts = meta[:, 0:2, :]
    ranks = meta[:, 2:4, :]
    cnt = cnt_row[0, N_GROUPS:N_GROUPS + N_EXPERTS]
    padded = ((cnt + EXP_TILE - 1) // EXP_TILE) * EXP_TILE
    ends = jnp.cumsum(padded)
    offs = ends - padded
    ids = jnp.arange(N_EXPERTS, dtype=I32)
    is_expert = experts[None] == ids[:, None, None, None]
    slots = jnp.sum(jnp.where(is_expert, offs[:, None, None, None], 0), axis=0) + ranks
    total = ends[-1]
    n_tiles = n_sorted // EXP_TILE
    tile_ids = jnp.arange(n_tiles, dtype=I32)
    tile_expert = jnp.minimum(
        jnp.sum((tile_ids[:, None] >= (ends // EXP_TILE)[None, :]).astype(I32), axis=1), N_EXPERTS - 1)
    n_valid = (total // EXP_TILE).reshape(1).astype(I32)
    later = (ids[None, :] > tile_expert[:, None]) & (padded[None, :] > 0)
    next_expert = jnp.min(jnp.where(later, ids[None, :], N_EXPERTS), axis=1)
    next_expert = jnp.where(next_expert == N_EXPERTS, tile_expert, next_expert).astype(I32)
    starts_run = jnp.concatenate([jnp.zeros((1,), I32), (tile_expert[1:] != tile_expert[:-1]).astype(I32)])
    run_id = jnp.cumsum(starts_run).astype(I32)
    gap_start = jnp.concatenate([offs + cnt, total.reshape(1)]).astype(I32)
    gap_len = jnp.concatenate([padded - cnt, (n_sorted - total).reshape(1)]).astype(I32)
    return slots.astype(I32), (tile_expert, n_valid, next_expert, run_id), gap_start, gap_len


def _layer(layer, n_layers, xp, xs, s0_l, c_all, p):
    (w_ada, b_ada, w_in, ln_v_g, ln_v_b, w_s, b_s, hgrn_lb, gnorm_g, w_out, ln1_g, ln1_b,
     w_rg, b_rg, w_re, b_re, w1, w3, w2, ln2_g, ln2_b) = p
    bsz, t, d = xp.shape
    n_streams, seq, _ = xs.shape
    alpha = float((2.0 * n_layers) ** 0.25)
    n_p = bsz * t
    n_s = n_streams * seq
    n_tok = n_p + n_s

    n_c = c_all.shape[0]
    c_pad = jnp.pad(c_all, ((0, (-n_c) % 8), (0, 0)))
    mod = _ada_call(c_pad, w_ada, b_ada)
    sh1, sc1 = mod[:n_c, 0:d], mod[:n_c, d:2 * d]

    w_in_b = w_in.astype(BF16)
    w_out_b = w_out.astype(BF16)
    lvg = ln_v_g.reshape(N_HEADS, 1, HEAD_DIM)
    lvb = ln_v_b.reshape(N_HEADS, 1, HEAD_DIM)
    b_s3 = b_s.reshape(N_HEADS, GMLP_CHUNK, 1)
    lb_h = jnp.transpose(hgrn_lb.reshape(hgrn_lb.shape[0], N_HEADS, HEAD_DIM), (1, 0, 2))
    gn = gnorm_g.reshape(N_HEADS, 1, HEAD_DIM)

    ca_p, co_p, state_p = _mixer_prompt_call(xp, sh1[:bsz], sc1[:bsz], w_in_b, lvg, lvb, w_s, b_s3, lb_h, gn, layer)
    ca_s, co_s, state_s, vrows = _mixer_sample_call(xs.reshape(n_s, d), sh1[bsz:], sc1[bsz:], w_in_b, lvg, lvb, w_s,
                                               b_s3, lb_h, gn, s0_l, layer, seq)

    group_stream = jnp.concatenate([
        jnp.repeat(jnp.arange(bsz, dtype=I32), t // MOD_GROUP),
        bsz + jnp.repeat(jnp.arange(n_streams, dtype=I32), seq // MOD_GROUP)])

    wr = jnp.concatenate([w_rg, w_re], axis=1)
    wr_hi = wr.astype(BF16)
    wr_lo = (wr - wr_hi.astype(F32)).astype(BF16)
    n_r = wr.shape[1]
    wr_cat = jnp.concatenate([wr_hi, wr_lo, jnp.zeros((d, LANES - 2 * n_r), BF16)], axis=1)
    br = jnp.pad(jnp.concatenate([b_rg, b_re]), (0, LANES - n_r)).reshape(1, LANES)
    tri = jnp.triu(jnp.ones((POST_TILE, POST_TILE), F32), 1).astype(BF16)

    x1, h2, meta, wts, cnt_row = _post_call(group_stream.reshape(n_tok // POST_TILE, 1, POST_TILE // MOD_GROUP),
                                            (ca_p, co_p), (ca_s, co_s), xp.reshape(n_p, d), xs.reshape(n_s, d),
                                            mod, w_out_b, ln1_g.reshape(1, d), ln1_b.reshape(1, d), wr_cat, br, tri, alpha)

    n_sorted = 2 * n_tok + N_EXPERTS * EXP_TILE
    slots, tile_tables, gap_start, gap_len = _routing_tables(meta, cnt_row, n_sorted)
    n_tiles = n_tok // TOK_TILE
    assert POST_TILE == TOK_TILE
    slots3 = slots.reshape(n_tiles, 1, 2 * TOK_TILE)

    xs_sorted = _dispatch_call(slots3, gap_start, gap_len, h2, n_sorted)
    ys_sorted = _expert_call(*tile_tables, xs_sorted, w1, w3, w2)
    yp, ys_out = _combine_call(slots3, group_stream.reshape(n_tiles, 1, TOK_TILE // MOD_GROUP), x1, wts, mod,
                               ln2_g.reshape(1, d), ln2_b.reshape(1, d), ys_sorted, n_p, alpha)

    v_rows = vrows.reshape(n_streams, seq, N_HEADS, HEAD_DIM)
    return yp.reshape(bsz, t, d), ys_out.reshape(n_streams, seq, d), state_p, state_s, v_rows


def kernel(x_prompt, x_sample, state_hgrn, c_prompt, c_sample, w_ada, b_ada, w_in, ln_v_g, ln_v_b, w_s, b_s, hgrn_lb, gnorm_g, w_out, ln1_g, ln1_b, w_router_g, b_router_g, w_router_e, b_router_e, w1, w3, w2, ln2_g, ln2_b):
    n_layers = w_ada.shape[0]
    assert x_prompt.shape[1] % MIX_TILE == 0 and x_prompt.shape[2] == 2 * N_HEADS * HEAD_DIM
    assert x_sample.shape[1] % MOD_GROUP == 0 and x_sample.shape[1] <= SUB_CHUNK
    assert (x_sample.shape[0] * x_sample.shape[1]) % TOK_TILE == 0 and TOK_TILE % POST_TILE == 0
    c_all = jnp.concatenate([c_prompt, c_sample], axis=0)
    xp, xs = x_prompt, x_sample
    sp_list, ss_list, vs_list = [], [], []
    for l in range(n_layers):
        p = (w_ada[l], b_ada[l], w_in[l], ln_v_g[l], ln_v_b[l], w_s[l], b_s[l], hgrn_lb, gnorm_g[l], w_out[l],
             ln1_g[l], ln1_b[l], w_router_g[l], b_router_g[l], w_router_e[l], b_router_e[l],
             w1[l], w3[l], w2[l], ln2_g[l], ln2_b[l])
        xp, xs, sp, ss, vs = _layer(l, n_layers, xp, xs, state_hgrn[l], c_all, p)
        sp_list.append(sp.astype(state_hgrn.dtype))
        ss_list.append(ss.astype(state_hgrn.dtype))
        vs_list.append(vs)
    return (xp, xs, jnp.stack(sp_list, axis=0), jnp.stack(ss_list, axis=0), jnp.stack(vs_list, axis=0))
```
